```python
import math
import jax, jax.numpy as jnp
from jax import lax
import numpy as np

D_MODEL = 1024
BATCH = 32
SEQ = 256
DEPTH = 2
DEC_BATCH = 2
DEC_SEQ = 2048
PAST_LEN = 256

GRID_W = 64
N_MIXERS = 2
N_ATTN_LAYERS = (DEPTH + 1) // 2
N_HYENA_LAYERS = DEPTH // 2
N_HEADS = 16
N_KV_HEADS = 4
HEAD_DIM = 64
ROPE_THETA = 10000.0
Q_BLOCK = 128
HYENA_ORDER = 2
SHORT_CONV = 3
FILTER_BANDS = 8
FILTER_FEAT = 1 + 2 * FILTER_BANDS
FILTER_HIDDEN = 64
N_GROUPS = 4
EXPERTS_PER_GROUP = 8
N_EXPERTS = N_GROUPS * EXPERTS_PER_GROUP
TOP_K_IN_GROUP = 2
EXPERT_HIDDEN = D_MODEL // 4
EPS = 1e-6

kernel_name = "hybrid_attn_hyena_hmoe_diffusion_step"


def rmsnorm(x, gain):
    xf = x.astype(jnp.float32)
    y = xf * lax.rsqrt(jnp.mean(xf * xf, axis=-1, keepdims=True) + EPS)
    return (y * gain.astype(jnp.float32)).astype(x.dtype)


def adaln(cond, w, b):
    ada = jax.nn.silu(cond) @ w + b
    ada = ada.reshape(cond.shape[0], 6, D_MODEL)[:, :, None, :]
    return [ada[:, i] for i in range(6)]


def modulate(h, shift, scale):
    return h * (1.0 + scale) + shift


def axial_rope_tables(L):
    rows = L // GRID_W
    row = jnp.repeat(jnp.arange(rows, dtype=jnp.float32), GRID_W)
    col = jnp.tile(jnp.arange(GRID_W, dtype=jnp.float32), rows)
    axis_dim = HEAD_DIM // 2
    inv_freq = ROPE_THETA ** (-jnp.arange(0, axis_dim, 2, dtype=jnp.float32) / axis_dim)
    ang_r = row[:, None] * inv_freq[None, :]
    ang_c = col[:, None] * inv_freq[None, :]
    return (jnp.cos(ang_r), jnp.sin(ang_r), jnp.cos(ang_c), jnp.sin(ang_c))


def rotate_axis(x, cos, sin):
    x1, x2 = jnp.split(x, 2, axis=-1)
    return jnp.concatenate([x1 * cos - x2 * sin, x2 * cos + x1 * sin], axis=-1)


def apply_axial_rope(x, tables):
    cos_r, sin_r, cos_c, sin_c = tables
    xf = x.astype(jnp.float32)
    xr, xc = jnp.split(xf, 2, axis=-1)
    out = jnp.concatenate([rotate_axis(xr, cos_r, sin_r), rotate_axis(xc, cos_c, sin_c)], axis=-1)
    return out.astype(x.dtype)


def block_attention(q, k, v):
    B, Hq, Lq, Dh = q.shape
    G = Hq // N_KV_HEADS
    nb = Lq // Q_BLOCK
    qb = jnp.moveaxis(q.reshape(B, N_KV_HEADS, G, nb, Q_BLOCK, Dh), 3, 0)
    scale = Dh ** -0.5

    def one_block(qblk):
        s = jnp.einsum('bkgqd,bksd->bkgqs', qblk, k, preferred_element_type=jnp.float32) * scale
        p = jax.nn.softmax(s, axis=-1)
        return jnp.einsum('bkgqs,bksd->bkgqd', p.astype(v.dtype), v)

    out = lax.map(one_block, qb)
    return jnp.moveaxis(out, 0, 3).reshape(B, Hq, Lq, Dh)


def attn_qkv(h, w_q, w_kv, q_gain, k_gain):
    B, L, _ = h.shape
    q = (h @ w_q).reshape(B, L, N_HEADS, HEAD_DIM)
    kv = (h @ w_kv).reshape(B, L, 2, N_KV_HEADS, HEAD_DIM)
    k, v = kv[:, :, 0], kv[:, :, 1]
    q = rmsnorm(q, q_gain)
    k = rmsnorm(k, k_gain)
    return q.transpose(0, 2, 1, 3), k.transpose(0, 2, 1, 3), v.transpose(0, 2, 1, 3)


def attn_out(o, w_o):
    B, H, L, Dh = o.shape
    return o.transpose(0, 2, 1, 3).reshape(B, L, H * Dh) @ w_o


def short_conv(u, w, b):
    L = u.shape[1]
    pad = SHORT_CONV // 2
    up = jnp.pad(u, ((0, 0), (pad, pad), (0, 0)))
    return sum(up[:, i:i + L] * w[i] for i in range(SHORT_CONV)) + b


def implicit_filters(L, w1, b1, w2, b2, w3, b3, freq, log_decay):
    f32 = jnp.float32
    t_norm = jnp.linspace(0.0, 1.0, L, dtype=f32)[:, None]
    bands = jnp.linspace(1e-4, FILTER_BANDS - 1, FILTER_BANDS, dtype=f32)[None, :]
    w = (2.0 * math.pi) * jnp.arange(L, dtype=f32)[:, None] / L
    feats = jnp.concatenate([t_norm, jnp.cos(bands * w), -jnp.sin(bands * w)], axis=-1)
    fr = freq.astype(f32)
    hdn = jnp.sin(fr * (feats @ w1.astype(f32) + b1.astype(f32)))
    hdn = jnp.sin(fr * (hdn @ w2.astype(f32) + b2.astype(f32)))
    filt = hdn @ w3.astype(f32) + b3.astype(f32)
    filt = filt * jnp.exp(-t_norm * jnp.exp(log_decay.astype(f32))[None, :])
    filt = filt.reshape(L, HYENA_ORDER, 2, D_MODEL).transpose(1, 2, 0, 3)
    norm = jnp.sum(jnp.abs(filt), axis=(1, 2), keepdims=True)
    return filt / (norm + EPS)


def bidirectional_long_conv(z, h_fwd, h_bwd):
    L = z.shape[1]
    first = h_fwd.at[0].add(h_bwd[0])
    circ = jnp.concatenate([first, jnp.zeros((1, D_MODEL), jnp.float32), jnp.flip(h_bwd[1:], axis=0)], axis=0)
    zf = jnp.fft.rfft(z.astype(jnp.float32), n=2 * L, axis=1)
    cf = jnp.fft.rfft(circ, n=2 * L, axis=0)
    y = jnp.fft.irfft(zf * cf[None], n=2 * L, axis=1)[:, :L]
    return y.astype(z.dtype)


def hyena_mixer(h, w_in, b_in, sw, sb, fw1, fb1, fw2, fb2, fw3, fb3, ffreq, log_decay, skip, w_out, b_out):
    L = h.shape[1]
    u = short_conv(h @ w_in + b_in, sw, sb)
    x1, x2, v = jnp.split(u, 3, axis=-1)
    filt = implicit_filters(L, fw1, fb1, fw2, fb2, fw3, fb3, ffreq, log_decay)
    z = v
    for o, gate in enumerate((x1, x2)):
        z = gate * (bidirectional_long_conv(z, filt[o, 0], filt[o, 1]) + skip[o] * z)
    return z @ w_out + b_out


def hierarchical_moe(h, wg, bg, we, be, w_gate, w_up, w_down):
    B, L, D = h.shape
    t = h.reshape(B * L, D)
    g_logits = (t @ wg + bg).astype(jnp.float32)
    g_prob = jax.nn.softmax(g_logits, axis=-1)
    g_idx = jnp.argmax(g_logits, axis=-1)
    g_top = jnp.max(g_prob, axis=-1, keepdims=True)
    e_logits = (jnp.einsum('nd,dge->nge', t, we) + be).astype(jnp.float32)
    e_in_group = jnp.einsum('nge,ng->ne', e_logits, jax.nn.one_hot(g_idx, N_GROUPS, dtype=jnp.float32))
    e_val, e_idx = lax.top_k(e_in_group, TOP_K_IN_GROUP)
    e_w = jax.nn.softmax(e_val, axis=-1) * g_top
    expert_id = g_idx[:, None] * EXPERTS_PER_GROUP + e_idx
    combine = jnp.sum(jax.nn.one_hot(expert_id, N_EXPERTS, dtype=jnp.float32) * e_w[..., None], axis=1)
    a = jnp.einsum('nd,edh->neh', t, w_gate)
    b = jnp.einsum('nd,edh->neh', t, w_up)
    act = jax.nn.silu(a) * b * combine[:, :, None].astype(t.dtype)
    out = jnp.einsum('neh,ehd->nd', act, w_down)
    return out.reshape(B, L, D)


def _trunk(x, cond, p, cache_k, cache_v):
    latent = cache_k is not None
    rope = axial_rope_tables(x.shape[1]) if latent else None
    ctx_k, ctx_v = [], []
    for l in range(DEPTH):
        sh1, sc1, g1, sh2, sc2, g2 = adaln(cond, p['w_ada'][l], p['b_ada'][l])
        h = modulate(rmsnorm(x, p['norm_mix'][l]), sh1, sc1)
        if l % N_MIXERS == 0:
            a = l // N_MIXERS
            q, k, v = attn_qkv(h, p['attn_w_q'][a], p['attn_w_kv'][a], p['attn_q_norm'][a], p['attn_k_norm'][a])
            if latent:
                q = apply_axial_rope(q, rope)
                k = apply_axial_rope(k, rope)
                k_all = jnp.concatenate([k, cache_k[:, a].astype(k.dtype)], axis=2)
                v_all = jnp.concatenate([v, cache_v[:, a].astype(v.dtype)], axis=2)
            else:
                ctx_k.append(k)
                ctx_v.append(v)
                k_all, v_all = k, v
            mix = attn_out(block_attention(q, k_all, v_all), p['attn_w_o'][a])
        else:
            j = l // N_MIXERS
            mix = hyena_mixer(h, p['hy_w_in'][j], p['hy_b_in'][j], p['hy_short_w'][j], p['hy_short_b'][j],
                              p['hy_filt_w1'][j], p['hy_filt_b1'][j], p['hy_filt_w2'][j], p['hy_filt_b2'][j],
                              p['hy_filt_w3'][j], p['hy_filt_b3'][j], p['hy_filt_freq'][j], p['hy_log_decay'][j],
                              p['hy_skip'][j], p['hy_w_out'][j], p['hy_b_out'][j])
        x = x + g1 * mix
        h = modulate(rmsnorm(x, p['norm_ffn'][l]), sh2, sc2)
        x = x + g2 * hierarchical_moe(h, p['router_group_w'][l], p['router_group_b'][l],
                                      p['router_expert_w'][l], p['router_expert_b'][l],
                                      p['moe_w_gate'][l], p['moe_w_up'][l], p['moe_w_down'][l])
    y = rmsnorm(x, p['final_norm'])
    if latent:
        return y, None, None
    return y, jnp.stack(ctx_k, axis=1), jnp.stack(ctx_v, axis=1)


def setup_inputs(seed: int = 0) -> dict:
    key = jax.random.key(seed)
    ks = jax.random.split(key, 40)
    f32 = jnp.float32

    def nrm(k, shape, scale):
        return jax.random.normal(k, shape, f32) * scale

    D = D_MODEL
    NA, NH = N_ATTN_LAYERS, N_HYENA_LAYERS
    kv_shape_dec = (DEC_BATCH, NA, N_KV_HEADS, PAST_LEN, HEAD_DIM)
    return {
        'x_prompt': nrm(ks[0], (BATCH, SEQ, D), 1.0),
        'x_sample': nrm(ks[1], (DEC_BATCH, DEC_SEQ, D), 1.0),
        'cache_k': nrm(ks[2], kv_shape_dec, 1.0),
        'cache_v': nrm(ks[3], kv_shape_dec, 1.0),
        'c': nrm(ks[4], (DEC_BATCH, D), 1.0),
        'c_ctx': nrm(ks[5], (D,), 1.0),
        'w_ada': nrm(ks[6], (DEPTH, D, 6 * D), D ** -0.5),
        'b_ada': nrm(ks[7], (DEPTH, 6 * D), 0.02),
        'norm_mix': 1.0 + nrm(ks[8], (DEPTH, D), 0.1),
        'norm_ffn': 1.0 + nrm(ks[9], (DEPTH, D), 0.1),
        'attn_w_q': nrm(ks[10], (NA, D, N_HEADS * HEAD_DIM), D ** -0.5),
        'attn_w_kv': nrm(ks[11], (NA, D, 2 * N_KV_HEADS * HEAD_DIM), D ** -0.5),
        'attn_q_norm': 1.0 + nrm(ks[12], (NA, HEAD_DIM), 0.1),
        'attn_k_norm': 1.0 + nrm(ks[13], (NA, HEAD_DIM), 0.1),
        'attn_w_o': nrm(ks[14], (NA, N_HEADS * HEAD_DIM, D), (N_HEADS * HEAD_DIM) ** -0.5),
        'hy_w_in': nrm(ks[15], (NH, D, 3 * D), D ** -0.5),
        'hy_b_in': nrm(ks[16], (NH, 3 * D), 0.02),
        'hy_short_w': nrm(ks[17], (NH, SHORT_CONV, 3 * D), SHORT_CONV ** -0.5),
        'hy_short_b': nrm(ks[18], (NH, 3 * D), 0.02),
        'hy_filt_w1': nrm(ks[19], (NH, FILTER_FEAT, FILTER_HIDDEN), FILTER_FEAT ** -0.5),
        'hy_filt_b1': nrm(ks[20], (NH, FILTER_HIDDEN), 0.1),
        'hy_filt_w2': nrm(ks[21], (NH, FILTER_HIDDEN, FILTER_HIDDEN), FILTER_HIDDEN ** -0.5),
        'hy_filt_b2': nrm(ks[22], (NH, FILTER_HIDDEN), 0.1),
        'hy_filt_w3': nrm(ks[23], (NH, FILTER_HIDDEN, 2 * HYENA_ORDER * D), FILTER_HIDDEN ** -0.5),
        'hy_filt_b3': nrm(ks[24], (NH, 2 * HYENA_ORDER * D), 0.1),
        'hy_filt_freq': 1.0 + nrm(ks[25], (NH, FILTER_HIDDEN), 0.1),
        'hy_log_decay': jax.random.uniform(ks[26], (NH, 2 * HYENA_ORDER * D), f32, math.log(3.0), math.log(15.0)),
        'hy_skip': nrm(ks[27], (NH, HYENA_ORDER, D), 0.1),
        'hy_w_out': nrm(ks[28], (NH, D, D), D ** -0.5),
        'hy_b_out': nrm(ks[29], (NH, D), 0.02),
        'router_group_w': nrm(ks[30], (DEPTH, D, N_GROUPS), D ** -0.5),
        'router_group_b': nrm(ks[31], (DEPTH, N_GROUPS), 0.01),
        'router_expert_w': nrm(ks[32], (DEPTH, D, N_GROUPS, EXPERTS_PER_GROUP), D ** -0.5),
        'router_expert_b': nrm(ks[33], (DEPTH, N_GROUPS, EXPERTS_PER_GROUP), 0.01),
        'moe_w_gate': nrm(ks[34], (DEPTH, N_EXPERTS, D, EXPERT_HIDDEN), D ** -0.5),
        'moe_w_up': nrm(ks[35], (DEPTH, N_EXPERTS, D, EXPERT_HIDDEN), D ** -0.5),
        'moe_w_down': nrm(ks[36], (DEPTH, N_EXPERTS, EXPERT_HIDDEN, D), EXPERT_HIDDEN ** -0.5),
        'final_norm': 1.0 + nrm(ks[37], (D,), 0.1),
    }


def reference(x_prompt, x_sample, cache_k, cache_v, c, c_ctx, w_ada, b_ada, norm_mix, norm_ffn,
              attn_w_q, attn_w_kv, attn_q_norm, attn_k_norm, attn_w_o,
              hy_w_in, hy_b_in, hy_short_w, hy_short_b, hy_filt_w1, hy_filt_b1, hy_filt_w2, hy_filt_b2,
              hy_filt_w3, hy_filt_b3, hy_filt_freq, hy_log_decay, hy_skip, hy_w_out, hy_b_out,
              router_group_w, router_group_b, router_expert_w, router_expert_b,
              moe_w_gate, moe_w_up, moe_w_down, final_norm):
    p = dict(w_ada=w_ada, b_ada=b_ada, norm_mix=norm_mix, norm_ffn=norm_ffn,
             attn_w_q=attn_w_q, attn_w_kv=attn_w_kv, attn_q_norm=attn_q_norm, attn_k_norm=attn_k_norm,
             attn_w_o=attn_w_o, hy_w_in=hy_w_in, hy_b_in=hy_b_in, hy_short_w=hy_short_w, hy_short_b=hy_short_b,
             hy_filt_w1=hy_filt_w1, hy_filt_b1=hy_filt_b1, hy_filt_w2=hy_filt_w2, hy_filt_b2=hy_filt_b2,
             hy_filt_w3=hy_filt_w3, hy_filt_b3=hy_filt_b3, hy_filt_freq=hy_filt_freq, hy_log_decay=hy_log_decay,
             hy_skip=hy_skip, hy_w_out=hy_w_out, hy_b_out=hy_b_out,
             router_group_w=router_group_w, router_group_b=router_group_b,
             router_expert_w=router_expert_w, router_expert_b=router_expert_b,
             moe_w_gate=moe_w_gate, moe_w_up=moe_w_up, moe_w_down=moe_w_down, final_norm=final_norm)
    cond_ctx = jnp.broadcast_to(c_ctx[None, :], (x_prompt.shape[0], D_MODEL))
    y_prompt, new_k, new_v = _trunk(x_prompt, cond_ctx, p, None, None)
    y_sample, _, _ = _trunk(x_sample, c, p, cache_k, cache_v)
    return (y_prompt, y_sample, new_k, new_v)
```

```python
import functools
import math

import numpy as np
import jax
import jax.numpy as jnp
from jax import lax
from jax.experimental import pallas as pl
from jax.experimental.pallas import tpu as pltpu

F32 = jnp.float32
BF16 = jnp.bfloat16

D = 1024
N_BATCH_CTX = 32
L_CTX = 256
N_BATCH_LAT = 2
L_LAT = 2048
PAST = 256
N_CTX = N_BATCH_CTX * L_CTX
N_LAT = N_BATCH_LAT * L_LAT
N_TOK = N_CTX + N_LAT
GRID_W = 64
N_HEADS = 16
N_KV = 4
DH = 64
Q_PER_KV = N_HEADS // N_KV
KV_W = N_KV * DH
ROPE_THETA = 10000.0
FILTER_FEAT = 17
FEAT_PAD = 32
FILTER_HIDDEN = 64
N_GROUPS = 4
E_PER_G = 8
N_EXP = N_GROUPS * E_PER_G
EXP_H = D // 4
EPS = 1e-6
ROUTE_W = 128
VMEM_LIMIT = 56 * 1024 * 1024
HI = lax.Precision.HIGHEST


def _cparams(sem):
    return pltpu.CompilerParams(dimension_semantics=sem, vmem_limit_bytes=VMEM_LIMIT)


def _cond_row(i, tm):
    n_ctx_tiles = N_CTX // tm
    return jnp.where(i < n_ctx_tiles, 0, 1 + (i - n_ctx_tiles) // (L_LAT // tm))


def _sigmoid(x):
    return 1.0 / (1.0 + jnp.exp(-x))


@functools.lru_cache(maxsize=None)
def _dft_table(L):
    k = np.arange(L, dtype=np.int64)
    ang = (np.outer(k, k) % (2 * L)).astype(np.float64) * (math.pi / L)
    return np.concatenate([np.cos(ang), np.sin(ang)], axis=0).astype(np.float32)


@functools.lru_cache(maxsize=None)
def _filter_feats(L):
    t = np.linspace(0.0, 1.0, L, dtype=np.float64)[:, None]
    bands = np.linspace(1e-4, 7.0, 8, dtype=np.float64)[None, :]
    w = (2.0 * math.pi) * np.arange(L, dtype=np.float64)[:, None] / L
    feats = np.concatenate([t, np.cos(bands * w), -np.sin(bands * w)], axis=-1)
    out = np.zeros((L, FEAT_PAD), np.float32)
    out[:, :FILTER_FEAT] = feats
    return out


@functools.lru_cache(maxsize=None)
def _rope_tables(tm):
    pos = np.arange(L_LAT)
    row = (pos // GRID_W).astype(np.float64)
    col = (pos % GRID_W).astype(np.float64)
    axis_dim = DH // 2
    inv_freq = ROPE_THETA ** (-np.arange(0, axis_dim, 2, dtype=np.float64) / axis_dim)
    lane = np.arange(KV_W)
    d = lane % DH
    is_col = (d // axis_dim) == 1
    fi = d % (axis_dim // 2)
    first_half = (d % axis_dim) < (axis_dim // 2)
    p = np.where(is_col[None, :], col[:, None], row[:, None])
    ang = p * inv_freq[fi][None, :]
    cos = np.cos(ang)
    sin = np.sin(ang) * np.where(first_half, -1.0, 1.0)[None, :]
    cos = np.concatenate([np.ones((tm, KV_W)), cos], axis=0).astype(np.float32)
    sin = np.concatenate([np.zeros((tm, KV_W)), sin], axis=0).astype(np.float32)
    return cos, sin


@functools.lru_cache(maxsize=None)
def _head_sum_matrix():
    lane = np.arange(KV_W)
    return (lane[:, None] // DH == lane[None, :] // DH).astype(np.float32)


def _ada_body(c_ref, w_ref, b_ref, o_ref):
    c = c_ref[...]
    s = c * _sigmoid(c)
    o_ref[0] = jnp.dot(s.astype(BF16), w_ref[0].astype(BF16), preferred_element_type=F32) + b_ref[0]


def _ada(cond8, w_ada, b_ada):
    depth = w_ada.shape[0]
    tn = 1536
    return pl.pallas_call(
        _ada_body,
        grid=(depth, 6 * D // tn),
        in_specs=[
            pl.BlockSpec((8, D), lambda l, j: (0, 0)),
            pl.BlockSpec((1, D, tn), lambda l, j: (l, 0, j)),
            pl.BlockSpec((1, 1, tn), lambda l, j: (l, 0, j)),
        ],
        out_specs=pl.BlockSpec((1, 8, tn), lambda l, j: (l, 0, j)),
        out_shape=jax.ShapeDtypeStruct((depth, 8, 6 * D), F32),
        compiler_params=_cparams(("arbitrary", "arbitrary")),
        name="ada",
    )(cond8, w_ada, b_ada.reshape(depth, 1, 6 * D))


def _norm_mod(x, gain_ref, ada_ref, part):
    ms = jnp.mean(x * x, axis=-1, keepdims=True)
    y = x * lax.rsqrt(ms + EPS) * gain_ref[...]
    return y * (1.0 + ada_ref[0, part + 1:part + 2, :]) + ada_ref[0, part:part + 1, :]


def _proj_body(x_ref, ada_ref, gain_ref, w_ref, b_ref, o_ref, *, part):
    h = _norm_mod(x_ref[...], gain_ref, ada_ref, part)
    acc = jnp.dot(h.astype(BF16), w_ref[...], preferred_element_type=F32) + b_ref[...]
    o_ref[...] = acc.astype(o_ref.dtype)


def _norm_proj(x, ada, gain, w, b, *, part, tm=512):
    n_out = w.shape[1]
    return pl.pallas_call(
        functools.partial(_proj_body, part=part),
        grid=(N_TOK // tm,),
        in_specs=[
            pl.BlockSpec((tm, D), lambda i: (i, 0)),
            pl.BlockSpec((1, 6, D), lambda i: (_cond_row(i, tm), 0, 0)),
            pl.BlockSpec((1, D), lambda i: (0, 0)),
            pl.BlockSpec((D, n_out), lambda i: (0, 0)),
            pl.BlockSpec((1, n_out), lambda i: (0, 0)),
        ],
        out_specs=pl.BlockSpec((tm, n_out), lambda i: (i, 0)),
        out_shape=jax.ShapeDtypeStruct((N_TOK, n_out), BF16),
        compiler_params=_cparams(("arbitrary",)),
        name="norm_proj",
    )(x, ada, gain.reshape(1, D), w, b.reshape(1, n_out))


def _head_rms_rope(x, gain, hs, cos, sin, lane):
    ss = jnp.dot((x * x).astype(BF16), hs, preferred_element_type=F32)
    xn = x * lax.rsqrt(ss * (1.0 / DH) + EPS) * gain
    fwd = pltpu.roll(xn, KV_W - DH // 4, 1)
    bwd = pltpu.roll(xn, DH // 4, 1)
    partner = jnp.where((lane & (DH // 4)) == 0, fwd, bwd)
    return xn * cos + partner * sin


def _qkv_body(x_ref, ada_ref, gain_ref, w_ref, qg_ref, kg_ref, hs_ref, cos_ref, sin_ref,
              q_ref, k_ref, v_ref):
    h = _norm_mod(x_ref[...], gain_ref, ada_ref, 0)
    acc = jnp.dot(h.astype(BF16), w_ref[...], preferred_element_type=F32)
    hs = hs_ref[...]
    cos = cos_ref[...]
    sin = sin_ref[...]
    lane = lax.broadcasted_iota(jnp.int32, (1, KV_W), 1)
    for c in range(N_KV):
        qc = _head_rms_rope(acc[:, c * KV_W:(c + 1) * KV_W], qg_ref[...], hs, cos, sin, lane)
        q_ref[:, c * KV_W:(c + 1) * KV_W] = (qc * (DH ** -0.5)).astype(BF16)
    k_ref[...] = _head_rms_rope(acc[:, D:D + KV_W], kg_ref[...], hs, cos, sin, lane)
    v_ref[...] = acc[:, D + KV_W:D + 2 * KV_W]


def _qkv(x, ada, gain, w_qkv, q_gain, k_gain, *, tm=256):
    cos, sin = _rope_tables(tm)
    n_ctx_tiles = N_CTX // tm
    lat_tiles = L_LAT // tm

    def rope_idx(i):
        return (jnp.where(i < n_ctx_tiles, 0, 1 + (i - n_ctx_tiles) % lat_tiles), 0)

    const = lambda i: (0, 0)
    return pl.pallas_call(
        _qkv_body,
        grid=(N_TOK // tm,),
        in_specs=[
            pl.BlockSpec((tm, D), lambda i: (i, 0)),
            pl.BlockSpec((1, 6, D), lambda i: (_cond_row(i, tm), 0, 0)),
            pl.BlockSpec((1, D), const),
            pl.BlockSpec((D, D + 2 * KV_W), const),
            pl.BlockSpec((1, KV_W), const),
            pl.BlockSpec((1, KV_W), const),
            pl.BlockSpec((KV_W, KV_W), const),
            pl.BlockSpec((tm, KV_W), rope_idx),
            pl.BlockSpec((tm, KV_W), rope_idx),
        ],
        out_specs=[
            pl.BlockSpec((tm, D), lambda i: (i, 0)),
            pl.BlockSpec((tm, KV_W), lambda i: (i, 0)),
            pl.BlockSpec((tm, KV_W), lambda i: (i, 0)),
        ],
        out_shape=[
            jax.ShapeDtypeStruct((N_TOK, D), BF16),
            jax.ShapeDtypeStruct((N_TOK, KV_W), F32),
            jax.ShapeDtypeStruct((N_TOK, KV_W), F32),
        ],
        compiler_params=_cparams(("arbitrary",)),
        name="qkv",
    )(x, ada, gain.reshape(1, D), w_qkv,
      jnp.tile(q_gain, Q_PER_KV).reshape(1, KV_W), jnp.tile(k_gain, N_KV).reshape(1, KV_W),
      jnp.asarray(_head_sum_matrix()).astype(BF16), jnp.asarray(cos), jnp.asarray(sin))


def _attn_body(q_ref, k_ref, v_ref, o_ref):
    lane = lax.broadcasted_iota(jnp.int32, (1, KV_W), 1)
    q = q_ref[...]
    k4 = k_ref[0, 0]
    v4 = v_ref[0, 0]

    def head(g, out):
        mine = (lane >> 6) == g
        s = lax.dot_general(jnp.where(mine, q, jnp.zeros_like(q)), k4,
                            (((1,), (1,)), ((), ())), preferred_element_type=F32)
        p = jnp.exp(s - jnp.max(s, axis=-1, keepdims=True))
        denom = jnp.sum(p, axis=-1, keepdims=True)
        og = jnp.dot(p.astype(BF16), v4, preferred_element_type=F32)
        return out + jnp.where(mine, og / denom, 0.0)

    out = lax.fori_loop(0, Q_PER_KV, head, jnp.zeros(q.shape, F32))
    o_ref[...] = out.astype(BF16)


def _attention(q, k4, v4, *, row0, n_batch, seq, tq):
    lk = k4.shape[2]
    per_b = seq // tq
    base = row0 // tq
    return pl.pallas_call(
        _attn_body,
        grid=(n_batch, N_KV, per_b),
        in_specs=[
            pl.BlockSpec((tq, KV_W), lambda b, h, i: (base + b * per_b + i, h)),
            pl.BlockSpec((1, 1, lk, KV_W), lambda b, h, i: (b, h, 0, 0)),
            pl.BlockSpec((1, 1, lk, KV_W), lambda b, h, i: (b, h, 0, 0)),
        ],
        out_specs=pl.BlockSpec((tq, KV_W), lambda b, h, i: (b * per_b + i, h)),
        out_shape=jax.ShapeDtypeStruct((n_batch * seq, D), BF16),
        compiler_params=_cparams(("arbitrary", "arbitrary", "arbitrary")),
        name="attention",
    )(q, k4, v4)


def _lane_tile_heads(x):
    b, l, _ = x.shape
    xh = x.reshape(b, l, N_KV, DH).transpose(0, 2, 1, 3).astype(BF16)
    return jnp.tile(xh, (1, 1, 1, KV_W // DH))


def _attention_ctx(q, k, v):
    k_ctx = k[:N_CTX].reshape(N_BATCH_CTX, L_CTX, KV_W)
    v_ctx = v[:N_CTX].reshape(N_BATCH_CTX, L_CTX, KV_W)
    return _attention(q, _lane_tile_heads(k_ctx), _lane_tile_heads(v_ctx),
                      row0=0, n_batch=N_BATCH_CTX, seq=L_CTX, tq=L_CTX)


def _attention_lat(q, k, v, past_k, past_v):
    past_k = past_k.transpose(0, 2, 1, 3).reshape(N_BATCH_LAT, PAST, KV_W)
    past_v = past_v.transpose(0, 2, 1, 3).reshape(N_BATCH_LAT, PAST, KV_W)
    k_lat = jnp.concatenate([k[N_CTX:].reshape(N_BATCH_LAT, L_LAT, KV_W), past_k], axis=1)
    v_lat = jnp.concatenate([v[N_CTX:].reshape(N_BATCH_LAT, L_LAT, KV_W), past_v], axis=1)
    return _attention(q, _lane_tile_heads(k_lat), _lane_tile_heads(v_lat),
                      row0=N_CTX, n_batch=N_BATCH_LAT, seq=L_LAT, tq=512)


def _resid_body(a_ref, w_ref, b_ref, x_ref, ada_ref, o_ref, *, gpart):
    acc = jnp.dot(a_ref[...], w_ref[...], preferred_element_type=F32) + b_ref[...]
    o_ref[...] = x_ref[...] + ada_ref[0, gpart:gpart + 1, :] * acc


def _resid_proj(a, w, b, x, ada, *, gpart, tm=512):
    k = a.shape[1]
    return pl.pallas_call(
        functools.partial(_resid_body, gpart=gpart),
        grid=(N_TOK // tm,),
        in_specs=[
            pl.BlockSpec((tm, k), lambda i: (i, 0)),
            pl.BlockSpec((k, D), lambda i: (0, 0)),
            pl.BlockSpec((1, D), lambda i: (0, 0)),
            pl.BlockSpec((tm, D), lambda i: (i, 0)),
            pl.BlockSpec((1, 6, D), lambda i: (_cond_row(i, tm), 0, 0)),
        ],
        out_specs=pl.BlockSpec((tm, D), lambda i: (i, 0)),
        out_shape=jax.ShapeDtypeStruct((N_TOK, D), F32),
        compiler_params=_cparams(("arbitrary",)),
        name="resid_proj",
    )(a, w, b.reshape(1, D), x, ada)


def _alt_sign(L):
    row = lax.broadcasted_iota(jnp.int32, (L, 1), 0)
    return row, jnp.where((row & 1) == 0, 1.0, -1.0)


def _filter_body(feat_ref, w1_ref, b1_ref, w2_ref, b2_ref, fq_ref, w3_ref, b3_ref, ld_ref, cs_ref,
                 fr_ref, fi_ref, fn_ref, *, L):
    feats = feat_ref[...]
    fq = fq_ref[...]
    h1 = jnp.sin(fq * (jnp.dot(feats, w1_ref[...], precision=HI, preferred_element_type=F32) + b1_ref[...]))
    h2 = jnp.sin(fq * (jnp.dot(h1, w2_ref[...], precision=HI, preferred_element_type=F32) + b2_ref[...]))
    t = feats[:, 0:1]
    _, alt = _alt_sign(L)
    filt = []
    for j in range(4):
        raw = jnp.dot(h2, w3_ref[j], precision=HI, preferred_element_type=F32) + b3_ref[j:j + 1, :]
        filt.append(raw * jnp.exp(-t * jnp.exp(ld_ref[j:j + 1, :])))
    for o in range(2):
        hf, hb = filt[2 * o], filt[2 * o + 1]
        l1 = jnp.sum(jnp.abs(hf), axis=0, keepdims=True) + jnp.sum(jnp.abs(hb), axis=0, keepdims=True)
        inv = 1.0 / (l1 + EPS)
        sym = (hf + hb) * inv
        asym = (hb - hf) * inv
        fr_ref[o] = jnp.dot(cs_ref[0:L, :], sym.astype(BF16), preferred_element_type=F32)
        fi_ref[o] = jnp.dot(cs_ref[L:2 * L, :], asym.astype(BF16), preferred_element_type=F32)
        fn_ref[o] = jnp.sum(sym * alt, axis=0, keepdims=True)


def _filter_spectra(L, cs, w1, b1, w2, b2, w3, b3, freq, log_decay, *, tc=256):
    w1p = jnp.zeros((FEAT_PAD, FILTER_HIDDEN), F32).at[:FILTER_FEAT].set(w1)
    w3r = w3.reshape(FILTER_HIDDEN, 4, D).transpose(1, 0, 2)
    const = lambda j: (0, 0)
    return pl.pallas_call(
        functools.partial(_filter_body, L=L),
        grid=(D // tc,),
        in_specs=[
            pl.BlockSpec((L, FEAT_PAD), const),
            pl.BlockSpec((FEAT_PAD, FILTER_HIDDEN), const),
            pl.BlockSpec((1, FILTER_HIDDEN), const),
            pl.BlockSpec((FILTER_HIDDEN, FILTER_HIDDEN), const),
            pl.BlockSpec((1, FILTER_HIDDEN), const),
            pl.BlockSpec((1, FILTER_HIDDEN), const),
            pl.BlockSpec((4, FILTER_HIDDEN, tc), lambda j: (0, 0, j)),
            pl.BlockSpec((4, tc), lambda j: (0, j)),
            pl.BlockSpec((4, tc), lambda j: (0, j)),
            pl.BlockSpec((2 * L, L), const, pipeline_mode=pl.Buffered(1)),
        ],
        out_specs=[
            pl.BlockSpec((2, L, tc), lambda j: (0, 0, j)),
            pl.BlockSpec((2, L, tc), lambda j: (0, 0, j)),
            pl.BlockSpec((2, 1, tc), lambda j: (0, 0, j)),
        ],
        out_shape=[
            jax.ShapeDtypeStruct((2, L, D), F32),
            jax.ShapeDtypeStruct((2, L, D), F32),
            jax.ShapeDtypeStruct((2, 1, D), F32),
        ],
        compiler_params=_cparams(("arbitrary",)),
        name="hyena_filter",
    )(jnp.asarray(_filter_feats(L)), w1p, b1.reshape(1, -1), w2, b2.reshape(1, -1), freq.reshape(1, -1),
      w3r, b3.reshape(4, D), log_decay.reshape(4, D), cs)


FREQ_CHUNK = 1024


def _hyconv_body(x1_ref, x2_ref, v_ref, sw_ref, sb_ref, skip_ref, fr_ref, fi_ref, fn_ref, cs_ref, o_ref,
                 z_ref, g_ref, zb_ref, yr_ref, yi_ref, *, L):
    row, alt = _alt_sign(L)
    kc = min(FREQ_CHUNK, L)
    chunk_row = lax.broadcasted_iota(jnp.int32, (kc, 1), 0)

    def short_conv(u_ref, p):
        u = u_ref[...].astype(F32)
        prev = jnp.where(row == 0, 0.0, pltpu.roll(u, 1, 0))
        nxt = jnp.where(row == L - 1, 0.0, pltpu.roll(u, L - 1, 0))
        return (prev * sw_ref[0, p:p + 1, :] + u * sw_ref[1, p:p + 1, :] + nxt * sw_ref[2, p:p + 1, :]
                + sb_ref[p:p + 1, :])

    z_ref[...] = short_conv(v_ref, 2)
    g_ref[0] = short_conv(x1_ref, 0)
    g_ref[1] = short_conv(x2_ref, 1)
    for o in range(2):
        z = z_ref[...]
        zb_ref[...] = z.astype(BF16)
        nyq = jnp.sum(z * alt, axis=0, keepdims=True) * fn_ref[o] * (0.5 / L)

        def to_freq(c, carry):
            lo = pl.multiple_of(c * kc, kc)
            zr = jnp.dot(cs_ref[pl.ds(lo, kc), :], zb_ref[...], preferred_element_type=F32)
            zs = jnp.dot(cs_ref[pl.ds(L + lo, kc), :], zb_ref[...], preferred_element_type=F32)
            fr = fr_ref[o, pl.ds(lo, kc), :]
            fi = fi_ref[o, pl.ds(lo, kc), :]
            yr = (zr * fr + zs * fi) * (1.0 / L)
            yr = jnp.where(chunk_row + lo == 0, 0.5 * yr, yr)
            yr_ref[pl.ds(lo, kc), :] = yr.astype(BF16)
            yi_ref[pl.ds(lo, kc), :] = ((zr * fi - zs * fr) * (1.0 / L)).astype(BF16)
            return carry

        lax.fori_loop(0, L // kc, to_freq, 0)

        def to_time(c, carry):
            lo = pl.multiple_of(c * kc, kc)
            y = (jnp.dot(cs_ref[pl.ds(lo, kc), :], yr_ref[...], preferred_element_type=F32)
                 - jnp.dot(cs_ref[pl.ds(L + lo, kc), :], yi_ref[...], preferred_element_type=F32))
            y = y + jnp.where(((chunk_row + lo) & 1) == 0, nyq, -nyq)
            z_ref[pl.ds(lo, kc), :] = g_ref[o, pl.ds(lo, kc), :] * (y + skip_ref[o:o + 1, :] * z_ref[pl.ds(lo, kc), :])
            return carry

        lax.fori_loop(0, L // kc, to_time, 0)
    o_ref[...] = z_ref[...].astype(BF16)


def _hyena_conv(up, sw, sb, skip, fr, fi, fn, cs, *, row0, n_batch, L, tc):
    n_ct = D // tc
    base = row0 // L
    u_spec = lambda p: pl.BlockSpec((L, tc), lambda j, b: (base + b, p * n_ct + j))
    once = pl.Buffered(1)
    return pl.pallas_call(
        functools.partial(_hyconv_body, L=L),
        grid=(n_ct, n_batch),
        in_specs=[
            u_spec(0), u_spec(1), u_spec(2),
            pl.BlockSpec((3, 3, tc), lambda j, b: (0, 0, j)),
            pl.BlockSpec((3, tc), lambda j, b: (0, j)),
            pl.BlockSpec((2, tc), lambda j, b: (0, j)),
            pl.BlockSpec((2, L, tc), lambda j, b: (0, 0, j), pipeline_mode=once),
            pl.BlockSpec((2, L, tc), lambda j, b: (0, 0, j), pipeline_mode=once),
            pl.BlockSpec((2, 1, tc), lambda j, b: (0, 0, j)),
            pl.BlockSpec((2 * L, L), lambda j, b: (0, 0), pipeline_mode=once),
        ],
        out_specs=pl.BlockSpec((L, tc), lambda j, b: (b, j)),
        out_shape=jax.ShapeDtypeStruct((n_batch * L, D), BF16),
        scratch_shapes=[
            pltpu.VMEM((L, tc), F32),
            pltpu.VMEM((2, L, tc), F32),
            pltpu.VMEM((L, tc), BF16),
            pltpu.VMEM((L, tc), BF16),
            pltpu.VMEM((L, tc), BF16),
        ],
        compiler_params=_cparams(("arbitrary", "arbitrary")),
        name="hyena_conv",
    )(up, up, up, sw.reshape(3, 3, D), sb.reshape(3, D), skip, fr, fi, fn, cs)


def _route_body(x_ref, ada_ref, gain_ref, wr_ref, br_ref, h_ref, comb_ref):
    h = _norm_mod(x_ref[...], gain_ref, ada_ref, 3)
    h_ref[...] = h.astype(BF16)
    logits = jnp.dot(h, wr_ref[...], precision=HI, preferred_element_type=F32) + br_ref[...]
    lane = lax.broadcasted_iota(jnp.int32, (1, ROUTE_W), 1)
    lane_f = lane.astype(F32)
    group_of_lane = (lane >> 3).astype(F32)
    neg = -jnp.inf
    big = float(ROUTE_W)
    is_g = (lane >= N_EXP) & (lane < N_EXP + N_GROUPS)
    gl = jnp.where(is_g, logits, neg)
    gmax = jnp.max(gl, axis=-1, keepdims=True)
    gidx = jnp.min(jnp.where(gl == gmax, lane_f - N_EXP, big), axis=-1, keepdims=True)
    g_top = 1.0 / jnp.sum(jnp.where(is_g, jnp.exp(gl - gmax), 0.0), axis=-1, keepdims=True)
    in_group = (lane < N_EXP) & (group_of_lane == gidx)
    el = jnp.where(in_group, logits, neg)
    v1 = jnp.max(el, axis=-1, keepdims=True)
    i1 = jnp.min(jnp.where(el == v1, lane_f, big), axis=-1, keepdims=True)
    el2 = jnp.where(lane_f == i1, neg, el)
    v2 = jnp.max(el2, axis=-1, keepdims=True)
    i2 = jnp.min(jnp.where(el2 == v2, lane_f, big), axis=-1, keepdims=True)
    r = jnp.exp(v2 - v1)
    w1 = g_top / (1.0 + r)
    w2 = g_top * r / (1.0 + r)
    comb_ref[...] = jnp.where(lane_f == i1, w1, 0.0) + jnp.where(lane_f == i2, w2, 0.0)


def _route(x, ada, gain, wg, bg, we, be, *, tm=512):
    wr = jnp.zeros((D, ROUTE_W), F32).at[:, :N_EXP].set(we.reshape(D, N_EXP)).at[:, N_EXP:N_EXP + N_GROUPS].set(wg)
    br = jnp.zeros((1, ROUTE_W), F32).at[0, :N_EXP].set(be.reshape(N_EXP)).at[0, N_EXP:N_EXP + N_GROUPS].set(bg)
    return pl.pallas_call(
        _route_body,
        grid=(N_TOK // tm,),
        in_specs=[
            pl.BlockSpec((tm, D), lambda i: (i, 0)),
            pl.BlockSpec((1, 6, D), lambda i: (_cond_row(i, tm), 0, 0)),
            pl.BlockSpec((1, D), lambda i: (0, 0)),
            pl.BlockSpec((D, ROUTE_W), lambda i: (0, 0)),
            pl.BlockSpec((1, ROUTE_W), lambda i: (0, 0)),
        ],
        out_specs=[
            pl.BlockSpec((tm, D), lambda i: (i, 0)),
            pl.BlockSpec((tm, ROUTE_W), lambda i: (i, 0)),
        ],
        out_shape=[
            jax.ShapeDtypeStruct((N_TOK, D), BF16),
            jax.ShapeDtypeStruct((N_TOK, ROUTE_W), F32),
        ],
        compiler_params=_cparams(("arbitrary",)),
        name="route",
    )(x, ada, gain.reshape(1, D), wr, br)


def _moe_body(h_ref, comb_ref, wg_ref, wu_ref, wd_ref, x_ref, ada_ref, o_ref, acc_ref):
    e = pl.program_id(1)

    @pl.when(e == 0)
    def _():
        acc_ref[...] = jnp.zeros_like(acc_ref)

    h = h_ref[...]
    a = jnp.dot(h, wg_ref[0].astype(BF16), preferred_element_type=F32)
    b = jnp.dot(h, wu_ref[0].astype(BF16), preferred_element_type=F32)
    lane = lax.broadcasted_iota(jnp.int32, (1, ROUTE_W), 1)
    cw = jnp.sum(jnp.where(lane == e, comb_ref[...], 0.0), axis=-1, keepdims=True)
    act = a * _sigmoid(a) * b * cw
    acc_ref[...] += jnp.dot(act.astype(BF16), wd_ref[0].astype(BF16), preferred_element_type=F32)

    @pl.when(e == N_EXP - 1)
    def _():
        o_ref[...] = x_ref[...] + ada_ref[0, 5:6, :] * acc_ref[...]


def _moe(h, comb, w_gate, w_up, w_down, x, ada, *, tm=1024):
    return pl.pallas_call(
        _moe_body,
        grid=(N_TOK // tm, N_EXP),
        in_specs=[
            pl.BlockSpec((tm, D), lambda i, e: (i, 0)),
            pl.BlockSpec((tm, ROUTE_W), lambda i, e: (i, 0)),
            pl.BlockSpec((1, D, EXP_H), lambda i, e: (e, 0, 0)),
            pl.BlockSpec((1, D, EXP_H), lambda i, e: (e, 0, 0)),
            pl.BlockSpec((1, EXP_H, D), lambda i, e: (e, 0, 0)),
            pl.BlockSpec((tm, D), lambda i, e: (i, 0)),
            pl.BlockSpec((1, 6, D), lambda i, e: (_cond_row(i, tm), 0, 0)),
        ],
        out_specs=pl.BlockSpec((tm, D), lambda i, e: (i, 0)),
        out_shape=jax.ShapeDtypeStruct((N_TOK, D), F32),
        scratch_shapes=[pltpu.VMEM((tm, D), F32)],
        compiler_params=_cparams(("arbitrary", "arbitrary")),
        name="moe",
    )(h, comb, w_gate, w_up, w_down, x, ada)


def _final_body(x_ref, gain_ref, o_ref):
    x = x_ref[...]
    ms = jnp.mean(x * x, axis=-1, keepdims=True)
    o_ref[...] = x * lax.rsqrt(ms + EPS) * gain_ref[...]


def _final_norm(x, gain, *, tm=1024):
    return pl.pallas_call(
        _final_body,
        grid=(N_TOK // tm,),
        in_specs=[pl.BlockSpec((tm, D), lambda i: (i, 0)), pl.BlockSpec((1, D), lambda i: (0, 0))],
        out_specs=pl.BlockSpec((tm, D), lambda i: (i, 0)),
        out_shape=jax.ShapeDtypeStruct((N_TOK, D), F32),
        compiler_params=_cparams(("arbitrary",)),
        name="final_norm",
    )(x, gain.reshape(1, D))


def kernel(x_prompt, x_sample, cache_k, cache_v, c, c_ctx, w_ada, b_ada, norm_mix, norm_ffn, attn_w_q, attn_w_kv, attn_q_norm, attn_k_norm, attn_w_o, hy_w_in, hy_b_in, hy_short_w, hy_short_b, hy_filt_w1, hy_filt_b1, hy_filt_w2, hy_filt_b2, hy_filt_w3, hy_filt_b3, hy_filt_freq, hy_log_decay, hy_skip, hy_w_out, hy_b_out, router_group_w, router_group_b, router_expert_w, router_expert_b, moe_w_gate, moe_w_up, moe_w_down, final_norm):
    depth = w_ada.shape[0]
    x = jnp.concatenate([x_prompt.reshape(N_CTX, D), x_sample.reshape(N_LAT, D)], axis=0)
    cond8 = jnp.zeros((8, D), F32).at[0].set(c_ctx).at[1:1 + N_BATCH_LAT].set(c)
    ada_all = _ada(cond8, w_ada, b_ada).reshape(depth, 8, 6, D)
    new_k = []
    new_v = []
    for l in range(depth):
        ada = ada_all[l]
        if l % 2 == 0:
            a = l // 2
            w_qkv = jnp.concatenate([attn_w_q[a], attn_w_kv[a]], axis=1).astype(BF16)
            q, k, v = _qkv(x, ada, norm_mix[l], w_qkv, attn_q_norm[a], attn_k_norm[a])
            kv_heads = lambda t: t[:N_CTX].reshape(N_BATCH_CTX, L_CTX, N_KV, DH).transpose(0, 2, 1, 3)
            new_k.append(kv_heads(k))
            new_v.append(kv_heads(v))
            o_ctx = _attention_ctx(q, k, v)
            o_lat = _attention_lat(q, k, v, cache_k[:, a], cache_v[:, a])
            mixed = jnp.concatenate([o_ctx, o_lat], axis=0)
            x = _resid_proj(mixed, attn_w_o[a].astype(BF16), jnp.zeros((D,), F32), x, ada, gpart=2)
        else:
            j = l // 2
            up = _norm_proj(x, ada, norm_mix[l], hy_w_in[j].astype(BF16), hy_b_in[j], part=0)
            outs = []
            for row0, n_batch, L, tc in ((0, N_BATCH_CTX, L_CTX, D), (N_CTX, N_BATCH_LAT, L_LAT, 256)):
                cs = jnp.asarray(_dft_table(L)).astype(BF16)
                fr, fi, fn = _filter_spectra(L, cs, hy_filt_w1[j], hy_filt_b1[j], hy_filt_w2[j], hy_filt_b2[j],
                                             hy_filt_w3[j], hy_filt_b3[j], hy_filt_freq[j], hy_log_decay[j])
                outs.append(_hyena_conv(up, hy_short_w[j], hy_short_b[j], hy_skip[j], fr, fi, fn, cs,
                                        row0=row0, n_batch=n_batch, L=L, tc=tc))
            mixed = jnp.concatenate(outs, axis=0)
            x = _resid_proj(mixed, hy_w_out[j].astype(BF16), hy_b_out[j], x, ada, gpart=2)
        h, comb = _route(x, ada, norm_ffn[l], router_group_w[l], router_group_b[l],
                         router_expert_w[l], router_expert_b[l])
        x = _moe(h, comb, moe_w_gate[l], moe_w_up[l], moe_w_down[l], x, ada)
    y = _final_norm(x, final_norm)
    y_prompt = y[:N_CTX].reshape(N_BATCH_CTX, L_CTX, D)
    y_sample = y[N_CTX:].reshape(N_BATCH_LAT, L_LAT, D)
    return (y_prompt, y_sample, jnp.stack(new_k, axis=1), jnp.stack(new_v, axis=1))
```

```python
import functools
import math

import numpy as np
import jax
import jax.numpy as jnp
from jax import lax
from jax.experimental import pallas as pl
from jax.experimental.pallas import tpu as pltpu

F32 = jnp.float32
BF16 = jnp.bfloat16

D = 1024
N_BATCH_CTX = 32
L_CTX = 256
N_BATCH_LAT = 2
L_LAT = 2048
PAST = 256
N_CTX = N_BATCH_CTX * L_CTX
N_LAT = N_BATCH_LAT * L_LAT
N_TOK = N_CTX + N_LAT
GRID_W = 64
N_HEADS = 16
N_KV = 4
DH = 64
Q_PER_KV = N_HEADS // N_KV
KV_W = N_KV * DH
ROPE_THETA = 10000.0
FILTER_FEAT = 17
FEAT_PAD = 32
FILTER_HIDDEN = 64
N_GROUPS = 4
E_PER_G = 8
N_EXP = N_GROUPS * E_PER_G
EXP_H = D // 4
EPS = 1e-6
ROUTE_W = 128
VMEM_LIMIT = 56 * 1024 * 1024
HI = lax.Precision.HIGHEST


def _cparams(sem):
    return pltpu.CompilerParams(dimension_semantics=sem, vmem_limit_bytes=VMEM_LIMIT)


def _cond_row(i, tm):
    n_ctx_tiles = N_CTX // tm
    return jnp.where(i < n_ctx_tiles, 0, 1 + (i - n_ctx_tiles) // (L_LAT // tm))


def _split_specs(tm, width):
    n_ctx_tiles = N_CTX // tm
    return [pl.BlockSpec((tm, width), lambda i: (jnp.minimum(i, n_ctx_tiles - 1), 0)),
            pl.BlockSpec((tm, width), lambda i: (jnp.maximum(i - n_ctx_tiles, 0), 0))]


def _pick_split(i, tm, ctx_ref, lat_ref):
    return jnp.where(i < N_CTX // tm, ctx_ref[...], lat_ref[...])


def _sigmoid(x):
    return 1.0 / (1.0 + jnp.exp(-x))


@functools.lru_cache(maxsize=None)
def _dft_table(L):
    k = np.arange(L, dtype=np.int64)
    ang = (np.outer(k, k) % (2 * L)).astype(np.float64) * (math.pi / L)
    return np.concatenate([np.cos(ang), np.sin(ang)], axis=0).astype(np.float32)


@functools.lru_cache(maxsize=None)
def _filter_feats(L):
    t = np.linspace(0.0, 1.0, L, dtype=np.float64)[:, None]
    bands = np.linspace(1e-4, 7.0, 8, dtype=np.float64)[None, :]
    w = (2.0 * math.pi) * np.arange(L, dtype=np.float64)[:, None] / L
    feats = np.concatenate([t, np.cos(bands * w), -np.sin(bands * w)], axis=-1)
    out = np.zeros((L, FEAT_PAD), np.float32)
    out[:, :FILTER_FEAT] = feats
    return out


@functools.lru_cache(maxsize=None)
def _rope_tables(tm):
    pos = np.arange(L_LAT)
    row = (pos // GRID_W).astype(np.float64)
    col = (pos % GRID_W).astype(np.float64)
    axis_dim = DH // 2
    inv_freq = ROPE_THETA ** (-np.arange(0, axis_dim, 2, dtype=np.float64) / axis_dim)
    lane = np.arange(KV_W)
    d = lane % DH
    is_col = (d // axis_dim) == 1
    fi = d % (axis_dim // 2)
    first_half = (d % axis_dim) < (axis_dim // 2)
    p = np.where(is_col[None, :], col[:, None], row[:, None])
    ang = p * inv_freq[fi][None, :]
    cos = np.cos(ang)
    sin = np.sin(ang) * np.where(first_half, -1.0, 1.0)[None, :]
    cos = np.concatenate([np.ones((tm, KV_W)), cos], axis=0).astype(np.float32)
    sin = np.concatenate([np.zeros((tm, KV_W)), sin], axis=0).astype(np.float32)
    return cos, sin


@functools.lru_cache(maxsize=None)
def _head_sum_matrix():
    lane = np.arange(KV_W)
    return (lane[:, None] // DH == lane[None, :] // DH).astype(np.float32)


def _ada_body(c_ref, w_ref, b_ref, o_ref):
    c = c_ref[...]
    s = c * _sigmoid(c)
    o_ref[0] = jnp.dot(s.astype(BF16), w_ref[0].astype(BF16), preferred_element_type=F32) + b_ref[0]


def _ada(cond8, w_ada, b_ada):
    depth = w_ada.shape[0]
    tn = 1536
    return pl.pallas_call(
        _ada_body,
        grid=(depth, 6 * D // tn),
        in_specs=[
            pl.BlockSpec((8, D), lambda l, j: (0, 0)),
            pl.BlockSpec((1, D, tn), lambda l, j: (l, 0, j)),
            pl.BlockSpec((1, 1, tn), lambda l, j: (l, 0, j)),
        ],
        out_specs=pl.BlockSpec((1, 8, tn), lambda l, j: (l, 0, j)),
        out_shape=jax.ShapeDtypeStruct((depth, 8, 6 * D), F32),
        compiler_params=_cparams(("arbitrary", "arbitrary")),
        name="ada",
    )(cond8, w_ada, b_ada.reshape(depth, 1, 6 * D))


def _norm_mod(x, gain_ref, ada_ref, part):
    ms = jnp.mean(x * x, axis=-1, keepdims=True)
    y = x * lax.rsqrt(ms + EPS) * gain_ref[...]
    return y * (1.0 + ada_ref[0, part + 1:part + 2, :]) + ada_ref[0, part:part + 1, :]


def _proj_body(x_ref, ada_ref, gain_ref, w_ref, b_ref, o_ref, *, part):
    h = _norm_mod(x_ref[...], gain_ref, ada_ref, part)
    acc = jnp.dot(h.astype(BF16), w_ref[...], preferred_element_type=F32) + b_ref[...]
    o_ref[...] = acc.astype(o_ref.dtype)


def _norm_proj(x, ada, gain, w, b, *, part, tm=512):
    n_out = w.shape[1]
    return pl.pallas_call(
        functools.partial(_proj_body, part=part),
        grid=(N_TOK // tm,),
        in_specs=[
            pl.BlockSpec((tm, D), lambda i: (i, 0)),
            pl.BlockSpec((1, 6, D), lambda i: (_cond_row(i, tm), 0, 0)),
            pl.BlockSpec((1, D), lambda i: (0, 0)),
            pl.BlockSpec((D, n_out), lambda i: (0, 0)),
            pl.BlockSpec((1, n_out), lambda i: (0, 0)),
        ],
        out_specs=pl.BlockSpec((tm, n_out), lambda i: (i, 0)),
        out_shape=jax.ShapeDtypeStruct((N_TOK, n_out), BF16),
        compiler_params=_cparams(("arbitrary",)),
        name="norm_proj",
    )(x, ada, gain.reshape(1, D), w, b.reshape(1, n_out))


def _head_rms_rope(x, gain, hs, cos, sin, lane):
    ss = jnp.dot((x * x).astype(BF16), hs, preferred_element_type=F32)
    xn = x * lax.rsqrt(ss * (1.0 / DH) + EPS) * gain
    fwd = pltpu.roll(xn, KV_W - DH // 4, 1)
    bwd = pltpu.roll(xn, DH // 4, 1)
    partner = jnp.where((lane & (DH // 4)) == 0, fwd, bwd)
    return xn * cos + partner * sin


def _qkv_body(*refs, tm, split_x):
    n_x = 2 if split_x else 1
    x_refs = refs[:n_x]
    ada_ref, gain_ref, w_ref, qg_ref, kg_ref, hs_ref, cos_ref, sin_ref, q_ref, k_ref, v_ref = refs[n_x:]
    x = _pick_split(pl.program_id(0), tm, *x_refs) if split_x else x_refs[0][...]
    h = _norm_mod(x, gain_ref, ada_ref, 0)
    acc = jnp.dot(h.astype(BF16), w_ref[...], preferred_element_type=F32)
    hs = hs_ref[...]
    cos = cos_ref[...]
    sin = sin_ref[...]
    lane = lax.broadcasted_iota(jnp.int32, (1, KV_W), 1)
    for c in range(N_KV):
        qc = _head_rms_rope(acc[:, c * KV_W:(c + 1) * KV_W], qg_ref[...], hs, cos, sin, lane)
        q_ref[:, c * KV_W:(c + 1) * KV_W] = (qc * (DH ** -0.5)).astype(BF16)
    k_ref[...] = _head_rms_rope(acc[:, D:D + KV_W], kg_ref[...], hs, cos, sin, lane)
    v_ref[...] = acc[:, D + KV_W:D + 2 * KV_W]


def _qkv(x_parts, ada, gain, w_qkv, q_gain, k_gain, *, tm=256):
    cos, sin = _rope_tables(tm)
    split_x = len(x_parts) == 2
    x_specs = _split_specs(tm, D) if split_x else [pl.BlockSpec((tm, D), lambda i: (i, 0))]
    n_ctx_tiles = N_CTX // tm
    lat_tiles = L_LAT // tm

    def rope_idx(i):
        return (jnp.where(i < n_ctx_tiles, 0, 1 + (i - n_ctx_tiles) % lat_tiles), 0)

    const = lambda i: (0, 0)
    return pl.pallas_call(
        functools.partial(_qkv_body, tm=tm, split_x=split_x),
        grid=(N_TOK // tm,),
        in_specs=x_specs + [
            pl.BlockSpec((1, 6, D), lambda i: (_cond_row(i, tm), 0, 0)),
            pl.BlockSpec((1, D), const),
            pl.BlockSpec((D, D + 2 * KV_W), const),
            pl.BlockSpec((1, KV_W), const),
            pl.BlockSpec((1, KV_W), const),
            pl.BlockSpec((KV_W, KV_W), const),
            pl.BlockSpec((tm, KV_W), rope_idx),
            pl.BlockSpec((tm, KV_W), rope_idx),
        ],
        out_specs=[
            pl.BlockSpec((tm, D), lambda i: (i, 0)),
            pl.BlockSpec((tm, KV_W), lambda i: (i, 0)),
            pl.BlockSpec((tm, KV_W), lambda i: (i, 0)),
        ],
        out_shape=[
            jax.ShapeDtypeStruct((N_TOK, D), BF16),
            jax.ShapeDtypeStruct((N_TOK, KV_W), F32),
            jax.ShapeDtypeStruct((N_TOK, KV_W), F32),
        ],
        compiler_params=_cparams(("arbitrary",)),
        name="qkv",
    )(*x_parts, ada, gain.reshape(1, D), w_qkv,
      jnp.tile(q_gain, Q_PER_KV).reshape(1, KV_W), jnp.tile(k_gain, N_KV).reshape(1, KV_W),
      jnp.asarray(_head_sum_matrix()).astype(BF16), jnp.asarray(cos), jnp.asarray(sin))


def _attn_body(q_ref, k_ref, v_ref, o_ref, *, n_kv, stack):
    lane = lax.broadcasted_iota(jnp.int32, (1, KV_W), 1)
    masks = [(lane >> 6) == g for g in range(Q_PER_KV)]
    tq = q_ref.shape[0]
    for kv in range(n_kv):
        q = q_ref[:, kv * KV_W:(kv + 1) * KV_W]
        out = jnp.zeros((tq, KV_W), F32)
        for pair in [masks[c:c + stack] for c in range(0, Q_PER_KV, stack)]:
            stacked = jnp.concatenate([jnp.where(m, q, jnp.zeros_like(q)) for m in pair], axis=0)
            s = lax.dot_general(stacked, k_ref[0, kv], (((1,), (1,)), ((), ())), preferred_element_type=F32)
            p = jnp.exp(s - jnp.max(s, axis=-1, keepdims=True))
            inv = 1.0 / jnp.sum(p, axis=-1, keepdims=True)
            og = jnp.dot(p.astype(BF16), v_ref[0, kv], preferred_element_type=F32) * inv
            for g, m in enumerate(pair):
                out = out + jnp.where(m, og[g * tq:(g + 1) * tq], 0.0)
        o_ref[:, kv * KV_W:(kv + 1) * KV_W] = out.astype(BF16)


def _attention(q, k4, v4, *, row0, n_batch, seq, tq, n_kv, stack):
    lk = k4.shape[2]
    per_b = seq // tq
    base = row0 // tq
    return pl.pallas_call(
        functools.partial(_attn_body, n_kv=n_kv, stack=stack),
        grid=(n_batch, N_KV // n_kv, per_b),
        in_specs=[
            pl.BlockSpec((tq, n_kv * KV_W), lambda b, h, i: (base + b * per_b + i, h)),
            pl.BlockSpec((1, n_kv, lk, KV_W), lambda b, h, i: (b, h, 0, 0)),
            pl.BlockSpec((1, n_kv, lk, KV_W), lambda b, h, i: (b, h, 0, 0)),
        ],
        out_specs=pl.BlockSpec((tq, n_kv * KV_W), lambda b, h, i: (b * per_b + i, h)),
        out_shape=jax.ShapeDtypeStruct((n_batch * seq, D), BF16),
        compiler_params=_cparams(("arbitrary", "arbitrary", "arbitrary")),
        name="attention",
    )(q, k4, v4)


def _lane_tile_heads(x):
    b, l, _ = x.shape
    xh = x.reshape(b, l, N_KV, DH).transpose(0, 2, 1, 3).astype(BF16)
    return jnp.tile(xh, (1, 1, 1, KV_W // DH))


def _attention_ctx(q, k, v):
    k_ctx = k[:N_CTX].reshape(N_BATCH_CTX, L_CTX, KV_W)
    v_ctx = v[:N_CTX].reshape(N_BATCH_CTX, L_CTX, KV_W)
    return _attention(q, _lane_tile_heads(k_ctx), _lane_tile_heads(v_ctx),
                      row0=0, n_batch=N_BATCH_CTX, seq=L_CTX, tq=L_CTX, n_kv=N_KV, stack=Q_PER_KV)


def _attention_lat(q, k, v, past_k, past_v):
    past_k = past_k.transpose(0, 2, 1, 3).reshape(N_BATCH_LAT, PAST, KV_W)
    past_v = past_v.transpose(0, 2, 1, 3).reshape(N_BATCH_LAT, PAST, KV_W)
    k_lat = jnp.concatenate([k[N_CTX:].reshape(N_BATCH_LAT, L_LAT, KV_W), past_k], axis=1)
    v_lat = jnp.concatenate([v[N_CTX:].reshape(N_BATCH_LAT, L_LAT, KV_W), past_v], axis=1)
    return _attention(q, _lane_tile_heads(k_lat), _lane_tile_heads(v_lat),
                      row0=N_CTX, n_batch=N_BATCH_LAT, seq=L_LAT, tq=256, n_kv=1, stack=1)


def _resid_body(*refs, gpart, tm, split_x):
    ac_ref, al_ref, w_ref, b_ref, ada_ref = refs[:5]
    x_refs, o_ref = refs[5:-1], refs[-1]
    i = pl.program_id(0)
    a = _pick_split(i, tm, ac_ref, al_ref)
    x = _pick_split(i, tm, *x_refs) if split_x else x_refs[0][...]
    acc = jnp.dot(a, w_ref[...], preferred_element_type=F32) + b_ref[...]
    o_ref[...] = x + ada_ref[0, gpart:gpart + 1, :] * acc


def _resid_proj(a_ctx, a_lat, w, b, x_parts, ada, *, gpart, tm=512):
    k = a_ctx.shape[1]
    split_x = len(x_parts) == 2
    x_specs = _split_specs(tm, D) if split_x else [pl.BlockSpec((tm, D), lambda i: (i, 0))]
    return pl.pallas_call(
        functools.partial(_resid_body, gpart=gpart, tm=tm, split_x=split_x),
        grid=(N_TOK // tm,),
        in_specs=_split_specs(tm, k) + [
            pl.BlockSpec((k, D), lambda i: (0, 0)),
            pl.BlockSpec((1, D), lambda i: (0, 0)),
            pl.BlockSpec((1, 6, D), lambda i: (_cond_row(i, tm), 0, 0)),
        ] + x_specs,
        out_specs=pl.BlockSpec((tm, D), lambda i: (i, 0)),
        out_shape=jax.ShapeDtypeStruct((N_TOK, D), F32),
        compiler_params=_cparams(("arbitrary",)),
        name="resid_proj",
    )(a_ctx, a_lat, w, b.reshape(1, D), ada, *x_parts)


def _alt_sign(L):
    row = lax.broadcasted_iota(jnp.int32, (L, 1), 0)
    return row, jnp.where((row & 1) == 0, 1.0, -1.0)


def _filter_body(feat_ref, w1_ref, b1_ref, w2_ref, b2_ref, fq_ref, w3_ref, b3_ref, ld_ref, cs_ref,
                 fr_ref, fi_ref, fn_ref, *, L):
    feats = feat_ref[...]
    fq = fq_ref[...]
    h1 = jnp.sin(fq * (jnp.dot(feats, w1_ref[...], precision=HI, preferred_element_type=F32) + b1_ref[...]))
    h2 = jnp.sin(fq * (jnp.dot(h1, w2_ref[...], precision=HI, preferred_element_type=F32) + b2_ref[...]))
    t = feats[:, 0:1]
    _, alt = _alt_sign(L)
    filt = []
    for j in range(4):
        raw = jnp.dot(h2, w3_ref[j], precision=HI, preferred_element_type=F32) + b3_ref[j:j + 1, :]
        filt.append(raw * jnp.exp(-t * jnp.exp(ld_ref[j:j + 1, :])))
    for o in range(2):
        hf, hb = filt[2 * o], filt[2 * o + 1]
        l1 = jnp.sum(jnp.abs(hf), axis=0, keepdims=True) + jnp.sum(jnp.abs(hb), axis=0, keepdims=True)
        inv = 1.0 / (l1 + EPS)
        sym = (hf + hb) * inv
        asym = (hb - hf) * inv
        fr_ref[o] = jnp.dot(cs_ref[0:L, :], sym.astype(BF16), preferred_element_type=F32)
        fi_ref[o] = jnp.dot(cs_ref[L:2 * L, :], asym.astype(BF16), preferred_element_type=F32)
        fn_ref[o] = jnp.sum(sym * alt, axis=0, keepdims=True)


def _filter_spectra(L, cs, w1, b1, w2, b2, w3, b3, freq, log_decay, *, tc=256):
    w1p = jnp.zeros((FEAT_PAD, FILTER_HIDDEN), F32).at[:FILTER_FEAT].set(w1)
    w3r = w3.reshape(FILTER_HIDDEN, 4, D).transpose(1, 0, 2)
    const = lambda j: (0, 0)
    return pl.pallas_call(
        functools.partial(_filter_body, L=L),
        grid=(D // tc,),
        in_specs=[
            pl.BlockSpec((L, FEAT_PAD), const),
            pl.BlockSpec((FEAT_PAD, FILTER_HIDDEN), const),
            pl.BlockSpec((1, FILTER_HIDDEN), const),
            pl.BlockSpec((FILTER_HIDDEN, FILTER_HIDDEN), const),
            pl.BlockSpec((1, FILTER_HIDDEN), const),
            pl.BlockSpec((1, FILTER_HIDDEN), const),
            pl.BlockSpec((4, FILTER_HIDDEN, tc), lambda j: (0, 0, j)),
            pl.BlockSpec((4, tc), lambda j: (0, j)),
            pl.BlockSpec((4, tc), lambda j: (0, j)),
            pl.BlockSpec((2 * L, L), const, pipeline_mode=pl.Buffered(1)),
        ],
        out_specs=[
            pl.BlockSpec((2, L, tc), lambda j: (0, 0, j)),
            pl.BlockSpec((2, L, tc), lambda j: (0, 0, j)),
            pl.BlockSpec((2, 1, tc), lambda j: (0, 0, j)),
        ],
        out_shape=[
            jax.ShapeDtypeStruct((2, L, D), F32),
            jax.ShapeDtypeStruct((2, L, D), F32),
            jax.ShapeDtypeStruct((2, 1, D), F32),
        ],
        compiler_params=_cparams(("arbitrary",)),
        name="hyena_filter",
    )(jnp.asarray(_filter_feats(L)), w1p, b1.reshape(1, -1), w2, b2.reshape(1, -1), freq.reshape(1, -1),
      w3r, b3.reshape(4, D), log_decay.reshape(4, D), cs)


FREQ_CHUNK = 1024


def _hyconv_body(x1_ref, x2_ref, v_ref, sw_ref, sb_ref, skip_ref, fr_ref, fi_ref, fn_ref, cs_ref, o_ref,
                 z_ref, g_ref, zb_ref, yr_ref, yi_ref, *, L):
    row, alt = _alt_sign(L)
    kc = min(FREQ_CHUNK, L)
    chunk_row = lax.broadcasted_iota(jnp.int32, (kc, 1), 0)

    def short_conv(u_ref, p):
        u = u_ref[...].astype(F32)
        prev = jnp.where(row == 0, 0.0, pltpu.roll(u, 1, 0))
        nxt = jnp.where(row == L - 1, 0.0, pltpu.roll(u, L - 1, 0))
        return (prev * sw_ref[0, p:p + 1, :] + u * sw_ref[1, p:p + 1, :] + nxt * sw_ref[2, p:p + 1, :]
                + sb_ref[p:p + 1, :])

    z_ref[...] = short_conv(v_ref, 2)
    g_ref[0] = short_conv(x1_ref, 0)
    g_ref[1] = short_conv(x2_ref, 1)
    for o in range(2):
        z = z_ref[...]
        zb_ref[...] = z.astype(BF16)
        nyq = jnp.sum(z * alt, axis=0, keepdims=True) * fn_ref[o] * (0.5 / L)

        def to_freq(c, carry):
            lo = pl.multiple_of(c * kc, kc)
            zr = jnp.dot(cs_ref[pl.ds(lo, kc), :], zb_ref[...], preferred_element_type=F32)
            zs = jnp.dot(cs_ref[pl.ds(L + lo, kc), :], zb_ref[...], preferred_element_type=F32)
            fr = fr_ref[o, pl.ds(lo, kc), :]
            fi = fi_ref[o, pl.ds(lo, kc), :]
            yr = (zr * fr + zs * fi) * (1.0 / L)
            yr = jnp.where(chunk_row + lo == 0, 0.5 * yr, yr)
            yr_ref[pl.ds(lo, kc), :] = yr.astype(BF16)
            yi_ref[pl.ds(lo, kc), :] = ((zr * fi - zs * fr) * (1.0 / L)).astype(BF16)
            return carry

        lax.fori_loop(0, L // kc, to_freq, 0)

        def to_time(c, carry):
            lo = pl.multiple_of(c * kc, kc)
            y = (jnp.dot(cs_ref[pl.ds(lo, kc), :], yr_ref[...], preferred_element_type=F32)
                 - jnp.dot(cs_ref[pl.ds(L + lo, kc), :], yi_ref[...], preferred_element_type=F32))
            y = y + jnp.where(((chunk_row + lo) & 1) == 0, nyq, -nyq)
            z_ref[pl.ds(lo, kc), :] = g_ref[o, pl.ds(lo, kc), :] * (y + skip_ref[o:o + 1, :] * z_ref[pl.ds(lo, kc), :])
            return carry

        lax.fori_loop(0, L // kc, to_time, 0)
    o_ref[...] = z_ref[...].astype(BF16)


def _hyena_conv(up, sw, sb, skip, fr, fi, fn, cs, *, row0, n_batch, L, tc):
    n_ct = D // tc
    base = row0 // L
    u_spec = lambda p: pl.BlockSpec((L, tc), lambda j, b: (base + b, p * n_ct + j))
    once = pl.Buffered(1)
    return pl.pallas_call(
        functools.partial(_hyconv_body, L=L),
        grid=(n_ct, n_batch),
        in_specs=[
            u_spec(0), u_spec(1), u_spec(2),
            pl.BlockSpec((3, 3, tc), lambda j, b: (0, 0, j)),
            pl.BlockSpec((3, tc), lambda j, b: (0, j)),
            pl.BlockSpec((2, tc), lambda j, b: (0, j)),
            pl.BlockSpec((2, L, tc), lambda j, b: (0, 0, j), pipeline_mode=once),
            pl.BlockSpec((2, L, tc), lambda j, b: (0, 0, j), pipeline_mode=once),
            pl.BlockSpec((2, 1, tc), lambda j, b: (0, 0, j)),
            pl.BlockSpec((2 * L, L), lambda j, b: (0, 0), pipeline_mode=once),
        ],
        out_specs=pl.BlockSpec((L, tc), lambda j, b: (b, j)),
        out_shape=jax.ShapeDtypeStruct((n_batch * L, D), BF16),
        scratch_shapes=[
            pltpu.VMEM((L, tc), F32),
            pltpu.VMEM((2, L, tc), F32),
            pltpu.VMEM((L, tc), BF16),
            pltpu.VMEM((L, tc), BF16),
            pltpu.VMEM((L, tc), BF16),
        ],
        compiler_params=_cparams(("arbitrary", "arbitrary")),
        name="hyena_conv",
    )(up, up, up, sw.reshape(3, 3, D), sb.reshape(3, D), skip, fr, fi, fn, cs)


def _route_body(x_ref, ada_ref, gain_ref, wr_ref, br_ref, h_ref, comb_ref):
    h = _norm_mod(x_ref[...], gain_ref, ada_ref, 3)
    h_ref[...] = h.astype(BF16)
    logits = jnp.dot(h, wr_ref[...], precision=HI, preferred_element_type=F32) + br_ref[...]
    lane = lax.broadcasted_iota(jnp.int32, (1, ROUTE_W), 1)
    lane_f = lane.astype(F32)
    group_of_lane = (lane >> 3).astype(F32)
    neg = -jnp.inf
    big = float(ROUTE_W)
    is_g = (lane >= N_EXP) & (lane < N_EXP + N_GROUPS)
    gl = jnp.where(is_g, logits, neg)
    gmax = jnp.max(gl, axis=-1, keepdims=True)
    gidx = jnp.min(jnp.where(gl == gmax, lane_f - N_EXP, big), axis=-1, keepdims=True)
    g_top = 1.0 / jnp.sum(jnp.where(is_g, jnp.exp(gl - gmax), 0.0), axis=-1, keepdims=True)
    in_group = (lane < N_EXP) & (group_of_lane == gidx)
    el = jnp.where(in_group, logits, neg)
    v1 = jnp.max(el, axis=-1, keepdims=True)
    i1 = jnp.min(jnp.where(el == v1, lane_f, big), axis=-1, keepdims=True)
    el2 = jnp.where(lane_f == i1, neg, el)
    v2 = jnp.max(el2, axis=-1, keepdims=True)
    i2 = jnp.min(jnp.where(el2 == v2, lane_f, big), axis=-1, keepdims=True)
    r = jnp.exp(v2 - v1)
    w1 = g_top / (1.0 + r)
    w2 = g_top * r / (1.0 + r)
    comb_ref[...] = jnp.where(lane_f == i1, w1, 0.0) + jnp.where(lane_f == i2, w2, 0.0)


def _route(x, ada, gain, wg, bg, we, be, *, tm=512):
    wr = jnp.zeros((D, ROUTE_W), F32).at[:, :N_EXP].set(we.reshape(D, N_EXP)).at[:, N_EXP:N_EXP + N_GROUPS].set(wg)
    br = jnp.zeros((1, ROUTE_W), F32).at[0, :N_EXP].set(be.reshape(N_EXP)).at[0, N_EXP:N_EXP + N_GROUPS].set(bg)
    return pl.pallas_call(
        _route_body,
        grid=(N_TOK // tm,),
        in_specs=[
            pl.BlockSpec((tm, D), lambda i: (i, 0)),
            pl.BlockSpec((1, 6, D), lambda i: (_cond_row(i, tm), 0, 0)),
            pl.BlockSpec((1, D), lambda i: (0, 0)),
            pl.BlockSpec((D, ROUTE_W), lambda i: (0, 0)),
            pl.BlockSpec((1, ROUTE_W), lambda i: (0, 0)),
        ],
        out_specs=[
            pl.BlockSpec((tm, D), lambda i: (i, 0)),
            pl.BlockSpec((tm, ROUTE_W), lambda i: (i, 0)),
        ],
        out_shape=[
            jax.ShapeDtypeStruct((N_TOK, D), BF16),
            jax.ShapeDtypeStruct((N_TOK, ROUTE_W), F32),
        ],
        compiler_params=_cparams(("arbitrary",)),
        name="route",
    )(x, ada, gain.reshape(1, D), wr, br)


def _moe_body(h_ref, comb_ref, wg_ref, wu_ref, wd_ref, x_ref, ada_ref, o_ref, acc_ref):
    e = pl.program_id(1)

    @pl.when(e == 0)
    def _():
        acc_ref[...] = jnp.zeros_like(acc_ref)

    h = h_ref[...]
    a = jnp.dot(h, wg_ref[0, 0].astype(BF16), preferred_element_type=F32)
    b = jnp.dot(h, wu_ref[0, 0].astype(BF16), preferred_element_type=F32)
    lane = lax.broadcasted_iota(jnp.int32, (1, ROUTE_W), 1)
    cw = jnp.sum(jnp.where(lane == e, comb_ref[...], 0.0), axis=-1, keepdims=True)
    act = a * _sigmoid(a) * b * cw
    acc_ref[...] += jnp.dot(act.astype(BF16), wd_ref[0, 0].astype(BF16), preferred_element_type=F32)

    @pl.when(e == N_EXP - 1)
    def _():
        o_ref[...] = x_ref[...] + ada_ref[0, 5:6, :] * acc_ref[...]


def _moe(h, comb, w_gate, w_up, w_down, x, ada, *, layer, tm=1024):
    return pl.pallas_call(
        _moe_body,
        grid=(N_TOK // tm, N_EXP),
        in_specs=[
            pl.BlockSpec((tm, D), lambda i, e: (i, 0)),
            pl.BlockSpec((tm, ROUTE_W), lambda i, e: (i, 0)),
            pl.BlockSpec((1, 1, D, EXP_H), lambda i, e: (layer, e, 0, 0)),
            pl.BlockSpec((1, 1, D, EXP_H), lambda i, e: (layer, e, 0, 0)),
            pl.BlockSpec((1, 1, EXP_H, D), lambda i, e: (layer, e, 0, 0)),
            pl.BlockSpec((tm, D), lambda i, e: (i, 0)),
            pl.BlockSpec((1, 6, D), lambda i, e: (_cond_row(i, tm), 0, 0)),
        ],
        out_specs=pl.BlockSpec((tm, D), lambda i, e: (i, 0)),
        out_shape=jax.ShapeDtypeStruct((N_TOK, D), F32),
        scratch_shapes=[pltpu.VMEM((tm, D), F32)],
        compiler_params=_cparams(("arbitrary", "arbitrary")),
        name="moe",
    )(h, comb, w_gate, w_up, w_down, x, ada)


def _final_body(x_ref, gain_ref, oc_ref, ol_ref, *, tm):
    x = x_ref[...]
    ms = jnp.mean(x * x, axis=-1, keepdims=True)
    y = x * lax.rsqrt(ms + EPS) * gain_ref[...]
    is_ctx = pl.program_id(0) < N_CTX // tm

    @pl.when(is_ctx)
    def _():
        oc_ref[...] = y

    @pl.when(jnp.logical_not(is_ctx))
    def _():
        ol_ref[...] = y


def _final_norm(x, gain, *, tm=1024):
    return pl.pallas_call(
        functools.partial(_final_body, tm=tm),
        grid=(N_TOK // tm,),
        in_specs=[pl.BlockSpec((tm, D), lambda i: (i, 0)), pl.BlockSpec((1, D), lambda i: (0, 0))],
        out_specs=_split_specs(tm, D),
        out_shape=[jax.ShapeDtypeStruct((N_CTX, D), F32), jax.ShapeDtypeStruct((N_LAT, D), F32)],
        compiler_params=_cparams(("arbitrary",)),
        name="final_norm",
    )(x, gain.reshape(1, D))


def kernel(x_prompt, x_sample, cache_k, cache_v, c, c_ctx, w_ada, b_ada, norm_mix, norm_ffn, attn_w_q, attn_w_kv, attn_q_norm, attn_k_norm, attn_w_o, hy_w_in, hy_b_in, hy_short_w, hy_short_b, hy_filt_w1, hy_filt_b1, hy_filt_w2, hy_filt_b2, hy_filt_w3, hy_filt_b3, hy_filt_freq, hy_log_decay, hy_skip, hy_w_out, hy_b_out, router_group_w, router_group_b, router_expert_w, router_expert_b, moe_w_gate, moe_w_up, moe_w_down, final_norm):
    depth = w_ada.shape[0]
    x_parts = (x_prompt.reshape(N_CTX, D), x_sample.reshape(N_LAT, D))
    cond8 = jnp.zeros((8, D), F32).at[0].set(c_ctx).at[1:1 + N_BATCH_LAT].set(c)
    ada_all = _ada(cond8, w_ada, b_ada).reshape(depth, 8, 6, D)
    new_k = []
    new_v = []
    for l in range(depth):
        ada = ada_all[l]
        if l % 2 == 0:
            a = l // 2
            w_qkv = jnp.concatenate([attn_w_q[a], attn_w_kv[a]], axis=1).astype(BF16)
            q, k, v = _qkv(x_parts, ada, norm_mix[l], w_qkv, attn_q_norm[a], attn_k_norm[a])
            kv_heads = lambda t: t[:N_CTX].reshape(N_BATCH_CTX, L_CTX, N_KV, DH).transpose(0, 2, 1, 3)
            new_k.append(kv_heads(k))
            new_v.append(kv_heads(v))
            o_ctx = _attention_ctx(q, k, v)
            o_lat = _attention_lat(q, k, v, cache_k[:, a], cache_v[:, a])
            x = _resid_proj(o_ctx, o_lat, attn_w_o[a].astype(BF16), jnp.zeros((D,), F32), x_parts, ada, gpart=2)
        else:
            j = l // 2
            (x,) = x_parts
            up = _norm_proj(x, ada, norm_mix[l], hy_w_in[j].astype(BF16), hy_b_in[j], part=0)
            outs = []
            for row0, n_batch, L, tc in ((0, N_BATCH_CTX, L_CTX, D), (N_CTX, N_BATCH_LAT, L_LAT, 256)):
                cs = jnp.asarray(_dft_table(L)).astype(BF16)
                fr, fi, fn = _filter_spectra(L, cs, hy_filt_w1[j], hy_filt_b1[j], hy_filt_w2[j], hy_filt_b2[j],
                                             hy_filt_w3[j], hy_filt_b3[j], hy_filt_freq[j], hy_log_decay[j])
                outs.append(_hyena_conv(up, hy_short_w[j], hy_short_b[j], hy_skip[j], fr, fi, fn, cs,
                                        row0=row0, n_batch=n_batch, L=L, tc=tc))
            x = _resid_proj(outs[0], outs[1], hy_w_out[j].astype(BF16), hy_b_out[j], x_parts, ada, gpart=2)
        h, comb = _route(x, ada, norm_ffn[l], router_group_w[l], router_group_b[l],
                         router_expert_w[l], router_expert_b[l])
        x_parts = (_moe(h, comb, moe_w_gate, moe_w_up, moe_w_down, x, ada, layer=l),)
    y_ctx, y_lat = _final_norm(x_parts[0], final_norm)
    return (y_ctx.reshape(N_BATCH_CTX, L_CTX, D), y_lat.reshape(N_BATCH_LAT, L_LAT, D),
            jnp.stack(new_k, axis=1), jnp.stack(new_v, axis=1))
```

```python
import functools
import math

import numpy as np
import jax
import jax.numpy as jnp
from jax import lax
from jax.experimental import pallas as pl
from jax.experimental.pallas import tpu as pltpu

F32 = jnp.float32
BF16 = jnp.bfloat16

D = 1024
N_BATCH_CTX = 32
L_CTX = 256
N_BATCH_LAT = 2
L_LAT = 2048
PAST = 256
N_CTX = N_BATCH_CTX * L_CTX
N_LAT = N_BATCH_LAT * L_LAT
N_TOK = N_CTX + N_LAT
GRID_W = 64
N_HEADS = 16
N_KV = 4
DH = 64
Q_PER_KV = N_HEADS // N_KV
KV_W = N_KV * DH
ROPE_THETA = 10000.0
FILTER_FEAT = 17
FEAT_PAD = 32
FILTER_HIDDEN = 64
N_GROUPS = 4
E_PER_G = 8
N_EXP = N_GROUPS * E_PER_G
EXP_H = D // 4
EPS = 1e-6
ROUTE_W = 128
VMEM_LIMIT = 56 * 1024 * 1024


def _cparams(sem):
    return pltpu.CompilerParams(dimension_semantics=sem, vmem_limit_bytes=VMEM_LIMIT)


def _cond_row(i, tm):
    n_ctx_tiles = N_CTX // tm
    return jnp.where(i < n_ctx_tiles, 0, 1 + (i - n_ctx_tiles) // (L_LAT // tm))


def _split_specs(tm, width):
    n_ctx_tiles = N_CTX // tm
    return [pl.BlockSpec((tm, width), lambda i: (jnp.minimum(i, n_ctx_tiles - 1), 0)),
            pl.BlockSpec((tm, width), lambda i: (jnp.maximum(i - n_ctx_tiles, 0), 0))]


def _pick_split(i, tm, ctx_ref, lat_ref):
    return jnp.where(i < N_CTX // tm, ctx_ref[...], lat_ref[...])


def _sigmoid(x):
    return 1.0 / (1.0 + jnp.exp(-x))


@functools.lru_cache(maxsize=None)
def _dft_table(L):
    k = np.arange(L, dtype=np.int64)
    ang = (np.outer(k, k) % (2 * L)).astype(np.float64) * (math.pi / L)
    return np.concatenate([np.cos(ang), np.sin(ang)], axis=0).astype(np.float32)


@functools.lru_cache(maxsize=None)
def _filter_feats(L):
    t = np.linspace(0.0, 1.0, L, dtype=np.float64)[:, None]
    bands = np.linspace(1e-4, 7.0, 8, dtype=np.float64)[None, :]
    w = (2.0 * math.pi) * np.arange(L, dtype=np.float64)[:, None] / L
    feats = np.concatenate([t, np.cos(bands * w), -np.sin(bands * w)], axis=-1)
    out = np.zeros((L, FEAT_PAD), np.float32)
    out[:, :FILTER_FEAT] = feats
    return out


@functools.lru_cache(maxsize=None)
def _rope_tables(tm):
    pos = np.arange(L_LAT)
    row = (pos // GRID_W).astype(np.float64)
    col = (pos % GRID_W).astype(np.float64)
    axis_dim = DH // 2
    inv_freq = ROPE_THETA ** (-np.arange(0, axis_dim, 2, dtype=np.float64) / axis_dim)
    lane = np.arange(KV_W)
    d = lane % DH
    is_col = (d // axis_dim) == 1
    fi = d % (axis_dim // 2)
    first_half = (d % axis_dim) < (axis_dim // 2)
    p = np.where(is_col[None, :], col[:, None], row[:, None])
    ang = p * inv_freq[fi][None, :]
    cos = np.cos(ang)
    sin = np.sin(ang) * np.where(first_half, -1.0, 1.0)[None, :]
    cos = np.concatenate([np.ones((tm, KV_W)), cos], axis=0).astype(np.float32)
    sin = np.concatenate([np.zeros((tm, KV_W)), sin], axis=0).astype(np.float32)
    return cos, sin


@functools.lru_cache(maxsize=None)
def _head_sum_matrix():
    lane = np.arange(KV_W)
    return (lane[:, None] // DH == lane[None, :] // DH).astype(np.float32)


def _ada_body(c_ref, w_ref, b_ref, o_ref):
    c = c_ref[...]
    s = c * _sigmoid(c)
    o_ref[0] = jnp.dot(s.astype(BF16), w_ref[0].astype(BF16), preferred_element_type=F32) + b_ref[0]


def _ada(cond8, w_ada, b_ada):
    depth = w_ada.shape[0]
    tn = 1536
    return pl.pallas_call(
        _ada_body,
        grid=(depth, 6 * D // tn),
        in_specs=[
            pl.BlockSpec((8, D), lambda l, j: (0, 0)),
            pl.BlockSpec((1, D, tn), lambda l, j: (l, 0, j)),
            pl.BlockSpec((1, 1, tn), lambda l, j: (l, 0, j)),
        ],
        out_specs=pl.BlockSpec((1, 8, tn), lambda l, j: (l, 0, j)),
        out_shape=jax.ShapeDtypeStruct((depth, 8, 6 * D), F32),
        compiler_params=_cparams(("arbitrary", "arbitrary")),
        name="ada",
    )(cond8, w_ada, b_ada.reshape(depth, 1, 6 * D))


def _norm_mod(x, gain_ref, ada_ref, part):
    ms = jnp.mean(x * x, axis=-1, keepdims=True)
    y = x * lax.rsqrt(ms + EPS) * gain_ref[...]
    return y * (1.0 + ada_ref[0, part + 1:part + 2, :]) + ada_ref[0, part:part + 1, :]


def _proj_body(x_ref, ada_ref, gain_ref, w_ref, b_ref, o_ref, *, part):
    h = _norm_mod(x_ref[...], gain_ref, ada_ref, part)
    acc = jnp.dot(h.astype(BF16), w_ref[...], preferred_element_type=F32) + b_ref[...]
    o_ref[...] = acc.astype(o_ref.dtype)


def _norm_proj(x, ada, gain, w, b, *, part, tm=1024):
    n_out = w.shape[1]
    return pl.pallas_call(
        functools.partial(_proj_body, part=part),
        grid=(N_TOK // tm,),
        in_specs=[
            pl.BlockSpec((tm, D), lambda i: (i, 0)),
            pl.BlockSpec((1, 6, D), lambda i: (_cond_row(i, tm), 0, 0)),
            pl.BlockSpec((1, D), lambda i: (0, 0)),
            pl.BlockSpec((D, n_out), lambda i: (0, 0)),
            pl.BlockSpec((1, n_out), lambda i: (0, 0)),
        ],
        out_specs=pl.BlockSpec((tm, n_out), lambda i: (i, 0)),
        out_shape=jax.ShapeDtypeStruct((N_TOK, n_out), BF16),
        compiler_params=_cparams(("arbitrary",)),
        name="norm_proj",
    )(x, ada, gain.reshape(1, D), w, b.reshape(1, n_out))


def _head_rms_rope(x, gain, hs, cos, sin, lane):
    ss = jnp.dot((x * x).astype(BF16), hs, preferred_element_type=F32)
    xn = x * lax.rsqrt(ss * (1.0 / DH) + EPS) * gain
    fwd = pltpu.roll(xn, KV_W - DH // 4, 1)
    bwd = pltpu.roll(xn, DH // 4, 1)
    partner = jnp.where((lane & (DH // 4)) == 0, fwd, bwd)
    return xn * cos + partner * sin


@functools.lru_cache(maxsize=None)
def _lane_tile_matrix():
    col = np.arange(N_KV * KV_W)
    src = (col // KV_W) * DH + col % DH
    return (np.arange(KV_W)[:, None] == src[None, :]).astype(np.float32)


def _qkv_body(*refs, tm, split_x):
    n_x = 2 if split_x else 1
    x_refs = refs[:n_x]
    (ada_ref, gain_ref, w_ref, qg_ref, kg_ref, hs_ref, cos_ref, sin_ref, tile_ref,
     q_ref, newk_ref, newv_ref, kc_ref, vc_ref, kl_ref, vl_ref) = refs[n_x:]
    step = pl.program_id(0)
    x = _pick_split(step, tm, *x_refs) if split_x else x_refs[0][...]
    h = _norm_mod(x, gain_ref, ada_ref, 0)
    acc = jnp.dot(h.astype(BF16), w_ref[...], preferred_element_type=F32)
    hs = hs_ref[...]
    cos = cos_ref[...]
    sin = sin_ref[...]
    lane = lax.broadcasted_iota(jnp.int32, (1, KV_W), 1)
    for c in range(N_KV):
        qc = _head_rms_rope(acc[:, c * KV_W:(c + 1) * KV_W], qg_ref[...], hs, cos, sin, lane)
        q_ref[:, c * KV_W:(c + 1) * KV_W] = (qc * (DH ** -0.5)).astype(BF16)
    k = _head_rms_rope(acc[:, D:D + KV_W], kg_ref[...], hs, cos, sin, lane)
    v = acc[:, D + KV_W:D + 2 * KV_W]
    k4 = jnp.dot(k.astype(BF16), tile_ref[...], preferred_element_type=F32).astype(BF16)
    v4 = jnp.dot(v.astype(BF16), tile_ref[...], preferred_element_type=F32).astype(BF16)
    is_ctx = step < N_CTX // tm

    @pl.when(is_ctx)
    def _():
        for bb in range(tm // L_CTX):
            rows = slice(bb * L_CTX, (bb + 1) * L_CTX)
            for hd in range(N_KV):
                newk_ref[bb, hd] = k[rows, hd * DH:(hd + 1) * DH]
                newv_ref[bb, hd] = v[rows, hd * DH:(hd + 1) * DH]
                kc_ref[bb, hd] = k4[rows, hd * KV_W:(hd + 1) * KV_W]
                vc_ref[bb, hd] = v4[rows, hd * KV_W:(hd + 1) * KV_W]

    @pl.when(jnp.logical_not(is_ctx))
    def _():
        for hd in range(N_KV):
            kl_ref[0, hd] = k4[:, hd * KV_W:(hd + 1) * KV_W]
            vl_ref[0, hd] = v4[:, hd * KV_W:(hd + 1) * KV_W]


def _qkv(x_parts, ada, gain, w_qkv, q_gain, k_gain):
    tm = 2 * L_CTX
    per_tile = tm // L_CTX
    cos, sin = _rope_tables(tm)
    split_x = len(x_parts) == 2
    x_specs = _split_specs(tm, D) if split_x else [pl.BlockSpec((tm, D), lambda i: (i, 0))]
    n_ctx_tiles = N_CTX // tm
    lat_tiles = L_LAT // tm

    def rope_idx(i):
        return (jnp.where(i < n_ctx_tiles, 0, 1 + (i - n_ctx_tiles) % lat_tiles), 0)

    def ctx_idx(i):
        return (jnp.minimum(i, n_ctx_tiles - 1), 0, 0, 0)

    def lat_idx(i):
        j = jnp.maximum(i - n_ctx_tiles, 0)
        return (j // lat_tiles, 0, j % lat_tiles, 0)

    const = lambda i: (0, 0)
    kv_ctx = jax.ShapeDtypeStruct((N_BATCH_CTX, N_KV, L_CTX, DH), F32)
    tiled_ctx = jax.ShapeDtypeStruct((N_BATCH_CTX, N_KV, L_CTX, KV_W), BF16)
    tiled_lat = jax.ShapeDtypeStruct((N_BATCH_LAT, N_KV, L_LAT, KV_W), BF16)
    return pl.pallas_call(
        functools.partial(_qkv_body, tm=tm, split_x=split_x),
        grid=(N_TOK // tm,),
        in_specs=x_specs + [
            pl.BlockSpec((1, 6, D), lambda i: (_cond_row(i, tm), 0, 0)),
            pl.BlockSpec((1, D), const),
            pl.BlockSpec((D, D + 2 * KV_W), const),
            pl.BlockSpec((1, KV_W), const),
            pl.BlockSpec((1, KV_W), const),
            pl.BlockSpec((KV_W, KV_W), const),
            pl.BlockSpec((tm, KV_W), rope_idx),
            pl.BlockSpec((tm, KV_W), rope_idx),
            pl.BlockSpec((KV_W, N_KV * KV_W), const),
        ],
        out_specs=[
            pl.BlockSpec((tm, D), lambda i: (i, 0)),
            pl.BlockSpec((per_tile, N_KV, L_CTX, DH), ctx_idx),
            pl.BlockSpec((per_tile, N_KV, L_CTX, DH), ctx_idx),
            pl.BlockSpec((per_tile, N_KV, L_CTX, KV_W), ctx_idx),
            pl.BlockSpec((per_tile, N_KV, L_CTX, KV_W), ctx_idx),
            pl.BlockSpec((1, N_KV, tm, KV_W), lat_idx),
            pl.BlockSpec((1, N_KV, tm, KV_W), lat_idx),
        ],
        out_shape=[jax.ShapeDtypeStruct((N_TOK, D), BF16), kv_ctx, kv_ctx, tiled_ctx, tiled_ctx, tiled_lat, tiled_lat],
        compiler_params=_cparams(("arbitrary",)),
        name="qkv",
    )(*x_parts, ada, gain.reshape(1, D), w_qkv,
      jnp.tile(q_gain, Q_PER_KV).reshape(1, KV_W), jnp.tile(k_gain, N_KV).reshape(1, KV_W),
      jnp.asarray(_head_sum_matrix()).astype(BF16), jnp.asarray(cos), jnp.asarray(sin),
      jnp.asarray(_lane_tile_matrix()).astype(BF16))


def _attn_body(*refs, n_kv, stack, n_seg):
    q_ref, k_refs, v_refs, o_ref = refs[0], refs[1:1 + n_seg], refs[1 + n_seg:1 + 2 * n_seg], refs[-1]
    lane = lax.broadcasted_iota(jnp.int32, (1, KV_W), 1)
    masks = [(lane >> 6) == g for g in range(Q_PER_KV)]
    tq = q_ref.shape[0]
    for kv in range(n_kv):
        q = q_ref[:, kv * KV_W:(kv + 1) * KV_W]
        out = jnp.zeros((tq, KV_W), F32)
        for c in range(0, Q_PER_KV, stack):
            pair = masks[c:c + stack]
            ones_blk = (c + stack) % Q_PER_KV if stack < Q_PER_KV else None
            stacked = jnp.concatenate([jnp.where(m, q, jnp.zeros_like(q)) for m in pair], axis=0)
            scores = [lax.dot_general(stacked, k_ref[0, kv], (((1,), (1,)), ((), ())), preferred_element_type=F32)
                      for k_ref in k_refs]
            top = functools.reduce(jnp.maximum, [jnp.max(s, axis=-1, keepdims=True) for s in scores])
            probs = [jnp.exp((s - top).astype(BF16)) for s in scores]
            if ones_blk is None:
                vals = [v_ref[0, kv] for v_ref in v_refs]
            else:
                vals = [jnp.where(masks[ones_blk], jnp.ones((), BF16), v_ref[0, kv]) for v_ref in v_refs]
            og = sum(jnp.dot(p, v, preferred_element_type=F32) for p, v in zip(probs, vals))
            if ones_blk is None:
                denom = sum(jnp.sum(p.astype(F32), axis=-1, keepdims=True) for p in probs)
            else:
                denom = og[:, ones_blk * DH:ones_blk * DH + 1]
            og = og * (1.0 / denom)
            for g, m in enumerate(pair):
                out = out + jnp.where(m, og[g * tq:(g + 1) * tq], 0.0)
        o_ref[:, kv * KV_W:(kv + 1) * KV_W] = out.astype(BF16)


def _attention(q, keys, values, *, row0, n_batch, seq, tq, n_kv, stack):
    per_b = seq // tq
    base = row0 // tq
    kv_specs = [pl.BlockSpec((1, n_kv, k.shape[2], KV_W), lambda b, h, i: (b, h, 0, 0)) for k in keys + values]
    return pl.pallas_call(
        functools.partial(_attn_body, n_kv=n_kv, stack=stack, n_seg=len(keys)),
        grid=(n_batch, N_KV // n_kv, per_b),
        in_specs=[pl.BlockSpec((tq, n_kv * KV_W), lambda b, h, i: (base + b * per_b + i, h))] + kv_specs,
        out_specs=pl.BlockSpec((tq, n_kv * KV_W), lambda b, h, i: (b * per_b + i, h)),
        out_shape=jax.ShapeDtypeStruct((n_batch * seq, D), BF16),
        compiler_params=_cparams(("arbitrary", "arbitrary", "arbitrary")),
        name="attention",
    )(q, *keys, *values)


def _lane_tile_heads(x):
    return jnp.tile(x.astype(BF16), (1, 1, 1, KV_W // DH))


def _resid_body(*refs, gpart, tm, split_x):
    ac_ref, al_ref, w_ref, b_ref, ada_ref = refs[:5]
    x_refs, o_ref = refs[5:-1], refs[-1]
    i = pl.program_id(0)
    a = _pick_split(i, tm, ac_ref, al_ref)
    x = _pick_split(i, tm, *x_refs) if split_x else x_refs[0][...]
    acc = jnp.dot(a, w_ref[...], preferred_element_type=F32) + b_ref[...]
    o_ref[...] = x + ada_ref[0, gpart:gpart + 1, :] * acc


def _resid_proj(a_ctx, a_lat, w, b, x_parts, ada, *, gpart, tm=1024):
    k = a_ctx.shape[1]
    split_x = len(x_parts) == 2
    x_specs = _split_specs(tm, D) if split_x else [pl.BlockSpec((tm, D), lambda i: (i, 0))]
    return pl.pallas_call(
        functools.partial(_resid_body, gpart=gpart, tm=tm, split_x=split_x),
        grid=(N_TOK // tm,),
        in_specs=_split_specs(tm, k) + [
            pl.BlockSpec((k, D), lambda i: (0, 0)),
            pl.BlockSpec((1, D), lambda i: (0, 0)),
            pl.BlockSpec((1, 6, D), lambda i: (_cond_row(i, tm), 0, 0)),
        ] + x_specs,
        out_specs=pl.BlockSpec((tm, D), lambda i: (i, 0)),
        out_shape=jax.ShapeDtypeStruct((N_TOK, D), F32),
        compiler_params=_cparams(("arbitrary",)),
        name="resid_proj",
    )(a_ctx, a_lat, w, b.reshape(1, D), ada, *x_parts)


def _alt_sign(L):
    row = lax.broadcasted_iota(jnp.int32, (L, 1), 0)
    return row, jnp.where((row & 1) == 0, 1.0, -1.0)


def _split_dot(a, b):
    a_hi, b_hi = a.astype(BF16), b.astype(BF16)
    a_lo = (a - a_hi.astype(F32)).astype(BF16)
    b_lo = (b - b_hi.astype(F32)).astype(BF16)
    return (jnp.dot(a_hi, b_hi, preferred_element_type=F32) + jnp.dot(a_lo, b_hi, preferred_element_type=F32)
            + jnp.dot(a_hi, b_lo, preferred_element_type=F32))


def _filter_body(feat_ref, w1_ref, b1_ref, w2_ref, b2_ref, fq_ref, w3_ref, b3_ref, ld_ref, cs_ref,
                 fr_ref, fi_ref, fn_ref, hid_ref, *, L):
    feats = feat_ref[...]

    @pl.when(pl.program_id(0) == 0)
    def _():
        fq = fq_ref[...]
        h1 = jnp.sin(fq * (_split_dot(feats, w1_ref[...]) + b1_ref[...]))
        hid_ref[...] = jnp.sin(fq * (_split_dot(h1, w2_ref[...]) + b2_ref[...])).astype(BF16)

    t = feats[:, 0:1]
    row, alt = _alt_sign(L)
    filt = []
    for j in range(4):
        raw = jnp.dot(hid_ref[...], w3_ref[j].astype(BF16), preferred_element_type=F32) + b3_ref[j:j + 1, :]
        filt.append(raw * jnp.exp(-t * jnp.exp(ld_ref[j:j + 1, :])))
    for o in range(2):
        hf, hb = filt[2 * o], filt[2 * o + 1]
        l1 = jnp.sum(jnp.abs(hf), axis=0, keepdims=True) + jnp.sum(jnp.abs(hb), axis=0, keepdims=True)
        inv = 1.0 / (l1 + EPS)
        sym = (hf + hb) * inv
        asym = (hb - hf) * inv
        fr = jnp.dot(cs_ref[0:L, :], sym.astype(BF16), preferred_element_type=F32) * (1.0 / L)
        fr_ref[o] = jnp.where(row == 0, 0.5 * fr, fr)
        fi_ref[o] = jnp.dot(cs_ref[L:2 * L, :], asym.astype(BF16), preferred_element_type=F32) * (1.0 / L)
        fn_ref[o] = jnp.sum(sym * alt, axis=0, keepdims=True) * (0.5 / L)


def _filter_spectra(L, cs, w1, b1, w2, b2, w3, b3, freq, log_decay, *, tc=256):
    w1p = jnp.zeros((FEAT_PAD, FILTER_HIDDEN), F32).at[:FILTER_FEAT].set(w1)
    w3r = w3.reshape(FILTER_HIDDEN, 4, D).transpose(1, 0, 2)
    const = lambda j: (0, 0)
    return pl.pallas_call(
        functools.partial(_filter_body, L=L),
        grid=(D // tc,),
        in_specs=[
            pl.BlockSpec((L, FEAT_PAD), const),
            pl.BlockSpec((FEAT_PAD, FILTER_HIDDEN), const),
            pl.BlockSpec((1, FILTER_HIDDEN), const),
            pl.BlockSpec((FILTER_HIDDEN, FILTER_HIDDEN), const),
            pl.BlockSpec((1, FILTER_HIDDEN), const),
            pl.BlockSpec((1, FILTER_HIDDEN), const),
            pl.BlockSpec((4, FILTER_HIDDEN, tc), lambda j: (0, 0, j)),
            pl.BlockSpec((4, tc), lambda j: (0, j)),
            pl.BlockSpec((4, tc), lambda j: (0, j)),
            pl.BlockSpec((2 * L, L), const, pipeline_mode=pl.Buffered(1)),
        ],
        out_specs=[
            pl.BlockSpec((2, L, tc), lambda j: (0, 0, j)),
            pl.BlockSpec((2, L, tc), lambda j: (0, 0, j)),
            pl.BlockSpec((2, 1, tc), lambda j: (0, 0, j)),
        ],
        out_shape=[
            jax.ShapeDtypeStruct((2, L, D), F32),
            jax.ShapeDtypeStruct((2, L, D), F32),
            jax.ShapeDtypeStruct((2, 1, D), F32),
        ],
        scratch_shapes=[pltpu.VMEM((L, FILTER_HIDDEN), BF16)],
        compiler_params=_cparams(("arbitrary",)),
        name="hyena_filter",
    )(jnp.asarray(_filter_feats(L)), w1p, b1.reshape(1, -1), w2, b2.reshape(1, -1), freq.reshape(1, -1),
      w3r, b3.reshape(4, D), log_decay.reshape(4, D), cs)


FREQ_CHUNK = 1024


def _hyconv_body(x1_ref, x2_ref, v_ref, sw_ref, sb_ref, skip_ref, fr_ref, fi_ref, fn_ref, cs_ref, o_ref,
                 z_ref, g_ref, zb_ref, yr_ref, yi_ref, *, L):
    row, alt = _alt_sign(L)
    kc = min(FREQ_CHUNK, L)
    chunk_row = lax.broadcasted_iota(jnp.int32, (kc, 1), 0)

    def short_conv(u_ref, p):
        u = u_ref[...].astype(F32)
        prev = jnp.where(row == 0, 0.0, pltpu.roll(u, 1, 0))
        nxt = jnp.where(row == L - 1, 0.0, pltpu.roll(u, L - 1, 0))
        return (prev * sw_ref[0, p:p + 1, :] + u * sw_ref[1, p:p + 1, :] + nxt * sw_ref[2, p:p + 1, :]
                + sb_ref[p:p + 1, :])

    z_ref[...] = short_conv(v_ref, 2)
    g_ref[0] = short_conv(x1_ref, 0)
    g_ref[1] = short_conv(x2_ref, 1)
    for o in range(2):
        z = z_ref[...]
        zb_ref[...] = z.astype(BF16)
        nyq = jnp.sum(z * alt, axis=0, keepdims=True) * fn_ref[o]

        def to_freq(c, carry):
            lo = pl.multiple_of(c * kc, kc)
            zr = jnp.dot(cs_ref[pl.ds(lo, kc), :], zb_ref[...], preferred_element_type=F32)
            zs = jnp.dot(cs_ref[pl.ds(L + lo, kc), :], zb_ref[...], preferred_element_type=F32)
            fr = fr_ref[o, pl.ds(lo, kc), :]
            fi = fi_ref[o, pl.ds(lo, kc), :]
            yr_ref[pl.ds(lo, kc), :] = (zr * fr + zs * fi).astype(BF16)
            yi_ref[pl.ds(lo, kc), :] = (zr * fi - zs * fr).astype(BF16)
            return carry

        lax.fori_loop(0, L // kc, to_freq, 0)

        def to_time(c, carry):
            lo = pl.multiple_of(c * kc, kc)
            y = (jnp.dot(cs_ref[pl.ds(lo, kc), :], yr_ref[...], preferred_element_type=F32)
                 - jnp.dot(cs_ref[pl.ds(L + lo, kc), :], yi_ref[...], preferred_element_type=F32))
            y = y + jnp.where(((chunk_row + lo) & 1) == 0, nyq, -nyq)
            z_ref[pl.ds(lo, kc), :] = g_ref[o, pl.ds(lo, kc), :] * (y + skip_ref[o:o + 1, :] * z_ref[pl.ds(lo, kc), :])
            return carry

        lax.fori_loop(0, L // kc, to_time, 0)
    o_ref[...] = z_ref[...].astype(BF16)


def _hyena_conv(up, sw, sb, skip, fr, fi, fn, cs, *, row0, n_batch, L, tc):
    n_ct = D // tc
    base = row0 // L
    u_spec = lambda p: pl.BlockSpec((L, tc), lambda j, b: (base + b, p * n_ct + j))
    once = pl.Buffered(1)
    return pl.pallas_call(
        functools.partial(_hyconv_body, L=L),
        grid=(n_ct, n_batch),
        in_specs=[
            u_spec(0), u_spec(1), u_spec(2),
            pl.BlockSpec((3, 3, tc), lambda j, b: (0, 0, j)),
            pl.BlockSpec((3, tc), lambda j, b: (0, j)),
            pl.BlockSpec((2, tc), lambda j, b: (0, j)),
            pl.BlockSpec((2, L, tc), lambda j, b: (0, 0, j), pipeline_mode=once),
            pl.BlockSpec((2, L, tc), lambda j, b: (0, 0, j), pipeline_mode=once),
            pl.BlockSpec((2, 1, tc), lambda j, b: (0, 0, j)),
            pl.BlockSpec((2 * L, L), lambda j, b: (0, 0), pipeline_mode=once),
        ],
        out_specs=pl.BlockSpec((L, tc), lambda j, b: (b, j)),
        out_shape=jax.ShapeDtypeStruct((n_batch * L, D), BF16),
        scratch_shapes=[
            pltpu.VMEM((L, tc), F32),
            pltpu.VMEM((2, L, tc), F32),
            pltpu.VMEM((L, tc), BF16),
            pltpu.VMEM((L, tc), BF16),
            pltpu.VMEM((L, tc), BF16),
        ],
        compiler_params=_cparams(("arbitrary", "arbitrary")),
        name="hyena_conv",
    )(up, up, up, sw.reshape(3, 3, D), sb.reshape(3, D), skip, fr, fi, fn, cs)


EXP_TILE = 512
SORT_TILES = (2 * N_TOK) // EXP_TILE + N_EXP
SORT_ROWS = SORT_TILES * EXP_TILE
N_QUARTERS = 4
QUARTER_W = D // (2 * N_QUARTERS)
META_I1, META_I2, META_R1, META_R2, META_W1, META_W2 = range(6)
HI_HALF = 0xFFFF0000


def _pack_pairs(x):
    bits = pltpu.bitcast(x.astype(BF16).astype(F32), jnp.uint32)
    return (bits[:, :QUARTER_W] >> 16) | bits[:, QUARTER_W:]


def _unpack_pairs(w):
    return pltpu.bitcast(w << 16, F32), pltpu.bitcast(w & jnp.uint32(HI_HALF), F32)


def _packed_shape(rows):
    return (rows // 8, N_QUARTERS, 8, QUARTER_W)


def _store_packed(ref, x):
    for q in range(N_QUARTERS):
        ref[:, q] = _pack_pairs(x[:, q * 2 * QUARTER_W:(q + 1) * 2 * QUARTER_W]).reshape(-1, 8, QUARTER_W)


def _load_unpacked(ref):
    halves = []
    for q in range(N_QUARTERS):
        halves.extend(_unpack_pairs(ref[:, q].reshape(-1, QUARTER_W)))
    return jnp.concatenate(halves, axis=1)


def _packed_row(ref, tile, sublane):
    return ref.at[tile, :, sublane, :]


def _lane_put(lane, values):
    out = jnp.where(lane == 0, values[0], 0.0)
    for k in range(1, len(values)):
        out = out + jnp.where(lane == k, values[k], 0.0)
    return out


def _route_body(x_ref, ada_ref, gain_ref, wr_hi_ref, wr_lo_ref, br_ref, tri_ref, hq_ref, meta_ref, cnt_ref, carry_ref):
    step = pl.program_id(0)

    @pl.when(step == 0)
    def _():
        carry_ref[...] = jnp.zeros_like(carry_ref)

    h = _norm_mod(x_ref[...], gain_ref, ada_ref, 3)
    _store_packed(hq_ref, h)
    h_hi = h.astype(BF16)
    h_lo = (h - h_hi.astype(F32)).astype(BF16)
    logits = (jnp.dot(h_hi, wr_hi_ref[...], preferred_element_type=F32)
              + jnp.dot(h_lo, wr_hi_ref[...], preferred_element_type=F32)
              + jnp.dot(h_hi, wr_lo_ref[...], preferred_element_type=F32) + br_ref[...])
    lane = lax.broadcasted_iota(jnp.int32, (1, ROUTE_W), 1)
    lane_f = lane.astype(F32)
    group_of_lane = (lane >> 3).astype(F32)
    neg = -jnp.inf
    big = float(ROUTE_W)
    is_g = (lane >= N_EXP) & (lane < N_EXP + N_GROUPS)
    gl = jnp.where(is_g, logits, neg)
    gmax = jnp.max(gl, axis=-1, keepdims=True)
    gidx = jnp.min(jnp.where(gl == gmax, lane_f - N_EXP, big), axis=-1, keepdims=True)
    g_top = 1.0 / jnp.sum(jnp.where(is_g, jnp.exp(gl - gmax), 0.0), axis=-1, keepdims=True)
    in_group = (lane < N_EXP) & (group_of_lane == gidx)
    el = jnp.where(in_group, logits, neg)
    v1 = jnp.max(el, axis=-1, keepdims=True)
    i1 = jnp.min(jnp.where(el == v1, lane_f, big), axis=-1, keepdims=True)
    el2 = jnp.where(lane_f == i1, neg, el)
    v2 = jnp.max(el2, axis=-1, keepdims=True)
    i2 = jnp.min(jnp.where(el2 == v2, lane_f, big), axis=-1, keepdims=True)
    r = jnp.exp(v2 - v1)
    w1 = g_top / (1.0 + r)
    w2 = g_top * r / (1.0 + r)
    sel = jnp.where((lane_f == i1) | (lane_f == i2), 1.0, 0.0)
    rank = jnp.dot(tri_ref[...], sel.astype(BF16), preferred_element_type=F32) + carry_ref[...]
    r1 = jnp.sum(jnp.where(lane_f == i1, rank, 0.0), axis=-1, keepdims=True)
    r2 = jnp.sum(jnp.where(lane_f == i2, rank, 0.0), axis=-1, keepdims=True)
    carry_ref[...] += jnp.sum(sel, axis=0, keepdims=True)
    cnt_ref[...] = carry_ref[...]
    meta_ref[...] = _lane_put(lane, (i1, i2, r1, r2, w1, w2))


@functools.lru_cache(maxsize=None)
def _strict_lower(n):
    r = np.arange(n)
    return (r[None, :] < r[:, None]).astype(np.float32)


def _route(x, ada, gain, wg, bg, we, be, *, tm=512):
    wr = jnp.zeros((D, ROUTE_W), F32).at[:, :N_EXP].set(we.reshape(D, N_EXP)).at[:, N_EXP:N_EXP + N_GROUPS].set(wg)
    br = jnp.zeros((1, ROUTE_W), F32).at[0, :N_EXP].set(be.reshape(N_EXP)).at[0, N_EXP:N_EXP + N_GROUPS].set(bg)
    wr_hi = wr.astype(BF16)
    wr_lo = (wr - wr_hi.astype(F32)).astype(BF16)
    return pl.pallas_call(
        _route_body,
        grid=(N_TOK // tm,),
        in_specs=[
            pl.BlockSpec((tm, D), lambda i: (i, 0)),
            pl.BlockSpec((1, 6, D), lambda i: (_cond_row(i, tm), 0, 0)),
            pl.BlockSpec((1, D), lambda i: (0, 0)),
            pl.BlockSpec((D, ROUTE_W), lambda i: (0, 0)),
            pl.BlockSpec((D, ROUTE_W), lambda i: (0, 0)),
            pl.BlockSpec((1, ROUTE_W), lambda i: (0, 0)),
            pl.BlockSpec((tm, tm), lambda i: (0, 0)),
        ],
        out_specs=[
            pl.BlockSpec(_packed_shape(tm), lambda i: (i, 0, 0, 0)),
            pl.BlockSpec((tm, ROUTE_W), lambda i: (i, 0)),
            pl.BlockSpec((1, ROUTE_W), lambda i: (0, 0)),
        ],
        out_shape=[
            jax.ShapeDtypeStruct(_packed_shape(N_TOK), jnp.uint32),
            jax.ShapeDtypeStruct((N_TOK, ROUTE_W), F32),
            jax.ShapeDtypeStruct((1, ROUTE_W), F32),
        ],
        scratch_shapes=[pltpu.VMEM((1, ROUTE_W), F32)],
        compiler_params=_cparams(("arbitrary",)),
        name="route",
    )(x, ada, gain.reshape(1, D), wr_hi, wr_lo, br, jnp.asarray(_strict_lower(tm)).astype(BF16))


def _slots_body(meta_ref, cnt_ref, upper_ref, pos_ref, plan_ref):
    lane = lax.broadcasted_iota(jnp.int32, (1, ROUTE_W), 1)
    lane_f = lane.astype(F32)
    tiles = jnp.floor((cnt_ref[...] + (EXP_TILE - 1.0)) * (1.0 / EXP_TILE))
    end_tile = jnp.dot(jnp.broadcast_to(tiles, (8, ROUTE_W)).astype(BF16), upper_ref[...],
                       preferred_element_type=F32)[0:1]
    start_row = (end_tile - tiles) * EXP_TILE
    meta = meta_ref[...]
    i1 = meta[:, META_I1:META_I1 + 1]
    i2 = meta[:, META_I2:META_I2 + 1]
    p1 = jnp.sum(jnp.where(lane_f == i1, start_row, 0.0), axis=-1, keepdims=True) + meta[:, META_R1:META_R1 + 1]
    p2 = jnp.sum(jnp.where(lane_f == i2, start_row, 0.0), axis=-1, keepdims=True) + meta[:, META_R2:META_R2 + 1]
    hi1, hi2 = jnp.floor(p1 * 0.125), jnp.floor(p2 * 0.125)
    pos_ref[0] = _lane_put(lane, (hi1, p1 - 8.0 * hi1, hi2, p2 - 8.0 * hi2)).T[:8, :].astype(jnp.int32)

    @pl.when(pl.program_id(0) == 0)
    def _():
        end_col = jnp.broadcast_to(end_tile, (ROUTE_W, ROUTE_W)).T
        expert = lax.broadcasted_iota(jnp.int32, (ROUTE_W, ROUTE_W), 0)
        tile = lax.broadcasted_iota(jnp.int32, (ROUTE_W, ROUTE_W), 1).astype(F32)
        passed = jnp.where((end_col <= tile) & (expert < N_EXP), 1.0, 0.0)
        tile_expert = jnp.minimum(jnp.sum(passed, axis=0, keepdims=True), N_EXP - 1.0)
        used = jnp.max(end_tile, axis=-1, keepdims=True)
        row = lax.broadcasted_iota(jnp.int32, (8, ROUTE_W), 0)
        plan_ref[...] = jnp.where(row == 0, tile_expert, jnp.where(row == 1, used, 0.0)).astype(jnp.int32)


@functools.lru_cache(maxsize=None)
def _upper_incl(n):
    r = np.arange(n)
    return (r[:, None] <= r[None, :]).astype(np.float32)


def _slots(meta, cnt, *, tm=2048):
    n_tiles = N_TOK // tm
    return pl.pallas_call(
        _slots_body,
        grid=(n_tiles,),
        in_specs=[
            pl.BlockSpec((tm, ROUTE_W), lambda i: (i, 0)),
            pl.BlockSpec((1, ROUTE_W), lambda i: (0, 0)),
            pl.BlockSpec((ROUTE_W, ROUTE_W), lambda i: (0, 0)),
        ],
        out_specs=[
            pl.BlockSpec((1, 8, tm), lambda i: (i, 0, 0)),
            pl.BlockSpec((8, ROUTE_W), lambda i: (0, 0)),
        ],
        out_shape=[
            jax.ShapeDtypeStruct((n_tiles, 8, tm), jnp.int32),
            jax.ShapeDtypeStruct((8, ROUTE_W), jnp.int32),
        ],
        compiler_params=_cparams(("arbitrary",)),
        name="moe_slots",
    )(meta, cnt, jnp.asarray(_upper_incl(ROUTE_W)).astype(BF16))


ROW_UNROLL = 8
N_DMA_LANES = 2


def _start_row_copies(tm, make_copy):
    def block(blk, carry):
        for j in range(ROW_UNROLL):
            lane = j % N_DMA_LANES
            for k in range(2):
                make_copy(blk, j, k, lane).start(priority=lane)
        return carry

    lax.fori_loop(0, tm // ROW_UNROLL, block, 0)


def _drain_row_copies(like_src, like_dst, sem):
    for lane in range(N_DMA_LANES):
        pltpu.make_async_copy(like_src, like_dst, sem.at[lane]).wait()


def _dispatch_body(t1_ref, s1_ref, t2_ref, s2_ref, hq_ref, xs_in_ref, xs_ref, sem, *, tm):
    del xs_in_ref
    slots = ((t1_ref, s1_ref), (t2_ref, s2_ref))

    def copy(blk, j, k, lane):
        i = blk * ROW_UNROLL + j
        return pltpu.make_async_copy(_packed_row(hq_ref, blk, j), _packed_row(xs_ref, slots[k][0][i], slots[k][1][i]),
                                     sem.at[lane])

    _start_row_copies(tm, copy)
    _drain_row_copies(hq_ref, xs_ref.at[pl.ds(0, tm // 8)], sem)


def _slot_specs(tm, ahead=0):
    last = N_TOK // tm - 1
    return [pl.BlockSpec((tm,), lambda t: (jnp.minimum(t + ahead, last),), memory_space=pltpu.SMEM) for _ in range(4)]


def _dispatch(slots, hq, init, *, tm=512):
    return pl.pallas_call(
        functools.partial(_dispatch_body, tm=tm),
        grid=(N_TOK // tm,),
        in_specs=_slot_specs(tm) + [
            pl.BlockSpec(_packed_shape(tm), lambda t: (t, 0, 0, 0)),
            pl.BlockSpec(memory_space=pl.ANY),
        ],
        out_specs=pl.BlockSpec(memory_space=pl.ANY),
        out_shape=jax.ShapeDtypeStruct(_packed_shape(SORT_ROWS), jnp.uint32),
        input_output_aliases={5: 0},
        scratch_shapes=[pltpu.SemaphoreType.DMA((N_DMA_LANES,))],
        compiler_params=_cparams(("arbitrary",)),
        name="moe_dispatch",
    )(*slots, hq, init)


def _experts_body(plan_ref, xs_ref, wg_ref, wu_ref, wd_ref, ys_ref, wgu_ref, wdn_ref):
    t = pl.program_id(0)
    in_use = t < plan_ref[1, 0]
    new_expert = (t == 0) | (plan_ref[0, t] != plan_ref[0, jnp.maximum(t - 1, 0)])

    @pl.when(in_use & new_expert)
    def _():
        wgu_ref[:, :EXP_H] = wg_ref[0, 0].astype(BF16)
        wgu_ref[:, EXP_H:] = wu_ref[0, 0].astype(BF16)
        wdn_ref[...] = wd_ref[0, 0].astype(BF16)

    @pl.when(in_use)
    def _():
        x = _load_unpacked(xs_ref).astype(BF16)
        ab = jnp.dot(x, wgu_ref[...], preferred_element_type=F32)
        a, b = ab[:, :EXP_H], ab[:, EXP_H:]
        act = a * _sigmoid(a) * b
        _store_packed(ys_ref, jnp.dot(act.astype(BF16), wdn_ref[...], preferred_element_type=F32))


def _experts(plan, xs, w_gate, w_up, w_down, *, layer):
    def tile_idx(t, plan):
        return (jnp.minimum(t, jnp.maximum(plan[1, 0] - 1, 0)), 0, 0, 0)

    grid_spec = pltpu.PrefetchScalarGridSpec(
        num_scalar_prefetch=1,
        grid=(SORT_TILES,),
        in_specs=[
            pl.BlockSpec(_packed_shape(EXP_TILE), tile_idx),
            pl.BlockSpec((1, 1, D, EXP_H), lambda t, plan: (layer, plan[0, t], 0, 0)),
            pl.BlockSpec((1, 1, D, EXP_H), lambda t, plan: (layer, plan[0, t], 0, 0)),
            pl.BlockSpec((1, 1, EXP_H, D), lambda t, plan: (layer, plan[0, t], 0, 0)),
        ],
        out_specs=pl.BlockSpec(_packed_shape(EXP_TILE), tile_idx),
        scratch_shapes=[pltpu.VMEM((D, 2 * EXP_H), BF16), pltpu.VMEM((EXP_H, D), BF16)],
    )
    return pl.pallas_call(
        _experts_body,
        grid_spec=grid_spec,
        out_shape=jax.ShapeDtypeStruct(_packed_shape(SORT_ROWS), jnp.uint32),
        input_output_aliases={1: 0},
        compiler_params=_cparams(("arbitrary",)),
        name="moe_experts",
    )(plan, xs, w_gate, w_up, w_down)


def _combine_body(*refs, tm, final):
    cur_slots, nxt_slots = refs[0:4], refs[4:8]
    meta_ref, x_ref, ada_ref = refs[8:11]
    rest = refs[11:]
    if final:
        fgain_ref, ys_ref, oc_ref, ol_ref, y_ref, sem = rest
    else:
        ys_ref, o_ref, y_ref, sem = rest
    step = pl.program_id(0)
    n_steps = pl.num_programs(0)

    def fetch(slot_refs, buf):
        slots = ((slot_refs[0], slot_refs[1]), (slot_refs[2], slot_refs[3]))

        def copy(blk, j, k, lane):
            i = blk * ROW_UNROLL + j
            return pltpu.make_async_copy(_packed_row(ys_ref, slots[k][0][i], slots[k][1][i]),
                                         _packed_row(y_ref.at[buf, k], blk, j), sem.at[buf, lane])

        _start_row_copies(tm, copy)

    @pl.when(step == 0)
    def _():
        fetch(cur_slots, 0)

    @pl.when(step + 1 < n_steps)
    def _():
        fetch(nxt_slots, (step + 1) % 2)

    buf = step % 2
    _drain_row_copies(ys_ref.at[pl.ds(0, tm // 8)], y_ref.at[buf, 0], sem.at[buf])
    y_ref = y_ref.at[buf]
    meta = meta_ref[...]
    mix = (meta[:, META_W1:META_W1 + 1] * _load_unpacked(y_ref.at[0])
           + meta[:, META_W2:META_W2 + 1] * _load_unpacked(y_ref.at[1]))
    x = x_ref[...] + ada_ref[0, 5:6, :] * mix
    if not final:
        o_ref[...] = x
        return
    ms = jnp.mean(x * x, axis=-1, keepdims=True)
    y = x * lax.rsqrt(ms + EPS) * fgain_ref[...]
    is_ctx = pl.program_id(0) < N_CTX // tm

    @pl.when(is_ctx)
    def _():
        oc_ref[...] = y

    @pl.when(jnp.logical_not(is_ctx))
    def _():
        ol_ref[...] = y


def _combine(slots, ys, meta, x, ada, final_gain=None, *, tm=512):
    final = final_gain is not None
    extra_in = [pl.BlockSpec((1, D), lambda t: (0, 0))] if final else []
    extra_args = [final_gain.reshape(1, D)] if final else []
    if final:
        out_specs = _split_specs(tm, D)
        out_shape = [jax.ShapeDtypeStruct((N_CTX, D), F32), jax.ShapeDtypeStruct((N_LAT, D), F32)]
    else:
        out_specs = pl.BlockSpec((tm, D), lambda t: (t, 0))
        out_shape = jax.ShapeDtypeStruct((N_TOK, D), F32)
    return pl.pallas_call(
        functools.partial(_combine_body, tm=tm, final=final),
        grid=(N_TOK // tm,),
        in_specs=_slot_specs(tm) + _slot_specs(tm, ahead=1) + [
            pl.BlockSpec((tm, ROUTE_W), lambda t: (t, 0)),
            pl.BlockSpec((tm, D), lambda t: (t, 0)),
            pl.BlockSpec((1, 6, D), lambda t: (_cond_row(t, tm), 0, 0)),
        ] + extra_in + [pl.BlockSpec(memory_space=pl.ANY)],
        out_specs=out_specs,
        out_shape=out_shape,
        scratch_shapes=[pltpu.VMEM((2, 2) + _packed_shape(tm), jnp.uint32),
                        pltpu.SemaphoreType.DMA((2, N_DMA_LANES))],
        compiler_params=_cparams(("arbitrary",)),
        name="moe_combine",
    )(*slots, *slots, meta, x, ada, *extra_args, ys)


def _moe(x, ada, gain, wg, bg, we, be, w_gate, w_up, w_down, *, layer, sort_init, final_gain=None):
    hq, meta, cnt = _route(x, ada, gain, wg, bg, we, be)
    pos, plan = _slots(meta, cnt)
    slots = [pos[:, r, :].reshape(N_TOK) for r in range(4)]
    xs = _dispatch(slots, hq, sort_init)
    ys = _experts(plan, xs, w_gate, w_up, w_down, layer=layer)
    return _combine(slots, ys, meta, x, ada, final_gain), ys


def kernel(x_prompt, x_sample, cache_k, cache_v, c, c_ctx, w_ada, b_ada, norm_mix, norm_ffn, attn_w_q, attn_w_kv, attn_q_norm, attn_k_norm, attn_w_o, hy_w_in, hy_b_in, hy_short_w, hy_short_b, hy_filt_w1, hy_filt_b1, hy_filt_w2, hy_filt_b2, hy_filt_w3, hy_filt_b3, hy_filt_freq, hy_log_decay, hy_skip, hy_w_out, hy_b_out, router_group_w, router_group_b, router_expert_w, router_expert_b, moe_w_gate, moe_w_up, moe_w_down, final_norm):
    depth = w_ada.shape[0]
    x_parts = (x_prompt.reshape(N_CTX, D), x_sample.reshape(N_LAT, D))
    cond8 = jnp.zeros((8, D), F32).at[0].set(c_ctx).at[1:1 + N_BATCH_LAT].set(c)
    ada_all = _ada(cond8, w_ada, b_ada).reshape(depth, 8, 6, D)
    new_k = []
    new_v = []
    sort_buf = jnp.zeros(_packed_shape(SORT_ROWS), jnp.uint32)
    for l in range(depth):
        ada = ada_all[l]
        if l % 2 == 0:
            a = l // 2
            w_qkv = jnp.concatenate([attn_w_q[a], attn_w_kv[a]], axis=1).astype(BF16)
            q, k_ctx, v_ctx, kc, vc, kl, vl = _qkv(x_parts, ada, norm_mix[l], w_qkv, attn_q_norm[a], attn_k_norm[a])
            new_k.append(k_ctx)
            new_v.append(v_ctx)
            o_ctx = _attention(q, [kc], [vc], row0=0, n_batch=N_BATCH_CTX, seq=L_CTX, tq=L_CTX,
                               n_kv=N_KV, stack=Q_PER_KV)
            o_lat = _attention(q, [kl, _lane_tile_heads(cache_k[:, a])], [vl, _lane_tile_heads(cache_v[:, a])],
                               row0=N_CTX, n_batch=N_BATCH_LAT, seq=L_LAT, tq=256, n_kv=1, stack=1)
            x = _resid_proj(o_ctx, o_lat, attn_w_o[a].astype(BF16), jnp.zeros((D,), F32), x_parts, ada, gpart=2)
        else:
            j = l // 2
            (x,) = x_parts
            up = _norm_proj(x, ada, norm_mix[l], hy_w_in[j].astype(BF16), hy_b_in[j], part=0)
            outs = []
            for row0, n_batch, L, tc in ((0, N_BATCH_CTX, L_CTX, D), (N_CTX, N_BATCH_LAT, L_LAT, 256)):
                cs = jnp.asarray(_dft_table(L)).astype(BF16)
                fr, fi, fn = _filter_spectra(L, cs, hy_filt_w1[j], hy_filt_b1[j], hy_filt_w2[j], hy_filt_b2[j],
                                             hy_filt_w3[j], hy_filt_b3[j], hy_filt_freq[j], hy_log_decay[j])
                outs.append(_hyena_conv(up, hy_short_w[j], hy_short_b[j], hy_skip[j], fr, fi, fn, cs,
                                        row0=row0, n_batch=n_batch, L=L, tc=tc))
            x = _resid_proj(outs[0], outs[1], hy_w_out[j].astype(BF16), hy_b_out[j], x_parts, ada, gpart=2)
        out, sort_buf = _moe(x, ada, norm_ffn[l], router_group_w[l], router_group_b[l], router_expert_w[l],
                             router_expert_b[l], moe_w_gate, moe_w_up, moe_w_down, layer=l, sort_init=sort_buf,
                             final_gain=final_norm if l == depth - 1 else None)
        x_parts = (out,)
    y_ctx, y_lat = x_parts[0]
    return (y_ctx.reshape(N_BATCH_CTX, L_CTX, D), y_lat.reshape(N_BATCH_LAT, L_LAT, D),
            jnp.stack(new_k, axis=1), jnp.stack(new_v, axis=1))
```

```python
import functools
import math

import numpy as np
import jax
import jax.numpy as jnp
from jax import lax
from jax.experimental import pallas as pl
from jax.experimental.pallas import tpu as pltpu

F32 = jnp.float32
BF16 = jnp.bfloat16

D = 1024
N_BATCH_CTX = 32
L_CTX = 256
N_BATCH_LAT = 2
L_LAT = 2048
PAST = 256
N_CTX = N_BATCH_CTX * L_CTX
N_LAT = N_BATCH_LAT * L_LAT
N_TOK = N_CTX + N_LAT
GRID_W = 64
N_HEADS = 16
N_KV = 4
DH = 64
Q_PER_KV = N_HEADS // N_KV
KV_W = N_KV * DH
ROPE_THETA = 10000.0
FILTER_FEAT = 17
FEAT_PAD = 32
FILTER_HIDDEN = 64
N_GROUPS = 4
E_PER_G = 8
N_EXP = N_GROUPS * E_PER_G
EXP_H = D // 4
EPS = 1e-6
ROUTE_W = 128
VMEM_LIMIT = 56 * 1024 * 1024


def _cparams(sem):
    return pltpu.CompilerParams(dimension_semantics=sem, vmem_limit_bytes=VMEM_LIMIT)


def _cond_row(i, tm):
    n_ctx_tiles = N_CTX // tm
    return jnp.where(i < n_ctx_tiles, 0, 1 + (i - n_ctx_tiles) // (L_LAT // tm))


def _split_specs(tm, width):
    n_ctx_tiles = N_CTX // tm
    return [pl.BlockSpec((tm, width), lambda i: (jnp.minimum(i, n_ctx_tiles - 1), 0)),
            pl.BlockSpec((tm, width), lambda i: (jnp.maximum(i - n_ctx_tiles, 0), 0))]


def _pick_split(i, tm, ctx_ref, lat_ref):
    return jnp.where(i < N_CTX // tm, ctx_ref[...], lat_ref[...])


def _sigmoid(x):
    return 1.0 / (1.0 + jnp.exp(-x))


@functools.lru_cache(maxsize=None)
def _dft_table(L):
    k = np.arange(L, dtype=np.int64)
    ang = (np.outer(k, k) % (2 * L)).astype(np.float64) * (math.pi / L)
    return np.concatenate([np.cos(ang), np.sin(ang)], axis=0).astype(np.float32)


@functools.lru_cache(maxsize=None)
def _filter_feats(L):
    t = np.linspace(0.0, 1.0, L, dtype=np.float64)[:, None]
    bands = np.linspace(1e-4, 7.0, 8, dtype=np.float64)[None, :]
    w = (2.0 * math.pi) * np.arange(L, dtype=np.float64)[:, None] / L
    feats = np.concatenate([t, np.cos(bands * w), -np.sin(bands * w)], axis=-1)
    out = np.zeros((L, FEAT_PAD), np.float32)
    out[:, :FILTER_FEAT] = feats
    return out


@functools.lru_cache(maxsize=None)
def _rope_tables(tm):
    pos = np.arange(L_LAT)
    row = (pos // GRID_W).astype(np.float64)
    col = (pos % GRID_W).astype(np.float64)
    axis_dim = DH // 2
    inv_freq = ROPE_THETA ** (-np.arange(0, axis_dim, 2, dtype=np.float64) / axis_dim)
    lane = np.arange(KV_W)
    d = lane % DH
    is_col = (d // axis_dim) == 1
    fi = d % (axis_dim // 2)
    first_half = (d % axis_dim) < (axis_dim // 2)
    p = np.where(is_col[None, :], col[:, None], row[:, None])
    ang = p * inv_freq[fi][None, :]
    cos = np.cos(ang)
    sin = np.sin(ang) * np.where(first_half, -1.0, 1.0)[None, :]
    cos = np.concatenate([np.ones((tm, KV_W)), cos], axis=0).astype(np.float32)
    sin = np.concatenate([np.zeros((tm, KV_W)), sin], axis=0).astype(np.float32)
    return cos, sin


@functools.lru_cache(maxsize=None)
def _head_sum_matrix():
    lane = np.arange(KV_W)
    return (lane[:, None] // DH == lane[None, :] // DH).astype(np.float32)


def _ada_body(c_ref, w_ref, b_ref, o_ref):
    c = c_ref[...]
    s = c * _sigmoid(c)
    o_ref[0] = jnp.dot(s.astype(BF16), w_ref[0].astype(BF16), preferred_element_type=F32) + b_ref[0]


def _ada(cond8, w_ada, b_ada):
    depth = w_ada.shape[0]
    tn = 1536
    return pl.pallas_call(
        _ada_body,
        grid=(depth, 6 * D // tn),
        in_specs=[
            pl.BlockSpec((8, D), lambda l, j: (0, 0)),
            pl.BlockSpec((1, D, tn), lambda l, j: (l, 0, j)),
            pl.BlockSpec((1, 1, tn), lambda l, j: (l, 0, j)),
        ],
        out_specs=pl.BlockSpec((1, 8, tn), lambda l, j: (l, 0, j)),
        out_shape=jax.ShapeDtypeStruct((depth, 8, 6 * D), F32),
        compiler_params=_cparams(("arbitrary", "arbitrary")),
        name="ada",
    )(cond8, w_ada, b_ada.reshape(depth, 1, 6 * D))


def _norm_mod(x, gain_ref, ada_ref, part):
    ms = jnp.mean(x * x, axis=-1, keepdims=True)
    y = x * lax.rsqrt(ms + EPS) * gain_ref[...]
    return y * (1.0 + ada_ref[0, part + 1:part + 2, :]) + ada_ref[0, part:part + 1, :]


def _proj_body(x_ref, ada_ref, gain_ref, w_ref, b_ref, o_ref, *, part):
    h = _norm_mod(x_ref[...], gain_ref, ada_ref, part)
    acc = jnp.dot(h.astype(BF16), w_ref[...], preferred_element_type=F32) + b_ref[...]
    o_ref[...] = acc.astype(o_ref.dtype)


def _norm_proj(x, ada, gain, w, b, *, part, tm=1024):
    n_out = w.shape[1]
    return pl.pallas_call(
        functools.partial(_proj_body, part=part),
        grid=(N_TOK // tm,),
        in_specs=[
            pl.BlockSpec((tm, D), lambda i: (i, 0)),
            pl.BlockSpec((1, 6, D), lambda i: (_cond_row(i, tm), 0, 0)),
            pl.BlockSpec((1, D), lambda i: (0, 0)),
            pl.BlockSpec((D, n_out), lambda i: (0, 0)),
            pl.BlockSpec((1, n_out), lambda i: (0, 0)),
        ],
        out_specs=pl.BlockSpec((tm, n_out), lambda i: (i, 0)),
        out_shape=jax.ShapeDtypeStruct((N_TOK, n_out), BF16),
        compiler_params=_cparams(("arbitrary",)),
        name="norm_proj",
    )(x, ada, gain.reshape(1, D), w, b.reshape(1, n_out))


def _head_rms_rope(x, gain, hs, cos, sin, lane):
    ss = jnp.dot((x * x).astype(BF16), hs, preferred_element_type=F32)
    xn = x * lax.rsqrt(ss * (1.0 / DH) + EPS) * gain
    fwd = pltpu.roll(xn, KV_W - DH // 4, 1)
    bwd = pltpu.roll(xn, DH // 4, 1)
    partner = jnp.where((lane & (DH // 4)) == 0, fwd, bwd)
    return xn * cos + partner * sin


@functools.lru_cache(maxsize=None)
def _lane_tile_matrix():
    col = np.arange(N_KV * KV_W)
    src = (col // KV_W) * DH + col % DH
    return (np.arange(KV_W)[:, None] == src[None, :]).astype(np.float32)


def _qkv_body(*refs, tm, split_x):
    n_x = 2 if split_x else 1
    x_refs = refs[:n_x]
    (ada_ref, gain_ref, w_ref, qg_ref, kg_ref, hs_ref, cos_ref, sin_ref, tile_ref, tile_t_ref,
     q_ref, newk_ref, newv_ref, kc_ref, vc_ref, kl_ref, vl_ref) = refs[n_x:]
    step = pl.program_id(0)
    x = _pick_split(step, tm, *x_refs) if split_x else x_refs[0][...]
    h = _norm_mod(x, gain_ref, ada_ref, 0)
    acc = jnp.dot(h.astype(BF16), w_ref[...], preferred_element_type=F32)
    hs = hs_ref[...]
    cos = cos_ref[...]
    sin = sin_ref[...]
    lane = lax.broadcasted_iota(jnp.int32, (1, KV_W), 1)
    for c in range(N_KV):
        qc = _head_rms_rope(acc[:, c * KV_W:(c + 1) * KV_W], qg_ref[...], hs, cos, sin, lane)
        q_ref[:, c * KV_W:(c + 1) * KV_W] = (qc * (DH ** -0.5)).astype(BF16)
    k = _head_rms_rope(acc[:, D:D + KV_W], kg_ref[...], hs, cos, sin, lane)
    v = acc[:, D + KV_W:D + 2 * KV_W]
    k4t = lax.dot_general(tile_t_ref[...], k.astype(BF16), (((1,), (1,)), ((), ())),
                          preferred_element_type=F32).astype(BF16)
    v4 = jnp.dot(v.astype(BF16), tile_ref[...], preferred_element_type=F32).astype(BF16)
    is_ctx = step < N_CTX // tm

    @pl.when(is_ctx)
    def _():
        for bb in range(tm // L_CTX):
            rows = slice(bb * L_CTX, (bb + 1) * L_CTX)
            for hd in range(N_KV):
                newk_ref[bb, hd] = k[rows, hd * DH:(hd + 1) * DH]
                newv_ref[bb, hd] = v[rows, hd * DH:(hd + 1) * DH]
                kc_ref[bb, hd] = k4t[hd * KV_W:(hd + 1) * KV_W, rows]
                vc_ref[bb, hd] = v4[rows, hd * KV_W:(hd + 1) * KV_W]

    @pl.when(jnp.logical_not(is_ctx))
    def _():
        for hd in range(N_KV):
            kl_ref[0, hd] = k4t[hd * KV_W:(hd + 1) * KV_W, :]
            vl_ref[0, hd] = v4[:, hd * KV_W:(hd + 1) * KV_W]


def _qkv(x_parts, ada, gain, w_qkv, q_gain, k_gain):
    tm = 2 * L_CTX
    per_tile = tm // L_CTX
    cos, sin = _rope_tables(tm)
    split_x = len(x_parts) == 2
    x_specs = _split_specs(tm, D) if split_x else [pl.BlockSpec((tm, D), lambda i: (i, 0))]
    n_ctx_tiles = N_CTX // tm
    lat_tiles = L_LAT // tm

    def rope_idx(i):
        return (jnp.where(i < n_ctx_tiles, 0, 1 + (i - n_ctx_tiles) % lat_tiles), 0)

    def ctx_idx(i):
        return (jnp.minimum(i, n_ctx_tiles - 1), 0, 0, 0)

    def lat_idx(i):
        j = jnp.maximum(i - n_ctx_tiles, 0)
        return (j // lat_tiles, 0, j % lat_tiles, 0)

    def lat_idx_t(i):
        j = jnp.maximum(i - n_ctx_tiles, 0)
        return (j // lat_tiles, 0, 0, j % lat_tiles)

    const = lambda i: (0, 0)
    kv_ctx = jax.ShapeDtypeStruct((N_BATCH_CTX, N_KV, L_CTX, DH), F32)
    tiled_ctx = jax.ShapeDtypeStruct((N_BATCH_CTX, N_KV, L_CTX, KV_W), BF16)
    tiled_lat = jax.ShapeDtypeStruct((N_BATCH_LAT, N_KV, L_LAT, KV_W), BF16)
    keys_ctx = jax.ShapeDtypeStruct((N_BATCH_CTX, N_KV, KV_W, L_CTX), BF16)
    keys_lat = jax.ShapeDtypeStruct((N_BATCH_LAT, N_KV, KV_W, L_LAT), BF16)
    return pl.pallas_call(
        functools.partial(_qkv_body, tm=tm, split_x=split_x),
        grid=(N_TOK // tm,),
        in_specs=x_specs + [
            pl.BlockSpec((1, 6, D), lambda i: (_cond_row(i, tm), 0, 0)),
            pl.BlockSpec((1, D), const),
            pl.BlockSpec((D, D + 2 * KV_W), const),
            pl.BlockSpec((1, KV_W), const),
            pl.BlockSpec((1, KV_W), const),
            pl.BlockSpec((KV_W, KV_W), const),
            pl.BlockSpec((tm, KV_W), rope_idx),
            pl.BlockSpec((tm, KV_W), rope_idx),
            pl.BlockSpec((KV_W, N_KV * KV_W), const),
            pl.BlockSpec((N_KV * KV_W, KV_W), const),
        ],
        out_specs=[
            pl.BlockSpec((tm, D), lambda i: (i, 0)),
            pl.BlockSpec((per_tile, N_KV, L_CTX, DH), ctx_idx),
            pl.BlockSpec((per_tile, N_KV, L_CTX, DH), ctx_idx),
            pl.BlockSpec((per_tile, N_KV, KV_W, L_CTX), ctx_idx),
            pl.BlockSpec((per_tile, N_KV, L_CTX, KV_W), ctx_idx),
            pl.BlockSpec((1, N_KV, KV_W, tm), lat_idx_t),
            pl.BlockSpec((1, N_KV, tm, KV_W), lat_idx),
        ],
        out_shape=[jax.ShapeDtypeStruct((N_TOK, D), BF16), kv_ctx, kv_ctx, keys_ctx, tiled_ctx, keys_lat, tiled_lat],
        compiler_params=_cparams(("arbitrary",)),
        name="qkv",
    )(*x_parts, ada, gain.reshape(1, D), w_qkv,
      jnp.tile(q_gain, Q_PER_KV).reshape(1, KV_W), jnp.tile(k_gain, N_KV).reshape(1, KV_W),
      jnp.asarray(_head_sum_matrix()).astype(BF16), jnp.asarray(cos), jnp.asarray(sin),
      jnp.asarray(_lane_tile_matrix()).astype(BF16), jnp.asarray(_lane_tile_matrix().T.copy()).astype(BF16))


def _attn_body(*refs, n_kv, stack, n_seg):
    q_ref, k_refs, v_refs, o_ref = refs[0], refs[1:1 + n_seg], refs[1 + n_seg:1 + 2 * n_seg], refs[-1]
    lane = lax.broadcasted_iota(jnp.int32, (1, KV_W), 1)
    masks = [(lane >> 6) == g for g in range(Q_PER_KV)]
    tq = q_ref.shape[0]
    for kv in range(n_kv):
        q = q_ref[:, kv * KV_W:(kv + 1) * KV_W]
        out = jnp.zeros((tq, KV_W), F32)
        for c in range(0, Q_PER_KV, stack):
            pair = masks[c:c + stack]
            ones_blk = (c + stack) % Q_PER_KV if stack < Q_PER_KV else None
            stacked = jnp.concatenate([jnp.where(m, q, jnp.zeros_like(q)) for m in pair], axis=0)
            scores = [jnp.dot(stacked, k_ref[0, kv], preferred_element_type=F32) for k_ref in k_refs]
            top = functools.reduce(jnp.maximum, [jnp.max(s, axis=-1, keepdims=True) for s in scores])
            probs = [jnp.exp((s - top).astype(BF16)) for s in scores]
            if ones_blk is None:
                vals = [v_ref[0, kv] for v_ref in v_refs]
            else:
                vals = [jnp.where(masks[ones_blk], jnp.ones((), BF16), v_ref[0, kv]) for v_ref in v_refs]
            og = sum(jnp.dot(p, v, preferred_element_type=F32) for p, v in zip(probs, vals))
            if ones_blk is None:
                denom = sum(jnp.sum(p.astype(F32), axis=-1, keepdims=True) for p in probs)
            else:
                denom = og[:, ones_blk * DH:ones_blk * DH + 1]
            og = og * (1.0 / denom)
            for g, m in enumerate(pair):
                out = out + jnp.where(m, og[g * tq:(g + 1) * tq], 0.0)
        o_ref[:, kv * KV_W:(kv + 1) * KV_W] = out.astype(BF16)


def _attention(q, keys, values, *, row0, n_batch, seq, tq, n_kv, stack):
    per_b = seq // tq
    base = row0 // tq
    kv_specs = [pl.BlockSpec((1, n_kv) + a.shape[2:], lambda b, h, i: (b, h, 0, 0)) for a in keys + values]
    return pl.pallas_call(
        functools.partial(_attn_body, n_kv=n_kv, stack=stack, n_seg=len(keys)),
        grid=(n_batch, N_KV // n_kv, per_b),
        in_specs=[pl.BlockSpec((tq, n_kv * KV_W), lambda b, h, i: (base + b * per_b + i, h))] + kv_specs,
        out_specs=pl.BlockSpec((tq, n_kv * KV_W), lambda b, h, i: (b * per_b + i, h)),
        out_shape=jax.ShapeDtypeStruct((n_batch * seq, D), BF16),
        compiler_params=_cparams(("arbitrary", "arbitrary", "arbitrary")),
        name="attention",
    )(q, *keys, *values)


def _lane_tile_heads(x):
    return jnp.tile(x.astype(BF16), (1, 1, 1, KV_W // DH))


def _resid_body(*refs, gpart, tm, split_x):
    ac_ref, al_ref, w_ref, b_ref, ada_ref = refs[:5]
    x_refs, o_ref = refs[5:-1], refs[-1]
    i = pl.program_id(0)
    a = _pick_split(i, tm, ac_ref, al_ref)
    x = _pick_split(i, tm, *x_refs) if split_x else x_refs[0][...]
    acc = jnp.dot(a, w_ref[...], preferred_element_type=F32) + b_ref[...]
    o_ref[...] = x + ada_ref[0, gpart:gpart + 1, :] * acc


def _resid_proj(a_ctx, a_lat, w, b, x_parts, ada, *, gpart, tm=1024):
    k = a_ctx.shape[1]
    split_x = len(x_parts) == 2
    x_specs = _split_specs(tm, D) if split_x else [pl.BlockSpec((tm, D), lambda i: (i, 0))]
    return pl.pallas_call(
        functools.partial(_resid_body, gpart=gpart, tm=tm, split_x=split_x),
        grid=(N_TOK // tm,),
        in_specs=_split_specs(tm, k) + [
            pl.BlockSpec((k, D), lambda i: (0, 0)),
            pl.BlockSpec((1, D), lambda i: (0, 0)),
            pl.BlockSpec((1, 6, D), lambda i: (_cond_row(i, tm), 0, 0)),
        ] + x_specs,
        out_specs=pl.BlockSpec((tm, D), lambda i: (i, 0)),
        out_shape=jax.ShapeDtypeStruct((N_TOK, D), F32),
        compiler_params=_cparams(("arbitrary",)),
        name="resid_proj",
    )(a_ctx, a_lat, w, b.reshape(1, D), ada, *x_parts)


def _alt_sign(L):
    row = lax.broadcasted_iota(jnp.int32, (L, 1), 0)
    return row, jnp.where((row & 1) == 0, 1.0, -1.0)


def _split_dot(a, b):
    a_hi, b_hi = a.astype(BF16), b.astype(BF16)
    a_lo = (a - a_hi.astype(F32)).astype(BF16)
    b_lo = (b - b_hi.astype(F32)).astype(BF16)
    return (jnp.dot(a_hi, b_hi, preferred_element_type=F32) + jnp.dot(a_lo, b_hi, preferred_element_type=F32)
            + jnp.dot(a_hi, b_lo, preferred_element_type=F32))


def _filter_body(feat_ref, w1_ref, b1_ref, w2_ref, b2_ref, fq_ref, w3_ref, b3_ref, ld_ref, cs_ref,
                 fr_ref, fi_ref, fn_ref, hid_ref, *, L):
    feats = feat_ref[...]

    @pl.when(pl.program_id(0) == 0)
    def _():
        fq = fq_ref[...]
        h1 = jnp.sin(fq * (_split_dot(feats, w1_ref[...]) + b1_ref[...]))
        hid_ref[...] = jnp.sin(fq * (_split_dot(h1, w2_ref[...]) + b2_ref[...])).astype(BF16)

    t = feats[:, 0:1]
    row, alt = _alt_sign(L)
    filt = []
    for j in range(4):
        raw = jnp.dot(hid_ref[...], w3_ref[j].astype(BF16), preferred_element_type=F32) + b3_ref[j:j + 1, :]
        filt.append(raw * jnp.exp(-t * jnp.exp(ld_ref[j:j + 1, :])))
    for o in range(2):
        hf, hb = filt[2 * o], filt[2 * o + 1]
        l1 = jnp.sum(jnp.abs(hf), axis=0, keepdims=True) + jnp.sum(jnp.abs(hb), axis=0, keepdims=True)
        inv = 1.0 / (l1 + EPS)
        sym = (hf + hb) * inv
        asym = (hb - hf) * inv
        fr = jnp.dot(cs_ref[0:L, :], sym.astype(BF16), preferred_element_type=F32) * (1.0 / L)
        fr_ref[o] = jnp.where(row == 0, 0.5 * fr, fr)
        fi_ref[o] = jnp.dot(cs_ref[L:2 * L, :], asym.astype(BF16), preferred_element_type=F32) * (1.0 / L)
        fn_ref[o] = jnp.sum(sym * alt, axis=0, keepdims=True) * (0.5 / L)


def _filter_spectra(L, cs, w1, b1, w2, b2, w3, b3, freq, log_decay, *, tc=256):
    w1p = jnp.zeros((FEAT_PAD, FILTER_HIDDEN), F32).at[:FILTER_FEAT].set(w1)
    w3r = w3.reshape(FILTER_HIDDEN, 4, D).transpose(1, 0, 2)
    const = lambda j: (0, 0)
    return pl.pallas_call(
        functools.partial(_filter_body, L=L),
        grid=(D // tc,),
        in_specs=[
            pl.BlockSpec((L, FEAT_PAD), const),
            pl.BlockSpec((FEAT_PAD, FILTER_HIDDEN), const),
            pl.BlockSpec((1, FILTER_HIDDEN), const),
            pl.BlockSpec((FILTER_HIDDEN, FILTER_HIDDEN), const),
            pl.BlockSpec((1, FILTER_HIDDEN), const),
            pl.BlockSpec((1, FILTER_HIDDEN), const),
            pl.BlockSpec((4, FILTER_HIDDEN, tc), lambda j: (0, 0, j)),
            pl.BlockSpec((4, tc), lambda j: (0, j)),
            pl.BlockSpec((4, tc), lambda j: (0, j)),
            pl.BlockSpec((2 * L, L), const, pipeline_mode=pl.Buffered(1)),
        ],
        out_specs=[
            pl.BlockSpec((2, L, tc), lambda j: (0, 0, j)),
            pl.BlockSpec((2, L, tc), lambda j: (0, 0, j)),
            pl.BlockSpec((2, 1, tc), lambda j: (0, 0, j)),
        ],
        out_shape=[
            jax.ShapeDtypeStruct((2, L, D), F32),
            jax.ShapeDtypeStruct((2, L, D), F32),
            jax.ShapeDtypeStruct((2, 1, D), F32),
        ],
        scratch_shapes=[pltpu.VMEM((L, FILTER_HIDDEN), BF16)],
        compiler_params=_cparams(("arbitrary",)),
        name="hyena_filter",
    )(jnp.asarray(_filter_feats(L)), w1p, b1.reshape(1, -1), w2, b2.reshape(1, -1), freq.reshape(1, -1),
      w3r, b3.reshape(4, D), log_decay.reshape(4, D), cs)


FREQ_CHUNK = 1024


def _hyconv_body(x1_ref, x2_ref, v_ref, sw_ref, sb_ref, skip_ref, fr_ref, fi_ref, fn_ref, cs_ref, o_ref,
                 z_ref, g_ref, zb_ref, yr_ref, yi_ref, *, L):
    row, alt = _alt_sign(L)
    kc = min(FREQ_CHUNK, L)
    chunk_row = lax.broadcasted_iota(jnp.int32, (kc, 1), 0)

    def short_conv(u_ref, p):
        u = u_ref[...].astype(F32)
        prev = jnp.where(row == 0, 0.0, pltpu.roll(u, 1, 0))
        nxt = jnp.where(row == L - 1, 0.0, pltpu.roll(u, L - 1, 0))
        return (prev * sw_ref[0, p:p + 1, :] + u * sw_ref[1, p:p + 1, :] + nxt * sw_ref[2, p:p + 1, :]
                + sb_ref[p:p + 1, :])

    z_ref[...] = short_conv(v_ref, 2)
    g_ref[0] = short_conv(x1_ref, 0)
    g_ref[1] = short_conv(x2_ref, 1)
    for o in range(2):
        z = z_ref[...]
        zb_ref[...] = z.astype(BF16)
        nyq = jnp.sum(z * alt, axis=0, keepdims=True) * fn_ref[o]

        def to_freq(c, carry):
            lo = pl.multiple_of(c * kc, kc)
            zr = jnp.dot(cs_ref[pl.ds(lo, kc), :], zb_ref[...], preferred_element_type=F32)
            zs = jnp.dot(cs_ref[pl.ds(L + lo, kc), :], zb_ref[...], preferred_element_type=F32)
            fr = fr_ref[o, pl.ds(lo, kc), :]
            fi = fi_ref[o, pl.ds(lo, kc), :]
            yr_ref[pl.ds(lo, kc), :] = (zr * fr + zs * fi).astype(BF16)
            yi_ref[pl.ds(lo, kc), :] = (zr * fi - zs * fr).astype(BF16)
            return carry

        lax.fori_loop(0, L // kc, to_freq, 0)

        def to_time(c, carry):
            lo = pl.multiple_of(c * kc, kc)
            y = (jnp.dot(cs_ref[pl.ds(lo, kc), :], yr_ref[...], preferred_element_type=F32)
                 - jnp.dot(cs_ref[pl.ds(L + lo, kc), :], yi_ref[...], preferred_element_type=F32))
            y = y + jnp.where(((chunk_row + lo) & 1) == 0, nyq, -nyq)
            z_ref[pl.ds(lo, kc), :] = g_ref[o, pl.ds(lo, kc), :] * (y + skip_ref[o:o + 1, :] * z_ref[pl.ds(lo, kc), :])
            return carry

        lax.fori_loop(0, L // kc, to_time, 0)
    o_ref[...] = z_ref[...].astype(BF16)


def _hyena_conv(up, sw, sb, skip, fr, fi, fn, cs, *, row0, n_batch, L, tc):
    n_ct = D // tc
    base = row0 // L
    u_spec = lambda p: pl.BlockSpec((L, tc), lambda j, b: (base + b, p * n_ct + j))
    once = pl.Buffered(1)
    return pl.pallas_call(
        functools.partial(_hyconv_body, L=L),
        grid=(n_ct, n_batch),
        in_specs=[
            u_spec(0), u_spec(1), u_spec(2),
            pl.BlockSpec((3, 3, tc), lambda j, b: (0, 0, j)),
            pl.BlockSpec((3, tc), lambda j, b: (0, j)),
            pl.BlockSpec((2, tc), lambda j, b: (0, j)),
            pl.BlockSpec((2, L, tc), lambda j, b: (0, 0, j), pipeline_mode=once),
            pl.BlockSpec((2, L, tc), lambda j, b: (0, 0, j), pipeline_mode=once),
            pl.BlockSpec((2, 1, tc), lambda j, b: (0, 0, j)),
            pl.BlockSpec((2 * L, L), lambda j, b: (0, 0), pipeline_mode=once),
        ],
        out_specs=pl.BlockSpec((L, tc), lambda j, b: (b, j)),
        out_shape=jax.ShapeDtypeStruct((n_batch * L, D), BF16),
        scratch_shapes=[
            pltpu.VMEM((L, tc), F32),
            pltpu.VMEM((2, L, tc), F32),
            pltpu.VMEM((L, tc), BF16),
            pltpu.VMEM((L, tc), BF16),
            pltpu.VMEM((L, tc), BF16),
        ],
        compiler_params=_cparams(("arbitrary", "arbitrary")),
        name="hyena_conv",
    )(up, up, up, sw.reshape(3, 3, D), sb.reshape(3, D), skip, fr, fi, fn, cs)


EXP_TILE = 512
SORT_TILES = (2 * N_TOK) // EXP_TILE + N_EXP
SORT_ROWS = SORT_TILES * EXP_TILE
N_QUARTERS = 4
QUARTER_W = D // (2 * N_QUARTERS)
META_I1, META_I2, META_R1, META_R2, META_W1, META_W2 = range(6)
HI_HALF = 0xFFFF0000


def _pack_pairs(x):
    bits = pltpu.bitcast(x.astype(BF16).astype(F32), jnp.uint32)
    return (bits[:, :QUARTER_W] >> 16) | bits[:, QUARTER_W:]


def _unpack_pairs(w):
    return pltpu.bitcast(w << 16, F32), pltpu.bitcast(w & jnp.uint32(HI_HALF), F32)


def _packed_shape(rows):
    return (rows // 8, N_QUARTERS, 8, QUARTER_W)


def _store_packed(ref, x):
    for q in range(N_QUARTERS):
        ref[:, q] = _pack_pairs(x[:, q * 2 * QUARTER_W:(q + 1) * 2 * QUARTER_W]).reshape(-1, 8, QUARTER_W)


def _load_unpacked(ref):
    halves = []
    for q in range(N_QUARTERS):
        halves.extend(_unpack_pairs(ref[:, q].reshape(-1, QUARTER_W)))
    return jnp.concatenate(halves, axis=1)


def _packed_row(ref, tile, sublane):
    return ref.at[tile, :, sublane, :]


def _lane_put(lane, values):
    out = jnp.where(lane == 0, values[0], 0.0)
    for k in range(1, len(values)):
        out = out + jnp.where(lane == k, values[k], 0.0)
    return out


def _route_body(x_ref, ada_ref, gain_ref, wr_hi_ref, wr_lo_ref, br_ref, tri_ref, hq_ref, meta_ref, cnt_ref, carry_ref):
    step = pl.program_id(0)

    @pl.when(step == 0)
    def _():
        carry_ref[...] = jnp.zeros_like(carry_ref)

    h = _norm_mod(x_ref[...], gain_ref, ada_ref, 3)
    _store_packed(hq_ref, h)
    h_hi = h.astype(BF16)
    h_lo = (h - h_hi.astype(F32)).astype(BF16)
    logits = (jnp.dot(h_hi, wr_hi_ref[...], preferred_element_type=F32)
              + jnp.dot(h_lo, wr_hi_ref[...], preferred_element_type=F32)
              + jnp.dot(h_hi, wr_lo_ref[...], preferred_element_type=F32) + br_ref[...])
    lane = lax.broadcasted_iota(jnp.int32, (1, ROUTE_W), 1)
    lane_f = lane.astype(F32)
    group_of_lane = (lane >> 3).astype(F32)
    neg = -jnp.inf
    big = float(ROUTE_W)
    is_g = (lane >= N_EXP) & (lane < N_EXP + N_GROUPS)
    gl = jnp.where(is_g, logits, neg)
    gmax = jnp.max(gl, axis=-1, keepdims=True)
    gidx = jnp.min(jnp.where(gl == gmax, lane_f - N_EXP, big), axis=-1, keepdims=True)
    g_top = 1.0 / jnp.sum(jnp.where(is_g, jnp.exp(gl - gmax), 0.0), axis=-1, keepdims=True)
    in_group = (lane < N_EXP) & (group_of_lane == gidx)
    el = jnp.where(in_group, logits, neg)
    v1 = jnp.max(el, axis=-1, keepdims=True)
    i1 = jnp.min(jnp.where(el == v1, lane_f, big), axis=-1, keepdims=True)
    el2 = jnp.where(lane_f == i1, neg, el)
    v2 = jnp.max(el2, axis=-1, keepdims=True)
    i2 = jnp.min(jnp.where(el2 == v2, lane_f, big), axis=-1, keepdims=True)
    r = jnp.exp(v2 - v1)
    w1 = g_top / (1.0 + r)
    w2 = g_top * r / (1.0 + r)
    sel = jnp.where((lane_f == i1) | (lane_f == i2), 1.0, 0.0)
    rank = jnp.dot(tri_ref[...], sel.astype(BF16), preferred_element_type=F32) + carry_ref[...]
    r1 = jnp.sum(jnp.where(lane_f == i1, rank, 0.0), axis=-1, keepdims=True)
    r2 = jnp.sum(jnp.where(lane_f == i2, rank, 0.0), axis=-1, keepdims=True)
    carry_ref[...] += jnp.sum(sel, axis=0, keepdims=True)
    cnt_ref[...] = carry_ref[...]
    meta_ref[...] = _lane_put(lane, (i1, i2, r1, r2, w1, w2))


@functools.lru_cache(maxsize=None)
def _strict_lower(n):
    r = np.arange(n)
    return (r[None, :] < r[:, None]).astype(np.float32)


def _route(x, ada, gain, wg, bg, we, be, *, tm=1024):
    wr = jnp.zeros((D, ROUTE_W), F32).at[:, :N_EXP].set(we.reshape(D, N_EXP)).at[:, N_EXP:N_EXP + N_GROUPS].set(wg)
    br = jnp.zeros((1, ROUTE_W), F32).at[0, :N_EXP].set(be.reshape(N_EXP)).at[0, N_EXP:N_EXP + N_GROUPS].set(bg)
    wr_hi = wr.astype(BF16)
    wr_lo = (wr - wr_hi.astype(F32)).astype(BF16)
    return pl.pallas_call(
        _route_body,
        grid=(N_TOK // tm,),
        in_specs=[
            pl.BlockSpec((tm, D), lambda i: (i, 0)),
            pl.BlockSpec((1, 6, D), lambda i: (_cond_row(i, tm), 0, 0)),
            pl.BlockSpec((1, D), lambda i: (0, 0)),
            pl.BlockSpec((D, ROUTE_W), lambda i: (0, 0)),
            pl.BlockSpec((D, ROUTE_W), lambda i: (0, 0)),
            pl.BlockSpec((1, ROUTE_W), lambda i: (0, 0)),
            pl.BlockSpec((tm, tm), lambda i: (0, 0)),
        ],
        out_specs=[
            pl.BlockSpec(_packed_shape(tm), lambda i: (i, 0, 0, 0)),
            pl.BlockSpec((tm, ROUTE_W), lambda i: (i, 0)),
            pl.BlockSpec((1, ROUTE_W), lambda i: (0, 0)),
        ],
        out_shape=[
            jax.ShapeDtypeStruct(_packed_shape(N_TOK), jnp.uint32),
            jax.ShapeDtypeStruct((N_TOK, ROUTE_W), F32),
            jax.ShapeDtypeStruct((1, ROUTE_W), F32),
        ],
        scratch_shapes=[pltpu.VMEM((1, ROUTE_W), F32)],
        compiler_params=_cparams(("arbitrary",)),
        name="route",
    )(x, ada, gain.reshape(1, D), wr_hi, wr_lo, br, jnp.asarray(_strict_lower(tm)).astype(BF16))


def _slots_body(meta_ref, cnt_ref, upper_ref, pos_ref, plan_ref):
    lane = lax.broadcasted_iota(jnp.int32, (1, ROUTE_W), 1)
    lane_f = lane.astype(F32)
    tiles = jnp.floor((cnt_ref[...] + (EXP_TILE - 1.0)) * (1.0 / EXP_TILE))
    end_tile = jnp.dot(jnp.broadcast_to(tiles, (8, ROUTE_W)).astype(BF16), upper_ref[...],
                       preferred_element_type=F32)[0:1]
    start_row = (end_tile - tiles) * EXP_TILE
    meta = meta_ref[...]
    i1 = meta[:, META_I1:META_I1 + 1]
    i2 = meta[:, META_I2:META_I2 + 1]
    p1 = jnp.sum(jnp.where(lane_f == i1, start_row, 0.0), axis=-1, keepdims=True) + meta[:, META_R1:META_R1 + 1]
    p2 = jnp.sum(jnp.where(lane_f == i2, start_row, 0.0), axis=-1, keepdims=True) + meta[:, META_R2:META_R2 + 1]
    hi1, hi2 = jnp.floor(p1 * 0.125), jnp.floor(p2 * 0.125)
    pos_ref[0] = _lane_put(lane, (hi1, p1 - 8.0 * hi1, hi2, p2 - 8.0 * hi2)).T[:8, :].astype(jnp.int32)

    @pl.when(pl.program_id(0) == 0)
    def _():
        end_col = jnp.broadcast_to(end_tile, (ROUTE_W, ROUTE_W)).T
        expert = lax.broadcasted_iota(jnp.int32, (ROUTE_W, ROUTE_W), 0)
        tile = lax.broadcasted_iota(jnp.int32, (ROUTE_W, ROUTE_W), 1).astype(F32)
        passed = jnp.where((end_col <= tile) & (expert < N_EXP), 1.0, 0.0)
        tile_expert = jnp.minimum(jnp.sum(passed, axis=0, keepdims=True), N_EXP - 1.0)
        used = jnp.max(end_tile, axis=-1, keepdims=True)
        row = lax.broadcasted_iota(jnp.int32, (8, ROUTE_W), 0)
        plan_ref[...] = jnp.where(row == 0, tile_expert, jnp.where(row == 1, used, 0.0)).astype(jnp.int32)


@functools.lru_cache(maxsize=None)
def _upper_incl(n):
    r = np.arange(n)
    return (r[:, None] <= r[None, :]).astype(np.float32)


def _slots(meta, cnt, *, tm=2048):
    n_tiles = N_TOK // tm
    return pl.pallas_call(
        _slots_body,
        grid=(n_tiles,),
        in_specs=[
            pl.BlockSpec((tm, ROUTE_W), lambda i: (i, 0)),
            pl.BlockSpec((1, ROUTE_W), lambda i: (0, 0)),
            pl.BlockSpec((ROUTE_W, ROUTE_W), lambda i: (0, 0)),
        ],
        out_specs=[
            pl.BlockSpec((1, 8, tm), lambda i: (i, 0, 0)),
            pl.BlockSpec((8, ROUTE_W), lambda i: (0, 0)),
        ],
        out_shape=[
            jax.ShapeDtypeStruct((n_tiles, 8, tm), jnp.int32),
            jax.ShapeDtypeStruct((8, ROUTE_W), jnp.int32),
        ],
        compiler_params=_cparams(("arbitrary",)),
        name="moe_slots",
    )(meta, cnt, jnp.asarray(_upper_incl(ROUTE_W)).astype(BF16))


ROW_UNROLL = 8
N_DMA_LANES = 2


def _start_row_copies(tm, make_copy):
    def block(blk, carry):
        for j in range(ROW_UNROLL):
            lane = j % N_DMA_LANES
            for k in range(2):
                make_copy(blk, j, k, lane).start(priority=lane)
        return carry

    lax.fori_loop(0, tm // ROW_UNROLL, block, 0)


def _drain_row_copies(like_src, like_dst, sem):
    for lane in range(N_DMA_LANES):
        pltpu.make_async_copy(like_src, like_dst, sem.at[lane]).wait()


def _dispatch_body(t1_ref, s1_ref, t2_ref, s2_ref, hq_ref, xs_in_ref, xs_ref, sem, *, tm):
    del xs_in_ref
    slots = ((t1_ref, s1_ref), (t2_ref, s2_ref))

    def copy(blk, j, k, lane):
        i = blk * ROW_UNROLL + j
        return pltpu.make_async_copy(_packed_row(hq_ref, blk, j), _packed_row(xs_ref, slots[k][0][i], slots[k][1][i]),
                                     sem.at[lane])

    _start_row_copies(tm, copy)
    _drain_row_copies(hq_ref, xs_ref.at[pl.ds(0, tm // 8)], sem)


def _slot_specs(tm, ahead=0):
    last = N_TOK // tm - 1
    return [pl.BlockSpec((tm,), lambda t: (jnp.minimum(t + ahead, last),), memory_space=pltpu.SMEM) for _ in range(4)]


def _dispatch(slots, hq, init, *, tm=512):
    return pl.pallas_call(
        functools.partial(_dispatch_body, tm=tm),
        grid=(N_TOK // tm,),
        in_specs=_slot_specs(tm) + [
            pl.BlockSpec(_packed_shape(tm), lambda t: (t, 0, 0, 0)),
            pl.BlockSpec(memory_space=pl.ANY),
        ],
        out_specs=pl.BlockSpec(memory_space=pl.ANY),
        out_shape=jax.ShapeDtypeStruct(_packed_shape(SORT_ROWS), jnp.uint32),
        input_output_aliases={5: 0},
        scratch_shapes=[pltpu.SemaphoreType.DMA((N_DMA_LANES,))],
        compiler_params=_cparams(("arbitrary",)),
        name="moe_dispatch",
    )(*slots, hq, init)


def _experts_body(plan_ref, xs_ref, wg_ref, wu_ref, wd_ref, ys_ref, wgu_ref, wdn_ref):
    t = pl.program_id(0)
    in_use = t < plan_ref[1, 0]
    new_expert = (t == 0) | (plan_ref[0, t] != plan_ref[0, jnp.maximum(t - 1, 0)])

    @pl.when(in_use & new_expert)
    def _():
        wgu_ref[:, :EXP_H] = wg_ref[0, 0].astype(BF16)
        wgu_ref[:, EXP_H:] = wu_ref[0, 0].astype(BF16)
        wdn_ref[...] = wd_ref[0, 0].astype(BF16)

    @pl.when(in_use)
    def _():
        x = _load_unpacked(xs_ref).astype(BF16)
        ab = jnp.dot(x, wgu_ref[...], preferred_element_type=F32)
        a, b = ab[:, :EXP_H], ab[:, EXP_H:]
        act = a * _sigmoid(a) * b
        _store_packed(ys_ref, jnp.dot(act.astype(BF16), wdn_ref[...], preferred_element_type=F32))


def _experts(plan, xs, w_gate, w_up, w_down, *, layer):
    def tile_idx(t, plan):
        return (jnp.minimum(t, jnp.maximum(plan[1, 0] - 1, 0)), 0, 0, 0)

    grid_spec = pltpu.PrefetchScalarGridSpec(
        num_scalar_prefetch=1,
        grid=(SORT_TILES,),
        in_specs=[
            pl.BlockSpec(_packed_shape(EXP_TILE), tile_idx),
            pl.BlockSpec((1, 1, D, EXP_H), lambda t, plan: (layer, plan[0, t], 0, 0)),
            pl.BlockSpec((1, 1, D, EXP_H), lambda t, plan: (layer, plan[0, t], 0, 0)),
            pl.BlockSpec((1, 1, EXP_H, D), lambda t, plan: (layer, plan[0, t], 0, 0)),
        ],
        out_specs=pl.BlockSpec(_packed_shape(EXP_TILE), tile_idx),
        scratch_shapes=[pltpu.VMEM((D, 2 * EXP_H), BF16), pltpu.VMEM((EXP_H, D), BF16)],
    )
    return pl.pallas_call(
        _experts_body,
        grid_spec=grid_spec,
        out_shape=jax.ShapeDtypeStruct(_packed_shape(SORT_ROWS), jnp.uint32),
        input_output_aliases={1: 0},
        compiler_params=_cparams(("arbitrary",)),
        name="moe_experts",
    )(plan, xs, w_gate, w_up, w_down)


def _combine_body(*refs, tm, final):
    cur_slots, nxt_slots = refs[0:4], refs[4:8]
    meta_ref, x_ref, ada_ref = refs[8:11]
    rest = refs[11:]
    if final:
        fgain_ref, ys_ref, oc_ref, ol_ref, y_ref, sem = rest
    else:
        ys_ref, o_ref, y_ref, sem = rest
    step = pl.program_id(0)
    n_steps = pl.num_programs(0)

    def fetch(slot_refs, buf):
        slots = ((slot_refs[0], slot_refs[1]), (slot_refs[2], slot_refs[3]))

        def copy(blk, j, k, lane):
            i = blk * ROW_UNROLL + j
            return pltpu.make_async_copy(_packed_row(ys_ref, slots[k][0][i], slots[k][1][i]),
                                         _packed_row(y_ref.at[buf, k], blk, j), sem.at[buf, lane])

        _start_row_copies(tm, copy)

    @pl.when(step == 0)
    def _():
        fetch(cur_slots, 0)

    @pl.when(step + 1 < n_steps)
    def _():
        fetch(nxt_slots, (step + 1) % 2)

    buf = step % 2
    _drain_row_copies(ys_ref.at[pl.ds(0, tm // 8)], y_ref.at[buf, 0], sem.at[buf])
    y_ref = y_ref.at[buf]
    meta = meta_ref[...]
    mix = (meta[:, META_W1:META_W1 + 1] * _load_unpacked(y_ref.at[0])
           + meta[:, META_W2:META_W2 + 1] * _load_unpacked(y_ref.at[1]))
    x = x_ref[...] + ada_ref[0, 5:6, :] * mix
    if not final:
        o_ref[...] = x
        return
    ms = jnp.mean(x * x, axis=-1, keepdims=True)
    y = x * lax.rsqrt(ms + EPS) * fgain_ref[...]
    is_ctx = pl.program_id(0) < N_CTX // tm

    @pl.when(is_ctx)
    def _():
        oc_ref[...] = y

    @pl.when(jnp.logical_not(is_ctx))
    def _():
        ol_ref[...] = y


def _combine(slots, ys, meta, x, ada, final_gain=None, *, tm=512):
    final = final_gain is not None
    extra_in = [pl.BlockSpec((1, D), lambda t: (0, 0))] if final else []
    extra_args = [final_gain.reshape(1, D)] if final else []
    if final:
        out_specs = _split_specs(tm, D)
        out_shape = [jax.ShapeDtypeStruct((N_CTX, D), F32), jax.ShapeDtypeStruct((N_LAT, D), F32)]
    else:
        out_specs = pl.BlockSpec((tm, D), lambda t: (t, 0))
        out_shape = jax.ShapeDtypeStruct((N_TOK, D), F32)
    return pl.pallas_call(
        functools.partial(_combine_body, tm=tm, final=final),
        grid=(N_TOK // tm,),
        in_specs=_slot_specs(tm) + _slot_specs(tm, ahead=1) + [
            pl.BlockSpec((tm, ROUTE_W), lambda t: (t, 0)),
            pl.BlockSpec((tm, D), lambda t: (t, 0)),
            pl.BlockSpec((1, 6, D), lambda t: (_cond_row(t, tm), 0, 0)),
        ] + extra_in + [pl.BlockSpec(memory_space=pl.ANY)],
        out_specs=out_specs,
        out_shape=out_shape,
        scratch_shapes=[pltpu.VMEM((2, 2) + _packed_shape(tm), jnp.uint32),
                        pltpu.SemaphoreType.DMA((2, N_DMA_LANES))],
        compiler_params=_cparams(("arbitrary",)),
        name="moe_combine",
    )(*slots, *slots, meta, x, ada, *extra_args, ys)


def _moe(x, ada, gain, wg, bg, we, be, w_gate, w_up, w_down, *, layer, sort_init, final_gain=None):
    hq, meta, cnt = _route(x, ada, gain, wg, bg, we, be)
    pos, plan = _slots(meta, cnt)
    slots = [pos[:, r, :].reshape(N_TOK) for r in range(4)]
    xs = _dispatch(slots, hq, sort_init)
    ys = _experts(plan, xs, w_gate, w_up, w_down, layer=layer)
    return _combine(slots, ys, meta, x, ada, final_gain), ys


def kernel(x_prompt, x_sample, cache_k, cache_v, c, c_ctx, w_ada, b_ada, norm_mix, norm_ffn, attn_w_q, attn_w_kv, attn_q_norm, attn_k_norm, attn_w_o, hy_w_in, hy_b_in, hy_short_w, hy_short_b, hy_filt_w1, hy_filt_b1, hy_filt_w2, hy_filt_b2, hy_filt_w3, hy_filt_b3, hy_filt_freq, hy_log_decay, hy_skip, hy_w_out, hy_b_out, router_group_w, router_group_b, router_expert_w, router_expert_b, moe_w_gate, moe_w_up, moe_w_down, final_norm):
    depth = w_ada.shape[0]
    x_parts = (x_prompt.reshape(N_CTX, D), x_sample.reshape(N_LAT, D))
    cond8 = jnp.zeros((8, D), F32).at[0].set(c_ctx).at[1:1 + N_BATCH_LAT].set(c)
    ada_all = _ada(cond8, w_ada, b_ada).reshape(depth, 8, 6, D)
    new_k = []
    new_v = []
    sort_buf = jnp.zeros(_packed_shape(SORT_ROWS), jnp.uint32)
    for l in range(depth):
        ada = ada_all[l]
        if l % 2 == 0:
            a = l // 2
            w_qkv = jnp.concatenate([attn_w_q[a], attn_w_kv[a]], axis=1).astype(BF16)
            q, k_ctx, v_ctx, kc, vc, kl, vl = _qkv(x_parts, ada, norm_mix[l], w_qkv, attn_q_norm[a], attn_k_norm[a])
            new_k.append(k_ctx)
            new_v.append(v_ctx)
            o_ctx = _attention(q, [kc], [vc], row0=0, n_batch=N_BATCH_CTX, seq=L_CTX, tq=L_CTX,
                               n_kv=N_KV, stack=Q_PER_KV)
            past_keys = _lane_tile_heads(cache_k[:, a]).transpose(0, 1, 3, 2)
            o_lat = _attention(q, [kl, past_keys], [vl, _lane_tile_heads(cache_v[:, a])],
                               row0=N_CTX, n_batch=N_BATCH_LAT, seq=L_LAT, tq=256, n_kv=1, stack=1)
            x = _resid_proj(o_ctx, o_lat, attn_w_o[a].astype(BF16), jnp.zeros((D,), F32), x_parts, ada, gpart=2)
        else:
            j = l // 2
            (x,) = x_parts
            up = _norm_proj(x, ada, norm_mix[l], hy_w_in[j].astype(BF16), hy_b_in[j], part=0)
            outs = []
            for row0, n_batch, L, tc in ((0, N_BATCH_CTX, L_CTX, D), (N_CTX, N_BATCH_LAT, L_LAT, 256)):
                cs = jnp.asarray(_dft_table(L)).astype(BF16)
                fr, fi, fn = _filter_spectra(L, cs, hy_filt_w1[j], hy_filt_b1[j], hy_filt_w2[j], hy_filt_b2[j],
                                             hy_filt_w3[j], hy_filt_b3[j], hy_filt_freq[j], hy_log_decay[j])
                outs.append(_hyena_conv(up, hy_short_w[j], hy_short_b[j], hy_skip[j], fr, fi, fn, cs,
                                        row0=row0, n_batch=n_batch, L=L, tc=tc))
            x = _resid_proj(outs[0], outs[1], hy_w_out[j].astype(BF16), hy_b_out[j], x_parts, ada, gpart=2)
        out, sort_buf = _moe(x, ada, norm_ffn[l], router_group_w[l], router_group_b[l], router_expert_w[l],
                             router_expert_b[l], moe_w_gate, moe_w_up, moe_w_down, layer=l, sort_init=sort_buf,
                             final_gain=final_norm if l == depth - 1 else None)
        x_parts = (out,)
    y_ctx, y_lat = x_parts[0]
    return (y_ctx.reshape(N_BATCH_CTX, L_CTX, D), y_lat.reshape(N_BATCH_LAT, L_LAT, D),
            jnp.stack(new_k, axis=1), jnp.stack(new_v, axis=1))
```

```python
import functools
import math

import numpy as np
import jax
import jax.numpy as jnp
from jax import lax
from jax.experimental import pallas as pl
from jax.experimental.pallas import tpu as pltpu

F32 = jnp.float32
BF16 = jnp.bfloat16

D = 1024
N_BATCH_CTX = 32
L_CTX = 256
N_BATCH_LAT = 2
L_LAT = 2048
PAST = 256
N_CTX = N_BATCH_CTX * L_CTX
N_LAT = N_BATCH_LAT * L_LAT
N_TOK = N_CTX + N_LAT
GRID_W = 64
N_HEADS = 16
N_KV = 4
DH = 64
Q_PER_KV = N_HEADS // N_KV
KV_W = N_KV * DH
ROPE_THETA = 10000.0
FILTER_FEAT = 17
FEAT_PAD = 32
FILTER_HIDDEN = 64
N_GROUPS = 4
E_PER_G = 8
N_EXP = N_GROUPS * E_PER_G
EXP_H = D // 4
EPS = 1e-6
ROUTE_W = 128
VMEM_LIMIT = 56 * 1024 * 1024


def _cparams(sem):
    return pltpu.CompilerParams(dimension_semantics=sem, vmem_limit_bytes=VMEM_LIMIT)


def _cond_row(i, tm):
    n_ctx_tiles = N_CTX // tm
    return jnp.where(i < n_ctx_tiles, 0, 1 + (i - n_ctx_tiles) // (L_LAT // tm))


def _split_specs(tm, width):
    n_ctx_tiles = N_CTX // tm
    return [pl.BlockSpec((tm, width), lambda i: (jnp.minimum(i, n_ctx_tiles - 1), 0)),
            pl.BlockSpec((tm, width), lambda i: (jnp.maximum(i - n_ctx_tiles, 0), 0))]


def _pick_split(i, tm, ctx_ref, lat_ref):
    return jnp.where(i < N_CTX // tm, ctx_ref[...], lat_ref[...])


def _sigmoid(x):
    return 1.0 / (1.0 + jnp.exp(-x))


@functools.lru_cache(maxsize=None)
def _dft_table(L):
    k = np.arange(L, dtype=np.int64)
    ang = (np.outer(k, k) % (2 * L)).astype(np.float64) * (math.pi / L)
    return np.concatenate([np.cos(ang), np.sin(ang)], axis=0).astype(np.float32)


@functools.lru_cache(maxsize=None)
def _filter_feats(L):
    t = np.linspace(0.0, 1.0, L, dtype=np.float64)[:, None]
    bands = np.linspace(1e-4, 7.0, 8, dtype=np.float64)[None, :]
    w = (2.0 * math.pi) * np.arange(L, dtype=np.float64)[:, None] / L
    feats = np.concatenate([t, np.cos(bands * w), -np.sin(bands * w)], axis=-1)
    out = np.zeros((L, FEAT_PAD), np.float32)
    out[:, :FILTER_FEAT] = feats
    return out


@functools.lru_cache(maxsize=None)
def _rope_tables(tm):
    pos = np.arange(L_LAT)
    row = (pos // GRID_W).astype(np.float64)
    col = (pos % GRID_W).astype(np.float64)
    axis_dim = DH // 2
    inv_freq = ROPE_THETA ** (-np.arange(0, axis_dim, 2, dtype=np.float64) / axis_dim)
    lane = np.arange(KV_W)
    d = lane % DH
    is_col = (d // axis_dim) == 1
    fi = d % (axis_dim // 2)
    first_half = (d % axis_dim) < (axis_dim // 2)
    p = np.where(is_col[None, :], col[:, None], row[:, None])
    ang = p * inv_freq[fi][None, :]
    cos = np.cos(ang)
    sin = np.sin(ang) * np.where(first_half, -1.0, 1.0)[None, :]
    cos = np.concatenate([np.ones((tm, KV_W)), cos], axis=0).astype(np.float32)
    sin = np.concatenate([np.zeros((tm, KV_W)), sin], axis=0).astype(np.float32)
    return cos, sin


@functools.lru_cache(maxsize=None)
def _head_sum_matrix():
    lane = np.arange(KV_W)
    return (lane[:, None] // DH == lane[None, :] // DH).astype(np.float32)


def _ada_body(c_ref, w_ref, b_ref, o_ref):
    c = c_ref[...]
    s = c * _sigmoid(c)
    o_ref[0] = jnp.dot(s.astype(BF16), w_ref[0].astype(BF16), preferred_element_type=F32) + b_ref[0]


def _ada(cond8, w_ada, b_ada):
    depth = w_ada.shape[0]
    tn = 1536
    return pl.pallas_call(
        _ada_body,
        grid=(depth, 6 * D // tn),
        in_specs=[
            pl.BlockSpec((8, D), lambda l, j: (0, 0)),
            pl.BlockSpec((1, D, tn), lambda l, j: (l, 0, j)),
            pl.BlockSpec((1, 1, tn), lambda l, j: (l, 0, j)),
        ],
        out_specs=pl.BlockSpec((1, 8, tn), lambda l, j: (l, 0, j)),
        out_shape=jax.ShapeDtypeStruct((depth, 8, 6 * D), F32),
        compiler_params=_cparams(("arbitrary", "arbitrary")),
        name="ada",
    )(cond8, w_ada, b_ada.reshape(depth, 1, 6 * D))


def _norm_mod(x, gain_ref, ada_ref, part):
    ms = jnp.mean(x * x, axis=-1, keepdims=True)
    y = x * lax.rsqrt(ms + EPS) * gain_ref[...]
    return y * (1.0 + ada_ref[0, part + 1:part + 2, :]) + ada_ref[0, part:part + 1, :]


def _proj_body(x_ref, ada_ref, gain_ref, w_ref, b_ref, o_ref, *, part):
    h = _norm_mod(x_ref[...], gain_ref, ada_ref, part)
    acc = jnp.dot(h.astype(BF16), w_ref[...], preferred_element_type=F32) + b_ref[...]
    o_ref[...] = acc.astype(o_ref.dtype)


def _norm_proj(x, ada, gain, w, b, *, part, tm=1024):
    n_out = w.shape[1]
    return pl.pallas_call(
        functools.partial(_proj_body, part=part),
        grid=(N_TOK // tm,),
        in_specs=[
            pl.BlockSpec((tm, D), lambda i: (i, 0)),
            pl.BlockSpec((1, 6, D), lambda i: (_cond_row(i, tm), 0, 0)),
            pl.BlockSpec((1, D), lambda i: (0, 0)),
            pl.BlockSpec((D, n_out), lambda i: (0, 0)),
            pl.BlockSpec((1, n_out), lambda i: (0, 0)),
        ],
        out_specs=pl.BlockSpec((tm, n_out), lambda i: (i, 0)),
        out_shape=jax.ShapeDtypeStruct((N_TOK, n_out), BF16),
        compiler_params=_cparams(("arbitrary",)),
        name="norm_proj",
    )(x, ada, gain.reshape(1, D), w, b.reshape(1, n_out))


def _head_rms_rope(x, gain, hs, cos, sin, lane):
    ss = jnp.dot((x * x).astype(BF16), hs, preferred_element_type=F32)
    xn = x * lax.rsqrt(ss * (1.0 / DH) + EPS) * gain
    fwd = pltpu.roll(xn, KV_W - DH // 4, 1)
    bwd = pltpu.roll(xn, DH // 4, 1)
    partner = jnp.where((lane & (DH // 4)) == 0, fwd, bwd)
    return xn * cos + partner * sin


@functools.lru_cache(maxsize=None)
def _lane_tile_matrix():
    col = np.arange(N_KV * KV_W)
    src = (col // KV_W) * DH + col % DH
    return (np.arange(KV_W)[:, None] == src[None, :]).astype(np.float32)


def _qkv_body(*refs, tm, split_x):
    n_x = 2 if split_x else 1
    x_refs = refs[:n_x]
    (ada_ref, gain_ref, w_ref, qg_ref, kg_ref, hs_ref, cos_ref, sin_ref, tile_ref, tile_t_ref,
     q_ref, newk_ref, newv_ref, kc_ref, vc_ref, kl_ref, vl_ref) = refs[n_x:]
    step = pl.program_id(0)
    x = _pick_split(step, tm, *x_refs) if split_x else x_refs[0][...]
    h = _norm_mod(x, gain_ref, ada_ref, 0)
    acc = jnp.dot(h.astype(BF16), w_ref[...], preferred_element_type=F32)
    hs = hs_ref[...]
    cos = cos_ref[...]
    sin = sin_ref[...]
    lane = lax.broadcasted_iota(jnp.int32, (1, KV_W), 1)
    for c in range(N_KV):
        qc = _head_rms_rope(acc[:, c * KV_W:(c + 1) * KV_W], qg_ref[...], hs, cos, sin, lane)
        q_ref[:, c * KV_W:(c + 1) * KV_W] = (qc * (DH ** -0.5)).astype(BF16)
    k = _head_rms_rope(acc[:, D:D + KV_W], kg_ref[...], hs, cos, sin, lane)
    v = acc[:, D + KV_W:D + 2 * KV_W]
    k4t = lax.dot_general(tile_t_ref[...], k.astype(BF16), (((1,), (1,)), ((), ())),
                          preferred_element_type=F32).astype(BF16)
    v4 = jnp.dot(v.astype(BF16), tile_ref[...], preferred_element_type=F32).astype(BF16)
    is_ctx = step < N_CTX // tm

    @pl.when(is_ctx)
    def _():
        for bb in range(tm // L_CTX):
            rows = slice(bb * L_CTX, (bb + 1) * L_CTX)
            for hd in range(N_KV):
                newk_ref[bb, hd] = k[rows, hd * DH:(hd + 1) * DH]
                newv_ref[bb, hd] = v[rows, hd * DH:(hd + 1) * DH]
                kc_ref[bb, hd] = k4t[hd * KV_W:(hd + 1) * KV_W, rows]
                vc_ref[bb, hd] = v4[rows, hd * KV_W:(hd + 1) * KV_W]

    @pl.when(jnp.logical_not(is_ctx))
    def _():
        for hd in range(N_KV):
            kl_ref[0, hd] = k4t[hd * KV_W:(hd + 1) * KV_W, :]
            vl_ref[0, hd] = v4[:, hd * KV_W:(hd + 1) * KV_W]


def _qkv(x_parts, ada, gain, w_qkv, q_gain, k_gain):
    tm = 2 * L_CTX
    per_tile = tm // L_CTX
    cos, sin = _rope_tables(tm)
    split_x = len(x_parts) == 2
    x_specs = _split_specs(tm, D) if split_x else [pl.BlockSpec((tm, D), lambda i: (i, 0))]
    n_ctx_tiles = N_CTX // tm
    lat_tiles = L_LAT // tm

    def rope_idx(i):
        return (jnp.where(i < n_ctx_tiles, 0, 1 + (i - n_ctx_tiles) % lat_tiles), 0)

    def ctx_idx(i):
        return (jnp.minimum(i, n_ctx_tiles - 1), 0, 0, 0)

    def lat_idx(i):
        j = jnp.maximum(i - n_ctx_tiles, 0)
        return (j // lat_tiles, 0, j % lat_tiles, 0)

    def lat_idx_t(i):
        j = jnp.maximum(i - n_ctx_tiles, 0)
        return (j // lat_tiles, 0, 0, j % lat_tiles)

    const = lambda i: (0, 0)
    kv_ctx = jax.ShapeDtypeStruct((N_BATCH_CTX, N_KV, L_CTX, DH), F32)
    tiled_ctx = jax.ShapeDtypeStruct((N_BATCH_CTX, N_KV, L_CTX, KV_W), BF16)
    tiled_lat = jax.ShapeDtypeStruct((N_BATCH_LAT, N_KV, L_LAT, KV_W), BF16)
    keys_ctx = jax.ShapeDtypeStruct((N_BATCH_CTX, N_KV, KV_W, L_CTX), BF16)
    keys_lat = jax.ShapeDtypeStruct((N_BATCH_LAT, N_KV, KV_W, L_LAT), BF16)
    return pl.pallas_call(
        functools.partial(_qkv_body, tm=tm, split_x=split_x),
        grid=(N_TOK // tm,),
        in_specs=x_specs + [
            pl.BlockSpec((1, 6, D), lambda i: (_cond_row(i, tm), 0, 0)),
            pl.BlockSpec((1, D), const),
            pl.BlockSpec((D, D + 2 * KV_W), const),
            pl.BlockSpec((1, KV_W), const),
            pl.BlockSpec((1, KV_W), const),
            pl.BlockSpec((KV_W, KV_W), const),
            pl.BlockSpec((tm, KV_W), rope_idx),
            pl.BlockSpec((tm, KV_W), rope_idx),
            pl.BlockSpec((KV_W, N_KV * KV_W), const),
            pl.BlockSpec((N_KV * KV_W, KV_W), const),
        ],
        out_specs=[
            pl.BlockSpec((tm, D), lambda i: (i, 0)),
            pl.BlockSpec((per_tile, N_KV, L_CTX, DH), ctx_idx),
            pl.BlockSpec((per_tile, N_KV, L_CTX, DH), ctx_idx),
            pl.BlockSpec((per_tile, N_KV, KV_W, L_CTX), ctx_idx),
            pl.BlockSpec((per_tile, N_KV, L_CTX, KV_W), ctx_idx),
            pl.BlockSpec((1, N_KV, KV_W, tm), lat_idx_t),
            pl.BlockSpec((1, N_KV, tm, KV_W), lat_idx),
        ],
        out_shape=[jax.ShapeDtypeStruct((N_TOK, D), BF16), kv_ctx, kv_ctx, keys_ctx, tiled_ctx, keys_lat, tiled_lat],
        compiler_params=_cparams(("arbitrary",)),
        name="qkv",
    )(*x_parts, ada, gain.reshape(1, D), w_qkv,
      jnp.tile(q_gain, Q_PER_KV).reshape(1, KV_W), jnp.tile(k_gain, N_KV).reshape(1, KV_W),
      jnp.asarray(_head_sum_matrix()).astype(BF16), jnp.asarray(cos), jnp.asarray(sin),
      jnp.asarray(_lane_tile_matrix()).astype(BF16), jnp.asarray(_lane_tile_matrix().T.copy()).astype(BF16))


def _attn_body(*refs, n_kv, stack, n_seg):
    q_ref, k_refs, v_refs, o_ref = refs[0], refs[1:1 + n_seg], refs[1 + n_seg:1 + 2 * n_seg], refs[-1]
    lane = lax.broadcasted_iota(jnp.int32, (1, KV_W), 1)
    masks = [(lane >> 6) == g for g in range(Q_PER_KV)]
    tq = q_ref.shape[0]
    for kv in range(n_kv):
        q = q_ref[:, kv * KV_W:(kv + 1) * KV_W]
        out = jnp.zeros((tq, KV_W), F32)
        for c in range(0, Q_PER_KV, stack):
            pair = masks[c:c + stack]
            ones_blk = (c + stack) % Q_PER_KV if stack < Q_PER_KV else None
            stacked = jnp.concatenate([jnp.where(m, q, jnp.zeros_like(q)) for m in pair], axis=0)
            scores = [jnp.dot(stacked, k_ref[0, kv], preferred_element_type=F32) for k_ref in k_refs]
            top = functools.reduce(jnp.maximum, [jnp.max(s, axis=-1, keepdims=True) for s in scores])
            probs = [jnp.exp((s - top).astype(BF16)) for s in scores]
            if ones_blk is None:
                vals = [v_ref[0, kv] for v_ref in v_refs]
            else:
                vals = [jnp.where(masks[ones_blk], jnp.ones((), BF16), v_ref[0, kv]) for v_ref in v_refs]
            og = sum(jnp.dot(p, v, preferred_element_type=F32) for p, v in zip(probs, vals))
            if ones_blk is None:
                denom = sum(jnp.sum(p.astype(F32), axis=-1, keepdims=True) for p in probs)
            else:
                denom = og[:, ones_blk * DH:ones_blk * DH + 1]
            og = og * (1.0 / denom)
            for g, m in enumerate(pair):
                out = out + jnp.where(m, og[g * tq:(g + 1) * tq], 0.0)
        o_ref[:, kv * KV_W:(kv + 1) * KV_W] = out.astype(BF16)


def _attention(q, keys, values, *, row0, n_batch, seq, tq, n_kv, stack):
    per_b = seq // tq
    base = row0 // tq
    kv_specs = [pl.BlockSpec((1, n_kv) + a.shape[2:], lambda b, h, i: (b, h, 0, 0)) for a in keys + values]
    return pl.pallas_call(
        functools.partial(_attn_body, n_kv=n_kv, stack=stack, n_seg=len(keys)),
        grid=(n_batch, N_KV // n_kv, per_b),
        in_specs=[pl.BlockSpec((tq, n_kv * KV_W), lambda b, h, i: (base + b * per_b + i, h))] + kv_specs,
        out_specs=pl.BlockSpec((tq, n_kv * KV_W), lambda b, h, i: (b * per_b + i, h)),
        out_shape=jax.ShapeDtypeStruct((n_batch * seq, D), BF16),
        compiler_params=_cparams(("arbitrary", "arbitrary", "arbitrary")),
        name="attention",
    )(q, *keys, *values)


def _lane_tile_heads(x):
    return jnp.tile(x.astype(BF16), (1, 1, 1, KV_W // DH))


def _alt_sign(L):
    row = lax.broadcasted_iota(jnp.int32, (L, 1), 0)
    return row, jnp.where((row & 1) == 0, 1.0, -1.0)


def _split_dot(a, b):
    a_hi, b_hi = a.astype(BF16), b.astype(BF16)
    a_lo = (a - a_hi.astype(F32)).astype(BF16)
    b_lo = (b - b_hi.astype(F32)).astype(BF16)
    return (jnp.dot(a_hi, b_hi, preferred_element_type=F32) + jnp.dot(a_lo, b_hi, preferred_element_type=F32)
            + jnp.dot(a_hi, b_lo, preferred_element_type=F32))


def _filter_body(feat_ref, w1_ref, b1_ref, w2_ref, b2_ref, fq_ref, w3_ref, b3_ref, ld_ref, cs_ref,
                 fr_ref, fi_ref, fn_ref, hid_ref, *, L):
    feats = feat_ref[...]

    @pl.when(pl.program_id(0) == 0)
    def _():
        fq = fq_ref[...]
        h1 = jnp.sin(fq * (_split_dot(feats, w1_ref[...]) + b1_ref[...]))
        hid_ref[...] = jnp.sin(fq * (_split_dot(h1, w2_ref[...]) + b2_ref[...])).astype(BF16)

    t = feats[:, 0:1]
    row, alt = _alt_sign(L)
    filt = []
    for j in range(4):
        raw = jnp.dot(hid_ref[...], w3_ref[j].astype(BF16), preferred_element_type=F32) + b3_ref[j:j + 1, :]
        filt.append(raw * jnp.exp(-t * jnp.exp(ld_ref[j:j + 1, :])))
    for o in range(2):
        hf, hb = filt[2 * o], filt[2 * o + 1]
        l1 = jnp.sum(jnp.abs(hf), axis=0, keepdims=True) + jnp.sum(jnp.abs(hb), axis=0, keepdims=True)
        inv = 1.0 / (l1 + EPS)
        sym = (hf + hb) * inv
        asym = (hb - hf) * inv
        fr = jnp.dot(cs_ref[0:L, :], sym.astype(BF16), preferred_element_type=F32) * (1.0 / L)
        fr_ref[o] = jnp.where(row == 0, 0.5 * fr, fr)
        fi_ref[o] = jnp.dot(cs_ref[L:2 * L, :], asym.astype(BF16), preferred_element_type=F32) * (1.0 / L)
        fn_ref[o] = jnp.sum(sym * alt, axis=0, keepdims=True) * (0.5 / L)


def _filter_spectra(L, cs, w1, b1, w2, b2, w3, b3, freq, log_decay, *, tc=256):
    w1p = jnp.zeros((FEAT_PAD, FILTER_HIDDEN), F32).at[:FILTER_FEAT].set(w1)
    w3r = w3.reshape(FILTER_HIDDEN, 4, D).transpose(1, 0, 2)
    const = lambda j: (0, 0)
    return pl.pallas_call(
        functools.partial(_filter_body, L=L),
        grid=(D // tc,),
        in_specs=[
            pl.BlockSpec((L, FEAT_PAD), const),
            pl.BlockSpec((FEAT_PAD, FILTER_HIDDEN), const),
            pl.BlockSpec((1, FILTER_HIDDEN), const),
            pl.BlockSpec((FILTER_HIDDEN, FILTER_HIDDEN), const),
            pl.BlockSpec((1, FILTER_HIDDEN), const),
            pl.BlockSpec((1, FILTER_HIDDEN), const),
            pl.BlockSpec((4, FILTER_HIDDEN, tc), lambda j: (0, 0, j)),
            pl.BlockSpec((4, tc), lambda j: (0, j)),
            pl.BlockSpec((4, tc), lambda j: (0, j)),
            pl.BlockSpec((2 * L, L), const, pipeline_mode=pl.Buffered(1)),
        ],
        out_specs=[
            pl.BlockSpec((2, L, tc), lambda j: (0, 0, j)),
            pl.BlockSpec((2, L, tc), lambda j: (0, 0, j)),
            pl.BlockSpec((2, 1, tc), lambda j: (0, 0, j)),
        ],
        out_shape=[
            jax.ShapeDtypeStruct((2, L, D), F32),
            jax.ShapeDtypeStruct((2, L, D), F32),
            jax.ShapeDtypeStruct((2, 1, D), F32),
        ],
        scratch_shapes=[pltpu.VMEM((L, FILTER_HIDDEN), BF16)],
        compiler_params=_cparams(("arbitrary",)),
        name="hyena_filter",
    )(jnp.asarray(_filter_feats(L)), w1p, b1.reshape(1, -1), w2, b2.reshape(1, -1), freq.reshape(1, -1),
      w3r, b3.reshape(4, D), log_decay.reshape(4, D), cs)


FREQ_CHUNK = 1024


def _hyconv_body(x1_ref, x2_ref, v_ref, sw_ref, sb_ref, skip_ref, fr_ref, fi_ref, fn_ref, cs_ref, o_ref,
                 z_ref, g_ref, zb_ref, yr_ref, yi_ref, *, L):
    row, alt = _alt_sign(L)
    kc = min(FREQ_CHUNK, L)
    chunk_row = lax.broadcasted_iota(jnp.int32, (kc, 1), 0)

    def short_conv(u_ref, p):
        u = u_ref[...].astype(F32)
        prev = jnp.where(row == 0, 0.0, pltpu.roll(u, 1, 0))
        nxt = jnp.where(row == L - 1, 0.0, pltpu.roll(u, L - 1, 0))
        return (prev * sw_ref[0, p:p + 1, :] + u * sw_ref[1, p:p + 1, :] + nxt * sw_ref[2, p:p + 1, :]
                + sb_ref[p:p + 1, :])

    z_ref[...] = short_conv(v_ref, 2)
    g_ref[0] = short_conv(x1_ref, 0)
    g_ref[1] = short_conv(x2_ref, 1)
    for o in range(2):
        z = z_ref[...]
        zb_ref[...] = z.astype(BF16)
        nyq = jnp.sum(z * alt, axis=0, keepdims=True) * fn_ref[o]

        def to_freq(c, carry):
            lo = pl.multiple_of(c * kc, kc)
            zr = jnp.dot(cs_ref[pl.ds(lo, kc), :], zb_ref[...], preferred_element_type=F32)
            zs = jnp.dot(cs_ref[pl.ds(L + lo, kc), :], zb_ref[...], preferred_element_type=F32)
            fr = fr_ref[o, pl.ds(lo, kc), :]
            fi = fi_ref[o, pl.ds(lo, kc), :]
            yr_ref[pl.ds(lo, kc), :] = (zr * fr + zs * fi).astype(BF16)
            yi_ref[pl.ds(lo, kc), :] = (zr * fi - zs * fr).astype(BF16)
            return carry

        lax.fori_loop(0, L // kc, to_freq, 0)

        def to_time(c, carry):
            lo = pl.multiple_of(c * kc, kc)
            y = (jnp.dot(cs_ref[pl.ds(lo, kc), :], yr_ref[...], preferred_element_type=F32)
                 - jnp.dot(cs_ref[pl.ds(L + lo, kc), :], yi_ref[...], preferred_element_type=F32))
            y = y + jnp.where(((chunk_row + lo) & 1) == 0, nyq, -nyq)
            z_ref[pl.ds(lo, kc), :] = g_ref[o, pl.ds(lo, kc), :] * (y + skip_ref[o:o + 1, :] * z_ref[pl.ds(lo, kc), :])
            return carry

        lax.fori_loop(0, L // kc, to_time, 0)
    o_ref[...] = z_ref[...].astype(BF16)


def _hyena_conv(up, sw, sb, skip, fr, fi, fn, cs, *, row0, n_batch, L, tc):
    n_ct = D // tc
    base = row0 // L
    u_spec = lambda p: pl.BlockSpec((L, tc), lambda j, b: (base + b, p * n_ct + j))
    once = pl.Buffered(1)
    return pl.pallas_call(
        functools.partial(_hyconv_body, L=L),
        grid=(n_ct, n_batch),
        in_specs=[
            u_spec(0), u_spec(1), u_spec(2),
            pl.BlockSpec((3, 3, tc), lambda j, b: (0, 0, j)),
            pl.BlockSpec((3, tc), lambda j, b: (0, j)),
            pl.BlockSpec((2, tc), lambda j, b: (0, j)),
            pl.BlockSpec((2, L, tc), lambda j, b: (0, 0, j), pipeline_mode=once),
            pl.BlockSpec((2, L, tc), lambda j, b: (0, 0, j), pipeline_mode=once),
            pl.BlockSpec((2, 1, tc), lambda j, b: (0, 0, j)),
            pl.BlockSpec((2 * L, L), lambda j, b: (0, 0), pipeline_mode=once),
        ],
        out_specs=pl.BlockSpec((L, tc), lambda j, b: (b, j)),
        out_shape=jax.ShapeDtypeStruct((n_batch * L, D), BF16),
        scratch_shapes=[
            pltpu.VMEM((L, tc), F32),
            pltpu.VMEM((2, L, tc), F32),
            pltpu.VMEM((L, tc), BF16),
            pltpu.VMEM((L, tc), BF16),
            pltpu.VMEM((L, tc), BF16),
        ],
        compiler_params=_cparams(("arbitrary", "arbitrary")),
        name="hyena_conv",
    )(up, up, up, sw.reshape(3, 3, D), sb.reshape(3, D), skip, fr, fi, fn, cs)


EXP_TILE = 512
SORT_TILES = (2 * N_TOK) // EXP_TILE + N_EXP
SORT_ROWS = SORT_TILES * EXP_TILE
N_QUARTERS = 4
QUARTER_W = D // (2 * N_QUARTERS)
META_I1, META_I2, META_R1, META_R2, META_W1, META_W2 = range(6)
HI_HALF = 0xFFFF0000


def _pack_pairs(x):
    bits = pltpu.bitcast(x.astype(BF16).astype(F32), jnp.uint32)
    return (bits[:, :QUARTER_W] >> 16) | bits[:, QUARTER_W:]


def _unpack_pairs(w):
    return pltpu.bitcast(w << 16, F32), pltpu.bitcast(w & jnp.uint32(HI_HALF), F32)


def _packed_shape(rows):
    return (rows // 8, N_QUARTERS, 8, QUARTER_W)


def _store_packed(ref, x):
    for q in range(N_QUARTERS):
        ref[:, q] = _pack_pairs(x[:, q * 2 * QUARTER_W:(q + 1) * 2 * QUARTER_W]).reshape(-1, 8, QUARTER_W)


def _load_unpacked(ref):
    halves = []
    for q in range(N_QUARTERS):
        halves.extend(_unpack_pairs(ref[:, q].reshape(-1, QUARTER_W)))
    return jnp.concatenate(halves, axis=1)


def _packed_row(ref, tile, sublane):
    return ref.at[tile, :, sublane, :]


def _lane_put(lane, values):
    out = jnp.where(lane == 0, values[0], 0.0)
    for k in range(1, len(values)):
        out = out + jnp.where(lane == k, values[k], 0.0)
    return out


def _route_body(*refs, tm, split_x):
    n_x = 2 if split_x else 1
    ac_ref, al_ref, w_ref, b_ref = refs[:4]
    x_refs = refs[4:4 + n_x]
    (ada_ref, gain_ref, wr_hi_ref, wr_lo_ref, br_ref, tri_ref,
     xo_ref, hq_ref, meta_ref, cnt_ref, carry_ref) = refs[4 + n_x:]
    step = pl.program_id(0)

    @pl.when(step == 0)
    def _():
        carry_ref[...] = jnp.zeros_like(carry_ref)

    mixed = jnp.dot(_pick_split(step, tm, ac_ref, al_ref), w_ref[...], preferred_element_type=F32) + b_ref[...]
    x = _pick_split(step, tm, *x_refs) if split_x else x_refs[0][...]
    x = x + ada_ref[0, 2:3, :] * mixed
    xo_ref[...] = x
    h = _norm_mod(x, gain_ref, ada_ref, 3)
    _store_packed(hq_ref, h)
    h_hi = h.astype(BF16)
    h_lo = (h - h_hi.astype(F32)).astype(BF16)
    logits = (jnp.dot(h_hi, wr_hi_ref[...], preferred_element_type=F32)
              + jnp.dot(h_lo, wr_hi_ref[...], preferred_element_type=F32)
              + jnp.dot(h_hi, wr_lo_ref[...], preferred_element_type=F32) + br_ref[...])
    lane = lax.broadcasted_iota(jnp.int32, (1, ROUTE_W), 1)
    lane_f = lane.astype(F32)
    group_of_lane = (lane >> 3).astype(F32)
    neg = -jnp.inf
    big = float(ROUTE_W)
    is_g = (lane >= N_EXP) & (lane < N_EXP + N_GROUPS)
    gl = jnp.where(is_g, logits, neg)
    gmax = jnp.max(gl, axis=-1, keepdims=True)
    gidx = jnp.min(jnp.where(gl == gmax, lane_f - N_EXP, big), axis=-1, keepdims=True)
    g_top = 1.0 / jnp.sum(jnp.where(is_g, jnp.exp(gl - gmax), 0.0), axis=-1, keepdims=True)
    in_group = (lane < N_EXP) & (group_of_lane == gidx)
    el = jnp.where(in_group, logits, neg)
    v1 = jnp.max(el, axis=-1, keepdims=True)
    i1 = jnp.min(jnp.where(el == v1, lane_f, big), axis=-1, keepdims=True)
    el2 = jnp.where(lane_f == i1, neg, el)
    v2 = jnp.max(el2, axis=-1, keepdims=True)
    i2 = jnp.min(jnp.where(el2 == v2, lane_f, big), axis=-1, keepdims=True)
    r = jnp.exp(v2 - v1)
    w1 = g_top / (1.0 + r)
    w2 = g_top * r / (1.0 + r)
    sel = jnp.where((lane_f == i1) | (lane_f == i2), 1.0, 0.0)
    rank = jnp.dot(tri_ref[...], sel.astype(BF16), preferred_element_type=F32) + carry_ref[...]
    r1 = jnp.sum(jnp.where(lane_f == i1, rank, 0.0), axis=-1, keepdims=True)
    r2 = jnp.sum(jnp.where(lane_f == i2, rank, 0.0), axis=-1, keepdims=True)
    carry_ref[...] += jnp.sum(sel, axis=0, keepdims=True)
    cnt_ref[...] = carry_ref[...]
    meta_ref[...] = _lane_put(lane, (i1, i2, r1, r2, w1, w2))


@functools.lru_cache(maxsize=None)
def _strict_lower(n):
    r = np.arange(n)
    return (r[None, :] < r[:, None]).astype(np.float32)


def _mix_route(a_ctx, a_lat, w_out, b_out, x_parts, ada, gain, wg, bg, we, be, *, tm=1024):
    k = a_ctx.shape[1]
    split_x = len(x_parts) == 2
    x_specs = _split_specs(tm, D) if split_x else [pl.BlockSpec((tm, D), lambda i: (i, 0))]
    wr = jnp.zeros((D, ROUTE_W), F32).at[:, :N_EXP].set(we.reshape(D, N_EXP)).at[:, N_EXP:N_EXP + N_GROUPS].set(wg)
    br = jnp.zeros((1, ROUTE_W), F32).at[0, :N_EXP].set(be.reshape(N_EXP)).at[0, N_EXP:N_EXP + N_GROUPS].set(bg)
    wr_hi = wr.astype(BF16)
    wr_lo = (wr - wr_hi.astype(F32)).astype(BF16)
    return pl.pallas_call(
        functools.partial(_route_body, tm=tm, split_x=split_x),
        grid=(N_TOK // tm,),
        in_specs=_split_specs(tm, k) + [
            pl.BlockSpec((k, D), lambda i: (0, 0)),
            pl.BlockSpec((1, D), lambda i: (0, 0)),
        ] + x_specs + [
            pl.BlockSpec((1, 6, D), lambda i: (_cond_row(i, tm), 0, 0)),
            pl.BlockSpec((1, D), lambda i: (0, 0)),
            pl.BlockSpec((D, ROUTE_W), lambda i: (0, 0)),
            pl.BlockSpec((D, ROUTE_W), lambda i: (0, 0)),
            pl.BlockSpec((1, ROUTE_W), lambda i: (0, 0)),
            pl.BlockSpec((tm, tm), lambda i: (0, 0)),
        ],
        out_specs=[
            pl.BlockSpec((tm, D), lambda i: (i, 0)),
            pl.BlockSpec(_packed_shape(tm), lambda i: (i, 0, 0, 0)),
            pl.BlockSpec((tm, ROUTE_W), lambda i: (i, 0)),
            pl.BlockSpec((1, ROUTE_W), lambda i: (0, 0)),
        ],
        out_shape=[
            jax.ShapeDtypeStruct((N_TOK, D), F32),
            jax.ShapeDtypeStruct(_packed_shape(N_TOK), jnp.uint32),
            jax.ShapeDtypeStruct((N_TOK, ROUTE_W), F32),
            jax.ShapeDtypeStruct((1, ROUTE_W), F32),
        ],
        scratch_shapes=[pltpu.VMEM((1, ROUTE_W), F32)],
        compiler_params=_cparams(("arbitrary",)),
        name="mix_route",
    )(a_ctx, a_lat, w_out, b_out.reshape(1, D), *x_parts, ada, gain.reshape(1, D), wr_hi, wr_lo, br,
      jnp.asarray(_strict_lower(tm)).astype(BF16))


def _slots_body(meta_ref, cnt_ref, upper_ref, pos_ref, plan_ref):
    lane = lax.broadcasted_iota(jnp.int32, (1, ROUTE_W), 1)
    lane_f = lane.astype(F32)
    tiles = jnp.floor((cnt_ref[...] + (EXP_TILE - 1.0)) * (1.0 / EXP_TILE))
    end_tile = jnp.dot(jnp.broadcast_to(tiles, (8, ROUTE_W)).astype(BF16), upper_ref[...],
                       preferred_element_type=F32)[0:1]
    start_row = (end_tile - tiles) * EXP_TILE
    meta = meta_ref[...]
    i1 = meta[:, META_I1:META_I1 + 1]
    i2 = meta[:, META_I2:META_I2 + 1]
    p1 = jnp.sum(jnp.where(lane_f == i1, start_row, 0.0), axis=-1, keepdims=True) + meta[:, META_R1:META_R1 + 1]
    p2 = jnp.sum(jnp.where(lane_f == i2, start_row, 0.0), axis=-1, keepdims=True) + meta[:, META_R2:META_R2 + 1]
    hi1, hi2 = jnp.floor(p1 * 0.125), jnp.floor(p2 * 0.125)
    pos_ref[0] = _lane_put(lane, (hi1, p1 - 8.0 * hi1, hi2, p2 - 8.0 * hi2)).T[:8, :].astype(jnp.int32)

    @pl.when(pl.program_id(0) == 0)
    def _():
        end_col = jnp.broadcast_to(end_tile, (ROUTE_W, ROUTE_W)).T
        expert = lax.broadcasted_iota(jnp.int32, (ROUTE_W, ROUTE_W), 0)
        tile = lax.broadcasted_iota(jnp.int32, (ROUTE_W, ROUTE_W), 1).astype(F32)
        passed = jnp.where((end_col <= tile) & (expert < N_EXP), 1.0, 0.0)
        tile_expert = jnp.minimum(jnp.sum(passed, axis=0, keepdims=True), N_EXP - 1.0)
        used = jnp.max(end_tile, axis=-1, keepdims=True)
        row = lax.broadcasted_iota(jnp.int32, (8, ROUTE_W), 0)
        plan_ref[...] = jnp.where(row == 0, tile_expert, jnp.where(row == 1, used, 0.0)).astype(jnp.int32)


@functools.lru_cache(maxsize=None)
def _upper_incl(n):
    r = np.arange(n)
    return (r[:, None] <= r[None, :]).astype(np.float32)


def _slots(meta, cnt, *, tm=2048):
    n_tiles = N_TOK // tm
    return pl.pallas_call(
        _slots_body,
        grid=(n_tiles,),
        in_specs=[
            pl.BlockSpec((tm, ROUTE_W), lambda i: (i, 0)),
            pl.BlockSpec((1, ROUTE_W), lambda i: (0, 0)),
            pl.BlockSpec((ROUTE_W, ROUTE_W), lambda i: (0, 0)),
        ],
        out_specs=[
            pl.BlockSpec((1, 8, tm), lambda i: (i, 0, 0)),
            pl.BlockSpec((8, ROUTE_W), lambda i: (0, 0)),
        ],
        out_shape=[
            jax.ShapeDtypeStruct((n_tiles, 8, tm), jnp.int32),
            jax.ShapeDtypeStruct((8, ROUTE_W), jnp.int32),
        ],
        compiler_params=_cparams(("arbitrary",)),
        name="moe_slots",
    )(meta, cnt, jnp.asarray(_upper_incl(ROUTE_W)).astype(BF16))


ROW_UNROLL = 8
N_DMA_LANES = 2


def _start_row_copies(tm, make_copy):
    def block(blk, carry):
        for j in range(ROW_UNROLL):
            lane = j % N_DMA_LANES
            for k in range(2):
                make_copy(blk, j, k, lane).start(priority=lane)
        return carry

    lax.fori_loop(0, tm // ROW_UNROLL, block, 0)


def _drain_row_copies(like_src, like_dst, sem):
    for lane in range(N_DMA_LANES):
        pltpu.make_async_copy(like_src, like_dst, sem.at[lane]).wait()


def _dispatch_body(t1_ref, s1_ref, t2_ref, s2_ref, hq_ref, xs_in_ref, xs_ref, sem, *, tm):
    del xs_in_ref
    slots = ((t1_ref, s1_ref), (t2_ref, s2_ref))

    def copy(blk, j, k, lane):
        i = blk * ROW_UNROLL + j
        return pltpu.make_async_copy(_packed_row(hq_ref, blk, j), _packed_row(xs_ref, slots[k][0][i], slots[k][1][i]),
                                     sem.at[lane])

    _start_row_copies(tm, copy)
    _drain_row_copies(hq_ref, xs_ref.at[pl.ds(0, tm // 8)], sem)


def _slot_specs(tm, ahead=0):
    last = N_TOK // tm - 1
    return [pl.BlockSpec((tm,), lambda t: (jnp.minimum(t + ahead, last),), memory_space=pltpu.SMEM) for _ in range(4)]


def _dispatch(slots, hq, init, *, tm=512):
    return pl.pallas_call(
        functools.partial(_dispatch_body, tm=tm),
        grid=(N_TOK // tm,),
        in_specs=_slot_specs(tm) + [
            pl.BlockSpec(_packed_shape(tm), lambda t: (t, 0, 0, 0)),
            pl.BlockSpec(memory_space=pl.ANY),
        ],
        out_specs=pl.BlockSpec(memory_space=pl.ANY),
        out_shape=jax.ShapeDtypeStruct(_packed_shape(SORT_ROWS), jnp.uint32),
        input_output_aliases={5: 0},
        scratch_shapes=[pltpu.SemaphoreType.DMA((N_DMA_LANES,))],
        compiler_params=_cparams(("arbitrary",)),
        name="moe_dispatch",
    )(*slots, hq, init)


def _experts_body(plan_ref, xs_ref, wg_ref, wu_ref, wd_ref, ys_ref, wgu_ref, wdn_ref):
    t = pl.program_id(0)
    in_use = t < plan_ref[1, 0]
    new_expert = (t == 0) | (plan_ref[0, t] != plan_ref[0, jnp.maximum(t - 1, 0)])

    @pl.when(in_use & new_expert)
    def _():
        wgu_ref[:, :EXP_H] = wg_ref[0, 0].astype(BF16)
        wgu_ref[:, EXP_H:] = wu_ref[0, 0].astype(BF16)
        wdn_ref[...] = wd_ref[0, 0].astype(BF16)

    @pl.when(in_use)
    def _():
        x = _load_unpacked(xs_ref).astype(BF16)
        ab = jnp.dot(x, wgu_ref[...], preferred_element_type=F32)
        a, b = ab[:, :EXP_H], ab[:, EXP_H:]
        act = a * _sigmoid(a) * b
        _store_packed(ys_ref, jnp.dot(act.astype(BF16), wdn_ref[...], preferred_element_type=F32))


def _experts(plan, xs, w_gate, w_up, w_down, *, layer):
    def tile_idx(t, plan):
        return (jnp.minimum(t, jnp.maximum(plan[1, 0] - 1, 0)), 0, 0, 0)

    grid_spec = pltpu.PrefetchScalarGridSpec(
        num_scalar_prefetch=1,
        grid=(SORT_TILES,),
        in_specs=[
            pl.BlockSpec(_packed_shape(EXP_TILE), tile_idx),
            pl.BlockSpec((1, 1, D, EXP_H), lambda t, plan: (layer, plan[0, t], 0, 0)),
            pl.BlockSpec((1, 1, D, EXP_H), lambda t, plan: (layer, plan[0, t], 0, 0)),
            pl.BlockSpec((1, 1, EXP_H, D), lambda t, plan: (layer, plan[0, t], 0, 0)),
        ],
        out_specs=pl.BlockSpec(_packed_shape(EXP_TILE), tile_idx),
        scratch_shapes=[pltpu.VMEM((D, 2 * EXP_H), BF16), pltpu.VMEM((EXP_H, D), BF16)],
    )
    return pl.pallas_call(
        _experts_body,
        grid_spec=grid_spec,
        out_shape=jax.ShapeDtypeStruct(_packed_shape(SORT_ROWS), jnp.uint32),
        input_output_aliases={1: 0},
        compiler_params=_cparams(("arbitrary",)),
        name="moe_experts",
    )(plan, xs, w_gate, w_up, w_down)


def _combine_body(*refs, tm, final):
    cur_slots, nxt_slots = refs[0:4], refs[4:8]
    meta_ref, x_ref, ada_ref = refs[8:11]
    rest = refs[11:]
    if final:
        fgain_ref, ys_ref, oc_ref, ol_ref, y_ref, sem = rest
    else:
        ys_ref, o_ref, y_ref, sem = rest
    step = pl.program_id(0)
    n_steps = pl.num_programs(0)

    def fetch(slot_refs, buf):
        slots = ((slot_refs[0], slot_refs[1]), (slot_refs[2], slot_refs[3]))

        def copy(blk, j, k, lane):
            i = blk * ROW_UNROLL + j
            return pltpu.make_async_copy(_packed_row(ys_ref, slots[k][0][i], slots[k][1][i]),
                                         _packed_row(y_ref.at[buf, k], blk, j), sem.at[buf, lane])

        _start_row_copies(tm, copy)

    @pl.when(step == 0)
    def _():
        fetch(cur_slots, 0)

    @pl.when(step + 1 < n_steps)
    def _():
        fetch(nxt_slots, (step + 1) % 2)

    buf = step % 2
    _drain_row_copies(ys_ref.at[pl.ds(0, tm // 8)], y_ref.at[buf, 0], sem.at[buf])
    y_ref = y_ref.at[buf]
    meta = meta_ref[...]
    mix = (meta[:, META_W1:META_W1 + 1] * _load_unpacked(y_ref.at[0])
           + meta[:, META_W2:META_W2 + 1] * _load_unpacked(y_ref.at[1]))
    x = x_ref[...] + ada_ref[0, 5:6, :] * mix
    if not final:
        o_ref[...] = x
        return
    ms = jnp.mean(x * x, axis=-1, keepdims=True)
    y = x * lax.rsqrt(ms + EPS) * fgain_ref[...]
    is_ctx = pl.program_id(0) < N_CTX // tm

    @pl.when(is_ctx)
    def _():
        oc_ref[...] = y

    @pl.when(jnp.logical_not(is_ctx))
    def _():
        ol_ref[...] = y


def _combine(slots, ys, meta, x, ada, final_gain=None, *, tm=512):
    final = final_gain is not None
    extra_in = [pl.BlockSpec((1, D), lambda t: (0, 0))] if final else []
    extra_args = [final_gain.reshape(1, D)] if final else []
    if final:
        out_specs = _split_specs(tm, D)
        out_shape = [jax.ShapeDtypeStruct((N_CTX, D), F32), jax.ShapeDtypeStruct((N_LAT, D), F32)]
    else:
        out_specs = pl.BlockSpec((tm, D), lambda t: (t, 0))
        out_shape = jax.ShapeDtypeStruct((N_TOK, D), F32)
    return pl.pallas_call(
        functools.partial(_combine_body, tm=tm, final=final),
        grid=(N_TOK // tm,),
        in_specs=_slot_specs(tm) + _slot_specs(tm, ahead=1) + [
            pl.BlockSpec((tm, ROUTE_W), lambda t: (t, 0)),
            pl.BlockSpec((tm, D), lambda t: (t, 0)),
            pl.BlockSpec((1, 6, D), lambda t: (_cond_row(t, tm), 0, 0)),
        ] + extra_in + [pl.BlockSpec(memory_space=pl.ANY)],
        out_specs=out_specs,
        out_shape=out_shape,
        scratch_shapes=[pltpu.VMEM((2, 2) + _packed_shape(tm), jnp.uint32),
                        pltpu.SemaphoreType.DMA((2, N_DMA_LANES))],
        compiler_params=_cparams(("arbitrary",)),
        name="moe_combine",
    )(*slots, *slots, meta, x, ada, *extra_args, ys)


def _moe(routed, ada, w_gate, w_up, w_down, *, layer, sort_init, final_gain=None):
    x, hq, meta, cnt = routed
    pos, plan = _slots(meta, cnt)
    slots = [pos[:, r, :].reshape(N_TOK) for r in range(4)]
    xs = _dispatch(slots, hq, sort_init)
    ys = _experts(plan, xs, w_gate, w_up, w_down, layer=layer)
    return _combine(slots, ys, meta, x, ada, final_gain), ys


def kernel(x_prompt, x_sample, cache_k, cache_v, c, c_ctx, w_ada, b_ada, norm_mix, norm_ffn, attn_w_q, attn_w_kv, attn_q_norm, attn_k_norm, attn_w_o, hy_w_in, hy_b_in, hy_short_w, hy_short_b, hy_filt_w1, hy_filt_b1, hy_filt_w2, hy_filt_b2, hy_filt_w3, hy_filt_b3, hy_filt_freq, hy_log_decay, hy_skip, hy_w_out, hy_b_out, router_group_w, router_group_b, router_expert_w, router_expert_b, moe_w_gate, moe_w_up, moe_w_down, final_norm):
    depth = w_ada.shape[0]
    x_parts = (x_prompt.reshape(N_CTX, D), x_sample.reshape(N_LAT, D))
    cond8 = jnp.zeros((8, D), F32).at[0].set(c_ctx).at[1:1 + N_BATCH_LAT].set(c)
    ada_all = _ada(cond8, w_ada, b_ada).reshape(depth, 8, 6, D)
    new_k = []
    new_v = []
    sort_buf = jnp.zeros(_packed_shape(SORT_ROWS), jnp.uint32)
    for l in range(depth):
        ada = ada_all[l]
        if l % 2 == 0:
            a = l // 2
            w_qkv = jnp.concatenate([attn_w_q[a], attn_w_kv[a]], axis=1).astype(BF16)
            q, k_ctx, v_ctx, kc, vc, kl, vl = _qkv(x_parts, ada, norm_mix[l], w_qkv, attn_q_norm[a], attn_k_norm[a])
            new_k.append(k_ctx)
            new_v.append(v_ctx)
            o_ctx = _attention(q, [kc], [vc], row0=0, n_batch=N_BATCH_CTX, seq=L_CTX, tq=L_CTX,
                               n_kv=N_KV, stack=Q_PER_KV)
            past_keys = _lane_tile_heads(cache_k[:, a]).transpose(0, 1, 3, 2)
            o_lat = _attention(q, [kl, past_keys], [vl, _lane_tile_heads(cache_v[:, a])],
                               row0=N_CTX, n_batch=N_BATCH_LAT, seq=L_LAT, tq=256, n_kv=1, stack=1)
            mixer = (o_ctx, o_lat, attn_w_o[a].astype(BF16), jnp.zeros((D,), F32))
        else:
            j = l // 2
            (x,) = x_parts
            up = _norm_proj(x, ada, norm_mix[l], hy_w_in[j].astype(BF16), hy_b_in[j], part=0)
            outs = []
            for row0, n_batch, L, tc in ((0, N_BATCH_CTX, L_CTX, D), (N_CTX, N_BATCH_LAT, L_LAT, 256)):
                cs = jnp.asarray(_dft_table(L)).astype(BF16)
                fr, fi, fn = _filter_spectra(L, cs, hy_filt_w1[j], hy_filt_b1[j], hy_filt_w2[j], hy_filt_b2[j],
                                             hy_filt_w3[j], hy_filt_b3[j], hy_filt_freq[j], hy_log_decay[j])
                outs.append(_hyena_conv(up, hy_short_w[j], hy_short_b[j], hy_skip[j], fr, fi, fn, cs,
                                        row0=row0, n_batch=n_batch, L=L, tc=tc))
            mixer = (outs[0], outs[1], hy_w_out[j].astype(BF16), hy_b_out[j])
        routed = _mix_route(*mixer, x_parts, ada, norm_ffn[l], router_group_w[l], router_group_b[l],
                            router_expert_w[l], router_expert_b[l])
        out, sort_buf = _moe(routed, ada, moe_w_gate, moe_w_up, moe_w_down, layer=l, sort_init=sort_buf,
                             final_gain=final_norm if l == depth - 1 else None)
        x_parts = (out,)
    y_ctx, y_lat = x_parts[0]
    return (y_ctx.reshape(N_BATCH_CTX, L_CTX, D), y_lat.reshape(N_BATCH_LAT, L_LAT, D),
            jnp.stack(new_k, axis=1), jnp.stack(new_v, axis=1))
```

```python
import functools
import math

import numpy as np
import jax
import jax.numpy as jnp
from jax import lax
from jax.experimental import pallas as pl
from jax.experimental.pallas import tpu as pltpu

F32 = jnp.float32
BF16 = jnp.bfloat16

D = 1024
N_BATCH_CTX = 32
L_CTX = 256
N_BATCH_LAT = 2
L_LAT = 2048
PAST = 256
N_CTX = N_BATCH_CTX * L_CTX
N_LAT = N_BATCH_LAT * L_LAT
N_TOK = N_CTX + N_LAT
GRID_W = 64
N_HEADS = 16
N_KV = 4
DH = 64
Q_PER_KV = N_HEADS // N_KV
KV_W = N_KV * DH
ROPE_THETA = 10000.0
FILTER_FEAT = 17
FEAT_PAD = 32
FILTER_HIDDEN = 64
N_GROUPS = 4
E_PER_G = 8
N_EXP = N_GROUPS * E_PER_G
EXP_H = D // 4
EPS = 1e-6
ROUTE_W = 128
VMEM_LIMIT = 56 * 1024 * 1024


def _cparams(sem):
    return pltpu.CompilerParams(dimension_semantics=sem, vmem_limit_bytes=VMEM_LIMIT)


def _cond_row(i, tm):
    n_ctx_tiles = N_CTX // tm
    return jnp.where(i < n_ctx_tiles, 0, 1 + (i - n_ctx_tiles) // (L_LAT // tm))


def _split_specs(tm, width):
    n_ctx_tiles = N_CTX // tm
    return [pl.BlockSpec((tm, width), lambda i: (jnp.minimum(i, n_ctx_tiles - 1), 0)),
            pl.BlockSpec((tm, width), lambda i: (jnp.maximum(i - n_ctx_tiles, 0), 0))]


def _pick_split(i, tm, ctx_ref, lat_ref):
    return jnp.where(i < N_CTX // tm, ctx_ref[...], lat_ref[...])


def _sigmoid(x):
    return 1.0 / (1.0 + jnp.exp(-x))


@functools.lru_cache(maxsize=None)
def _dft_table(L):
    k = np.arange(L, dtype=np.int64)
    ang = (np.outer(k, k) % (2 * L)).astype(np.float64) * (math.pi / L)
    return np.concatenate([np.cos(ang), np.sin(ang)], axis=0).astype(np.float32)


@functools.lru_cache(maxsize=None)
def _filter_feats(L):
    t = np.linspace(0.0, 1.0, L, dtype=np.float64)[:, None]
    bands = np.linspace(1e-4, 7.0, 8, dtype=np.float64)[None, :]
    w = (2.0 * math.pi) * np.arange(L, dtype=np.float64)[:, None] / L
    feats = np.concatenate([t, np.cos(bands * w), -np.sin(bands * w)], axis=-1)
    out = np.zeros((L, FEAT_PAD), np.float32)
    out[:, :FILTER_FEAT] = feats
    return out


@functools.lru_cache(maxsize=None)
def _rope_tables(tm):
    pos = np.arange(L_LAT)
    row = (pos // GRID_W).astype(np.float64)
    col = (pos % GRID_W).astype(np.float64)
    axis_dim = DH // 2
    inv_freq = ROPE_THETA ** (-np.arange(0, axis_dim, 2, dtype=np.float64) / axis_dim)
    lane = np.arange(KV_W)
    d = lane % DH
    is_col = (d // axis_dim) == 1
    fi = d % (axis_dim // 2)
    first_half = (d % axis_dim) < (axis_dim // 2)
    p = np.where(is_col[None, :], col[:, None], row[:, None])
    ang = p * inv_freq[fi][None, :]
    cos = np.cos(ang)
    sin = np.sin(ang) * np.where(first_half, -1.0, 1.0)[None, :]
    cos = np.concatenate([np.ones((tm, KV_W)), cos], axis=0).astype(np.float32)
    sin = np.concatenate([np.zeros((tm, KV_W)), sin], axis=0).astype(np.float32)
    return cos, sin


@functools.lru_cache(maxsize=None)
def _head_sum_matrix():
    lane = np.arange(KV_W)
    return (lane[:, None] // DH == lane[None, :] // DH).astype(np.float32)


def _ada_body(c_ref, w_ref, b_ref, o_ref):
    c = c_ref[...]
    s = c * _sigmoid(c)
    o_ref[0] = jnp.dot(s.astype(BF16), w_ref[0].astype(BF16), preferred_element_type=F32) + b_ref[0]


def _ada(cond8, w_ada, b_ada):
    depth = w_ada.shape[0]
    tn = 1536
    return pl.pallas_call(
        _ada_body,
        grid=(depth, 6 * D // tn),
        in_specs=[
            pl.BlockSpec((8, D), lambda l, j: (0, 0)),
            pl.BlockSpec((1, D, tn), lambda l, j: (l, 0, j)),
            pl.BlockSpec((1, 1, tn), lambda l, j: (l, 0, j)),
        ],
        out_specs=pl.BlockSpec((1, 8, tn), lambda l, j: (l, 0, j)),
        out_shape=jax.ShapeDtypeStruct((depth, 8, 6 * D), F32),
        compiler_params=_cparams(("arbitrary", "arbitrary")),
        name="ada",
    )(cond8, w_ada, b_ada.reshape(depth, 1, 6 * D))


def _norm_mod(x, gain_ref, ada_ref, part):
    ms = jnp.mean(x * x, axis=-1, keepdims=True)
    y = x * lax.rsqrt(ms + EPS) * gain_ref[...]
    return y * (1.0 + ada_ref[0, part + 1:part + 2, :]) + ada_ref[0, part:part + 1, :]


def _head_rms_rope(x, gain, hs, cos, sin, lane):
    ss = jnp.dot((x * x).astype(BF16), hs, preferred_element_type=F32)
    xn = x * lax.rsqrt(ss * (1.0 / DH) + EPS) * gain
    fwd = pltpu.roll(xn, KV_W - DH // 4, 1)
    bwd = pltpu.roll(xn, DH // 4, 1)
    partner = jnp.where((lane & (DH // 4)) == 0, fwd, bwd)
    return xn * cos + partner * sin


@functools.lru_cache(maxsize=None)
def _lane_tile_matrix():
    col = np.arange(N_KV * KV_W)
    src = (col // KV_W) * DH + col % DH
    return (np.arange(KV_W)[:, None] == src[None, :]).astype(np.float32)


def _qkv_body(*refs, tm, split_x):
    n_x = 2 if split_x else 1
    x_refs = refs[:n_x]
    (ada_ref, gain_ref, w_ref, qg_ref, kg_ref, hs_ref, cos_ref, sin_ref, tile_ref, tile_t_ref,
     q_ref, newk_ref, newv_ref, kc_ref, vc_ref, kl_ref, vl_ref) = refs[n_x:]
    step = pl.program_id(0)
    x = _pick_split(step, tm, *x_refs) if split_x else x_refs[0][...]
    h = _norm_mod(x, gain_ref, ada_ref, 0)
    acc = jnp.dot(h.astype(BF16), w_ref[...], preferred_element_type=F32)
    hs = hs_ref[...]
    cos = cos_ref[...]
    sin = sin_ref[...]
    lane = lax.broadcasted_iota(jnp.int32, (1, KV_W), 1)
    for c in range(N_KV):
        qc = _head_rms_rope(acc[:, c * KV_W:(c + 1) * KV_W], qg_ref[...], hs, cos, sin, lane)
        q_ref[:, c * KV_W:(c + 1) * KV_W] = (qc * (DH ** -0.5)).astype(BF16)
    k = _head_rms_rope(acc[:, D:D + KV_W], kg_ref[...], hs, cos, sin, lane)
    v = acc[:, D + KV_W:D + 2 * KV_W]
    k4t = lax.dot_general(tile_t_ref[...], k.astype(BF16), (((1,), (1,)), ((), ())),
                          preferred_element_type=F32).astype(BF16)
    v4 = jnp.dot(v.astype(BF16), tile_ref[...], preferred_element_type=F32).astype(BF16)
    is_ctx = step < N_CTX // tm

    @pl.when(is_ctx)
    def _():
        for bb in range(tm // L_CTX):
            rows = slice(bb * L_CTX, (bb + 1) * L_CTX)
            for hd in range(N_KV):
                newk_ref[bb, hd] = k[rows, hd * DH:(hd + 1) * DH]
                newv_ref[bb, hd] = v[rows, hd * DH:(hd + 1) * DH]
                kc_ref[bb, hd] = k4t[hd * KV_W:(hd + 1) * KV_W, rows]
                vc_ref[bb, hd] = v4[rows, hd * KV_W:(hd + 1) * KV_W]

    @pl.when(jnp.logical_not(is_ctx))
    def _():
        for hd in range(N_KV):
            kl_ref[0, hd] = k4t[hd * KV_W:(hd + 1) * KV_W, :]
            vl_ref[0, hd] = v4[:, hd * KV_W:(hd + 1) * KV_W]


def _qkv(x_parts, ada, gain, w_qkv, q_gain, k_gain):
    tm = 2 * L_CTX
    per_tile = tm // L_CTX
    cos, sin = _rope_tables(tm)
    split_x = len(x_parts) == 2
    x_specs = _split_specs(tm, D) if split_x else [pl.BlockSpec((tm, D), lambda i: (i, 0))]
    n_ctx_tiles = N_CTX // tm
    lat_tiles = L_LAT // tm

    def rope_idx(i):
        return (jnp.where(i < n_ctx_tiles, 0, 1 + (i - n_ctx_tiles) % lat_tiles), 0)

    def ctx_idx(i):
        return (jnp.minimum(i, n_ctx_tiles - 1), 0, 0, 0)

    def lat_idx(i):
        j = jnp.maximum(i - n_ctx_tiles, 0)
        return (j // lat_tiles, 0, j % lat_tiles, 0)

    def lat_idx_t(i):
        j = jnp.maximum(i - n_ctx_tiles, 0)
        return (j // lat_tiles, 0, 0, j % lat_tiles)

    const = lambda i: (0, 0)
    kv_ctx = jax.ShapeDtypeStruct((N_BATCH_CTX, N_KV, L_CTX, DH), F32)
    tiled_ctx = jax.ShapeDtypeStruct((N_BATCH_CTX, N_KV, L_CTX, KV_W), BF16)
    tiled_lat = jax.ShapeDtypeStruct((N_BATCH_LAT, N_KV, L_LAT, KV_W), BF16)
    keys_ctx = jax.ShapeDtypeStruct((N_BATCH_CTX, N_KV, KV_W, L_CTX), BF16)
    keys_lat = jax.ShapeDtypeStruct((N_BATCH_LAT, N_KV, KV_W, L_LAT), BF16)
    return pl.pallas_call(
        functools.partial(_qkv_body, tm=tm, split_x=split_x),
        grid=(N_TOK // tm,),
        in_specs=x_specs + [
            pl.BlockSpec((1, 6, D), lambda i: (_cond_row(i, tm), 0, 0)),
            pl.BlockSpec((1, D), const),
            pl.BlockSpec((D, D + 2 * KV_W), const),
            pl.BlockSpec((1, KV_W), const),
            pl.BlockSpec((1, KV_W), const),
            pl.BlockSpec((KV_W, KV_W), const),
            pl.BlockSpec((tm, KV_W), rope_idx),
            pl.BlockSpec((tm, KV_W), rope_idx),
            pl.BlockSpec((KV_W, N_KV * KV_W), const),
            pl.BlockSpec((N_KV * KV_W, KV_W), const),
        ],
        out_specs=[
            pl.BlockSpec((tm, D), lambda i: (i, 0)),
            pl.BlockSpec((per_tile, N_KV, L_CTX, DH), ctx_idx),
            pl.BlockSpec((per_tile, N_KV, L_CTX, DH), ctx_idx),
            pl.BlockSpec((per_tile, N_KV, KV_W, L_CTX), ctx_idx),
            pl.BlockSpec((per_tile, N_KV, L_CTX, KV_W), ctx_idx),
            pl.BlockSpec((1, N_KV, KV_W, tm), lat_idx_t),
            pl.BlockSpec((1, N_KV, tm, KV_W), lat_idx),
        ],
        out_shape=[jax.ShapeDtypeStruct((N_TOK, D), BF16), kv_ctx, kv_ctx, keys_ctx, tiled_ctx, keys_lat, tiled_lat],
        compiler_params=_cparams(("arbitrary",)),
        name="qkv",
    )(*x_parts, ada, gain.reshape(1, D), w_qkv,
      jnp.tile(q_gain, Q_PER_KV).reshape(1, KV_W), jnp.tile(k_gain, N_KV).reshape(1, KV_W),
      jnp.asarray(_head_sum_matrix()).astype(BF16), jnp.asarray(cos), jnp.asarray(sin),
      jnp.asarray(_lane_tile_matrix()).astype(BF16), jnp.asarray(_lane_tile_matrix().T.copy()).astype(BF16))


def _attn_body(*refs, n_kv, stack, n_seg):
    q_ref, k_refs, v_refs, o_ref = refs[0], refs[1:1 + n_seg], refs[1 + n_seg:1 + 2 * n_seg], refs[-1]
    lane = lax.broadcasted_iota(jnp.int32, (1, KV_W), 1)
    masks = [(lane >> 6) == g for g in range(Q_PER_KV)]
    tq = q_ref.shape[0]
    for kv in range(n_kv):
        q = q_ref[:, kv * KV_W:(kv + 1) * KV_W]
        out = jnp.zeros((tq, KV_W), F32)
        for c in range(0, Q_PER_KV, stack):
            pair = masks[c:c + stack]
            ones_blk = (c + stack) % Q_PER_KV if stack < Q_PER_KV else None
            stacked = jnp.concatenate([jnp.where(m, q, jnp.zeros_like(q)) for m in pair], axis=0)
            scores = [jnp.dot(stacked, k_ref[0, kv], preferred_element_type=F32) for k_ref in k_refs]
            top = functools.reduce(jnp.maximum, [jnp.max(s, axis=-1, keepdims=True) for s in scores])
            probs = [jnp.exp((s - top).astype(BF16)) for s in scores]
            if ones_blk is None:
                vals = [v_ref[0, kv] for v_ref in v_refs]
            else:
                vals = [jnp.where(masks[ones_blk], jnp.ones((), BF16), v_ref[0, kv]) for v_ref in v_refs]
            og = sum(jnp.dot(p, v, preferred_element_type=F32) for p, v in zip(probs, vals))
            if ones_blk is None:
                denom = sum(jnp.sum(p.astype(F32), axis=-1, keepdims=True) for p in probs)
            else:
                denom = og[:, ones_blk * DH:ones_blk * DH + 1]
            og = og * (1.0 / denom)
            for g, m in enumerate(pair):
                out = out + jnp.where(m, og[g * tq:(g + 1) * tq], 0.0)
        o_ref[:, kv * KV_W:(kv + 1) * KV_W] = out.astype(BF16)


def _attention(q, keys, values, *, row0, n_batch, seq, tq, n_kv, stack):
    per_b = seq // tq
    base = row0 // tq
    kv_specs = [pl.BlockSpec((1, n_kv) + a.shape[2:], lambda b, h, i: (b, h, 0, 0)) for a in keys + values]
    return pl.pallas_call(
        functools.partial(_attn_body, n_kv=n_kv, stack=stack, n_seg=len(keys)),
        grid=(n_batch, N_KV // n_kv, per_b),
        in_specs=[pl.BlockSpec((tq, n_kv * KV_W), lambda b, h, i: (base + b * per_b + i, h))] + kv_specs,
        out_specs=pl.BlockSpec((tq, n_kv * KV_W), lambda b, h, i: (b * per_b + i, h)),
        out_shape=jax.ShapeDtypeStruct((n_batch * seq, D), BF16),
        compiler_params=_cparams(("arbitrary", "arbitrary", "arbitrary")),
        name="attention",
    )(q, *keys, *values)


def _lane_tile_heads(x):
    return jnp.tile(x.astype(BF16), (1, 1, 1, KV_W // DH))


def _alt_sign(L):
    row = lax.broadcasted_iota(jnp.int32, (L, 1), 0)
    return row, jnp.where((row & 1) == 0, 1.0, -1.0)


def _split_dot(a, b):
    a_hi, b_hi = a.astype(BF16), b.astype(BF16)
    a_lo = (a - a_hi.astype(F32)).astype(BF16)
    b_lo = (b - b_hi.astype(F32)).astype(BF16)
    return (jnp.dot(a_hi, b_hi, preferred_element_type=F32) + jnp.dot(a_lo, b_hi, preferred_element_type=F32)
            + jnp.dot(a_hi, b_lo, preferred_element_type=F32))


def _filter_body(feat_ref, w1_ref, b1_ref, w2_ref, b2_ref, fq_ref, w3_ref, b3_ref, ld_ref, cs_ref,
                 fr_ref, fi_ref, fn_ref, hid_ref, *, L):
    feats = feat_ref[...]

    @pl.when(pl.program_id(0) == 0)
    def _():
        fq = fq_ref[...]
        h1 = jnp.sin(fq * (_split_dot(feats, w1_ref[...]) + b1_ref[...]))
        hid_ref[...] = jnp.sin(fq * (_split_dot(h1, w2_ref[...]) + b2_ref[...])).astype(BF16)

    t = feats[:, 0:1]
    row, alt = _alt_sign(L)
    filt = []
    for j in range(4):
        raw = jnp.dot(hid_ref[...], w3_ref[j].astype(BF16), preferred_element_type=F32) + b3_ref[j:j + 1, :]
        filt.append(raw * jnp.exp(-t * jnp.exp(ld_ref[j:j + 1, :])))
    for o in range(2):
        hf, hb = filt[2 * o], filt[2 * o + 1]
        l1 = jnp.sum(jnp.abs(hf), axis=0, keepdims=True) + jnp.sum(jnp.abs(hb), axis=0, keepdims=True)
        inv = 1.0 / (l1 + EPS)
        sym = (hf + hb) * inv
        asym = (hb - hf) * inv
        fr = jnp.dot(cs_ref[0:L, :], sym.astype(BF16), preferred_element_type=F32) * (1.0 / L)
        fr_ref[o] = jnp.where(row == 0, 0.5 * fr, fr)
        fi_ref[o] = jnp.dot(cs_ref[L:2 * L, :], asym.astype(BF16), preferred_element_type=F32) * (1.0 / L)
        fn_ref[o] = jnp.sum(sym * alt, axis=0, keepdims=True) * (0.5 / L)


def _filter_spectra(L, cs, w1, b1, w2, b2, w3, b3, freq, log_decay, *, tc=256):
    w1p = jnp.zeros((FEAT_PAD, FILTER_HIDDEN), F32).at[:FILTER_FEAT].set(w1)
    w3r = w3.reshape(FILTER_HIDDEN, 4, D).transpose(1, 0, 2)
    const = lambda j: (0, 0)
    return pl.pallas_call(
        functools.partial(_filter_body, L=L),
        grid=(D // tc,),
        in_specs=[
            pl.BlockSpec((L, FEAT_PAD), const),
            pl.BlockSpec((FEAT_PAD, FILTER_HIDDEN), const),
            pl.BlockSpec((1, FILTER_HIDDEN), const),
            pl.BlockSpec((FILTER_HIDDEN, FILTER_HIDDEN), const),
            pl.BlockSpec((1, FILTER_HIDDEN), const),
            pl.BlockSpec((1, FILTER_HIDDEN), const),
            pl.BlockSpec((4, FILTER_HIDDEN, tc), lambda j: (0, 0, j)),
            pl.BlockSpec((4, tc), lambda j: (0, j)),
            pl.BlockSpec((4, tc), lambda j: (0, j)),
            pl.BlockSpec((2 * L, L), const, pipeline_mode=pl.Buffered(1)),
        ],
        out_specs=[
            pl.BlockSpec((2, L, tc), lambda j: (0, 0, j)),
            pl.BlockSpec((2, L, tc), lambda j: (0, 0, j)),
            pl.BlockSpec((2, 1, tc), lambda j: (0, 0, j)),
        ],
        out_shape=[
            jax.ShapeDtypeStruct((2, L, D), F32),
            jax.ShapeDtypeStruct((2, L, D), F32),
            jax.ShapeDtypeStruct((2, 1, D), F32),
        ],
        scratch_shapes=[pltpu.VMEM((L, FILTER_HIDDEN), BF16)],
        compiler_params=_cparams(("arbitrary",)),
        name="hyena_filter",
    )(jnp.asarray(_filter_feats(L)), w1p, b1.reshape(1, -1), w2, b2.reshape(1, -1), freq.reshape(1, -1),
      w3r, b3.reshape(4, D), log_decay.reshape(4, D), cs)


FREQ_CHUNK = 1024


def _hyconv_body(x1_ref, x2_ref, v_ref, sw_ref, sb_ref, skip_ref, fr_ref, fi_ref, fn_ref, cs_ref, o_ref,
                 z_ref, g_ref, zb_ref, yr_ref, yi_ref, *, L):
    row, alt = _alt_sign(L)
    kc = min(FREQ_CHUNK, L)
    chunk_row = lax.broadcasted_iota(jnp.int32, (kc, 1), 0)

    def short_conv(u_ref, p):
        u = u_ref[...].astype(F32)
        prev = jnp.where(row == 0, 0.0, pltpu.roll(u, 1, 0))
        nxt = jnp.where(row == L - 1, 0.0, pltpu.roll(u, L - 1, 0))
        return (prev * sw_ref[0, p:p + 1, :] + u * sw_ref[1, p:p + 1, :] + nxt * sw_ref[2, p:p + 1, :]
                + sb_ref[p:p + 1, :])

    z_ref[...] = short_conv(v_ref, 2)
    g_ref[0] = short_conv(x1_ref, 0)
    g_ref[1] = short_conv(x2_ref, 1)
    for o in range(2):
        z = z_ref[...]
        zb_ref[...] = z.astype(BF16)
        nyq = jnp.sum(z * alt, axis=0, keepdims=True) * fn_ref[o]

        def to_freq(c, carry):
            lo = pl.multiple_of(c * kc, kc)
            zr = jnp.dot(cs_ref[pl.ds(lo, kc), :], zb_ref[...], preferred_element_type=F32)
            zs = jnp.dot(cs_ref[pl.ds(L + lo, kc), :], zb_ref[...], preferred_element_type=F32)
            fr = fr_ref[o, pl.ds(lo, kc), :]
            fi = fi_ref[o, pl.ds(lo, kc), :]
            yr_ref[pl.ds(lo, kc), :] = (zr * fr + zs * fi).astype(BF16)
            yi_ref[pl.ds(lo, kc), :] = (zr * fi - zs * fr).astype(BF16)
            return carry

        lax.fori_loop(0, L // kc, to_freq, 0)

        def to_time(c, carry):
            lo = pl.multiple_of(c * kc, kc)
            y = (jnp.dot(cs_ref[pl.ds(lo, kc), :], yr_ref[...], preferred_element_type=F32)
                 - jnp.dot(cs_ref[pl.ds(L + lo, kc), :], yi_ref[...], preferred_element_type=F32))
            y = y + jnp.where(((chunk_row + lo) & 1) == 0, nyq, -nyq)
            z_ref[pl.ds(lo, kc), :] = g_ref[o, pl.ds(lo, kc), :] * (y + skip_ref[o:o + 1, :] * z_ref[pl.ds(lo, kc), :])
            return carry

        lax.fori_loop(0, L // kc, to_time, 0)
    o_ref[...] = z_ref[...].astype(BF16)


def _hyena_conv(up, sw, sb, skip, fr, fi, fn, cs, *, row0, n_batch, L, tc):
    n_ct = D // tc
    base = row0 // L
    u_spec = lambda p: pl.BlockSpec((L, tc), lambda j, b: (base + b, p * n_ct + j))
    once = pl.Buffered(1)
    return pl.pallas_call(
        functools.partial(_hyconv_body, L=L),
        grid=(n_ct, n_batch),
        in_specs=[
            u_spec(0), u_spec(1), u_spec(2),
            pl.BlockSpec((3, 3, tc), lambda j, b: (0, 0, j)),
            pl.BlockSpec((3, tc), lambda j, b: (0, j)),
            pl.BlockSpec((2, tc), lambda j, b: (0, j)),
            pl.BlockSpec((2, L, tc), lambda j, b: (0, 0, j), pipeline_mode=once),
            pl.BlockSpec((2, L, tc), lambda j, b: (0, 0, j), pipeline_mode=once),
            pl.BlockSpec((2, 1, tc), lambda j, b: (0, 0, j)),
            pl.BlockSpec((2 * L, L), lambda j, b: (0, 0), pipeline_mode=once),
        ],
        out_specs=pl.BlockSpec((L, tc), lambda j, b: (b, j)),
        out_shape=jax.ShapeDtypeStruct((n_batch * L, D), BF16),
        scratch_shapes=[
            pltpu.VMEM((L, tc), F32),
            pltpu.VMEM((2, L, tc), F32),
            pltpu.VMEM((L, tc), BF16),
            pltpu.VMEM((L, tc), BF16),
            pltpu.VMEM((L, tc), BF16),
        ],
        compiler_params=_cparams(("arbitrary", "arbitrary")),
        name="hyena_conv",
    )(up, up, up, sw.reshape(3, 3, D), sb.reshape(3, D), skip, fr, fi, fn, cs)


EXP_TILE = 512
SORT_TILES = (2 * N_TOK) // EXP_TILE + N_EXP
SORT_ROWS = SORT_TILES * EXP_TILE
N_QUARTERS = 4
QUARTER_W = D // (2 * N_QUARTERS)
META_I1, META_I2, META_R1, META_R2, META_W1, META_W2 = range(6)
HI_HALF = 0xFFFF0000


def _pack_pairs(x):
    bits = pltpu.bitcast(x.astype(BF16).astype(F32), jnp.uint32)
    return (bits[:, :QUARTER_W] >> 16) | bits[:, QUARTER_W:]


def _unpack_pairs(w):
    return pltpu.bitcast(w << 16, F32), pltpu.bitcast(w & jnp.uint32(HI_HALF), F32)


def _packed_shape(rows):
    return (rows // 8, N_QUARTERS, 8, QUARTER_W)


def _store_packed(ref, x):
    for q in range(N_QUARTERS):
        ref[:, q] = _pack_pairs(x[:, q * 2 * QUARTER_W:(q + 1) * 2 * QUARTER_W]).reshape(-1, 8, QUARTER_W)


def _load_unpacked(ref):
    halves = []
    for q in range(N_QUARTERS):
        halves.extend(_unpack_pairs(ref[:, q].reshape(-1, QUARTER_W)))
    return jnp.concatenate(halves, axis=1)


def _packed_row(ref, tile, sublane):
    return ref.at[tile, :, sublane, :]


def _lane_put(lane, values):
    out = jnp.where(lane == 0, values[0], 0.0)
    for k in range(1, len(values)):
        out = out + jnp.where(lane == k, values[k], 0.0)
    return out


def _route_body(*refs, tm, split_x):
    n_x = 2 if split_x else 1
    ac_ref, al_ref, w_ref, b_ref = refs[:4]
    x_refs = refs[4:4 + n_x]
    (ada_ref, gain_ref, wr_hi_ref, wr_lo_ref, br_ref, tri_ref,
     xo_ref, hq_ref, meta_ref, cnt_ref, carry_ref) = refs[4 + n_x:]
    step = pl.program_id(0)

    @pl.when(step == 0)
    def _():
        carry_ref[...] = jnp.zeros_like(carry_ref)

    mixed = jnp.dot(_pick_split(step, tm, ac_ref, al_ref), w_ref[...], preferred_element_type=F32) + b_ref[...]
    x = _pick_split(step, tm, *x_refs) if split_x else x_refs[0][...]
    x = x + ada_ref[0, 2:3, :] * mixed
    xo_ref[...] = x
    h = _norm_mod(x, gain_ref, ada_ref, 3)
    _store_packed(hq_ref, h)
    h_hi = h.astype(BF16)
    h_lo = (h - h_hi.astype(F32)).astype(BF16)
    logits = (jnp.dot(h_hi, wr_hi_ref[...], preferred_element_type=F32)
              + jnp.dot(h_lo, wr_hi_ref[...], preferred_element_type=F32)
              + jnp.dot(h_hi, wr_lo_ref[...], preferred_element_type=F32) + br_ref[...])
    lane = lax.broadcasted_iota(jnp.int32, (1, ROUTE_W), 1)
    lane_f = lane.astype(F32)
    group_of_lane = (lane >> 3).astype(F32)
    neg = -jnp.inf
    big = float(ROUTE_W)
    is_g = (lane >= N_EXP) & (lane < N_EXP + N_GROUPS)
    gl = jnp.where(is_g, logits, neg)
    gmax = jnp.max(gl, axis=-1, keepdims=True)
    gidx = jnp.min(jnp.where(gl == gmax, lane_f - N_EXP, big), axis=-1, keepdims=True)
    g_top = 1.0 / jnp.sum(jnp.where(is_g, jnp.exp(gl - gmax), 0.0), axis=-1, keepdims=True)
    in_group = (lane < N_EXP) & (group_of_lane == gidx)
    el = jnp.where(in_group, logits, neg)
    v1 = jnp.max(el, axis=-1, keepdims=True)
    i1 = jnp.min(jnp.where(el == v1, lane_f, big), axis=-1, keepdims=True)
    el2 = jnp.where(lane_f == i1, neg, el)
    v2 = jnp.max(el2, axis=-1, keepdims=True)
    i2 = jnp.min(jnp.where(el2 == v2, lane_f, big), axis=-1, keepdims=True)
    r = jnp.exp(v2 - v1)
    w1 = g_top / (1.0 + r)
    w2 = g_top * r / (1.0 + r)
    sel = jnp.where((lane_f == i1) | (lane_f == i2), 1.0, 0.0)
    rank = jnp.dot(tri_ref[...], sel.astype(BF16), preferred_element_type=F32) + carry_ref[...]
    r1 = jnp.sum(jnp.where(lane_f == i1, rank, 0.0), axis=-1, keepdims=True)
    r2 = jnp.sum(jnp.where(lane_f == i2, rank, 0.0), axis=-1, keepdims=True)
    carry_ref[...] += jnp.sum(sel, axis=0, keepdims=True)
    cnt_ref[...] = carry_ref[...]
    meta_ref[...] = _lane_put(lane, (i1, i2, r1, r2, w1, w2))


@functools.lru_cache(maxsize=None)
def _strict_lower(n):
    r = np.arange(n)
    return (r[None, :] < r[:, None]).astype(np.float32)


def _mix_route(a_ctx, a_lat, w_out, b_out, x_parts, ada, gain, wg, bg, we, be, *, tm=1024):
    k = a_ctx.shape[1]
    split_x = len(x_parts) == 2
    x_specs = _split_specs(tm, D) if split_x else [pl.BlockSpec((tm, D), lambda i: (i, 0))]
    wr = jnp.zeros((D, ROUTE_W), F32).at[:, :N_EXP].set(we.reshape(D, N_EXP)).at[:, N_EXP:N_EXP + N_GROUPS].set(wg)
    br = jnp.zeros((1, ROUTE_W), F32).at[0, :N_EXP].set(be.reshape(N_EXP)).at[0, N_EXP:N_EXP + N_GROUPS].set(bg)
    wr_hi = wr.astype(BF16)
    wr_lo = (wr - wr_hi.astype(F32)).astype(BF16)
    return pl.pallas_call(
        functools.partial(_route_body, tm=tm, split_x=split_x),
        grid=(N_TOK // tm,),
        in_specs=_split_specs(tm, k) + [
            pl.BlockSpec((k, D), lambda i: (0, 0)),
            pl.BlockSpec((1, D), lambda i: (0, 0)),
        ] + x_specs + [
            pl.BlockSpec((1, 6, D), lambda i: (_cond_row(i, tm), 0, 0)),
            pl.BlockSpec((1, D), lambda i: (0, 0)),
            pl.BlockSpec((D, ROUTE_W), lambda i: (0, 0)),
            pl.BlockSpec((D, ROUTE_W), lambda i: (0, 0)),
            pl.BlockSpec((1, ROUTE_W), lambda i: (0, 0)),
            pl.BlockSpec((tm, tm), lambda i: (0, 0)),
        ],
        out_specs=[
            pl.BlockSpec((tm, D), lambda i: (i, 0)),
            pl.BlockSpec(_packed_shape(tm), lambda i: (i, 0, 0, 0)),
            pl.BlockSpec((tm, ROUTE_W), lambda i: (i, 0)),
            pl.BlockSpec((1, ROUTE_W), lambda i: (0, 0)),
        ],
        out_shape=[
            jax.ShapeDtypeStruct((N_TOK, D), F32),
            jax.ShapeDtypeStruct(_packed_shape(N_TOK), jnp.uint32),
            jax.ShapeDtypeStruct((N_TOK, ROUTE_W), F32),
            jax.ShapeDtypeStruct((1, ROUTE_W), F32),
        ],
        scratch_shapes=[pltpu.VMEM((1, ROUTE_W), F32)],
        compiler_params=_cparams(("arbitrary",)),
        name="mix_route",
    )(a_ctx, a_lat, w_out, b_out.reshape(1, D), *x_parts, ada, gain.reshape(1, D), wr_hi, wr_lo, br,
      jnp.asarray(_strict_lower(tm)).astype(BF16))


def _slots_body(meta_ref, cnt_ref, upper_ref, pos_ref, plan_ref):
    lane = lax.broadcasted_iota(jnp.int32, (1, ROUTE_W), 1)
    lane_f = lane.astype(F32)
    tiles = jnp.floor((cnt_ref[...] + (EXP_TILE - 1.0)) * (1.0 / EXP_TILE))
    end_tile = jnp.dot(jnp.broadcast_to(tiles, (8, ROUTE_W)).astype(BF16), upper_ref[...],
                       preferred_element_type=F32)[0:1]
    start_row = (end_tile - tiles) * EXP_TILE
    meta = meta_ref[...]
    i1 = meta[:, META_I1:META_I1 + 1]
    i2 = meta[:, META_I2:META_I2 + 1]
    p1 = jnp.sum(jnp.where(lane_f == i1, start_row, 0.0), axis=-1, keepdims=True) + meta[:, META_R1:META_R1 + 1]
    p2 = jnp.sum(jnp.where(lane_f == i2, start_row, 0.0), axis=-1, keepdims=True) + meta[:, META_R2:META_R2 + 1]
    hi1, hi2 = jnp.floor(p1 * 0.125), jnp.floor(p2 * 0.125)
    pos_ref[0] = _lane_put(lane, (hi1, p1 - 8.0 * hi1, hi2, p2 - 8.0 * hi2)).T[:8, :].astype(jnp.int32)

    @pl.when(pl.program_id(0) == 0)
    def _():
        end_col = jnp.broadcast_to(end_tile, (ROUTE_W, ROUTE_W)).T
        expert = lax.broadcasted_iota(jnp.int32, (ROUTE_W, ROUTE_W), 0)
        tile = lax.broadcasted_iota(jnp.int32, (ROUTE_W, ROUTE_W), 1).astype(F32)
        passed = jnp.where((end_col <= tile) & (expert < N_EXP), 1.0, 0.0)
        tile_expert = jnp.minimum(jnp.sum(passed, axis=0, keepdims=True), N_EXP - 1.0)
        used = jnp.max(end_tile, axis=-1, keepdims=True)
        row = lax.broadcasted_iota(jnp.int32, (8, ROUTE_W), 0)
        plan_ref[...] = jnp.where(row == 0, tile_expert, jnp.where(row == 1, used, 0.0)).astype(jnp.int32)


@functools.lru_cache(maxsize=None)
def _upper_incl(n):
    r = np.arange(n)
    return (r[:, None] <= r[None, :]).astype(np.float32)


def _slots(meta, cnt, *, tm=2048):
    n_tiles = N_TOK // tm
    return pl.pallas_call(
        _slots_body,
        grid=(n_tiles,),
        in_specs=[
            pl.BlockSpec((tm, ROUTE_W), lambda i: (i, 0)),
            pl.BlockSpec((1, ROUTE_W), lambda i: (0, 0)),
            pl.BlockSpec((ROUTE_W, ROUTE_W), lambda i: (0, 0)),
        ],
        out_specs=[
            pl.BlockSpec((1, 8, tm), lambda i: (i, 0, 0)),
            pl.BlockSpec((8, ROUTE_W), lambda i: (0, 0)),
        ],
        out_shape=[
            jax.ShapeDtypeStruct((n_tiles, 8, tm), jnp.int32),
            jax.ShapeDtypeStruct((8, ROUTE_W), jnp.int32),
        ],
        compiler_params=_cparams(("arbitrary",)),
        name="moe_slots",
    )(meta, cnt, jnp.asarray(_upper_incl(ROUTE_W)).astype(BF16))


ROW_UNROLL = 8
N_DMA_LANES = 2


def _start_row_copies(tm, make_copy):
    def block(blk, carry):
        for j in range(ROW_UNROLL):
            lane = j % N_DMA_LANES
            for k in range(2):
                make_copy(blk, j, k, lane).start(priority=lane)
        return carry

    lax.fori_loop(0, tm // ROW_UNROLL, block, 0)


def _drain_row_copies(like_src, like_dst, sem):
    for lane in range(N_DMA_LANES):
        pltpu.make_async_copy(like_src, like_dst, sem.at[lane]).wait()


def _dispatch_body(t1_ref, s1_ref, t2_ref, s2_ref, hq_ref, xs_in_ref, xs_ref, sem, *, tm):
    del xs_in_ref
    slots = ((t1_ref, s1_ref), (t2_ref, s2_ref))

    def copy(blk, j, k, lane):
        i = blk * ROW_UNROLL + j
        return pltpu.make_async_copy(_packed_row(hq_ref, blk, j), _packed_row(xs_ref, slots[k][0][i], slots[k][1][i]),
                                     sem.at[lane])

    _start_row_copies(tm, copy)
    _drain_row_copies(hq_ref, xs_ref.at[pl.ds(0, tm // 8)], sem)


def _slot_specs(tm, ahead=0):
    last = N_TOK // tm - 1
    return [pl.BlockSpec((tm,), lambda t: (jnp.minimum(t + ahead, last),), memory_space=pltpu.SMEM) for _ in range(4)]


def _dispatch(slots, hq, init, *, tm=512):
    return pl.pallas_call(
        functools.partial(_dispatch_body, tm=tm),
        grid=(N_TOK // tm,),
        in_specs=_slot_specs(tm) + [
            pl.BlockSpec(_packed_shape(tm), lambda t: (t, 0, 0, 0)),
            pl.BlockSpec(memory_space=pl.ANY),
        ],
        out_specs=pl.BlockSpec(memory_space=pl.ANY),
        out_shape=jax.ShapeDtypeStruct(_packed_shape(SORT_ROWS), jnp.uint32),
        input_output_aliases={5: 0},
        scratch_shapes=[pltpu.SemaphoreType.DMA((N_DMA_LANES,))],
        compiler_params=_cparams(("arbitrary",)),
        name="moe_dispatch",
    )(*slots, hq, init)


def _experts_body(plan_ref, xs_ref, wg_ref, wu_ref, wd_ref, ys_ref, wgu_ref, wdn_ref):
    t = pl.program_id(0)
    in_use = t < plan_ref[1, 0]
    new_expert = (t == 0) | (plan_ref[0, t] != plan_ref[0, jnp.maximum(t - 1, 0)])

    @pl.when(in_use & new_expert)
    def _():
        wgu_ref[:, :EXP_H] = wg_ref[0, 0].astype(BF16)
        wgu_ref[:, EXP_H:] = wu_ref[0, 0].astype(BF16)
        wdn_ref[...] = wd_ref[0, 0].astype(BF16)

    @pl.when(in_use)
    def _():
        x = _load_unpacked(xs_ref).astype(BF16)
        ab = jnp.dot(x, wgu_ref[...], preferred_element_type=F32)
        a, b = ab[:, :EXP_H], ab[:, EXP_H:]
        act = a * _sigmoid(a) * b
        _store_packed(ys_ref, jnp.dot(act.astype(BF16), wdn_ref[...], preferred_element_type=F32))


def _experts(plan, xs, w_gate, w_up, w_down, *, layer):
    def tile_idx(t, plan):
        return (jnp.minimum(t, jnp.maximum(plan[1, 0] - 1, 0)), 0, 0, 0)

    grid_spec = pltpu.PrefetchScalarGridSpec(
        num_scalar_prefetch=1,
        grid=(SORT_TILES,),
        in_specs=[
            pl.BlockSpec(_packed_shape(EXP_TILE), tile_idx),
            pl.BlockSpec((1, 1, D, EXP_H), lambda t, plan: (layer, plan[0, t], 0, 0)),
            pl.BlockSpec((1, 1, D, EXP_H), lambda t, plan: (layer, plan[0, t], 0, 0)),
            pl.BlockSpec((1, 1, EXP_H, D), lambda t, plan: (layer, plan[0, t], 0, 0)),
        ],
        out_specs=pl.BlockSpec(_packed_shape(EXP_TILE), tile_idx),
        scratch_shapes=[pltpu.VMEM((D, 2 * EXP_H), BF16), pltpu.VMEM((EXP_H, D), BF16)],
    )
    return pl.pallas_call(
        _experts_body,
        grid_spec=grid_spec,
        out_shape=jax.ShapeDtypeStruct(_packed_shape(SORT_ROWS), jnp.uint32),
        input_output_aliases={1: 0},
        compiler_params=_cparams(("arbitrary",)),
        name="moe_experts",
    )(plan, xs, w_gate, w_up, w_down)


def _combine_body(*refs, tm, tail):
    cur_slots, nxt_slots = refs[0:4], refs[4:8]
    meta_ref, x_ref, ada_ref = refs[8:11]
    rest = refs[11:]
    if tail == "final":
        fgain_ref, ys_ref, oc_ref, ol_ref, y_ref, sem = rest
    elif tail == "proj":
        nada_ref, ngain_ref, pw_ref, pb_ref, ys_ref, o_ref, up_ref, y_ref, sem = rest
    else:
        ys_ref, o_ref, y_ref, sem = rest
    step = pl.program_id(0)
    n_steps = pl.num_programs(0)

    def fetch(slot_refs, buf):
        slots = ((slot_refs[0], slot_refs[1]), (slot_refs[2], slot_refs[3]))

        def copy(blk, j, k, lane):
            i = blk * ROW_UNROLL + j
            return pltpu.make_async_copy(_packed_row(ys_ref, slots[k][0][i], slots[k][1][i]),
                                         _packed_row(y_ref.at[buf, k], blk, j), sem.at[buf, lane])

        _start_row_copies(tm, copy)

    @pl.when(step == 0)
    def _():
        fetch(cur_slots, 0)

    @pl.when(step + 1 < n_steps)
    def _():
        fetch(nxt_slots, (step + 1) % 2)

    buf = step % 2
    _drain_row_copies(ys_ref.at[pl.ds(0, tm // 8)], y_ref.at[buf, 0], sem.at[buf])
    y_ref = y_ref.at[buf]
    meta = meta_ref[...]
    mix = (meta[:, META_W1:META_W1 + 1] * _load_unpacked(y_ref.at[0])
           + meta[:, META_W2:META_W2 + 1] * _load_unpacked(y_ref.at[1]))
    x = x_ref[...] + ada_ref[0, 5:6, :] * mix
    if tail == "proj":
        o_ref[...] = x
        h = _norm_mod(x, ngain_ref, nada_ref, 0)
        up_ref[...] = (jnp.dot(h.astype(BF16), pw_ref[...], preferred_element_type=F32) + pb_ref[...]).astype(BF16)
        return
    if tail != "final":
        o_ref[...] = x
        return
    ms = jnp.mean(x * x, axis=-1, keepdims=True)
    y = x * lax.rsqrt(ms + EPS) * fgain_ref[...]
    is_ctx = pl.program_id(0) < N_CTX // tm

    @pl.when(is_ctx)
    def _():
        oc_ref[...] = y

    @pl.when(jnp.logical_not(is_ctx))
    def _():
        ol_ref[...] = y


def _combine(slots, ys, meta, x, ada, *, final_gain=None, next_proj=None, tm=512):
    row_spec = pl.BlockSpec((tm, D), lambda t: (t, 0))
    ada_spec = pl.BlockSpec((1, 6, D), lambda t: (_cond_row(t, tm), 0, 0))
    const = lambda t: (0, 0)
    if final_gain is not None:
        tail = "final"
        extra_in = [pl.BlockSpec((1, D), const)]
        extra_args = [final_gain.reshape(1, D)]
        out_specs = _split_specs(tm, D)
        out_shape = [jax.ShapeDtypeStruct((N_CTX, D), F32), jax.ShapeDtypeStruct((N_LAT, D), F32)]
    elif next_proj is not None:
        tail = "proj"
        nada, ngain, pw, pb = next_proj
        n_out = pw.shape[1]
        extra_in = [ada_spec, pl.BlockSpec((1, D), const), pl.BlockSpec((D, n_out), const), pl.BlockSpec((1, n_out), const)]
        extra_args = [nada, ngain.reshape(1, D), pw, pb.reshape(1, n_out)]
        out_specs = [row_spec, pl.BlockSpec((tm, n_out), lambda t: (t, 0))]
        out_shape = [jax.ShapeDtypeStruct((N_TOK, D), F32), jax.ShapeDtypeStruct((N_TOK, n_out), BF16)]
    else:
        tail = "plain"
        extra_in, extra_args = [], []
        out_specs = row_spec
        out_shape = jax.ShapeDtypeStruct((N_TOK, D), F32)
    return pl.pallas_call(
        functools.partial(_combine_body, tm=tm, tail=tail),
        grid=(N_TOK // tm,),
        in_specs=_slot_specs(tm) + _slot_specs(tm, ahead=1) + [
            pl.BlockSpec((tm, ROUTE_W), lambda t: (t, 0)), row_spec, ada_spec,
        ] + extra_in + [pl.BlockSpec(memory_space=pl.ANY)],
        out_specs=out_specs,
        out_shape=out_shape,
        scratch_shapes=[pltpu.VMEM((2, 2) + _packed_shape(tm), jnp.uint32),
                        pltpu.SemaphoreType.DMA((2, N_DMA_LANES))],
        compiler_params=_cparams(("arbitrary",)),
        name="moe_combine",
    )(*slots, *slots, meta, x, ada, *extra_args, ys)


def _moe(routed, ada, w_gate, w_up, w_down, *, layer, sort_init, final_gain=None, next_proj=None):
    x, hq, meta, cnt = routed
    pos, plan = _slots(meta, cnt)
    slots = [pos[:, r, :].reshape(N_TOK) for r in range(4)]
    xs = _dispatch(slots, hq, sort_init)
    ys = _experts(plan, xs, w_gate, w_up, w_down, layer=layer)
    return _combine(slots, ys, meta, x, ada, final_gain=final_gain, next_proj=next_proj), ys


def kernel(x_prompt, x_sample, cache_k, cache_v, c, c_ctx, w_ada, b_ada, norm_mix, norm_ffn, attn_w_q, attn_w_kv, attn_q_norm, attn_k_norm, attn_w_o, hy_w_in, hy_b_in, hy_short_w, hy_short_b, hy_filt_w1, hy_filt_b1, hy_filt_w2, hy_filt_b2, hy_filt_w3, hy_filt_b3, hy_filt_freq, hy_log_decay, hy_skip, hy_w_out, hy_b_out, router_group_w, router_group_b, router_expert_w, router_expert_b, moe_w_gate, moe_w_up, moe_w_down, final_norm):
    depth = w_ada.shape[0]
    x_parts = (x_prompt.reshape(N_CTX, D), x_sample.reshape(N_LAT, D))
    cond8 = jnp.zeros((8, D), F32).at[0].set(c_ctx).at[1:1 + N_BATCH_LAT].set(c)
    ada_all = _ada(cond8, w_ada, b_ada).reshape(depth, 8, 6, D)
    new_k = []
    new_v = []
    sort_buf = jnp.zeros(_packed_shape(SORT_ROWS), jnp.uint32)
    for l in range(depth):
        ada = ada_all[l]
        if l % 2 == 0:
            a = l // 2
            w_qkv = jnp.concatenate([attn_w_q[a], attn_w_kv[a]], axis=1).astype(BF16)
            q, k_ctx, v_ctx, kc, vc, kl, vl = _qkv(x_parts, ada, norm_mix[l], w_qkv, attn_q_norm[a], attn_k_norm[a])
            new_k.append(k_ctx)
            new_v.append(v_ctx)
            o_ctx = _attention(q, [kc], [vc], row0=0, n_batch=N_BATCH_CTX, seq=L_CTX, tq=L_CTX,
                               n_kv=N_KV, stack=Q_PER_KV)
            past_keys = _lane_tile_heads(cache_k[:, a]).transpose(0, 1, 3, 2)
            o_lat = _attention(q, [kl, past_keys], [vl, _lane_tile_heads(cache_v[:, a])],
                               row0=N_CTX, n_batch=N_BATCH_LAT, seq=L_LAT, tq=256, n_kv=1, stack=1)
            mixer = (o_ctx, o_lat, attn_w_o[a].astype(BF16), jnp.zeros((D,), F32))
        else:
            j = l // 2
            x, up = x_parts[0]
            x_parts = (x,)
            outs = []
            for row0, n_batch, L, tc in ((0, N_BATCH_CTX, L_CTX, D), (N_CTX, N_BATCH_LAT, L_LAT, 256)):
                cs = jnp.asarray(_dft_table(L)).astype(BF16)
                fr, fi, fn = _filter_spectra(L, cs, hy_filt_w1[j], hy_filt_b1[j], hy_filt_w2[j], hy_filt_b2[j],
                                             hy_filt_w3[j], hy_filt_b3[j], hy_filt_freq[j], hy_log_decay[j])
                outs.append(_hyena_conv(up, hy_short_w[j], hy_short_b[j], hy_skip[j], fr, fi, fn, cs,
                                        row0=row0, n_batch=n_batch, L=L, tc=tc))
            mixer = (outs[0], outs[1], hy_w_out[j].astype(BF16), hy_b_out[j])
        routed = _mix_route(*mixer, x_parts, ada, norm_ffn[l], router_group_w[l], router_group_b[l],
                            router_expert_w[l], router_expert_b[l])
        last = l == depth - 1
        next_is_hyena = not last and (l + 1) % 2 == 1
        next_proj = (ada_all[l + 1], norm_mix[l + 1], hy_w_in[(l + 1) // 2].astype(BF16),
                     hy_b_in[(l + 1) // 2]) if next_is_hyena else None
        out, sort_buf = _moe(routed, ada, moe_w_gate, moe_w_up, moe_w_down, layer=l, sort_init=sort_buf,
                             final_gain=final_norm if last else None, next_proj=next_proj)
        x_parts = (out,)
    y_ctx, y_lat = x_parts[0]
    return (y_ctx.reshape(N_BATCH_CTX, L_CTX, D), y_lat.reshape(N_BATCH_LAT, L_LAT, D),
            jnp.stack(new_k, axis=1), jnp.stack(new_v, axis=1))
```

```python
import functools
import math

import numpy as np
import jax
import jax.numpy as jnp
from jax import lax
from jax.experimental import pallas as pl
from jax.experimental.pallas import tpu as pltpu

F32 = jnp.float32
BF16 = jnp.bfloat16

D = 1024
N_BATCH_CTX = 32
L_CTX = 256
N_BATCH_LAT = 2
L_LAT = 2048
PAST = 256
N_CTX = N_BATCH_CTX * L_CTX
N_LAT = N_BATCH_LAT * L_LAT
N_TOK = N_CTX + N_LAT
GRID_W = 64
N_HEADS = 16
N_KV = 4
DH = 64
Q_PER_KV = N_HEADS // N_KV
KV_W = N_KV * DH
ROPE_THETA = 10000.0
FILTER_FEAT = 17
FEAT_PAD = 32
FILTER_HIDDEN = 64
N_GROUPS = 4
E_PER_G = 8
N_EXP = N_GROUPS * E_PER_G
EXP_H = D // 4
EPS = 1e-6
ROUTE_W = 128
VMEM_LIMIT = 56 * 1024 * 1024


def _cparams(sem):
    return pltpu.CompilerParams(dimension_semantics=sem, vmem_limit_bytes=VMEM_LIMIT)


def _cond_row(i, tm):
    n_ctx_tiles = N_CTX // tm
    return jnp.where(i < n_ctx_tiles, 0, 1 + (i - n_ctx_tiles) // (L_LAT // tm))


def _split_specs(tm, width):
    n_ctx_tiles = N_CTX // tm
    return [pl.BlockSpec((tm, width), lambda i: (jnp.minimum(i, n_ctx_tiles - 1), 0)),
            pl.BlockSpec((tm, width), lambda i: (jnp.maximum(i - n_ctx_tiles, 0), 0))]


def _pick_split(i, tm, ctx_ref, lat_ref):
    return jnp.where(i < N_CTX // tm, ctx_ref[...], lat_ref[...])


def _sigmoid(x):
    return 1.0 / (1.0 + jnp.exp(-x))


@functools.lru_cache(maxsize=None)
def _dft_table(L):
    k = np.arange(L, dtype=np.int64)
    ang = (np.outer(k, k) % (2 * L)).astype(np.float64) * (math.pi / L)
    return np.concatenate([np.cos(ang), np.sin(ang)], axis=0).astype(np.float32)


@functools.lru_cache(maxsize=None)
def _filter_feats(L):
    t = np.linspace(0.0, 1.0, L, dtype=np.float64)[:, None]
    bands = np.linspace(1e-4, 7.0, 8, dtype=np.float64)[None, :]
    w = (2.0 * math.pi) * np.arange(L, dtype=np.float64)[:, None] / L
    feats = np.concatenate([t, np.cos(bands * w), -np.sin(bands * w)], axis=-1)
    out = np.zeros((L, FEAT_PAD), np.float32)
    out[:, :FILTER_FEAT] = feats
    return out


@functools.lru_cache(maxsize=None)
def _rope_tables(tm):
    pos = np.arange(L_LAT)
    row = (pos // GRID_W).astype(np.float64)
    col = (pos % GRID_W).astype(np.float64)
    axis_dim = DH // 2
    inv_freq = ROPE_THETA ** (-np.arange(0, axis_dim, 2, dtype=np.float64) / axis_dim)
    lane = np.arange(KV_W)
    d = lane % DH
    is_col = (d // axis_dim) == 1
    fi = d % (axis_dim // 2)
    first_half = (d % axis_dim) < (axis_dim // 2)
    p = np.where(is_col[None, :], col[:, None], row[:, None])
    ang = p * inv_freq[fi][None, :]
    cos = np.cos(ang)
    sin = np.sin(ang) * np.where(first_half, -1.0, 1.0)[None, :]
    cos = np.concatenate([np.ones((tm, KV_W)), cos], axis=0).astype(np.float32)
    sin = np.concatenate([np.zeros((tm, KV_W)), sin], axis=0).astype(np.float32)
    return cos, sin


@functools.lru_cache(maxsize=None)
def _head_sum_matrix():
    lane = np.arange(KV_W)
    return (lane[:, None] // DH == lane[None, :] // DH).astype(np.float32)


def _ada_body(c_ref, w_ref, b_ref, o_ref):
    c = c_ref[...]
    s = c * _sigmoid(c)
    o_ref[0] = jnp.dot(s.astype(BF16), w_ref[0].astype(BF16), preferred_element_type=F32) + b_ref[0]


def _ada(cond8, w_ada, b_ada):
    depth = w_ada.shape[0]
    tn = 1536
    return pl.pallas_call(
        _ada_body,
        grid=(depth, 6 * D // tn),
        in_specs=[
            pl.BlockSpec((8, D), lambda l, j: (0, 0)),
            pl.BlockSpec((1, D, tn), lambda l, j: (l, 0, j)),
            pl.BlockSpec((1, 1, tn), lambda l, j: (l, 0, j)),
        ],
        out_specs=pl.BlockSpec((1, 8, tn), lambda l, j: (l, 0, j)),
        out_shape=jax.ShapeDtypeStruct((depth, 8, 6 * D), F32),
        compiler_params=_cparams(("arbitrary", "arbitrary")),
        name="ada",
    )(cond8, w_ada, b_ada.reshape(depth, 1, 6 * D))


def _norm_mod(x, gain_ref, ada_ref, part):
    ms = jnp.mean(x * x, axis=-1, keepdims=True)
    y = x * lax.rsqrt(ms + EPS) * gain_ref[...]
    return y * (1.0 + ada_ref[0, part + 1:part + 2, :]) + ada_ref[0, part:part + 1, :]


def _proj_body(x_ref, ada_ref, gain_ref, w_ref, b_ref, o_ref, *, part):
    h = _norm_mod(x_ref[...], gain_ref, ada_ref, part)
    acc = jnp.dot(h.astype(BF16), w_ref[...], preferred_element_type=F32) + b_ref[...]
    o_ref[...] = acc.astype(o_ref.dtype)


def _norm_proj(x, ada, gain, w, b, *, part, tm=1024):
    n_out = w.shape[1]
    return pl.pallas_call(
        functools.partial(_proj_body, part=part),
        grid=(N_TOK // tm,),
        in_specs=[
            pl.BlockSpec((tm, D), lambda i: (i, 0)),
            pl.BlockSpec((1, 6, D), lambda i: (_cond_row(i, tm), 0, 0)),
            pl.BlockSpec((1, D), lambda i: (0, 0)),
            pl.BlockSpec((D, n_out), lambda i: (0, 0)),
            pl.BlockSpec((1, n_out), lambda i: (0, 0)),
        ],
        out_specs=pl.BlockSpec((tm, n_out), lambda i: (i, 0)),
        out_shape=jax.ShapeDtypeStruct((N_TOK, n_out), BF16),
        compiler_params=_cparams(("arbitrary",)),
        name="norm_proj",
    )(x, ada, gain.reshape(1, D), w, b.reshape(1, n_out))


def _head_rms_rope(x, gain, hs, cos, sin, lane):
    ss = jnp.dot((x * x).astype(BF16), hs, preferred_element_type=F32)
    xn = x * lax.rsqrt(ss * (1.0 / DH) + EPS) * gain
    fwd = pltpu.roll(xn, KV_W - DH // 4, 1)
    bwd = pltpu.roll(xn, DH // 4, 1)
    partner = jnp.where((lane & (DH // 4)) == 0, fwd, bwd)
    return xn * cos + partner * sin


@functools.lru_cache(maxsize=None)
def _lane_tile_matrix():
    col = np.arange(N_KV * KV_W)
    src = (col // KV_W) * DH + col % DH
    return (np.arange(KV_W)[:, None] == src[None, :]).astype(np.float32)


def _qkv_body(*refs, tm, split_x):
    n_x = 2 if split_x else 1
    x_refs = refs[:n_x]
    (ada_ref, gain_ref, w_ref, qg_ref, kg_ref, hs_ref, cos_ref, sin_ref, tile_ref, tile_t_ref,
     q_ref, newk_ref, newv_ref, kc_ref, vc_ref, kl_ref, vl_ref) = refs[n_x:]
    step = pl.program_id(0)
    x = _pick_split(step, tm, *x_refs) if split_x else x_refs[0][...]
    h = _norm_mod(x, gain_ref, ada_ref, 0)
    acc = jnp.dot(h.astype(BF16), w_ref[...], preferred_element_type=F32)
    hs = hs_ref[...]
    cos = cos_ref[...]
    sin = sin_ref[...]
    lane = lax.broadcasted_iota(jnp.int32, (1, KV_W), 1)
    for c in range(N_KV):
        qc = _head_rms_rope(acc[:, c * KV_W:(c + 1) * KV_W], qg_ref[...], hs, cos, sin, lane)
        q_ref[:, c * KV_W:(c + 1) * KV_W] = (qc * (DH ** -0.5)).astype(BF16)
    k = _head_rms_rope(acc[:, D:D + KV_W], kg_ref[...], hs, cos, sin, lane)
    v = acc[:, D + KV_W:D + 2 * KV_W]
    k4t = lax.dot_general(tile_t_ref[...], k.astype(BF16), (((1,), (1,)), ((), ())),
                          preferred_element_type=F32).astype(BF16)
    v4 = jnp.dot(v.astype(BF16), tile_ref[...], preferred_element_type=F32).astype(BF16)
    is_ctx = step < N_CTX // tm

    @pl.when(is_ctx)
    def _():
        for bb in range(tm // L_CTX):
            rows = slice(bb * L_CTX, (bb + 1) * L_CTX)
            for hd in range(N_KV):
                newk_ref[bb, hd] = k[rows, hd * DH:(hd + 1) * DH]
                newv_ref[bb, hd] = v[rows, hd * DH:(hd + 1) * DH]
                kc_ref[bb, hd] = k4t[hd * KV_W:(hd + 1) * KV_W, rows]
                vc_ref[bb, hd] = v4[rows, hd * KV_W:(hd + 1) * KV_W]

    @pl.when(jnp.logical_not(is_ctx))
    def _():
        for hd in range(N_KV):
            kl_ref[0, hd] = k4t[hd * KV_W:(hd + 1) * KV_W, :]
            vl_ref[0, hd] = v4[:, hd * KV_W:(hd + 1) * KV_W]


def _qkv(x_parts, ada, gain, w_qkv, q_gain, k_gain):
    tm = 2 * L_CTX
    per_tile = tm // L_CTX
    cos, sin = _rope_tables(tm)
    split_x = len(x_parts) == 2
    x_specs = _split_specs(tm, D) if split_x else [pl.BlockSpec((tm, D), lambda i: (i, 0))]
    n_ctx_tiles = N_CTX // tm
    lat_tiles = L_LAT // tm

    def rope_idx(i):
        return (jnp.where(i < n_ctx_tiles, 0, 1 + (i - n_ctx_tiles) % lat_tiles), 0)

    def ctx_idx(i):
        return (jnp.minimum(i, n_ctx_tiles - 1), 0, 0, 0)

    def lat_idx(i):
        j = jnp.maximum(i - n_ctx_tiles, 0)
        return (j // lat_tiles, 0, j % lat_tiles, 0)

    def lat_idx_t(i):
        j = jnp.maximum(i - n_ctx_tiles, 0)
        return (j // lat_tiles, 0, 0, j % lat_tiles)

    const = lambda i: (0, 0)
    kv_ctx = jax.ShapeDtypeStruct((N_BATCH_CTX, N_KV, L_CTX, DH), F32)
    tiled_ctx = jax.ShapeDtypeStruct((N_BATCH_CTX, N_KV, L_CTX, KV_W), BF16)
    tiled_lat = jax.ShapeDtypeStruct((N_BATCH_LAT, N_KV, L_LAT, KV_W), BF16)
    keys_ctx = jax.ShapeDtypeStruct((N_BATCH_CTX, N_KV, KV_W, L_CTX), BF16)
    keys_lat = jax.ShapeDtypeStruct((N_BATCH_LAT, N_KV, KV_W, L_LAT), BF16)
    return pl.pallas_call(
        functools.partial(_qkv_body, tm=tm, split_x=split_x),
        grid=(N_TOK // tm,),
        in_specs=x_specs + [
            pl.BlockSpec((1, 6, D), lambda i: (_cond_row(i, tm), 0, 0)),
            pl.BlockSpec((1, D), const),
            pl.BlockSpec((D, D + 2 * KV_W), const),
            pl.BlockSpec((1, KV_W), const),
            pl.BlockSpec((1, KV_W), const),
            pl.BlockSpec((KV_W, KV_W), const),
            pl.BlockSpec((tm, KV_W), rope_idx),
            pl.BlockSpec((tm, KV_W), rope_idx),
            pl.BlockSpec((KV_W, N_KV * KV_W), const),
            pl.BlockSpec((N_KV * KV_W, KV_W), const),
        ],
        out_specs=[
            pl.BlockSpec((tm, D), lambda i: (i, 0)),
            pl.BlockSpec((per_tile, N_KV, L_CTX, DH), ctx_idx),
            pl.BlockSpec((per_tile, N_KV, L_CTX, DH), ctx_idx),
            pl.BlockSpec((per_tile, N_KV, KV_W, L_CTX), ctx_idx),
            pl.BlockSpec((per_tile, N_KV, L_CTX, KV_W), ctx_idx),
            pl.BlockSpec((1, N_KV, KV_W, tm), lat_idx_t),
            pl.BlockSpec((1, N_KV, tm, KV_W), lat_idx),
        ],
        out_shape=[jax.ShapeDtypeStruct((N_TOK, D), BF16), kv_ctx, kv_ctx, keys_ctx, tiled_ctx, keys_lat, tiled_lat],
        compiler_params=_cparams(("arbitrary",)),
        name="qkv",
    )(*x_parts, ada, gain.reshape(1, D), w_qkv,
      jnp.tile(q_gain, Q_PER_KV).reshape(1, KV_W), jnp.tile(k_gain, N_KV).reshape(1, KV_W),
      jnp.asarray(_head_sum_matrix()).astype(BF16), jnp.asarray(cos), jnp.asarray(sin),
      jnp.asarray(_lane_tile_matrix()).astype(BF16), jnp.asarray(_lane_tile_matrix().T.copy()).astype(BF16))


def _attn_body(*refs, n_kv, stack, n_seg):
    q_ref, k_refs, v_refs, o_ref = refs[0], refs[1:1 + n_seg], refs[1 + n_seg:1 + 2 * n_seg], refs[-1]
    lane = lax.broadcasted_iota(jnp.int32, (1, KV_W), 1)
    masks = [(lane >> 6) == g for g in range(Q_PER_KV)]
    tq = q_ref.shape[0]
    for kv in range(n_kv):
        q = q_ref[:, kv * KV_W:(kv + 1) * KV_W]
        out = jnp.zeros((tq, KV_W), F32)
        for c in range(0, Q_PER_KV, stack):
            pair = masks[c:c + stack]
            ones_blk = (c + stack) % Q_PER_KV if stack < Q_PER_KV else None
            stacked = jnp.concatenate([jnp.where(m, q, jnp.zeros_like(q)) for m in pair], axis=0)
            scores = [jnp.dot(stacked, k_ref[0, kv], preferred_element_type=F32) for k_ref in k_refs]
            top = functools.reduce(jnp.maximum, [jnp.max(s, axis=-1, keepdims=True) for s in scores])
            probs = [jnp.exp((s - top).astype(BF16)) for s in scores]
            if ones_blk is None:
                vals = [v_ref[0, kv] for v_ref in v_refs]
            else:
                vals = [jnp.where(masks[ones_blk], jnp.ones((), BF16), v_ref[0, kv]) for v_ref in v_refs]
            og = sum(jnp.dot(p, v, preferred_element_type=F32) for p, v in zip(probs, vals))
            if ones_blk is None:
                denom = sum(jnp.sum(p.astype(F32), axis=-1, keepdims=True) for p in probs)
            else:
                denom = og[:, ones_blk * DH:ones_blk * DH + 1]
            og = og * (1.0 / denom)
            for g, m in enumerate(pair):
                out = out + jnp.where(m, og[g * tq:(g + 1) * tq], 0.0)
        o_ref[:, kv * KV_W:(kv + 1) * KV_W] = out.astype(BF16)


def _attention(q, keys, values, *, row0, n_batch, seq, tq, n_kv, stack):
    per_b = seq // tq
    base = row0 // tq
    kv_specs = [pl.BlockSpec((1, n_kv) + a.shape[2:], lambda b, h, i: (b, h, 0, 0)) for a in keys + values]
    return pl.pallas_call(
        functools.partial(_attn_body, n_kv=n_kv, stack=stack, n_seg=len(keys)),
        grid=(n_batch, N_KV // n_kv, per_b),
        in_specs=[pl.BlockSpec((tq, n_kv * KV_W), lambda b, h, i: (base + b * per_b + i, h))] + kv_specs,
        out_specs=pl.BlockSpec((tq, n_kv * KV_W), lambda b, h, i: (b * per_b + i, h)),
        out_shape=jax.ShapeDtypeStruct((n_batch * seq, D), BF16),
        compiler_params=_cparams(("arbitrary", "arbitrary", "arbitrary")),
        name="attention",
    )(q, *keys, *values)


def _lane_tile_heads(x):
    return jnp.tile(x.astype(BF16), (1, 1, 1, KV_W // DH))


def _alt_sign(L):
    row = lax.broadcasted_iota(jnp.int32, (L, 1), 0)
    return row, jnp.where((row & 1) == 0, 1.0, -1.0)


def _split_dot(a, b):
    a_hi, b_hi = a.astype(BF16), b.astype(BF16)
    a_lo = (a - a_hi.astype(F32)).astype(BF16)
    b_lo = (b - b_hi.astype(F32)).astype(BF16)
    return (jnp.dot(a_hi, b_hi, preferred_element_type=F32) + jnp.dot(a_lo, b_hi, preferred_element_type=F32)
            + jnp.dot(a_hi, b_lo, preferred_element_type=F32))


def _filter_body(feat_ref, w1_ref, b1_ref, w2_ref, b2_ref, fq_ref, w3_ref, b3_ref, ld_ref, cs_ref,
                 fr_ref, fi_ref, fn_ref, hid_ref, *, L):
    feats = feat_ref[...]

    @pl.when(pl.program_id(0) == 0)
    def _():
        fq = fq_ref[...]
        h1 = jnp.sin(fq * (_split_dot(feats, w1_ref[...]) + b1_ref[...]))
        hid_ref[...] = jnp.sin(fq * (_split_dot(h1, w2_ref[...]) + b2_ref[...])).astype(BF16)

    t = feats[:, 0:1]
    row, alt = _alt_sign(L)
    filt = []
    for j in range(4):
        raw = jnp.dot(hid_ref[...], w3_ref[j].astype(BF16), preferred_element_type=F32) + b3_ref[j:j + 1, :]
        filt.append(raw * jnp.exp(-t * jnp.exp(ld_ref[j:j + 1, :])))
    for o in range(2):
        hf, hb = filt[2 * o], filt[2 * o + 1]
        l1 = jnp.sum(jnp.abs(hf), axis=0, keepdims=True) + jnp.sum(jnp.abs(hb), axis=0, keepdims=True)
        inv = 1.0 / (l1 + EPS)
        sym = (hf + hb) * inv
        asym = (hb - hf) * inv
        fr = jnp.dot(cs_ref[0:L, :], sym.astype(BF16), preferred_element_type=F32) * (1.0 / L)
        fr_ref[o] = jnp.where(row == 0, 0.5 * fr, fr)
        fi_ref[o] = jnp.dot(cs_ref[L:2 * L, :], asym.astype(BF16), preferred_element_type=F32) * (1.0 / L)
        fn_ref[o] = jnp.sum(sym * alt, axis=0, keepdims=True) * (0.5 / L)


def _filter_spectra(L, cs, w1, b1, w2, b2, w3, b3, freq, log_decay, *, tc=256):
    w1p = jnp.zeros((FEAT_PAD, FILTER_HIDDEN), F32).at[:FILTER_FEAT].set(w1)
    w3r = w3.reshape(FILTER_HIDDEN, 4, D).transpose(1, 0, 2)
    const = lambda j: (0, 0)
    return pl.pallas_call(
        functools.partial(_filter_body, L=L),
        grid=(D // tc,),
        in_specs=[
            pl.BlockSpec((L, FEAT_PAD), const),
            pl.BlockSpec((FEAT_PAD, FILTER_HIDDEN), const),
            pl.BlockSpec((1, FILTER_HIDDEN), const),
            pl.BlockSpec((FILTER_HIDDEN, FILTER_HIDDEN), const),
            pl.BlockSpec((1, FILTER_HIDDEN), const),
            pl.BlockSpec((1, FILTER_HIDDEN), const),
            pl.BlockSpec((4, FILTER_HIDDEN, tc), lambda j: (0, 0, j)),
            pl.BlockSpec((4, tc), lambda j: (0, j)),
            pl.BlockSpec((4, tc), lambda j: (0, j)),
            pl.BlockSpec((2 * L, L), const, pipeline_mode=pl.Buffered(1)),
        ],
        out_specs=[
            pl.BlockSpec((2, L, tc), lambda j: (0, 0, j)),
            pl.BlockSpec((2, L, tc), lambda j: (0, 0, j)),
            pl.BlockSpec((2, 1, tc), lambda j: (0, 0, j)),
        ],
        out_shape=[
            jax.ShapeDtypeStruct((2, L, D), F32),
            jax.ShapeDtypeStruct((2, L, D), F32),
            jax.ShapeDtypeStruct((2, 1, D), F32),
        ],
        scratch_shapes=[pltpu.VMEM((L, FILTER_HIDDEN), BF16)],
        compiler_params=_cparams(("arbitrary",)),
        name="hyena_filter",
    )(jnp.asarray(_filter_feats(L)), w1p, b1.reshape(1, -1), w2, b2.reshape(1, -1), freq.reshape(1, -1),
      w3r, b3.reshape(4, D), log_decay.reshape(4, D), cs)


FREQ_CHUNK = 1024


def _hyconv_body(x1_ref, x2_ref, v_ref, sw_ref, sb_ref, skip_ref, fr_ref, fi_ref, fn_ref, cs_ref, o_ref,
                 z_ref, g_ref, zb_ref, yr_ref, yi_ref, *, L):
    row, alt = _alt_sign(L)
    kc = min(FREQ_CHUNK, L)
    chunk_row = lax.broadcasted_iota(jnp.int32, (kc, 1), 0)

    def short_conv(u_ref, p):
        u = u_ref[...].astype(F32)
        prev = jnp.where(row == 0, 0.0, pltpu.roll(u, 1, 0))
        nxt = jnp.where(row == L - 1, 0.0, pltpu.roll(u, L - 1, 0))
        return (prev * sw_ref[0, p:p + 1, :] + u * sw_ref[1, p:p + 1, :] + nxt * sw_ref[2, p:p + 1, :]
                + sb_ref[p:p + 1, :])

    z_ref[...] = short_conv(v_ref, 2)
    g_ref[0] = short_conv(x1_ref, 0)
    g_ref[1] = short_conv(x2_ref, 1)
    for o in range(2):
        z = z_ref[...]
        zb_ref[...] = z.astype(BF16)
        nyq = jnp.sum(z * alt, axis=0, keepdims=True) * fn_ref[o]

        def to_freq(c, carry):
            lo = pl.multiple_of(c * kc, kc)
            zr = jnp.dot(cs_ref[pl.ds(lo, kc), :], zb_ref[...], preferred_element_type=F32)
            zs = jnp.dot(cs_ref[pl.ds(L + lo, kc), :], zb_ref[...], preferred_element_type=F32)
            fr = fr_ref[o, pl.ds(lo, kc), :]
            fi = fi_ref[o, pl.ds(lo, kc), :]
            yr_ref[pl.ds(lo, kc), :] = (zr * fr + zs * fi).astype(BF16)
            yi_ref[pl.ds(lo, kc), :] = (zr * fi - zs * fr).astype(BF16)
            return carry

        lax.fori_loop(0, L // kc, to_freq, 0)

        def to_time(c, carry):
            lo = pl.multiple_of(c * kc, kc)
            y = (jnp.dot(cs_ref[pl.ds(lo, kc), :], yr_ref[...], preferred_element_type=F32)
                 - jnp.dot(cs_ref[pl.ds(L + lo, kc), :], yi_ref[...], preferred_element_type=F32))
            y = y + jnp.where(((chunk_row + lo) & 1) == 0, nyq, -nyq)
            z_ref[pl.ds(lo, kc), :] = g_ref[o, pl.ds(lo, kc), :] * (y + skip_ref[o:o + 1, :] * z_ref[pl.ds(lo, kc), :])
            return carry

        lax.fori_loop(0, L // kc, to_time, 0)
    o_ref[...] = z_ref[...].astype(BF16)


def _hyena_conv(up, sw, sb, skip, fr, fi, fn, cs, *, row0, n_batch, L, tc):
    n_ct = D // tc
    base = row0 // L
    u_spec = lambda p: pl.BlockSpec((L, tc), lambda j, b: (base + b, p * n_ct + j))
    once = pl.Buffered(1)
    return pl.pallas_call(
        functools.partial(_hyconv_body, L=L),
        grid=(n_ct, n_batch),
        in_specs=[
            u_spec(0), u_spec(1), u_spec(2),
            pl.BlockSpec((3, 3, tc), lambda j, b: (0, 0, j)),
            pl.BlockSpec((3, tc), lambda j, b: (0, j)),
            pl.BlockSpec((2, tc), lambda j, b: (0, j)),
            pl.BlockSpec((2, L, tc), lambda j, b: (0, 0, j), pipeline_mode=once),
            pl.BlockSpec((2, L, tc), lambda j, b: (0, 0, j), pipeline_mode=once),
            pl.BlockSpec((2, 1, tc), lambda j, b: (0, 0, j)),
            pl.BlockSpec((2 * L, L), lambda j, b: (0, 0), pipeline_mode=once),
        ],
        out_specs=pl.BlockSpec((L, tc), lambda j, b: (b, j)),
        out_shape=jax.ShapeDtypeStruct((n_batch * L, D), BF16),
        scratch_shapes=[
            pltpu.VMEM((L, tc), F32),
            pltpu.VMEM((2, L, tc), F32),
            pltpu.VMEM((L, tc), BF16),
            pltpu.VMEM((L, tc), BF16),
            pltpu.VMEM((L, tc), BF16),
        ],
        compiler_params=_cparams(("arbitrary", "arbitrary")),
        name="hyena_conv",
    )(up, up, up, sw.reshape(3, 3, D), sb.reshape(3, D), skip, fr, fi, fn, cs)


EXP_TILE = 512
SORT_TILES = (2 * N_TOK) // EXP_TILE + N_EXP
SORT_ROWS = SORT_TILES * EXP_TILE
N_QUARTERS = 4
QUARTER_W = D // (2 * N_QUARTERS)
META_I1, META_I2, META_R1, META_R2, META_W1, META_W2 = range(6)
HI_HALF = 0xFFFF0000


def _pack_pairs(x):
    bits = pltpu.bitcast(x.astype(BF16).astype(F32), jnp.uint32)
    return (bits[:, :QUARTER_W] >> 16) | bits[:, QUARTER_W:]


def _unpack_pairs(w):
    return pltpu.bitcast(w << 16, F32), pltpu.bitcast(w & jnp.uint32(HI_HALF), F32)


def _packed_shape(rows):
    return (rows // 8, N_QUARTERS, 8, QUARTER_W)


def _store_packed(ref, x):
    for q in range(N_QUARTERS):
        ref[:, q] = _pack_pairs(x[:, q * 2 * QUARTER_W:(q + 1) * 2 * QUARTER_W]).reshape(-1, 8, QUARTER_W)


def _load_unpacked(ref):
    halves = []
    for q in range(N_QUARTERS):
        halves.extend(_unpack_pairs(ref[:, q].reshape(-1, QUARTER_W)))
    return jnp.concatenate(halves, axis=1)


def _packed_row(ref, tile, sublane):
    return ref.at[tile, :, sublane, :]


def _lane_put(lane, values):
    out = jnp.where(lane == 0, values[0], 0.0)
    for k in range(1, len(values)):
        out = out + jnp.where(lane == k, values[k], 0.0)
    return out


def _route_body(*refs, tm, split_x):
    n_x = 2 if split_x else 1
    ac_ref, al_ref, w_ref, b_ref = refs[:4]
    x_refs = refs[4:4 + n_x]
    (ada_ref, gain_ref, wr_hi_ref, wr_lo_ref, br_ref, tri_ref,
     xo_ref, hq_ref, meta_ref, cnt_ref, carry_ref) = refs[4 + n_x:]
    step = pl.program_id(0)

    @pl.when(step == 0)
    def _():
        carry_ref[...] = jnp.zeros_like(carry_ref)

    mixed = jnp.dot(_pick_split(step, tm, ac_ref, al_ref), w_ref[...], preferred_element_type=F32) + b_ref[...]
    x = _pick_split(step, tm, *x_refs) if split_x else x_refs[0][...]
    x = x + ada_ref[0, 2:3, :] * mixed
    xo_ref[...] = x
    h = _norm_mod(x, gain_ref, ada_ref, 3)
    _store_packed(hq_ref, h)
    h_hi = h.astype(BF16)
    h_lo = (h - h_hi.astype(F32)).astype(BF16)
    logits = (jnp.dot(h_hi, wr_hi_ref[...], preferred_element_type=F32)
              + jnp.dot(h_lo, wr_hi_ref[...], preferred_element_type=F32)
              + jnp.dot(h_hi, wr_lo_ref[...], preferred_element_type=F32) + br_ref[...])
    lane = lax.broadcasted_iota(jnp.int32, (1, ROUTE_W), 1)
    lane_f = lane.astype(F32)
    group_of_lane = (lane >> 3).astype(F32)
    neg = -jnp.inf
    big = float(ROUTE_W)
    is_g = (lane >= N_EXP) & (lane < N_EXP + N_GROUPS)
    gl = jnp.where(is_g, logits, neg)
    gmax = jnp.max(gl, axis=-1, keepdims=True)
    gidx = jnp.min(jnp.where(gl == gmax, lane_f - N_EXP, big), axis=-1, keepdims=True)
    g_top = 1.0 / jnp.sum(jnp.where(is_g, jnp.exp(gl - gmax), 0.0), axis=-1, keepdims=True)
    in_group = (lane < N_EXP) & (group_of_lane == gidx)
    el = jnp.where(in_group, logits, neg)
    v1 = jnp.max(el, axis=-1, keepdims=True)
    i1 = jnp.min(jnp.where(el == v1, lane_f, big), axis=-1, keepdims=True)
    el2 = jnp.where(lane_f == i1, neg, el)
    v2 = jnp.max(el2, axis=-1, keepdims=True)
    i2 = jnp.min(jnp.where(el2 == v2, lane_f, big), axis=-1, keepdims=True)
    r = jnp.exp(v2 - v1)
    w1 = g_top / (1.0 + r)
    w2 = g_top * r / (1.0 + r)
    sel = jnp.where((lane_f == i1) | (lane_f == i2), 1.0, 0.0)
    rank = jnp.dot(tri_ref[...], sel.astype(BF16), preferred_element_type=F32) + carry_ref[...]
    r1 = jnp.sum(jnp.where(lane_f == i1, rank, 0.0), axis=-1, keepdims=True)
    r2 = jnp.sum(jnp.where(lane_f == i2, rank, 0.0), axis=-1, keepdims=True)
    carry_ref[...] += jnp.sum(sel, axis=0, keepdims=True)
    cnt_ref[...] = carry_ref[...]
    meta_ref[...] = _lane_put(lane, (i1, i2, r1, r2, w1, w2))


@functools.lru_cache(maxsize=None)
def _strict_lower(n):
    r = np.arange(n)
    return (r[None, :] < r[:, None]).astype(np.float32)


def _mix_route(a_ctx, a_lat, w_out, b_out, x_parts, ada, gain, wg, bg, we, be, *, tm=1024):
    k = a_ctx.shape[1]
    split_x = len(x_parts) == 2
    x_specs = _split_specs(tm, D) if split_x else [pl.BlockSpec((tm, D), lambda i: (i, 0))]
    wr = jnp.zeros((D, ROUTE_W), F32).at[:, :N_EXP].set(we.reshape(D, N_EXP)).at[:, N_EXP:N_EXP + N_GROUPS].set(wg)
    br = jnp.zeros((1, ROUTE_W), F32).at[0, :N_EXP].set(be.reshape(N_EXP)).at[0, N_EXP:N_EXP + N_GROUPS].set(bg)
    wr_hi = wr.astype(BF16)
    wr_lo = (wr - wr_hi.astype(F32)).astype(BF16)
    return pl.pallas_call(
        functools.partial(_route_body, tm=tm, split_x=split_x),
        grid=(N_TOK // tm,),
        in_specs=_split_specs(tm, k) + [
            pl.BlockSpec((k, D), lambda i: (0, 0)),
            pl.BlockSpec((1, D), lambda i: (0, 0)),
        ] + x_specs + [
            pl.BlockSpec((1, 6, D), lambda i: (_cond_row(i, tm), 0, 0)),
            pl.BlockSpec((1, D), lambda i: (0, 0)),
            pl.BlockSpec((D, ROUTE_W), lambda i: (0, 0)),
            pl.BlockSpec((D, ROUTE_W), lambda i: (0, 0)),
            pl.BlockSpec((1, ROUTE_W), lambda i: (0, 0)),
            pl.BlockSpec((tm, tm), lambda i: (0, 0)),
        ],
        out_specs=[
            pl.BlockSpec((tm, D), lambda i: (i, 0)),
            pl.BlockSpec(_packed_shape(tm), lambda i: (i, 0, 0, 0)),
            pl.BlockSpec((tm, ROUTE_W), lambda i: (i, 0)),
            pl.BlockSpec((1, ROUTE_W), lambda i: (0, 0)),
        ],
        out_shape=[
            jax.ShapeDtypeStruct((N_TOK, D), F32),
            jax.ShapeDtypeStruct(_packed_shape(N_TOK), jnp.uint32),
            jax.ShapeDtypeStruct((N_TOK, ROUTE_W), F32),
            jax.ShapeDtypeStruct((1, ROUTE_W), F32),
        ],
        scratch_shapes=[pltpu.VMEM((1, ROUTE_W), F32)],
        compiler_params=_cparams(("arbitrary",)),
        name="mix_route",
    )(a_ctx, a_lat, w_out, b_out.reshape(1, D), *x_parts, ada, gain.reshape(1, D), wr_hi, wr_lo, br,
      jnp.asarray(_strict_lower(tm)).astype(BF16))


def _slots_body(meta_ref, cnt_ref, upper_ref, pos_ref, plan_ref):
    lane = lax.broadcasted_iota(jnp.int32, (1, ROUTE_W), 1)
    lane_f = lane.astype(F32)
    tiles = jnp.floor((cnt_ref[...] + (EXP_TILE - 1.0)) * (1.0 / EXP_TILE))
    end_tile = jnp.dot(jnp.broadcast_to(tiles, (8, ROUTE_W)).astype(BF16), upper_ref[...],
                       preferred_element_type=F32)[0:1]
    start_row = (end_tile - tiles) * EXP_TILE
    meta = meta_ref[...]
    i1 = meta[:, META_I1:META_I1 + 1]
    i2 = meta[:, META_I2:META_I2 + 1]
    p1 = jnp.sum(jnp.where(lane_f == i1, start_row, 0.0), axis=-1, keepdims=True) + meta[:, META_R1:META_R1 + 1]
    p2 = jnp.sum(jnp.where(lane_f == i2, start_row, 0.0), axis=-1, keepdims=True) + meta[:, META_R2:META_R2 + 1]
    hi1, hi2 = jnp.floor(p1 * 0.125), jnp.floor(p2 * 0.125)
    pos_ref[0] = _lane_put(lane, (hi1, p1 - 8.0 * hi1, hi2, p2 - 8.0 * hi2)).T[:8, :].astype(jnp.int32)

    @pl.when(pl.program_id(0) == 0)
    def _():
        end_col = jnp.broadcast_to(end_tile, (ROUTE_W, ROUTE_W)).T
        expert = lax.broadcasted_iota(jnp.int32, (ROUTE_W, ROUTE_W), 0)
        tile = lax.broadcasted_iota(jnp.int32, (ROUTE_W, ROUTE_W), 1).astype(F32)
        passed = jnp.where((end_col <= tile) & (expert < N_EXP), 1.0, 0.0)
        tile_expert = jnp.minimum(jnp.sum(passed, axis=0, keepdims=True), N_EXP - 1.0)
        used = jnp.max(end_tile, axis=-1, keepdims=True)
        row = lax.broadcasted_iota(jnp.int32, (8, ROUTE_W), 0)
        plan_ref[...] = jnp.where(row == 0, tile_expert, jnp.where(row == 1, used, 0.0)).astype(jnp.int32)


@functools.lru_cache(maxsize=None)
def _upper_incl(n):
    r = np.arange(n)
    return (r[:, None] <= r[None, :]).astype(np.float32)


def _slots(meta, cnt, *, tm=2048):
    n_tiles = N_TOK // tm
    return pl.pallas_call(
        _slots_body,
        grid=(n_tiles,),
        in_specs=[
            pl.BlockSpec((tm, ROUTE_W), lambda i: (i, 0)),
            pl.BlockSpec((1, ROUTE_W), lambda i: (0, 0)),
            pl.BlockSpec((ROUTE_W, ROUTE_W), lambda i: (0, 0)),
        ],
        out_specs=[
            pl.BlockSpec((1, 8, tm), lambda i: (i, 0, 0)),
            pl.BlockSpec((8, ROUTE_W), lambda i: (0, 0)),
        ],
        out_shape=[
            jax.ShapeDtypeStruct((n_tiles, 8, tm), jnp.int32),
            jax.ShapeDtypeStruct((8, ROUTE_W), jnp.int32),
        ],
        compiler_params=_cparams(("arbitrary",)),
        name="moe_slots",
    )(meta, cnt, jnp.asarray(_upper_incl(ROUTE_W)).astype(BF16))


ROW_UNROLL = 8
N_DMA_LANES = 2


def _start_row_copies(tm, make_copy):
    def block(blk, carry):
        for j in range(ROW_UNROLL):
            lane = j % N_DMA_LANES
            for k in range(2):
                make_copy(blk, j, k, lane).start(priority=lane)
        return carry

    lax.fori_loop(0, tm // ROW_UNROLL, block, 0)


def _drain_row_copies(like_src, like_dst, sem):
    for lane in range(N_DMA_LANES):
        pltpu.make_async_copy(like_src, like_dst, sem.at[lane]).wait()


def _dispatch_body(t1_ref, s1_ref, t2_ref, s2_ref, hq_ref, xs_in_ref, xs_ref, sem, *, tm):
    del xs_in_ref
    slots = ((t1_ref, s1_ref), (t2_ref, s2_ref))

    def copy(blk, j, k, lane):
        i = blk * ROW_UNROLL + j
        return pltpu.make_async_copy(_packed_row(hq_ref, blk, j), _packed_row(xs_ref, slots[k][0][i], slots[k][1][i]),
                                     sem.at[lane])

    _start_row_copies(tm, copy)
    _drain_row_copies(hq_ref, xs_ref.at[pl.ds(0, tm // 8)], sem)


def _slot_specs(tm, ahead=0):
    last = N_TOK // tm - 1
    return [pl.BlockSpec((tm,), lambda t: (jnp.minimum(t + ahead, last),), memory_space=pltpu.SMEM) for _ in range(4)]


def _dispatch(slots, hq, init, *, tm=512):
    return pl.pallas_call(
        functools.partial(_dispatch_body, tm=tm),
        grid=(N_TOK // tm,),
        in_specs=_slot_specs(tm) + [
            pl.BlockSpec(_packed_shape(tm), lambda t: (t, 0, 0, 0)),
            pl.BlockSpec(memory_space=pl.ANY),
        ],
        out_specs=pl.BlockSpec(memory_space=pl.ANY),
        out_shape=jax.ShapeDtypeStruct(_packed_shape(SORT_ROWS), jnp.uint32),
        input_output_aliases={5: 0},
        scratch_shapes=[pltpu.SemaphoreType.DMA((N_DMA_LANES,))],
        compiler_params=_cparams(("arbitrary",)),
        name="moe_dispatch",
    )(*slots, hq, init)


def _experts_body(plan_ref, xs_ref, wg_ref, wu_ref, wd_ref, ys_ref, wgu_ref, wdn_ref):
    t = pl.program_id(0)
    in_use = t < plan_ref[1, 0]
    new_expert = (t == 0) | (plan_ref[0, t] != plan_ref[0, jnp.maximum(t - 1, 0)])

    @pl.when(in_use & new_expert)
    def _():
        wgu_ref[:, :EXP_H] = wg_ref[0, 0].astype(BF16)
        wgu_ref[:, EXP_H:] = wu_ref[0, 0].astype(BF16)
        wdn_ref[...] = wd_ref[0, 0].astype(BF16)

    @pl.when(in_use)
    def _():
        x = _load_unpacked(xs_ref).astype(BF16)
        ab = jnp.dot(x, wgu_ref[...], preferred_element_type=F32)
        a, b = ab[:, :EXP_H], ab[:, EXP_H:]
        act = a * _sigmoid(a) * b
        _store_packed(ys_ref, jnp.dot(act.astype(BF16), wdn_ref[...], preferred_element_type=F32))


def _experts(plan, xs, w_gate, w_up, w_down, *, layer):
    def tile_idx(t, plan):
        return (jnp.minimum(t, jnp.maximum(plan[1, 0] - 1, 0)), 0, 0, 0)

    grid_spec = pltpu.PrefetchScalarGridSpec(
        num_scalar_prefetch=1,
        grid=(SORT_TILES,),
        in_specs=[
            pl.BlockSpec(_packed_shape(EXP_TILE), tile_idx),
            pl.BlockSpec((1, 1, D, EXP_H), lambda t, plan: (layer, plan[0, t], 0, 0)),
            pl.BlockSpec((1, 1, D, EXP_H), lambda t, plan: (layer, plan[0, t], 0, 0)),
            pl.BlockSpec((1, 1, EXP_H, D), lambda t, plan: (layer, plan[0, t], 0, 0)),
        ],
        out_specs=pl.BlockSpec(_packed_shape(EXP_TILE), tile_idx),
        scratch_shapes=[pltpu.VMEM((D, 2 * EXP_H), BF16), pltpu.VMEM((EXP_H, D), BF16)],
    )
    return pl.pallas_call(
        _experts_body,
        grid_spec=grid_spec,
        out_shape=jax.ShapeDtypeStruct(_packed_shape(SORT_ROWS), jnp.uint32),
        input_output_aliases={1: 0},
        compiler_params=_cparams(("arbitrary",)),
        name="moe_experts",
    )(plan, xs, w_gate, w_up, w_down)


def _combine_body(*refs, tm, final):
    cur_slots, nxt_slots = refs[0:4], refs[4:8]
    meta_ref, x_ref, ada_ref = refs[8:11]
    rest = refs[11:]
    if final:
        fgain_ref, ys_ref, oc_ref, ol_ref, y_ref, sem = rest
    else:
        ys_ref, o_ref, y_ref, sem = rest
    step = pl.program_id(0)
    n_steps = pl.num_programs(0)

    def fetch(slot_refs, buf):
        slots = ((slot_refs[0], slot_refs[1]), (slot_refs[2], slot_refs[3]))

        def copy(blk, j, k, lane):
            i = blk * ROW_UNROLL + j
            return pltpu.make_async_copy(_packed_row(ys_ref, slots[k][0][i], slots[k][1][i]),
                                         _packed_row(y_ref.at[buf, k], blk, j), sem.at[buf, lane])

        _start_row_copies(tm, copy)

    @pl.when(step == 0)
    def _():
        fetch(cur_slots, 0)

    @pl.when(step + 1 < n_steps)
    def _():
        fetch(nxt_slots, (step + 1) % 2)

    buf = step % 2
    _drain_row_copies(ys_ref.at[pl.ds(0, tm // 8)], y_ref.at[buf, 0], sem.at[buf])
    y_ref = y_ref.at[buf]
    meta = meta_ref[...]
    mix = (meta[:, META_W1:META_W1 + 1] * _load_unpacked(y_ref.at[0])
           + meta[:, META_W2:META_W2 + 1] * _load_unpacked(y_ref.at[1]))
    x = x_ref[...] + ada_ref[0, 5:6, :] * mix
    if not final:
        o_ref[...] = x
        return
    ms = jnp.mean(x * x, axis=-1, keepdims=True)
    y = x * lax.rsqrt(ms + EPS) * fgain_ref[...]
    is_ctx = pl.program_id(0) < N_CTX // tm

    @pl.when(is_ctx)
    def _():
        oc_ref[...] = y

    @pl.when(jnp.logical_not(is_ctx))
    def _():
        ol_ref[...] = y


def _combine(slots, ys, meta, x, ada, final_gain=None, *, tm=512):
    final = final_gain is not None
    extra_in = [pl.BlockSpec((1, D), lambda t: (0, 0))] if final else []
    extra_args = [final_gain.reshape(1, D)] if final else []
    if final:
        out_specs = _split_specs(tm, D)
        out_shape = [jax.ShapeDtypeStruct((N_CTX, D), F32), jax.ShapeDtypeStruct((N_LAT, D), F32)]
    else:
        out_specs = pl.BlockSpec((tm, D), lambda t: (t, 0))
        out_shape = jax.ShapeDtypeStruct((N_TOK, D), F32)
    return pl.pallas_call(
        functools.partial(_combine_body, tm=tm, final=final),
        grid=(N_TOK // tm,),
        in_specs=_slot_specs(tm) + _slot_specs(tm, ahead=1) + [
            pl.BlockSpec((tm, ROUTE_W), lambda t: (t, 0)),
            pl.BlockSpec((tm, D), lambda t: (t, 0)),
            pl.BlockSpec((1, 6, D), lambda t: (_cond_row(t, tm), 0, 0)),
        ] + extra_in + [pl.BlockSpec(memory_space=pl.ANY)],
        out_specs=out_specs,
        out_shape=out_shape,
        scratch_shapes=[pltpu.VMEM((2, 2) + _packed_shape(tm), jnp.uint32),
                        pltpu.SemaphoreType.DMA((2, N_DMA_LANES))],
        compiler_params=_cparams(("arbitrary",)),
        name="moe_combine",
    )(*slots, *slots, meta, x, ada, *extra_args, ys)


def _moe(routed, ada, w_gate, w_up, w_down, *, layer, sort_init, final_gain=None):
    x, hq, meta, cnt = routed
    pos, plan = _slots(meta, cnt)
    slots = [pos[:, r, :].reshape(N_TOK) for r in range(4)]
    xs = _dispatch(slots, hq, sort_init)
    ys = _experts(plan, xs, w_gate, w_up, w_down, layer=layer)
    return _combine(slots, ys, meta, x, ada, final_gain), ys


def kernel(x_prompt, x_sample, cache_k, cache_v, c, c_ctx, w_ada, b_ada, norm_mix, norm_ffn, attn_w_q, attn_w_kv, attn_q_norm, attn_k_norm, attn_w_o, hy_w_in, hy_b_in, hy_short_w, hy_short_b, hy_filt_w1, hy_filt_b1, hy_filt_w2, hy_filt_b2, hy_filt_w3, hy_filt_b3, hy_filt_freq, hy_log_decay, hy_skip, hy_w_out, hy_b_out, router_group_w, router_group_b, router_expert_w, router_expert_b, moe_w_gate, moe_w_up, moe_w_down, final_norm):
    depth = w_ada.shape[0]
    x_parts = (x_prompt.reshape(N_CTX, D), x_sample.reshape(N_LAT, D))
    cond8 = jnp.zeros((8, D), F32).at[0].set(c_ctx).at[1:1 + N_BATCH_LAT].set(c)
    ada_all = _ada(cond8, w_ada, b_ada).reshape(depth, 8, 6, D)
    new_k = []
    new_v = []
    sort_buf = jnp.zeros(_packed_shape(SORT_ROWS), jnp.uint32)
    for l in range(depth):
        ada = ada_all[l]
        if l % 2 == 0:
            a = l // 2
            w_qkv = jnp.concatenate([attn_w_q[a], attn_w_kv[a]], axis=1).astype(BF16)
            q, k_ctx, v_ctx, kc, vc, kl, vl = _qkv(x_parts, ada, norm_mix[l], w_qkv, attn_q_norm[a], attn_k_norm[a])
            new_k.append(k_ctx)
            new_v.append(v_ctx)
            o_ctx = _attention(q, [kc], [vc], row0=0, n_batch=N_BATCH_CTX, seq=L_CTX, tq=L_CTX,
                               n_kv=N_KV, stack=Q_PER_KV)
            past_keys = _lane_tile_heads(cache_k[:, a]).transpose(0, 1, 3, 2)
            o_lat = _attention(q, [kl, past_keys], [vl, _lane_tile_heads(cache_v[:, a])],
                               row0=N_CTX, n_batch=N_BATCH_LAT, seq=L_LAT, tq=1024, n_kv=1, stack=1)
            mixer = (o_ctx, o_lat, attn_w_o[a].astype(BF16), jnp.zeros((D,), F32))
        else:
            j = l // 2
            (x,) = x_parts
            up = _norm_proj(x, ada, norm_mix[l], hy_w_in[j].astype(BF16), hy_b_in[j], part=0)
            outs = []
            for row0, n_batch, L, tc in ((0, N_BATCH_CTX, L_CTX, D), (N_CTX, N_BATCH_LAT, L_LAT, 256)):
                cs = jnp.asarray(_dft_table(L)).astype(BF16)
                fr, fi, fn = _filter_spectra(L, cs, hy_filt_w1[j], hy_filt_b1[j], hy_filt_w2[j], hy_filt_b2[j],
                                             hy_filt_w3[j], hy_filt_b3[j], hy_filt_freq[j], hy_log_decay[j])
                outs.append(_hyena_conv(up, hy_short_w[j], hy_short_b[j], hy_skip[j], fr, fi, fn, cs,
                                        row0=row0, n_batch=n_batch, L=L, tc=tc))
            mixer = (outs[0], outs[1], hy_w_out[j].astype(BF16), hy_b_out[j])
        routed = _mix_route(*mixer, x_parts, ada, norm_ffn[l], router_group_w[l], router_group_b[l],
                            router_expert_w[l], router_expert_b[l])
        out, sort_buf = _moe(routed, ada, moe_w_gate, moe_w_up, moe_w_down, layer=l, sort_init=sort_buf,
                             final_gain=final_norm if l == depth - 1 else None)
        x_parts = (out,)
    y_ctx, y_lat = x_parts[0]
    return (y_ctx.reshape(N_BATCH_CTX, L_CTX, D), y_lat.reshape(N_BATCH_LAT, L_LAT, D),
            jnp.stack(new_k, axis=1), jnp.stack(new_v, axis=1))
```

```python
import functools
import math

import numpy as np
import jax
import jax.numpy as jnp
from jax import lax
from jax.experimental import pallas as pl
from jax.experimental.pallas import tpu as pltpu

F32 = jnp.float32
BF16 = jnp.bfloat16

D = 1024
N_BATCH_CTX = 32
L_CTX = 256
N_BATCH_LAT = 2
L_LAT = 2048
PAST = 256
N_CTX = N_BATCH_CTX * L_CTX
N_LAT = N_BATCH_LAT * L_LAT
N_TOK = N_CTX + N_LAT
GRID_W = 64
N_HEADS = 16
N_KV = 4
DH = 64
Q_PER_KV = N_HEADS // N_KV
KV_W = N_KV * DH
ROPE_THETA = 10000.0
FILTER_FEAT = 17
FEAT_PAD = 32
FILTER_HIDDEN = 64
N_GROUPS = 4
E_PER_G = 8
N_EXP = N_GROUPS * E_PER_G
EXP_H = D // 4
EPS = 1e-6
ROUTE_W = 128
VMEM_LIMIT = 56 * 1024 * 1024


def _cparams(sem):
    return pltpu.CompilerParams(dimension_semantics=sem, vmem_limit_bytes=VMEM_LIMIT)


def _cond_row(i, tm):
    n_ctx_tiles = N_CTX // tm
    return jnp.where(i < n_ctx_tiles, 0, 1 + (i - n_ctx_tiles) // (L_LAT // tm))


def _split_specs(tm, width):
    n_ctx_tiles = N_CTX // tm
    return [pl.BlockSpec((tm, width), lambda i: (jnp.minimum(i, n_ctx_tiles - 1), 0)),
            pl.BlockSpec((tm, width), lambda i: (jnp.maximum(i - n_ctx_tiles, 0), 0))]


def _pick_split(i, tm, ctx_ref, lat_ref):
    return jnp.where(i < N_CTX // tm, ctx_ref[...], lat_ref[...])


def _sigmoid(x):
    return 1.0 / (1.0 + jnp.exp(-x))


@functools.lru_cache(maxsize=None)
def _dft_table(L):
    k = np.arange(L, dtype=np.int64)
    ang = (np.outer(k, k) % (2 * L)).astype(np.float64) * (math.pi / L)
    return np.concatenate([np.cos(ang), np.sin(ang)], axis=0).astype(np.float32)


@functools.lru_cache(maxsize=None)
def _filter_feats(L):
    t = np.linspace(0.0, 1.0, L, dtype=np.float64)[:, None]
    bands = np.linspace(1e-4, 7.0, 8, dtype=np.float64)[None, :]
    w = (2.0 * math.pi) * np.arange(L, dtype=np.float64)[:, None] / L
    feats = np.concatenate([t, np.cos(bands * w), -np.sin(bands * w)], axis=-1)
    out = np.zeros((L, FEAT_PAD), np.float32)
    out[:, :FILTER_FEAT] = feats
    return out


@functools.lru_cache(maxsize=None)
def _rope_tables(tm):
    pos = np.arange(L_LAT)
    row = (pos // GRID_W).astype(np.float64)
    col = (pos % GRID_W).astype(np.float64)
    axis_dim = DH // 2
    inv_freq = ROPE_THETA ** (-np.arange(0, axis_dim, 2, dtype=np.float64) / axis_dim)
    lane = np.arange(KV_W)
    d = lane % DH
    is_col = (d // axis_dim) == 1
    fi = d % (axis_dim // 2)
    first_half = (d % axis_dim) < (axis_dim // 2)
    p = np.where(is_col[None, :], col[:, None], row[:, None])
    ang = p * inv_freq[fi][None, :]
    cos = np.cos(ang)
    sin = np.sin(ang) * np.where(first_half, -1.0, 1.0)[None, :]
    cos = np.concatenate([np.ones((tm, KV_W)), cos], axis=0).astype(np.float32)
    sin = np.concatenate([np.zeros((tm, KV_W)), sin], axis=0).astype(np.float32)
    return cos, sin


@functools.lru_cache(maxsize=None)
def _head_sum_matrix():
    lane = np.arange(KV_W)
    return (lane[:, None] // DH == lane[None, :] // DH).astype(np.float32)


def _ada_body(c_ref, w_ref, b_ref, o_ref):
    c = c_ref[...]
    s = c * _sigmoid(c)
    o_ref[0] = jnp.dot(s.astype(BF16), w_ref[0].astype(BF16), preferred_element_type=F32) + b_ref[0]


def _ada(cond8, w_ada, b_ada):
    depth = w_ada.shape[0]
    tn = 1536
    return pl.pallas_call(
        _ada_body,
        grid=(depth, 6 * D // tn),
        in_specs=[
            pl.BlockSpec((8, D), lambda l, j: (0, 0)),
            pl.BlockSpec((1, D, tn), lambda l, j: (l, 0, j)),
            pl.BlockSpec((1, 1, tn), lambda l, j: (l, 0, j)),
        ],
        out_specs=pl.BlockSpec((1, 8, tn), lambda l, j: (l, 0, j)),
        out_shape=jax.ShapeDtypeStruct((depth, 8, 6 * D), F32),
        compiler_params=_cparams(("arbitrary", "arbitrary")),
        name="ada",
    )(cond8, w_ada, b_ada.reshape(depth, 1, 6 * D))


def _norm_mod(x, gain_ref, ada_ref, part):
    ms = jnp.mean(x * x, axis=-1, keepdims=True)
    y = x * lax.rsqrt(ms + EPS) * gain_ref[...]
    return y * (1.0 + ada_ref[0, part + 1:part + 2, :]) + ada_ref[0, part:part + 1, :]


def _proj_body(x_ref, ada_ref, gain_ref, w_ref, b_ref, o_ref, *, part):
    h = _norm_mod(x_ref[...], gain_ref, ada_ref, part)
    acc = jnp.dot(h.astype(BF16), w_ref[...], preferred_element_type=F32) + b_ref[...]
    o_ref[...] = acc.astype(o_ref.dtype)


def _norm_proj(x, ada, gain, w, b, *, part, tm=1024):
    n_out = w.shape[1]
    return pl.pallas_call(
        functools.partial(_proj_body, part=part),
        grid=(N_TOK // tm,),
        in_specs=[
            pl.BlockSpec((tm, D), lambda i: (i, 0)),
            pl.BlockSpec((1, 6, D), lambda i: (_cond_row(i, tm), 0, 0)),
            pl.BlockSpec((1, D), lambda i: (0, 0)),
            pl.BlockSpec((D, n_out), lambda i: (0, 0)),
            pl.BlockSpec((1, n_out), lambda i: (0, 0)),
        ],
        out_specs=pl.BlockSpec((tm, n_out), lambda i: (i, 0)),
        out_shape=jax.ShapeDtypeStruct((N_TOK, n_out), BF16),
        compiler_params=_cparams(("arbitrary",)),
        name="norm_proj",
    )(x, ada, gain.reshape(1, D), w, b.reshape(1, n_out))


def _head_rms_rope(x, gain, hs, cos, sin, lane):
    ss = jnp.dot((x * x).astype(BF16), hs, preferred_element_type=F32)
    xn = x * lax.rsqrt(ss * (1.0 / DH) + EPS) * gain
    fwd = pltpu.roll(xn, KV_W - DH // 4, 1)
    bwd = pltpu.roll(xn, DH // 4, 1)
    partner = jnp.where((lane & (DH // 4)) == 0, fwd, bwd)
    return xn * cos + partner * sin


@functools.lru_cache(maxsize=None)
def _lane_tile_matrix():
    col = np.arange(N_KV * KV_W)
    src = (col // KV_W) * DH + col % DH
    return (np.arange(KV_W)[:, None] == src[None, :]).astype(np.float32)


def _qkv_body(*refs, tm, split_x):
    n_x = 2 if split_x else 1
    x_refs = refs[:n_x]
    (ada_ref, gain_ref, w_ref, qg_ref, kg_ref, hs_ref, cos_ref, sin_ref, tile_ref, tile_t_ref,
     q_ref, newk_ref, newv_ref, kc_ref, vc_ref, kl_ref, vl_ref) = refs[n_x:]
    step = pl.program_id(0)
    x = _pick_split(step, tm, *x_refs) if split_x else x_refs[0][...]
    h = _norm_mod(x, gain_ref, ada_ref, 0)
    acc = jnp.dot(h.astype(BF16), w_ref[...], preferred_element_type=F32)
    hs = hs_ref[...]
    cos = cos_ref[...]
    sin = sin_ref[...]
    lane = lax.broadcasted_iota(jnp.int32, (1, KV_W), 1)
    for c in range(N_KV):
        qc = _head_rms_rope(acc[:, c * KV_W:(c + 1) * KV_W], qg_ref[...], hs, cos, sin, lane)
        q_ref[:, c * KV_W:(c + 1) * KV_W] = (qc * (DH ** -0.5)).astype(BF16)
    k = _head_rms_rope(acc[:, D:D + KV_W], kg_ref[...], hs, cos, sin, lane)
    v = acc[:, D + KV_W:D + 2 * KV_W]
    k4t = lax.dot_general(tile_t_ref[...], k.astype(BF16), (((1,), (1,)), ((), ())),
                          preferred_element_type=F32).astype(BF16)
    v4 = jnp.dot(v.astype(BF16), tile_ref[...], preferred_element_type=F32).astype(BF16)
    is_ctx = step < N_CTX // tm

    @pl.when(is_ctx)
    def _():
        for bb in range(tm // L_CTX):
            rows = slice(bb * L_CTX, (bb + 1) * L_CTX)
            for hd in range(N_KV):
                newk_ref[bb, hd] = k[rows, hd * DH:(hd + 1) * DH]
                newv_ref[bb, hd] = v[rows, hd * DH:(hd + 1) * DH]
                kc_ref[bb, hd] = k4t[hd * KV_W:(hd + 1) * KV_W, rows]
                vc_ref[bb, hd] = v4[rows, hd * KV_W:(hd + 1) * KV_W]

    @pl.when(jnp.logical_not(is_ctx))
    def _():
        for hd in range(N_KV):
            kl_ref[0, hd] = k4t[hd * KV_W:(hd + 1) * KV_W, :]
            vl_ref[0, hd] = v4[:, hd * KV_W:(hd + 1) * KV_W]


def _qkv(x_parts, ada, gain, w_qkv, q_gain, k_gain):
    tm = 2 * L_CTX
    per_tile = tm // L_CTX
    cos, sin = _rope_tables(tm)
    split_x = len(x_parts) == 2
    x_specs = _split_specs(tm, D) if split_x else [pl.BlockSpec((tm, D), lambda i: (i, 0))]
    n_ctx_tiles = N_CTX // tm
    lat_tiles = L_LAT // tm

    def rope_idx(i):
        return (jnp.where(i < n_ctx_tiles, 0, 1 + (i - n_ctx_tiles) % lat_tiles), 0)

    def ctx_idx(i):
        return (jnp.minimum(i, n_ctx_tiles - 1), 0, 0, 0)

    def lat_idx(i):
        j = jnp.maximum(i - n_ctx_tiles, 0)
        return (j // lat_tiles, 0, j % lat_tiles, 0)

    def lat_idx_t(i):
        j = jnp.maximum(i - n_ctx_tiles, 0)
        return (j // lat_tiles, 0, 0, j % lat_tiles)

    const = lambda i: (0, 0)
    kv_ctx = jax.ShapeDtypeStruct((N_BATCH_CTX, N_KV, L_CTX, DH), F32)
    tiled_ctx = jax.ShapeDtypeStruct((N_BATCH_CTX, N_KV, L_CTX, KV_W), BF16)
    tiled_lat = jax.ShapeDtypeStruct((N_BATCH_LAT, N_KV, L_LAT, KV_W), BF16)
    keys_ctx = jax.ShapeDtypeStruct((N_BATCH_CTX, N_KV, KV_W, L_CTX), BF16)
    keys_lat = jax.ShapeDtypeStruct((N_BATCH_LAT, N_KV, KV_W, L_LAT), BF16)
    return pl.pallas_call(
        functools.partial(_qkv_body, tm=tm, split_x=split_x),
        grid=(N_TOK // tm,),
        in_specs=x_specs + [
            pl.BlockSpec((1, 6, D), lambda i: (_cond_row(i, tm), 0, 0)),
            pl.BlockSpec((1, D), const),
            pl.BlockSpec((D, D + 2 * KV_W), const),
            pl.BlockSpec((1, KV_W), const),
            pl.BlockSpec((1, KV_W), const),
            pl.BlockSpec((KV_W, KV_W), const),
            pl.BlockSpec((tm, KV_W), rope_idx),
            pl.BlockSpec((tm, KV_W), rope_idx),
            pl.BlockSpec((KV_W, N_KV * KV_W), const),
            pl.BlockSpec((N_KV * KV_W, KV_W), const),
        ],
        out_specs=[
            pl.BlockSpec((tm, D), lambda i: (i, 0)),
            pl.BlockSpec((per_tile, N_KV, L_CTX, DH), ctx_idx),
            pl.BlockSpec((per_tile, N_KV, L_CTX, DH), ctx_idx),
            pl.BlockSpec((per_tile, N_KV, KV_W, L_CTX), ctx_idx),
            pl.BlockSpec((per_tile, N_KV, L_CTX, KV_W), ctx_idx),
            pl.BlockSpec((1, N_KV, KV_W, tm), lat_idx_t),
            pl.BlockSpec((1, N_KV, tm, KV_W), lat_idx),
        ],
        out_shape=[jax.ShapeDtypeStruct((N_TOK, D), BF16), kv_ctx, kv_ctx, keys_ctx, tiled_ctx, keys_lat, tiled_lat],
        compiler_params=_cparams(("arbitrary",)),
        name="qkv",
    )(*x_parts, ada, gain.reshape(1, D), w_qkv,
      jnp.tile(q_gain, Q_PER_KV).reshape(1, KV_W), jnp.tile(k_gain, N_KV).reshape(1, KV_W),
      jnp.asarray(_head_sum_matrix()).astype(BF16), jnp.asarray(cos), jnp.asarray(sin),
      jnp.asarray(_lane_tile_matrix()).astype(BF16), jnp.asarray(_lane_tile_matrix().T.copy()).astype(BF16))


def _attn_body(*refs, n_kv, stack, n_seg):
    q_ref, k_refs, v_refs, o_ref = refs[0], refs[1:1 + n_seg], refs[1 + n_seg:1 + 2 * n_seg], refs[-1]
    lane = lax.broadcasted_iota(jnp.int32, (1, KV_W), 1)
    masks = [(lane >> 6) == g for g in range(Q_PER_KV)]
    tq = q_ref.shape[0]
    for kv in range(n_kv):
        q = q_ref[:, kv * KV_W:(kv + 1) * KV_W]
        out = jnp.zeros((tq, KV_W), F32)
        for c in range(0, Q_PER_KV, stack):
            pair = masks[c:c + stack]
            ones_blk = (c + stack) % Q_PER_KV if stack < Q_PER_KV else None
            stacked = jnp.concatenate([jnp.where(m, q, jnp.zeros_like(q)) for m in pair], axis=0)
            scores = [jnp.dot(stacked, k_ref[0, kv], preferred_element_type=F32) for k_ref in k_refs]
            top = functools.reduce(jnp.maximum, [jnp.max(s, axis=-1, keepdims=True) for s in scores])
            probs = [jnp.exp((s - top).astype(BF16)) for s in scores]
            if ones_blk is None:
                vals = [v_ref[0, kv] for v_ref in v_refs]
            else:
                vals = [jnp.where(masks[ones_blk], jnp.ones((), BF16), v_ref[0, kv]) for v_ref in v_refs]
            og = sum(jnp.dot(p, v, preferred_element_type=F32) for p, v in zip(probs, vals))
            if ones_blk is None:
                denom = sum(jnp.sum(p.astype(F32), axis=-1, keepdims=True) for p in probs)
            else:
                denom = og[:, ones_blk * DH:ones_blk * DH + 1]
            og = og * (1.0 / denom)
            for g, m in enumerate(pair):
                out = out + jnp.where(m, og[g * tq:(g + 1) * tq], 0.0)
        o_ref[:, kv * KV_W:(kv + 1) * KV_W] = out.astype(BF16)


def _attention(q, keys, values, *, row0, n_batch, seq, tq, n_kv, stack):
    per_b = seq // tq
    base = row0 // tq
    kv_specs = [pl.BlockSpec((1, n_kv) + a.shape[2:], lambda b, h, i: (b, h, 0, 0)) for a in keys + values]
    return pl.pallas_call(
        functools.partial(_attn_body, n_kv=n_kv, stack=stack, n_seg=len(keys)),
        grid=(n_batch, N_KV // n_kv, per_b),
        in_specs=[pl.BlockSpec((tq, n_kv * KV_W), lambda b, h, i: (base + b * per_b + i, h))] + kv_specs,
        out_specs=pl.BlockSpec((tq, n_kv * KV_W), lambda b, h, i: (b * per_b + i, h)),
        out_shape=jax.ShapeDtypeStruct((n_batch * seq, D), BF16),
        compiler_params=_cparams(("arbitrary", "arbitrary", "arbitrary")),
        name="attention",
    )(q, *keys, *values)


def _lane_tile_heads(x):
    return jnp.tile(x.astype(BF16), (1, 1, 1, KV_W // DH))


def _alt_sign(L):
    row = lax.broadcasted_iota(jnp.int32, (L, 1), 0)
    return row, jnp.where((row & 1) == 0, 1.0, -1.0)


def _split_dot(a, b):
    a_hi, b_hi = a.astype(BF16), b.astype(BF16)
    a_lo = (a - a_hi.astype(F32)).astype(BF16)
    b_lo = (b - b_hi.astype(F32)).astype(BF16)
    return (jnp.dot(a_hi, b_hi, preferred_element_type=F32) + jnp.dot(a_lo, b_hi, preferred_element_type=F32)
            + jnp.dot(a_hi, b_lo, preferred_element_type=F32))


def _filter_body(feat_ref, w1_ref, b1_ref, w2_ref, b2_ref, fq_ref, w3_ref, b3_ref, ld_ref, cs_ref,
                 fr_ref, fi_ref, fn_ref, hid_ref, *, L):
    feats = feat_ref[...]

    @pl.when(pl.program_id(0) == 0)
    def _():
        fq = fq_ref[...]
        h1 = jnp.sin(fq * (_split_dot(feats, w1_ref[...]) + b1_ref[...]))
        hid_ref[...] = jnp.sin(fq * (_split_dot(h1, w2_ref[...]) + b2_ref[...])).astype(BF16)

    t = feats[:, 0:1]
    row, alt = _alt_sign(L)
    filt = []
    for j in range(4):
        raw = jnp.dot(hid_ref[...], w3_ref[j].astype(BF16), preferred_element_type=F32) + b3_ref[j:j + 1, :]
        filt.append(raw * jnp.exp(-t * jnp.exp(ld_ref[j:j + 1, :])))
    for o in range(2):
        hf, hb = filt[2 * o], filt[2 * o + 1]
        l1 = jnp.sum(jnp.abs(hf), axis=0, keepdims=True) + jnp.sum(jnp.abs(hb), axis=0, keepdims=True)
        inv = 1.0 / (l1 + EPS)
        sym = (hf + hb) * inv
        asym = (hb - hf) * inv
        fr = jnp.dot(cs_ref[0:L, :], sym.astype(BF16), preferred_element_type=F32) * (1.0 / L)
        fr_ref[o] = jnp.where(row == 0, 0.5 * fr, fr)
        fi_ref[o] = jnp.dot(cs_ref[L:2 * L, :], asym.astype(BF16), preferred_element_type=F32) * (1.0 / L)
        fn_ref[o] = jnp.sum(sym * alt, axis=0, keepdims=True) * (0.5 / L)


def _filter_spectra(L, cs, w1, b1, w2, b2, w3, b3, freq, log_decay, *, tc=256):
    w1p = jnp.zeros((FEAT_PAD, FILTER_HIDDEN), F32).at[:FILTER_FEAT].set(w1)
    w3r = w3.reshape(FILTER_HIDDEN, 4, D).transpose(1, 0, 2)
    const = lambda j: (0, 0)
    return pl.pallas_call(
        functools.partial(_filter_body, L=L),
        grid=(D // tc,),
        in_specs=[
            pl.BlockSpec((L, FEAT_PAD), const),
            pl.BlockSpec((FEAT_PAD, FILTER_HIDDEN), const),
            pl.BlockSpec((1, FILTER_HIDDEN), const),
            pl.BlockSpec((FILTER_HIDDEN, FILTER_HIDDEN), const),
            pl.BlockSpec((1, FILTER_HIDDEN), const),
            pl.BlockSpec((1, FILTER_HIDDEN), const),
            pl.BlockSpec((4, FILTER_HIDDEN, tc), lambda j: (0, 0, j)),
            pl.BlockSpec((4, tc), lambda j: (0, j)),
            pl.BlockSpec((4, tc), lambda j: (0, j)),
            pl.BlockSpec((2 * L, L), const, pipeline_mode=pl.Buffered(1)),
        ],
        out_specs=[
            pl.BlockSpec((2, L, tc), lambda j: (0, 0, j)),
            pl.BlockSpec((2, L, tc), lambda j: (0, 0, j)),
            pl.BlockSpec((2, 1, tc), lambda j: (0, 0, j)),
        ],
        out_shape=[
            jax.ShapeDtypeStruct((2, L, D), F32),
            jax.ShapeDtypeStruct((2, L, D), F32),
            jax.ShapeDtypeStruct((2, 1, D), F32),
        ],
        scratch_shapes=[pltpu.VMEM((L, FILTER_HIDDEN), BF16)],
        compiler_params=_cparams(("arbitrary",)),
        name="hyena_filter",
    )(jnp.asarray(_filter_feats(L)), w1p, b1.reshape(1, -1), w2, b2.reshape(1, -1), freq.reshape(1, -1),
      w3r, b3.reshape(4, D), log_decay.reshape(4, D), cs)


FREQ_CHUNK = 1024


def _hyconv_body(x1_ref, x2_ref, v_ref, sw_ref, sb_ref, skip_ref, fr_ref, fi_ref, fn_ref, cs_ref, o_ref,
                 z_ref, g_ref, zb_ref, yr_ref, yi_ref, *, L):
    row, alt = _alt_sign(L)
    kc = min(FREQ_CHUNK, L)
    chunk_row = lax.broadcasted_iota(jnp.int32, (kc, 1), 0)

    def short_conv(u_ref, p):
        u = u_ref[...].astype(F32)
        prev = jnp.where(row == 0, 0.0, pltpu.roll(u, 1, 0))
        nxt = jnp.where(row == L - 1, 0.0, pltpu.roll(u, L - 1, 0))
        return (prev * sw_ref[0, p:p + 1, :] + u * sw_ref[1, p:p + 1, :] + nxt * sw_ref[2, p:p + 1, :]
                + sb_ref[p:p + 1, :])

    z_ref[...] = short_conv(v_ref, 2)
    g_ref[0] = short_conv(x1_ref, 0)
    g_ref[1] = short_conv(x2_ref, 1)
    for o in range(2):
        z = z_ref[...]
        zb_ref[...] = z.astype(BF16)
        nyq = jnp.sum(z * alt, axis=0, keepdims=True) * fn_ref[o]

        def to_freq(c, carry):
            lo = pl.multiple_of(c * kc, kc)
            zr = jnp.dot(cs_ref[pl.ds(lo, kc), :], zb_ref[...], preferred_element_type=F32)
            zs = jnp.dot(cs_ref[pl.ds(L + lo, kc), :], zb_ref[...], preferred_element_type=F32)
            fr = fr_ref[o, pl.ds(lo, kc), :]
            fi = fi_ref[o, pl.ds(lo, kc), :]
            yr_ref[pl.ds(lo, kc), :] = (zr * fr + zs * fi).astype(BF16)
            yi_ref[pl.ds(lo, kc), :] = (zr * fi - zs * fr).astype(BF16)
            return carry

        lax.fori_loop(0, L // kc, to_freq, 0, unroll=True)

        def to_time(c, carry):
            lo = pl.multiple_of(c * kc, kc)
            y = (jnp.dot(cs_ref[pl.ds(lo, kc), :], yr_ref[...], preferred_element_type=F32)
                 - jnp.dot(cs_ref[pl.ds(L + lo, kc), :], yi_ref[...], preferred_element_type=F32))
            y = y + jnp.where(((chunk_row + lo) & 1) == 0, nyq, -nyq)
            z_ref[pl.ds(lo, kc), :] = g_ref[o, pl.ds(lo, kc), :] * (y + skip_ref[o:o + 1, :] * z_ref[pl.ds(lo, kc), :])
            return carry

        lax.fori_loop(0, L // kc, to_time, 0, unroll=True)
    o_ref[...] = z_ref[...].astype(BF16)


def _hyena_conv(up, sw, sb, skip, fr, fi, fn, cs, *, row0, n_batch, L, tc):
    n_ct = D // tc
    base = row0 // L
    u_spec = lambda p: pl.BlockSpec((L, tc), lambda j, b: (base + b, p * n_ct + j))
    once = pl.Buffered(1)
    return pl.pallas_call(
        functools.partial(_hyconv_body, L=L),
        grid=(n_ct, n_batch),
        in_specs=[
            u_spec(0), u_spec(1), u_spec(2),
            pl.BlockSpec((3, 3, tc), lambda j, b: (0, 0, j)),
            pl.BlockSpec((3, tc), lambda j, b: (0, j)),
            pl.BlockSpec((2, tc), lambda j, b: (0, j)),
            pl.BlockSpec((2, L, tc), lambda j, b: (0, 0, j), pipeline_mode=once),
            pl.BlockSpec((2, L, tc), lambda j, b: (0, 0, j), pipeline_mode=once),
            pl.BlockSpec((2, 1, tc), lambda j, b: (0, 0, j)),
            pl.BlockSpec((2 * L, L), lambda j, b: (0, 0), pipeline_mode=once),
        ],
        out_specs=pl.BlockSpec((L, tc), lambda j, b: (b, j)),
        out_shape=jax.ShapeDtypeStruct((n_batch * L, D), BF16),
        scratch_shapes=[
            pltpu.VMEM((L, tc), F32),
            pltpu.VMEM((2, L, tc), F32),
            pltpu.VMEM((L, tc), BF16),
            pltpu.VMEM((L, tc), BF16),
            pltpu.VMEM((L, tc), BF16),
        ],
        compiler_params=_cparams(("arbitrary", "arbitrary")),
        name="hyena_conv",
    )(up, up, up, sw.reshape(3, 3, D), sb.reshape(3, D), skip, fr, fi, fn, cs)


EXP_TILE = 512
SORT_TILES = (2 * N_TOK) // EXP_TILE + N_EXP
SORT_ROWS = SORT_TILES * EXP_TILE
N_QUARTERS = 4
QUARTER_W = D // (2 * N_QUARTERS)
META_I1, META_I2, META_R1, META_R2, META_W1, META_W2 = range(6)
HI_HALF = 0xFFFF0000


def _pack_pairs(x):
    bits = pltpu.bitcast(x.astype(BF16).astype(F32), jnp.uint32)
    return (bits[:, :QUARTER_W] >> 16) | bits[:, QUARTER_W:]


def _unpack_pairs(w):
    return pltpu.bitcast(w << 16, F32), pltpu.bitcast(w & jnp.uint32(HI_HALF), F32)


def _packed_shape(rows):
    return (rows // 8, N_QUARTERS, 8, QUARTER_W)


def _store_packed(ref, x):
    for q in range(N_QUARTERS):
        ref[:, q] = _pack_pairs(x[:, q * 2 * QUARTER_W:(q + 1) * 2 * QUARTER_W]).reshape(-1, 8, QUARTER_W)


def _load_unpacked(ref):
    halves = []
    for q in range(N_QUARTERS):
        halves.extend(_unpack_pairs(ref[:, q].reshape(-1, QUARTER_W)))
    return jnp.concatenate(halves, axis=1)


def _packed_row(ref, tile, sublane):
    return ref.at[tile, :, sublane, :]


def _lane_put(lane, values):
    out = jnp.where(lane == 0, values[0], 0.0)
    for k in range(1, len(values)):
        out = out + jnp.where(lane == k, values[k], 0.0)
    return out


def _route_body(*refs, tm, split_x):
    n_x = 2 if split_x else 1
    ac_ref, al_ref, w_ref, b_ref = refs[:4]
    x_refs = refs[4:4 + n_x]
    (ada_ref, gain_ref, wr_hi_ref, wr_lo_ref, br_ref, tri_ref,
     xo_ref, hq_ref, meta_ref, cnt_ref, carry_ref) = refs[4 + n_x:]
    step = pl.program_id(0)

    @pl.when(step == 0)
    def _():
        carry_ref[...] = jnp.zeros_like(carry_ref)

    mixed = jnp.dot(_pick_split(step, tm, ac_ref, al_ref), w_ref[...], preferred_element_type=F32) + b_ref[...]
    x = _pick_split(step, tm, *x_refs) if split_x else x_refs[0][...]
    x = x + ada_ref[0, 2:3, :] * mixed
    xo_ref[...] = x
    h = _norm_mod(x, gain_ref, ada_ref, 3)
    _store_packed(hq_ref, h)
    h_hi = h.astype(BF16)
    h_lo = (h - h_hi.astype(F32)).astype(BF16)
    logits = (jnp.dot(h_hi, wr_hi_ref[...], preferred_element_type=F32)
              + jnp.dot(h_lo, wr_hi_ref[...], preferred_element_type=F32)
              + jnp.dot(h_hi, wr_lo_ref[...], preferred_element_type=F32) + br_ref[...])
    lane = lax.broadcasted_iota(jnp.int32, (1, ROUTE_W), 1)
    lane_f = lane.astype(F32)
    group_of_lane = (lane >> 3).astype(F32)
    neg = -jnp.inf
    big = float(ROUTE_W)
    is_g = (lane >= N_EXP) & (lane < N_EXP + N_GROUPS)
    gl = jnp.where(is_g, logits, neg)
    gmax = jnp.max(gl, axis=-1, keepdims=True)
    gidx = jnp.min(jnp.where(gl == gmax, lane_f - N_EXP, big), axis=-1, keepdims=True)
    g_top = 1.0 / jnp.sum(jnp.where(is_g, jnp.exp(gl - gmax), 0.0), axis=-1, keepdims=True)
    in_group = (lane < N_EXP) & (group_of_lane == gidx)
    el = jnp.where(in_group, logits, neg)
    v1 = jnp.max(el, axis=-1, keepdims=True)
    i1 = jnp.min(jnp.where(el == v1, lane_f, big), axis=-1, keepdims=True)
    el2 = jnp.where(lane_f == i1, neg, el)
    v2 = jnp.max(el2, axis=-1, keepdims=True)
    i2 = jnp.min(jnp.where(el2 == v2, lane_f, big), axis=-1, keepdims=True)
    r = jnp.exp(v2 - v1)
    w1 = g_top / (1.0 + r)
    w2 = g_top * r / (1.0 + r)
    sel = jnp.where((lane_f == i1) | (lane_f == i2), 1.0, 0.0)
    rank = jnp.dot(tri_ref[...], sel.astype(BF16), preferred_element_type=F32) + carry_ref[...]
    r1 = jnp.sum(jnp.where(lane_f == i1, rank, 0.0), axis=-1, keepdims=True)
    r2 = jnp.sum(jnp.where(lane_f == i2, rank, 0.0), axis=-1, keepdims=True)
    carry_ref[...] += jnp.sum(sel, axis=0, keepdims=True)
    cnt_ref[...] = carry_ref[...]
    meta_ref[...] = _lane_put(lane, (i1, i2, r1, r2, w1, w2))


@functools.lru_cache(maxsize=None)
def _strict_lower(n):
    r = np.arange(n)
    return (r[None, :] < r[:, None]).astype(np.float32)


def _mix_route(a_ctx, a_lat, w_out, b_out, x_parts, ada, gain, wg, bg, we, be, *, tm=1024):
    k = a_ctx.shape[1]
    split_x = len(x_parts) == 2
    x_specs = _split_specs(tm, D) if split_x else [pl.BlockSpec((tm, D), lambda i: (i, 0))]
    wr = jnp.zeros((D, ROUTE_W), F32).at[:, :N_EXP].set(we.reshape(D, N_EXP)).at[:, N_EXP:N_EXP + N_GROUPS].set(wg)
    br = jnp.zeros((1, ROUTE_W), F32).at[0, :N_EXP].set(be.reshape(N_EXP)).at[0, N_EXP:N_EXP + N_GROUPS].set(bg)
    wr_hi = wr.astype(BF16)
    wr_lo = (wr - wr_hi.astype(F32)).astype(BF16)
    return pl.pallas_call(
        functools.partial(_route_body, tm=tm, split_x=split_x),
        grid=(N_TOK // tm,),
        in_specs=_split_specs(tm, k) + [
            pl.BlockSpec((k, D), lambda i: (0, 0)),
            pl.BlockSpec((1, D), lambda i: (0, 0)),
        ] + x_specs + [
            pl.BlockSpec((1, 6, D), lambda i: (_cond_row(i, tm), 0, 0)),
            pl.BlockSpec((1, D), lambda i: (0, 0)),
            pl.BlockSpec((D, ROUTE_W), lambda i: (0, 0)),
            pl.BlockSpec((D, ROUTE_W), lambda i: (0, 0)),
            pl.BlockSpec((1, ROUTE_W), lambda i: (0, 0)),
            pl.BlockSpec((tm, tm), lambda i: (0, 0)),
        ],
        out_specs=[
            pl.BlockSpec((tm, D), lambda i: (i, 0)),
            pl.BlockSpec(_packed_shape(tm), lambda i: (i, 0, 0, 0)),
            pl.BlockSpec((tm, ROUTE_W), lambda i: (i, 0)),
            pl.BlockSpec((1, ROUTE_W), lambda i: (0, 0)),
        ],
        out_shape=[
            jax.ShapeDtypeStruct((N_TOK, D), F32),
            jax.ShapeDtypeStruct(_packed_shape(N_TOK), jnp.uint32),
            jax.ShapeDtypeStruct((N_TOK, ROUTE_W), F32),
            jax.ShapeDtypeStruct((1, ROUTE_W), F32),
        ],
        scratch_shapes=[pltpu.VMEM((1, ROUTE_W), F32)],
        compiler_params=_cparams(("arbitrary",)),
        name="mix_route",
    )(a_ctx, a_lat, w_out, b_out.reshape(1, D), *x_parts, ada, gain.reshape(1, D), wr_hi, wr_lo, br,
      jnp.asarray(_strict_lower(tm)).astype(BF16))


def _slots_body(meta_ref, cnt_ref, upper_ref, pos_ref, plan_ref):
    lane = lax.broadcasted_iota(jnp.int32, (1, ROUTE_W), 1)
    lane_f = lane.astype(F32)
    tiles = jnp.floor((cnt_ref[...] + (EXP_TILE - 1.0)) * (1.0 / EXP_TILE))
    end_tile = jnp.dot(jnp.broadcast_to(tiles, (8, ROUTE_W)).astype(BF16), upper_ref[...],
                       preferred_element_type=F32)[0:1]
    start_row = (end_tile - tiles) * EXP_TILE
    meta = meta_ref[...]
    i1 = meta[:, META_I1:META_I1 + 1]
    i2 = meta[:, META_I2:META_I2 + 1]
    p1 = jnp.sum(jnp.where(lane_f == i1, start_row, 0.0), axis=-1, keepdims=True) + meta[:, META_R1:META_R1 + 1]
    p2 = jnp.sum(jnp.where(lane_f == i2, start_row, 0.0), axis=-1, keepdims=True) + meta[:, META_R2:META_R2 + 1]
    hi1, hi2 = jnp.floor(p1 * 0.125), jnp.floor(p2 * 0.125)
    pos_ref[0] = _lane_put(lane, (hi1, p1 - 8.0 * hi1, hi2, p2 - 8.0 * hi2)).T[:8, :].astype(jnp.int32)

    @pl.when(pl.program_id(0) == 0)
    def _():
        end_col = jnp.broadcast_to(end_tile, (ROUTE_W, ROUTE_W)).T
        expert = lax.broadcasted_iota(jnp.int32, (ROUTE_W, ROUTE_W), 0)
        tile = lax.broadcasted_iota(jnp.int32, (ROUTE_W, ROUTE_W), 1).astype(F32)
        passed = jnp.where((end_col <= tile) & (expert < N_EXP), 1.0, 0.0)
        tile_expert = jnp.minimum(jnp.sum(passed, axis=0, keepdims=True), N_EXP - 1.0)
        used = jnp.max(end_tile, axis=-1, keepdims=True)
        row = lax.broadcasted_iota(jnp.int32, (8, ROUTE_W), 0)
        plan_ref[...] = jnp.where(row == 0, tile_expert, jnp.where(row == 1, used, 0.0)).astype(jnp.int32)


@functools.lru_cache(maxsize=None)
def _upper_incl(n):
    r = np.arange(n)
    return (r[:, None] <= r[None, :]).astype(np.float32)


def _slots(meta, cnt, *, tm=2048):
    n_tiles = N_TOK // tm
    return pl.pallas_call(
        _slots_body,
        grid=(n_tiles,),
        in_specs=[
            pl.BlockSpec((tm, ROUTE_W), lambda i: (i, 0)),
            pl.BlockSpec((1, ROUTE_W), lambda i: (0, 0)),
            pl.BlockSpec((ROUTE_W, ROUTE_W), lambda i: (0, 0)),
        ],
        out_specs=[
            pl.BlockSpec((1, 8, tm), lambda i: (i, 0, 0)),
            pl.BlockSpec((8, ROUTE_W), lambda i: (0, 0)),
        ],
        out_shape=[
            jax.ShapeDtypeStruct((n_tiles, 8, tm), jnp.int32),
            jax.ShapeDtypeStruct((8, ROUTE_W), jnp.int32),
        ],
        compiler_params=_cparams(("arbitrary",)),
        name="moe_slots",
    )(meta, cnt, jnp.asarray(_upper_incl(ROUTE_W)).astype(BF16))


ROW_UNROLL = 8
N_DMA_LANES = 2


def _start_row_copies(tm, make_copy):
    def block(blk, carry):
        for j in range(ROW_UNROLL):
            lane = j % N_DMA_LANES
            for k in range(2):
                make_copy(blk, j, k, lane).start(priority=lane)
        return carry

    lax.fori_loop(0, tm // ROW_UNROLL, block, 0)


def _drain_row_copies(like_src, like_dst, sem):
    for lane in range(N_DMA_LANES):
        pltpu.make_async_copy(like_src, like_dst, sem.at[lane]).wait()


def _dispatch_body(t1_ref, s1_ref, t2_ref, s2_ref, hq_ref, xs_in_ref, xs_ref, sem, *, tm):
    del xs_in_ref
    slots = ((t1_ref, s1_ref), (t2_ref, s2_ref))

    def copy(blk, j, k, lane):
        i = blk * ROW_UNROLL + j
        return pltpu.make_async_copy(_packed_row(hq_ref, blk, j), _packed_row(xs_ref, slots[k][0][i], slots[k][1][i]),
                                     sem.at[lane])

    _start_row_copies(tm, copy)
    _drain_row_copies(hq_ref, xs_ref.at[pl.ds(0, tm // 8)], sem)


def _slot_specs(tm, ahead=0):
    last = N_TOK // tm - 1
    return [pl.BlockSpec((tm,), lambda t: (jnp.minimum(t + ahead, last),), memory_space=pltpu.SMEM) for _ in range(4)]


def _dispatch(slots, hq, init, *, tm=512):
    return pl.pallas_call(
        functools.partial(_dispatch_body, tm=tm),
        grid=(N_TOK // tm,),
        in_specs=_slot_specs(tm) + [
            pl.BlockSpec(_packed_shape(tm), lambda t: (t, 0, 0, 0)),
            pl.BlockSpec(memory_space=pl.ANY),
        ],
        out_specs=pl.BlockSpec(memory_space=pl.ANY),
        out_shape=jax.ShapeDtypeStruct(_packed_shape(SORT_ROWS), jnp.uint32),
        input_output_aliases={5: 0},
        scratch_shapes=[pltpu.SemaphoreType.DMA((N_DMA_LANES,))],
        compiler_params=_cparams(("arbitrary",)),
        name="moe_dispatch",
    )(*slots, hq, init)


def _experts_body(plan_ref, xs_ref, wg_ref, wu_ref, wd_ref, ys_ref, wgu_ref, wdn_ref):
    t = pl.program_id(0)
    in_use = t < plan_ref[1, 0]
    new_expert = (t == 0) | (plan_ref[0, t] != plan_ref[0, jnp.maximum(t - 1, 0)])

    @pl.when(in_use & new_expert)
    def _():
        wgu_ref[:, :EXP_H] = wg_ref[0, 0].astype(BF16)
        wgu_ref[:, EXP_H:] = wu_ref[0, 0].astype(BF16)
        wdn_ref[...] = wd_ref[0, 0].astype(BF16)

    @pl.when(in_use)
    def _():
        x = _load_unpacked(xs_ref).astype(BF16)
        ab = jnp.dot(x, wgu_ref[...], preferred_element_type=F32)
        a, b = ab[:, :EXP_H], ab[:, EXP_H:]
        act = a * _sigmoid(a) * b
        _store_packed(ys_ref, jnp.dot(act.astype(BF16), wdn_ref[...], preferred_element_type=F32))


def _experts(plan, xs, w_gate, w_up, w_down, *, layer):
    def tile_idx(t, plan):
        return (jnp.minimum(t, jnp.maximum(plan[1, 0] - 1, 0)), 0, 0, 0)

    grid_spec = pltpu.PrefetchScalarGridSpec(
        num_scalar_prefetch=1,
        grid=(SORT_TILES,),
        in_specs=[
            pl.BlockSpec(_packed_shape(EXP_TILE), tile_idx),
            pl.BlockSpec((1, 1, D, EXP_H), lambda t, plan: (layer, plan[0, t], 0, 0)),
            pl.BlockSpec((1, 1, D, EXP_H), lambda t, plan: (layer, plan[0, t], 0, 0)),
            pl.BlockSpec((1, 1, EXP_H, D), lambda t, plan: (layer, plan[0, t], 0, 0)),
        ],
        out_specs=pl.BlockSpec(_packed_shape(EXP_TILE), tile_idx),
        scratch_shapes=[pltpu.VMEM((D, 2 * EXP_H), BF16), pltpu.VMEM((EXP_H, D), BF16)],
    )
    return pl.pallas_call(
        _experts_body,
        grid_spec=grid_spec,
        out_shape=jax.ShapeDtypeStruct(_packed_shape(SORT_ROWS), jnp.uint32),
        input_output_aliases={1: 0},
        compiler_params=_cparams(("arbitrary",)),
        name="moe_experts",
    )(plan, xs, w_gate, w_up, w_down)


def _combine_body(*refs, tm, final):
    cur_slots, nxt_slots = refs[0:4], refs[4:8]
    meta_ref, x_ref, ada_ref = refs[8:11]
    rest = refs[11:]
    if final:
        fgain_ref, ys_ref, oc_ref, ol_ref, y_ref, sem = rest
    else:
        ys_ref, o_ref, y_ref, sem = rest
    step = pl.program_id(0)
    n_steps = pl.num_programs(0)

    def fetch(slot_refs, buf):
        slots = ((slot_refs[0], slot_refs[1]), (slot_refs[2], slot_refs[3]))

        def copy(blk, j, k, lane):
            i = blk * ROW_UNROLL + j
            return pltpu.make_async_copy(_packed_row(ys_ref, slots[k][0][i], slots[k][1][i]),
                                         _packed_row(y_ref.at[buf, k], blk, j), sem.at[buf, lane])

        _start_row_copies(tm, copy)

    @pl.when(step == 0)
    def _():
        fetch(cur_slots, 0)

    @pl.when(step + 1 < n_steps)
    def _():
        fetch(nxt_slots, (step + 1) % 2)

    buf = step % 2
    _drain_row_copies(ys_ref.at[pl.ds(0, tm // 8)], y_ref.at[buf, 0], sem.at[buf])
    y_ref = y_ref.at[buf]
    meta = meta_ref[...]
    mix = (meta[:, META_W1:META_W1 + 1] * _load_unpacked(y_ref.at[0])
           + meta[:, META_W2:META_W2 + 1] * _load_unpacked(y_ref.at[1]))
    x = x_ref[...] + ada_ref[0, 5:6, :] * mix
    if not final:
        o_ref[...] = x
        return
    ms = jnp.mean(x * x, axis=-1, keepdims=True)
    y = x * lax.rsqrt(ms + EPS) * fgain_ref[...]
    is_ctx = pl.program_id(0) < N_CTX // tm

    @pl.when(is_ctx)
    def _():
        oc_ref[...] = y

    @pl.when(jnp.logical_not(is_ctx))
    def _():
        ol_ref[...] = y


def _combine(slots, ys, meta, x, ada, final_gain=None, *, tm=512):
    final = final_gain is not None
    extra_in = [pl.BlockSpec((1, D), lambda t: (0, 0))] if final else []
    extra_args = [final_gain.reshape(1, D)] if final else []
    if final:
        out_specs = _split_specs(tm, D)
        out_shape = [jax.ShapeDtypeStruct((N_CTX, D), F32), jax.ShapeDtypeStruct((N_LAT, D), F32)]
    else:
        out_specs = pl.BlockSpec((tm, D), lambda t: (t, 0))
        out_shape = jax.ShapeDtypeStruct((N_TOK, D), F32)
    return pl.pallas_call(
        functools.partial(_combine_body, tm=tm, final=final),
        grid=(N_TOK // tm,),
        in_specs=_slot_specs(tm) + _slot_specs(tm, ahead=1) + [
            pl.BlockSpec((tm, ROUTE_W), lambda t: (t, 0)),
            pl.BlockSpec((tm, D), lambda t: (t, 0)),
            pl.BlockSpec((1, 6, D), lambda t: (_cond_row(t, tm), 0, 0)),
        ] + extra_in + [pl.BlockSpec(memory_space=pl.ANY)],
        out_specs=out_specs,
        out_shape=out_shape,
        scratch_shapes=[pltpu.VMEM((2, 2) + _packed_shape(tm), jnp.uint32),
                        pltpu.SemaphoreType.DMA((2, N_DMA_LANES))],
        compiler_params=_cparams(("arbitrary",)),
        name="moe_combine",
    )(*slots, *slots, meta, x, ada, *extra_args, ys)


def _moe(routed, ada, w_gate, w_up, w_down, *, layer, sort_init, final_gain=None):
    x, hq, meta, cnt = routed
    pos, plan = _slots(meta, cnt)
    slots = [pos[:, r, :].reshape(N_TOK) for r in range(4)]
    xs = _dispatch(slots, hq, sort_init)
    ys = _experts(plan, xs, w_gate, w_up, w_down, layer=layer)
    return _combine(slots, ys, meta, x, ada, final_gain), ys


def kernel(x_prompt, x_sample, cache_k, cache_v, c, c_ctx, w_ada, b_ada, norm_mix, norm_ffn, attn_w_q, attn_w_kv, attn_q_norm, attn_k_norm, attn_w_o, hy_w_in, hy_b_in, hy_short_w, hy_short_b, hy_filt_w1, hy_filt_b1, hy_filt_w2, hy_filt_b2, hy_filt_w3, hy_filt_b3, hy_filt_freq, hy_log_decay, hy_skip, hy_w_out, hy_b_out, router_group_w, router_group_b, router_expert_w, router_expert_b, moe_w_gate, moe_w_up, moe_w_down, final_norm):
    depth = w_ada.shape[0]
    x_parts = (x_prompt.reshape(N_CTX, D), x_sample.reshape(N_LAT, D))
    cond8 = jnp.zeros((8, D), F32).at[0].set(c_ctx).at[1:1 + N_BATCH_LAT].set(c)
    ada_all = _ada(cond8, w_ada, b_ada).reshape(depth, 8, 6, D)
    new_k = []
    new_v = []
    sort_buf = jnp.zeros(_packed_shape(SORT_ROWS), jnp.uint32)
    for l in range(depth):
        ada = ada_all[l]
        if l % 2 == 0:
            a = l // 2
            w_qkv = jnp.concatenate([attn_w_q[a], attn_w_kv[a]], axis=1).astype(BF16)
            q, k_ctx, v_ctx, kc, vc, kl, vl = _qkv(x_parts, ada, norm_mix[l], w_qkv, attn_q_norm[a], attn_k_norm[a])
            new_k.append(k_ctx)
            new_v.append(v_ctx)
            o_ctx = _attention(q, [kc], [vc], row0=0, n_batch=N_BATCH_CTX, seq=L_CTX, tq=L_CTX,
                               n_kv=N_KV, stack=Q_PER_KV)
            past_keys = _lane_tile_heads(cache_k[:, a]).transpose(0, 1, 3, 2)
            o_lat = _attention(q, [kl, past_keys], [vl, _lane_tile_heads(cache_v[:, a])],
                               row0=N_CTX, n_batch=N_BATCH_LAT, seq=L_LAT, tq=1024, n_kv=1, stack=1)
            mixer = (o_ctx, o_lat, attn_w_o[a].astype(BF16), jnp.zeros((D,), F32))
        else:
            j = l // 2
            (x,) = x_parts
            up = _norm_proj(x, ada, norm_mix[l], hy_w_in[j].astype(BF16), hy_b_in[j], part=0)
            outs = []
            for row0, n_batch, L, tc in ((0, N_BATCH_CTX, L_CTX, D), (N_CTX, N_BATCH_LAT, L_LAT, 256)):
                cs = jnp.asarray(_dft_table(L)).astype(BF16)
                fr, fi, fn = _filter_spectra(L, cs, hy_filt_w1[j], hy_filt_b1[j], hy_filt_w2[j], hy_filt_b2[j],
                                             hy_filt_w3[j], hy_filt_b3[j], hy_filt_freq[j], hy_log_decay[j])
                outs.append(_hyena_conv(up, hy_short_w[j], hy_short_b[j], hy_skip[j], fr, fi, fn, cs,
                                        row0=row0, n_batch=n_batch, L=L, tc=tc))
            mixer = (outs[0], outs[1], hy_w_out[j].astype(BF16), hy_b_out[j])
        routed = _mix_route(*mixer, x_parts, ada, norm_ffn[l], router_group_w[l], router_group_b[l],
                            router_expert_w[l], router_expert_b[l])
        out, sort_buf = _moe(routed, ada, moe_w_gate, moe_w_up, moe_w_down, layer=l, sort_init=sort_buf,
                             final_gain=final_norm if l == depth - 1 else None)
        x_parts = (out,)
    y_ctx, y_lat = x_parts[0]
    return (y_ctx.reshape(N_BATCH_CTX, L_CTX, D), y_lat.reshape(N_BATCH_LAT, L_LAT, D),
            jnp.stack(new_k, axis=1), jnp.stack(new_v, axis=1))
```

```python
import functools
import math

import numpy as np
import jax
import jax.numpy as jnp
from jax import lax
from jax.experimental import pallas as pl
from jax.experimental.pallas import tpu as pltpu

F32 = jnp.float32
BF16 = jnp.bfloat16

D = 1024
N_BATCH_CTX = 32
L_CTX = 256
N_BATCH_LAT = 2
L_LAT = 2048
PAST = 256
N_CTX = N_BATCH_CTX * L_CTX
N_LAT = N_BATCH_LAT * L_LAT
N_TOK = N_CTX + N_LAT
GRID_W = 64
N_HEADS = 16
N_KV = 4
DH = 64
Q_PER_KV = N_HEADS // N_KV
KV_W = N_KV * DH
ROPE_THETA = 10000.0
FILTER_FEAT = 17
FEAT_PAD = 32
FILTER_HIDDEN = 64
N_GROUPS = 4
E_PER_G = 8
N_EXP = N_GROUPS * E_PER_G
EXP_H = D // 4
EPS = 1e-6
ROUTE_W = 128
VMEM_LIMIT = 56 * 1024 * 1024


def _cparams(sem):
    return pltpu.CompilerParams(dimension_semantics=sem, vmem_limit_bytes=VMEM_LIMIT)


def _cond_row(i, tm):
    n_ctx_tiles = N_CTX // tm
    return jnp.where(i < n_ctx_tiles, 0, 1 + (i - n_ctx_tiles) // (L_LAT // tm))


def _split_specs(tm, width):
    n_ctx_tiles = N_CTX // tm
    return [pl.BlockSpec((tm, width), lambda i: (jnp.minimum(i, n_ctx_tiles - 1), 0)),
            pl.BlockSpec((tm, width), lambda i: (jnp.maximum(i - n_ctx_tiles, 0), 0))]


def _pick_split(i, tm, ctx_ref, lat_ref):
    return jnp.where(i < N_CTX // tm, ctx_ref[...], lat_ref[...])


def _sigmoid(x):
    return 1.0 / (1.0 + jnp.exp(-x))


@functools.lru_cache(maxsize=None)
def _dft_table(L):
    k = np.arange(L, dtype=np.int64)
    ang = (np.outer(k, k) % (2 * L)).astype(np.float64) * (math.pi / L)
    return np.concatenate([np.cos(ang), np.sin(ang)], axis=0).astype(np.float32)


@functools.lru_cache(maxsize=None)
def _filter_feats(L):
    t = np.linspace(0.0, 1.0, L, dtype=np.float64)[:, None]
    bands = np.linspace(1e-4, 7.0, 8, dtype=np.float64)[None, :]
    w = (2.0 * math.pi) * np.arange(L, dtype=np.float64)[:, None] / L
    feats = np.concatenate([t, np.cos(bands * w), -np.sin(bands * w)], axis=-1)
    out = np.zeros((L, FEAT_PAD), np.float32)
    out[:, :FILTER_FEAT] = feats
    return out


@functools.lru_cache(maxsize=None)
def _rope_tables(tm):
    pos = np.arange(L_LAT)
    row = (pos // GRID_W).astype(np.float64)
    col = (pos % GRID_W).astype(np.float64)
    axis_dim = DH // 2
    inv_freq = ROPE_THETA ** (-np.arange(0, axis_dim, 2, dtype=np.float64) / axis_dim)
    lane = np.arange(KV_W)
    d = lane % DH
    is_col = (d // axis_dim) == 1
    fi = d % (axis_dim // 2)
    first_half = (d % axis_dim) < (axis_dim // 2)
    p = np.where(is_col[None, :], col[:, None], row[:, None])
    ang = p * inv_freq[fi][None, :]
    cos = np.cos(ang)
    sin = np.sin(ang) * np.where(first_half, -1.0, 1.0)[None, :]
    cos = np.concatenate([np.ones((tm, KV_W)), cos], axis=0).astype(np.float32)
    sin = np.concatenate([np.zeros((tm, KV_W)), sin], axis=0).astype(np.float32)
    return cos, sin


@functools.lru_cache(maxsize=None)
def _head_sum_matrix():
    lane = np.arange(KV_W)
    return (lane[:, None] // DH == lane[None, :] // DH).astype(np.float32)


def _ada_body(c_ref, w_ref, b_ref, o_ref):
    c = c_ref[...]
    s = c * _sigmoid(c)
    o_ref[0] = jnp.dot(s.astype(BF16), w_ref[0].astype(BF16), preferred_element_type=F32) + b_ref[0]


def _ada(cond8, w_ada, b_ada):
    depth = w_ada.shape[0]
    tn = 1536
    return pl.pallas_call(
        _ada_body,
        grid=(depth, 6 * D // tn),
        in_specs=[
            pl.BlockSpec((8, D), lambda l, j: (0, 0)),
            pl.BlockSpec((1, D, tn), lambda l, j: (l, 0, j)),
            pl.BlockSpec((1, 1, tn), lambda l, j: (l, 0, j)),
        ],
        out_specs=pl.BlockSpec((1, 8, tn), lambda l, j: (l, 0, j)),
        out_shape=jax.ShapeDtypeStruct((depth, 8, 6 * D), F32),
        compiler_params=_cparams(("arbitrary", "arbitrary")),
        name="ada",
    )(cond8, w_ada, b_ada.reshape(depth, 1, 6 * D))


def _norm_mod(x, gain_ref, ada_ref, part):
    ms = jnp.mean(x * x, axis=-1, keepdims=True)
    y = x * lax.rsqrt(ms + EPS) * gain_ref[...]
    return y * (1.0 + ada_ref[0, part + 1:part + 2, :]) + ada_ref[0, part:part + 1, :]


def _proj_body(x_ref, ada_ref, gain_ref, w_ref, b_ref, o_ref, *, part):
    h = _norm_mod(x_ref[...], gain_ref, ada_ref, part)
    acc = jnp.dot(h.astype(BF16), w_ref[...], preferred_element_type=F32) + b_ref[...]
    o_ref[...] = acc.astype(o_ref.dtype)


def _norm_proj(x, ada, gain, w, b, *, part, tm=1024):
    n_out = w.shape[1]
    return pl.pallas_call(
        functools.partial(_proj_body, part=part),
        grid=(N_TOK // tm,),
        in_specs=[
            pl.BlockSpec((tm, D), lambda i: (i, 0)),
            pl.BlockSpec((1, 6, D), lambda i: (_cond_row(i, tm), 0, 0)),
            pl.BlockSpec((1, D), lambda i: (0, 0)),
            pl.BlockSpec((D, n_out), lambda i: (0, 0)),
            pl.BlockSpec((1, n_out), lambda i: (0, 0)),
        ],
        out_specs=pl.BlockSpec((tm, n_out), lambda i: (i, 0)),
        out_shape=jax.ShapeDtypeStruct((N_TOK, n_out), BF16),
        compiler_params=_cparams(("arbitrary",)),
        name="norm_proj",
    )(x, ada, gain.reshape(1, D), w, b.reshape(1, n_out))


def _head_rms_rope(x, gain, hs, cos, sin, lane):
    ss = jnp.dot((x * x).astype(BF16), hs, preferred_element_type=F32)
    xn = x * lax.rsqrt(ss * (1.0 / DH) + EPS) * gain
    fwd = pltpu.roll(xn, KV_W - DH // 4, 1)
    bwd = pltpu.roll(xn, DH // 4, 1)
    partner = jnp.where((lane & (DH // 4)) == 0, fwd, bwd)
    return xn * cos + partner * sin


@functools.lru_cache(maxsize=None)
def _lane_tile_matrix():
    col = np.arange(N_KV * KV_W)
    src = (col // KV_W) * DH + col % DH
    return (np.arange(KV_W)[:, None] == src[None, :]).astype(np.float32)


def _qkv_body(*refs, tm, split_x):
    n_x = 2 if split_x else 1
    x_refs = refs[:n_x]
    (ada_ref, gain_ref, w_ref, qg_ref, kg_ref, hs_ref, cos_ref, sin_ref, tile_ref, tile_t_ref,
     q_ref, newk_ref, newv_ref, kc_ref, vc_ref, kl_ref, vl_ref) = refs[n_x:]
    step = pl.program_id(0)
    x = _pick_split(step, tm, *x_refs) if split_x else x_refs[0][...]
    h = _norm_mod(x, gain_ref, ada_ref, 0)
    acc = jnp.dot(h.astype(BF16), w_ref[...], preferred_element_type=F32)
    hs = hs_ref[...]
    cos = cos_ref[...]
    sin = sin_ref[...]
    lane = lax.broadcasted_iota(jnp.int32, (1, KV_W), 1)
    for c in range(N_KV):
        qc = _head_rms_rope(acc[:, c * KV_W:(c + 1) * KV_W], qg_ref[...], hs, cos, sin, lane)
        q_ref[:, c * KV_W:(c + 1) * KV_W] = (qc * (DH ** -0.5)).astype(BF16)
    k = _head_rms_rope(acc[:, D:D + KV_W], kg_ref[...], hs, cos, sin, lane)
    v = acc[:, D + KV_W:D + 2 * KV_W]
    k4t = lax.dot_general(tile_t_ref[...], k.astype(BF16), (((1,), (1,)), ((), ())),
                          preferred_element_type=F32).astype(BF16)
    v4 = jnp.dot(v.astype(BF16), tile_ref[...], preferred_element_type=F32).astype(BF16)
    is_ctx = step < N_CTX // tm

    @pl.when(is_ctx)
    def _():
        for bb in range(tm // L_CTX):
            rows = slice(bb * L_CTX, (bb + 1) * L_CTX)
            for hd in range(N_KV):
                newk_ref[bb, hd] = k[rows, hd * DH:(hd + 1) * DH]
                newv_ref[bb, hd] = v[rows, hd * DH:(hd + 1) * DH]
                kc_ref[bb, hd] = k4t[hd * KV_W:(hd + 1) * KV_W, rows]
                vc_ref[bb, hd] = v4[rows, hd * KV_W:(hd + 1) * KV_W]

    @pl.when(jnp.logical_not(is_ctx))
    def _():
        for hd in range(N_KV):
            kl_ref[0, hd] = k4t[hd * KV_W:(hd + 1) * KV_W, :]
            vl_ref[0, hd] = v4[:, hd * KV_W:(hd + 1) * KV_W]


def _qkv(x_parts, ada, gain, w_qkv, q_gain, k_gain):
    tm = 2 * L_CTX
    per_tile = tm // L_CTX
    cos, sin = _rope_tables(tm)
    split_x = len(x_parts) == 2
    x_specs = _split_specs(tm, D) if split_x else [pl.BlockSpec((tm, D), lambda i: (i, 0))]
    n_ctx_tiles = N_CTX // tm
    lat_tiles = L_LAT // tm

    def rope_idx(i):
        return (jnp.where(i < n_ctx_tiles, 0, 1 + (i - n_ctx_tiles) % lat_tiles), 0)

    def ctx_idx(i):
        return (jnp.minimum(i, n_ctx_tiles - 1), 0, 0, 0)

    def lat_idx(i):
        j = jnp.maximum(i - n_ctx_tiles, 0)
        return (j // lat_tiles, 0, j % lat_tiles, 0)

    def lat_idx_t(i):
        j = jnp.maximum(i - n_ctx_tiles, 0)
        return (j // lat_tiles, 0, 0, j % lat_tiles)

    const = lambda i: (0, 0)
    kv_ctx = jax.ShapeDtypeStruct((N_BATCH_CTX, N_KV, L_CTX, DH), F32)
    tiled_ctx = jax.ShapeDtypeStruct((N_BATCH_CTX, N_KV, L_CTX, KV_W), BF16)
    tiled_lat = jax.ShapeDtypeStruct((N_BATCH_LAT, N_KV, L_LAT, KV_W), BF16)
    keys_ctx = jax.ShapeDtypeStruct((N_BATCH_CTX, N_KV, KV_W, L_CTX), BF16)
    keys_lat = jax.ShapeDtypeStruct((N_BATCH_LAT, N_KV, KV_W, L_LAT), BF16)
    return pl.pallas_call(
        functools.partial(_qkv_body, tm=tm, split_x=split_x),
        grid=(N_TOK // tm,),
        in_specs=x_specs + [
            pl.BlockSpec((1, 6, D), lambda i: (_cond_row(i, tm), 0, 0)),
            pl.BlockSpec((1, D), const),
            pl.BlockSpec((D, D + 2 * KV_W), const),
            pl.BlockSpec((1, KV_W), const),
            pl.BlockSpec((1, KV_W), const),
            pl.BlockSpec((KV_W, KV_W), const),
            pl.BlockSpec((tm, KV_W), rope_idx),
            pl.BlockSpec((tm, KV_W), rope_idx),
            pl.BlockSpec((KV_W, N_KV * KV_W), const),
            pl.BlockSpec((N_KV * KV_W, KV_W), const),
        ],
        out_specs=[
            pl.BlockSpec((tm, D), lambda i: (i, 0)),
            pl.BlockSpec((per_tile, N_KV, L_CTX, DH), ctx_idx),
            pl.BlockSpec((per_tile, N_KV, L_CTX, DH), ctx_idx),
            pl.BlockSpec((per_tile, N_KV, KV_W, L_CTX), ctx_idx),
            pl.BlockSpec((per_tile, N_KV, L_CTX, KV_W), ctx_idx),
            pl.BlockSpec((1, N_KV, KV_W, tm), lat_idx_t),
            pl.BlockSpec((1, N_KV, tm, KV_W), lat_idx),
        ],
        out_shape=[jax.ShapeDtypeStruct((N_TOK, D), BF16), kv_ctx, kv_ctx, keys_ctx, tiled_ctx, keys_lat, tiled_lat],
        compiler_params=_cparams(("arbitrary",)),
        name="qkv",
    )(*x_parts, ada, gain.reshape(1, D), w_qkv,
      jnp.tile(q_gain, Q_PER_KV).reshape(1, KV_W), jnp.tile(k_gain, N_KV).reshape(1, KV_W),
      jnp.asarray(_head_sum_matrix()).astype(BF16), jnp.asarray(cos), jnp.asarray(sin),
      jnp.asarray(_lane_tile_matrix()).astype(BF16), jnp.asarray(_lane_tile_matrix().T.copy()).astype(BF16))


def _attn_body(*refs, n_kv, stack, n_seg):
    q_ref, k_refs, v_refs, o_ref = refs[0], refs[1:1 + n_seg], refs[1 + n_seg:1 + 2 * n_seg], refs[-1]
    lane = lax.broadcasted_iota(jnp.int32, (1, KV_W), 1)
    masks = [(lane >> 6) == g for g in range(Q_PER_KV)]
    tq = q_ref.shape[0]
    for kv in range(n_kv):
        q = q_ref[:, kv * KV_W:(kv + 1) * KV_W]
        out = jnp.zeros((tq, KV_W), F32)
        for c in range(0, Q_PER_KV, stack):
            pair = masks[c:c + stack]
            ones_blk = (c + stack) % Q_PER_KV if stack < Q_PER_KV else None
            stacked = jnp.concatenate([jnp.where(m, q, jnp.zeros_like(q)) for m in pair], axis=0)
            scores = [jnp.dot(stacked, k_ref[0, kv], preferred_element_type=F32) for k_ref in k_refs]
            top = functools.reduce(jnp.maximum, [jnp.max(s, axis=-1, keepdims=True) for s in scores])
            probs = [jnp.exp((s - top).astype(BF16)) for s in scores]
            if ones_blk is None:
                vals = [v_ref[0, kv] for v_ref in v_refs]
            else:
                vals = [jnp.where(masks[ones_blk], jnp.ones((), BF16), v_ref[0, kv]) for v_ref in v_refs]
            og = sum(jnp.dot(p, v, preferred_element_type=F32) for p, v in zip(probs, vals))
            if ones_blk is None:
                denom = sum(jnp.sum(p.astype(F32), axis=-1, keepdims=True) for p in probs)
            else:
                denom = og[:, ones_blk * DH:ones_blk * DH + 1]
            og = og * (1.0 / denom)
            for g, m in enumerate(pair):
                out = out + jnp.where(m, og[g * tq:(g + 1) * tq], 0.0)
        o_ref[:, kv * KV_W:(kv + 1) * KV_W] = out.astype(BF16)


def _attention(q, keys, values, *, row0, n_batch, seq, tq, n_kv, stack):
    per_b = seq // tq
    base = row0 // tq
    kv_specs = [pl.BlockSpec((1, n_kv) + a.shape[2:], lambda b, h, i: (b, h, 0, 0)) for a in keys + values]
    return pl.pallas_call(
        functools.partial(_attn_body, n_kv=n_kv, stack=stack, n_seg=len(keys)),
        grid=(n_batch, N_KV // n_kv, per_b),
        in_specs=[pl.BlockSpec((tq, n_kv * KV_W), lambda b, h, i: (base + b * per_b + i, h))] + kv_specs,
        out_specs=pl.BlockSpec((tq, n_kv * KV_W), lambda b, h, i: (b * per_b + i, h)),
        out_shape=jax.ShapeDtypeStruct((n_batch * seq, D), BF16),
        compiler_params=_cparams(("arbitrary", "arbitrary", "arbitrary")),
        name="attention",
    )(q, *keys, *values)


def _lane_tile_heads(x):
    return jnp.tile(x.astype(BF16), (1, 1, 1, KV_W // DH))


def _alt_sign(L):
    row = lax.broadcasted_iota(jnp.int32, (L, 1), 0)
    return row, jnp.where((row & 1) == 0, 1.0, -1.0)


def _split_dot(a, b):
    a_hi, b_hi = a.astype(BF16), b.astype(BF16)
    a_lo = (a - a_hi.astype(F32)).astype(BF16)
    b_lo = (b - b_hi.astype(F32)).astype(BF16)
    return (jnp.dot(a_hi, b_hi, preferred_element_type=F32) + jnp.dot(a_lo, b_hi, preferred_element_type=F32)
            + jnp.dot(a_hi, b_lo, preferred_element_type=F32))


def _filter_body(feat_ref, w1_ref, b1_ref, w2_ref, b2_ref, fq_ref, w3_ref, b3_ref, ld_ref, cs_ref,
                 fr_ref, fi_ref, fn_ref, hid_ref, *, L):
    feats = feat_ref[...]

    @pl.when(pl.program_id(0) == 0)
    def _():
        fq = fq_ref[...]
        h1 = jnp.sin(fq * (_split_dot(feats, w1_ref[...]) + b1_ref[...]))
        hid_ref[...] = jnp.sin(fq * (_split_dot(h1, w2_ref[...]) + b2_ref[...])).astype(BF16)

    t = feats[:, 0:1]
    row, alt = _alt_sign(L)
    filt = []
    for j in range(4):
        raw = jnp.dot(hid_ref[...], w3_ref[j].astype(BF16), preferred_element_type=F32) + b3_ref[j:j + 1, :]
        filt.append(raw * jnp.exp(-t * jnp.exp(ld_ref[j:j + 1, :])))
    for o in range(2):
        hf, hb = filt[2 * o], filt[2 * o + 1]
        l1 = jnp.sum(jnp.abs(hf), axis=0, keepdims=True) + jnp.sum(jnp.abs(hb), axis=0, keepdims=True)
        inv = 1.0 / (l1 + EPS)
        sym = (hf + hb) * inv
        asym = (hb - hf) * inv
        fr = jnp.dot(cs_ref[0:L, :], sym.astype(BF16), preferred_element_type=F32) * (1.0 / L)
        fr_ref[o] = jnp.where(row == 0, 0.5 * fr, fr)
        fi_ref[o] = jnp.dot(cs_ref[L:2 * L, :], asym.astype(BF16), preferred_element_type=F32) * (1.0 / L)
        fn_ref[o] = jnp.sum(sym * alt, axis=0, keepdims=True) * (0.5 / L)


def _filter_spectra(L, cs, w1, b1, w2, b2, w3, b3, freq, log_decay, *, tc=256):
    w1p = jnp.zeros((FEAT_PAD, FILTER_HIDDEN), F32).at[:FILTER_FEAT].set(w1)
    w3r = w3.reshape(FILTER_HIDDEN, 4, D).transpose(1, 0, 2)
    const = lambda j: (0, 0)
    return pl.pallas_call(
        functools.partial(_filter_body, L=L),
        grid=(D // tc,),
        in_specs=[
            pl.BlockSpec((L, FEAT_PAD), const),
            pl.BlockSpec((FEAT_PAD, FILTER_HIDDEN), const),
            pl.BlockSpec((1, FILTER_HIDDEN), const),
            pl.BlockSpec((FILTER_HIDDEN, FILTER_HIDDEN), const),
            pl.BlockSpec((1, FILTER_HIDDEN), const),
            pl.BlockSpec((1, FILTER_HIDDEN), const),
            pl.BlockSpec((4, FILTER_HIDDEN, tc), lambda j: (0, 0, j)),
            pl.BlockSpec((4, tc), lambda j: (0, j)),
            pl.BlockSpec((4, tc), lambda j: (0, j)),
            pl.BlockSpec((2 * L, L), const, pipeline_mode=pl.Buffered(1)),
        ],
        out_specs=[
            pl.BlockSpec((2, L, tc), lambda j: (0, 0, j)),
            pl.BlockSpec((2, L, tc), lambda j: (0, 0, j)),
            pl.BlockSpec((2, 1, tc), lambda j: (0, 0, j)),
        ],
        out_shape=[
            jax.ShapeDtypeStruct((2, L, D), F32),
            jax.ShapeDtypeStruct((2, L, D), F32),
            jax.ShapeDtypeStruct((2, 1, D), F32),
        ],
        scratch_shapes=[pltpu.VMEM((L, FILTER_HIDDEN), BF16)],
        compiler_params=_cparams(("arbitrary",)),
        name="hyena_filter",
    )(jnp.asarray(_filter_feats(L)), w1p, b1.reshape(1, -1), w2, b2.reshape(1, -1), freq.reshape(1, -1),
      w3r, b3.reshape(4, D), log_decay.reshape(4, D), cs)


FREQ_CHUNK = 1024


def _hyconv_body(x1_ref, x2_ref, v_ref, sw_ref, sb_ref, skip_ref, fr_ref, fi_ref, fn_ref, cs_ref, o_ref,
                 z_ref, g_ref, zb_ref, yr_ref, yi_ref, *, L):
    row, alt = _alt_sign(L)
    kc = min(FREQ_CHUNK, L)
    chunk_row = lax.broadcasted_iota(jnp.int32, (kc, 1), 0)

    def short_conv(u_ref, p):
        u = u_ref[...].astype(F32)
        prev = jnp.where(row == 0, 0.0, pltpu.roll(u, 1, 0))
        nxt = jnp.where(row == L - 1, 0.0, pltpu.roll(u, L - 1, 0))
        return (prev * sw_ref[0, p:p + 1, :] + u * sw_ref[1, p:p + 1, :] + nxt * sw_ref[2, p:p + 1, :]
                + sb_ref[p:p + 1, :])

    z_ref[...] = short_conv(v_ref, 2)
    g_ref[0] = short_conv(x1_ref, 0)
    g_ref[1] = short_conv(x2_ref, 1)
    for o in range(2):
        z = z_ref[...]
        zb_ref[...] = z.astype(BF16)
        nyq = jnp.sum(z * alt, axis=0, keepdims=True) * fn_ref[o]

        def to_freq(c, carry):
            lo = pl.multiple_of(c * kc, kc)
            zr = jnp.dot(cs_ref[pl.ds(lo, kc), :], zb_ref[...], preferred_element_type=F32)
            zs = jnp.dot(cs_ref[pl.ds(L + lo, kc), :], zb_ref[...], preferred_element_type=F32)
            fr = fr_ref[o, pl.ds(lo, kc), :]
            fi = fi_ref[o, pl.ds(lo, kc), :]
            yr_ref[pl.ds(lo, kc), :] = (zr * fr + zs * fi).astype(BF16)
            yi_ref[pl.ds(lo, kc), :] = (zr * fi - zs * fr).astype(BF16)
            return carry

        lax.fori_loop(0, L // kc, to_freq, 0, unroll=True)

        def to_time(c, carry):
            lo = pl.multiple_of(c * kc, kc)
            y = (jnp.dot(cs_ref[pl.ds(lo, kc), :], yr_ref[...], preferred_element_type=F32)
                 - jnp.dot(cs_ref[pl.ds(L + lo, kc), :], yi_ref[...], preferred_element_type=F32))
            y = y + jnp.where(((chunk_row + lo) & 1) == 0, nyq, -nyq)
            z_ref[pl.ds(lo, kc), :] = g_ref[o, pl.ds(lo, kc), :] * (y + skip_ref[o:o + 1, :] * z_ref[pl.ds(lo, kc), :])
            return carry

        lax.fori_loop(0, L // kc, to_time, 0, unroll=True)
    o_ref[...] = z_ref[...].astype(BF16)


def _hyena_conv(up, sw, sb, skip, fr, fi, fn, cs, *, row0, n_batch, L, tc):
    n_ct = D // tc
    base = row0 // L
    u_spec = lambda p: pl.BlockSpec((L, tc), lambda j, b: (base + b, p * n_ct + j))
    once = pl.Buffered(1)
    return pl.pallas_call(
        functools.partial(_hyconv_body, L=L),
        grid=(n_ct, n_batch),
        in_specs=[
            u_spec(0), u_spec(1), u_spec(2),
            pl.BlockSpec((3, 3, tc), lambda j, b: (0, 0, j)),
            pl.BlockSpec((3, tc), lambda j, b: (0, j)),
            pl.BlockSpec((2, tc), lambda j, b: (0, j)),
            pl.BlockSpec((2, L, tc), lambda j, b: (0, 0, j), pipeline_mode=once),
            pl.BlockSpec((2, L, tc), lambda j, b: (0, 0, j), pipeline_mode=once),
            pl.BlockSpec((2, 1, tc), lambda j, b: (0, 0, j)),
            pl.BlockSpec((2 * L, L), lambda j, b: (0, 0), pipeline_mode=once),
        ],
        out_specs=pl.BlockSpec((L, tc), lambda j, b: (b, j)),
        out_shape=jax.ShapeDtypeStruct((n_batch * L, D), BF16),
        scratch_shapes=[
            pltpu.VMEM((L, tc), F32),
            pltpu.VMEM((2, L, tc), F32),
            pltpu.VMEM((L, tc), BF16),
            pltpu.VMEM((L, tc), BF16),
            pltpu.VMEM((L, tc), BF16),
        ],
        compiler_params=_cparams(("arbitrary", "arbitrary")),
        name="hyena_conv",
    )(up, up, up, sw.reshape(3, 3, D), sb.reshape(3, D), skip, fr, fi, fn, cs)


EXP_TILE = 512
SORT_TILES = (2 * N_TOK) // EXP_TILE + N_EXP
SORT_ROWS = SORT_TILES * EXP_TILE
N_QUARTERS = 4
QUARTER_W = D // (2 * N_QUARTERS)
META_I1, META_I2, META_R1, META_R2, META_W1, META_W2 = range(6)
HI_HALF = 0xFFFF0000


def _pack_pairs(x):
    bits = pltpu.bitcast(x.astype(BF16).astype(F32), jnp.uint32)
    return (bits[:, :QUARTER_W] >> 16) | bits[:, QUARTER_W:]


def _unpack_pairs(w):
    return pltpu.bitcast(w << 16, F32), pltpu.bitcast(w & jnp.uint32(HI_HALF), F32)


def _packed_shape(rows):
    return (rows // 8, N_QUARTERS, 8, QUARTER_W)


def _store_packed(ref, x):
    for q in range(N_QUARTERS):
        ref[:, q] = _pack_pairs(x[:, q * 2 * QUARTER_W:(q + 1) * 2 * QUARTER_W]).reshape(-1, 8, QUARTER_W)


def _load_unpacked(ref):
    halves = []
    for q in range(N_QUARTERS):
        halves.extend(_unpack_pairs(ref[:, q].reshape(-1, QUARTER_W)))
    return jnp.concatenate(halves, axis=1)


def _packed_row(ref, tile, sublane):
    return ref.at[tile, :, sublane, :]


def _lane_put(lane, values):
    out = jnp.where(lane == 0, values[0], 0.0)
    for k in range(1, len(values)):
        out = out + jnp.where(lane == k, values[k], 0.0)
    return out


def _route_body(*refs, tm, split_x):
    n_x = 2 if split_x else 1
    ac_ref, al_ref, w_ref, b_ref = refs[:4]
    x_refs = refs[4:4 + n_x]
    (ada_ref, gain_ref, wr_hi_ref, wr_lo_ref, br_ref, tri_ref,
     xo_ref, hq_ref, meta_ref, cnt_ref, carry_ref) = refs[4 + n_x:]
    step = pl.program_id(0)

    @pl.when(step == 0)
    def _():
        carry_ref[...] = jnp.zeros_like(carry_ref)

    mixed = jnp.dot(_pick_split(step, tm, ac_ref, al_ref), w_ref[...], preferred_element_type=F32) + b_ref[...]
    x = _pick_split(step, tm, *x_refs) if split_x else x_refs[0][...]
    x = x + ada_ref[0, 2:3, :] * mixed
    xo_ref[...] = x
    h = _norm_mod(x, gain_ref, ada_ref, 3)
    _store_packed(hq_ref, h)
    h_hi = h.astype(BF16)
    h_lo = (h - h_hi.astype(F32)).astype(BF16)
    logits = (jnp.dot(h_hi, wr_hi_ref[...], preferred_element_type=F32)
              + jnp.dot(h_lo, wr_hi_ref[...], preferred_element_type=F32)
              + jnp.dot(h_hi, wr_lo_ref[...], preferred_element_type=F32) + br_ref[...])
    lane = lax.broadcasted_iota(jnp.int32, (1, ROUTE_W), 1)
    lane_f = lane.astype(F32)
    group_of_lane = (lane >> 3).astype(F32)
    neg = -jnp.inf
    big = float(ROUTE_W)
    is_g = (lane >= N_EXP) & (lane < N_EXP + N_GROUPS)
    gl = jnp.where(is_g, logits, neg)
    gmax = jnp.max(gl, axis=-1, keepdims=True)
    gidx = jnp.min(jnp.where(gl == gmax, lane_f - N_EXP, big), axis=-1, keepdims=True)
    g_top = 1.0 / jnp.sum(jnp.where(is_g, jnp.exp(gl - gmax), 0.0), axis=-1, keepdims=True)
    in_group = (lane < N_EXP) & (group_of_lane == gidx)
    el = jnp.where(in_group, logits, neg)
    v1 = jnp.max(el, axis=-1, keepdims=True)
    i1 = jnp.min(jnp.where(el == v1, lane_f, big), axis=-1, keepdims=True)
    el2 = jnp.where(lane_f == i1, neg, el)
    v2 = jnp.max(el2, axis=-1, keepdims=True)
    i2 = jnp.min(jnp.where(el2 == v2, lane_f, big), axis=-1, keepdims=True)
    r = jnp.exp(v2 - v1)
    w1 = g_top / (1.0 + r)
    w2 = g_top * r / (1.0 + r)
    sel = jnp.where((lane_f == i1) | (lane_f == i2), 1.0, 0.0)
    rank = jnp.dot(tri_ref[...], sel.astype(BF16), preferred_element_type=F32) + carry_ref[...]
    r1 = jnp.sum(jnp.where(lane_f == i1, rank, 0.0), axis=-1, keepdims=True)
    r2 = jnp.sum(jnp.where(lane_f == i2, rank, 0.0), axis=-1, keepdims=True)
    carry_ref[...] += jnp.sum(sel, axis=0, keepdims=True)
    cnt_ref[...] = carry_ref[...]
    meta_ref[...] = _lane_put(lane, (i1, i2, r1, r2, w1, w2))


@functools.lru_cache(maxsize=None)
def _strict_lower(n):
    r = np.arange(n)
    return (r[None, :] < r[:, None]).astype(np.float32)


def _mix_route(a_ctx, a_lat, w_out, b_out, x_parts, ada, gain, wg, bg, we, be, *, tm=1024):
    k = a_ctx.shape[1]
    split_x = len(x_parts) == 2
    x_specs = _split_specs(tm, D) if split_x else [pl.BlockSpec((tm, D), lambda i: (i, 0))]
    wr = jnp.zeros((D, ROUTE_W), F32).at[:, :N_EXP].set(we.reshape(D, N_EXP)).at[:, N_EXP:N_EXP + N_GROUPS].set(wg)
    br = jnp.zeros((1, ROUTE_W), F32).at[0, :N_EXP].set(be.reshape(N_EXP)).at[0, N_EXP:N_EXP + N_GROUPS].set(bg)
    wr_hi = wr.astype(BF16)
    wr_lo = (wr - wr_hi.astype(F32)).astype(BF16)
    return pl.pallas_call(
        functools.partial(_route_body, tm=tm, split_x=split_x),
        grid=(N_TOK // tm,),
        in_specs=_split_specs(tm, k) + [
            pl.BlockSpec((k, D), lambda i: (0, 0)),
            pl.BlockSpec((1, D), lambda i: (0, 0)),
        ] + x_specs + [
            pl.BlockSpec((1, 6, D), lambda i: (_cond_row(i, tm), 0, 0)),
            pl.BlockSpec((1, D), lambda i: (0, 0)),
            pl.BlockSpec((D, ROUTE_W), lambda i: (0, 0)),
            pl.BlockSpec((D, ROUTE_W), lambda i: (0, 0)),
            pl.BlockSpec((1, ROUTE_W), lambda i: (0, 0)),
            pl.BlockSpec((tm, tm), lambda i: (0, 0)),
        ],
        out_specs=[
            pl.BlockSpec((tm, D), lambda i: (i, 0)),
            pl.BlockSpec(_packed_shape(tm), lambda i: (i, 0, 0, 0)),
            pl.BlockSpec((tm, ROUTE_W), lambda i: (i, 0)),
            pl.BlockSpec((1, ROUTE_W), lambda i: (0, 0)),
        ],
        out_shape=[
            jax.ShapeDtypeStruct((N_TOK, D), F32),
            jax.ShapeDtypeStruct(_packed_shape(N_TOK), jnp.uint32),
            jax.ShapeDtypeStruct((N_TOK, ROUTE_W), F32),
            jax.ShapeDtypeStruct((1, ROUTE_W), F32),
        ],
        scratch_shapes=[pltpu.VMEM((1, ROUTE_W), F32)],
        compiler_params=_cparams(("arbitrary",)),
        name="mix_route",
    )(a_ctx, a_lat, w_out, b_out.reshape(1, D), *x_parts, ada, gain.reshape(1, D), wr_hi, wr_lo, br,
      jnp.asarray(_strict_lower(tm)).astype(BF16))


def _slots_body(meta_ref, cnt_ref, upper_ref, pos_ref, plan_ref):
    lane = lax.broadcasted_iota(jnp.int32, (1, ROUTE_W), 1)
    lane_f = lane.astype(F32)
    tiles = jnp.floor((cnt_ref[...] + (EXP_TILE - 1.0)) * (1.0 / EXP_TILE))
    end_tile = jnp.dot(jnp.broadcast_to(tiles, (8, ROUTE_W)).astype(BF16), upper_ref[...],
                       preferred_element_type=F32)[0:1]
    start_row = (end_tile - tiles) * EXP_TILE
    meta = meta_ref[...]
    i1 = meta[:, META_I1:META_I1 + 1]
    i2 = meta[:, META_I2:META_I2 + 1]
    p1 = jnp.sum(jnp.where(lane_f == i1, start_row, 0.0), axis=-1, keepdims=True) + meta[:, META_R1:META_R1 + 1]
    p2 = jnp.sum(jnp.where(lane_f == i2, start_row, 0.0), axis=-1, keepdims=True) + meta[:, META_R2:META_R2 + 1]
    hi1, hi2 = jnp.floor(p1 * 0.125), jnp.floor(p2 * 0.125)
    pos_ref[0] = _lane_put(lane, (hi1, p1 - 8.0 * hi1, hi2, p2 - 8.0 * hi2)).T[:8, :].astype(jnp.int32)

    @pl.when(pl.program_id(0) == 0)
    def _():
        end_col = jnp.broadcast_to(end_tile, (ROUTE_W, ROUTE_W)).T
        expert = lax.broadcasted_iota(jnp.int32, (ROUTE_W, ROUTE_W), 0)
        tile = lax.broadcasted_iota(jnp.int32, (ROUTE_W, ROUTE_W), 1).astype(F32)
        passed = jnp.where((end_col <= tile) & (expert < N_EXP), 1.0, 0.0)
        tile_expert = jnp.minimum(jnp.sum(passed, axis=0, keepdims=True), N_EXP - 1.0)
        used = jnp.max(end_tile, axis=-1, keepdims=True)
        row = lax.broadcasted_iota(jnp.int32, (8, ROUTE_W), 0)
        plan_ref[...] = jnp.where(row == 0, tile_expert, jnp.where(row == 1, used, 0.0)).astype(jnp.int32)


@functools.lru_cache(maxsize=None)
def _upper_incl(n):
    r = np.arange(n)
    return (r[:, None] <= r[None, :]).astype(np.float32)


def _slots(meta, cnt, *, tm=2048):
    n_tiles = N_TOK // tm
    return pl.pallas_call(
        _slots_body,
        grid=(n_tiles,),
        in_specs=[
            pl.BlockSpec((tm, ROUTE_W), lambda i: (i, 0)),
            pl.BlockSpec((1, ROUTE_W), lambda i: (0, 0)),
            pl.BlockSpec((ROUTE_W, ROUTE_W), lambda i: (0, 0)),
        ],
        out_specs=[
            pl.BlockSpec((1, 8, tm), lambda i: (i, 0, 0)),
            pl.BlockSpec((8, ROUTE_W), lambda i: (0, 0)),
        ],
        out_shape=[
            jax.ShapeDtypeStruct((n_tiles, 8, tm), jnp.int32),
            jax.ShapeDtypeStruct((8, ROUTE_W), jnp.int32),
        ],
        compiler_params=_cparams(("arbitrary",)),
        name="moe_slots",
    )(meta, cnt, jnp.asarray(_upper_incl(ROUTE_W)).astype(BF16))


ROW_UNROLL = 8
N_DMA_LANES = 2


def _start_row_copies(tm, make_copy):
    def block(blk, carry):
        for j in range(ROW_UNROLL):
            lane = j % N_DMA_LANES
            for k in range(2):
                make_copy(blk, j, k, lane).start(priority=lane)
        return carry

    lax.fori_loop(0, tm // ROW_UNROLL, block, 0)


def _drain_row_copies(like_src, like_dst, sem):
    for lane in range(N_DMA_LANES):
        pltpu.make_async_copy(like_src, like_dst, sem.at[lane]).wait()


def _dispatch_body(t1_ref, s1_ref, t2_ref, s2_ref, hq_ref, xs_in_ref, xs_ref, sem, *, tm):
    del xs_in_ref
    slots = ((t1_ref, s1_ref), (t2_ref, s2_ref))

    def copy(blk, j, k, lane):
        i = blk * ROW_UNROLL + j
        return pltpu.make_async_copy(_packed_row(hq_ref, blk, j), _packed_row(xs_ref, slots[k][0][i], slots[k][1][i]),
                                     sem.at[lane])

    _start_row_copies(tm, copy)
    _drain_row_copies(hq_ref, xs_ref.at[pl.ds(0, tm // 8)], sem)


def _slot_specs(tm, ahead=0):
    last = N_TOK // tm - 1
    return [pl.BlockSpec((tm,), lambda t: (jnp.minimum(t + ahead, last),), memory_space=pltpu.SMEM) for _ in range(4)]


def _dispatch(slots, hq, init, *, tm=1024):
    return pl.pallas_call(
        functools.partial(_dispatch_body, tm=tm),
        grid=(N_TOK // tm,),
        in_specs=_slot_specs(tm) + [
            pl.BlockSpec(_packed_shape(tm), lambda t: (t, 0, 0, 0)),
            pl.BlockSpec(memory_space=pl.ANY),
        ],
        out_specs=pl.BlockSpec(memory_space=pl.ANY),
        out_shape=jax.ShapeDtypeStruct(_packed_shape(SORT_ROWS), jnp.uint32),
        input_output_aliases={5: 0},
        scratch_shapes=[pltpu.SemaphoreType.DMA((N_DMA_LANES,))],
        compiler_params=_cparams(("arbitrary",)),
        name="moe_dispatch",
    )(*slots, hq, init)


def _experts_body(plan_ref, xs_ref, wg_ref, wu_ref, wd_ref, ys_ref, wgu_ref, wdn_ref):
    t = pl.program_id(0)
    in_use = t < plan_ref[1, 0]
    new_expert = (t == 0) | (plan_ref[0, t] != plan_ref[0, jnp.maximum(t - 1, 0)])

    @pl.when(in_use & new_expert)
    def _():
        wgu_ref[:, :EXP_H] = wg_ref[0, 0].astype(BF16)
        wgu_ref[:, EXP_H:] = wu_ref[0, 0].astype(BF16)
        wdn_ref[...] = wd_ref[0, 0].astype(BF16)

    @pl.when(in_use)
    def _():
        x = _load_unpacked(xs_ref).astype(BF16)
        ab = jnp.dot(x, wgu_ref[...], preferred_element_type=F32)
        a, b = ab[:, :EXP_H], ab[:, EXP_H:]
        act = a * _sigmoid(a) * b
        _store_packed(ys_ref, jnp.dot(act.astype(BF16), wdn_ref[...], preferred_element_type=F32))


def _experts(plan, xs, w_gate, w_up, w_down, *, layer):
    def tile_idx(t, plan):
        return (jnp.minimum(t, jnp.maximum(plan[1, 0] - 1, 0)), 0, 0, 0)

    grid_spec = pltpu.PrefetchScalarGridSpec(
        num_scalar_prefetch=1,
        grid=(SORT_TILES,),
        in_specs=[
            pl.BlockSpec(_packed_shape(EXP_TILE), tile_idx),
            pl.BlockSpec((1, 1, D, EXP_H), lambda t, plan: (layer, plan[0, t], 0, 0)),
            pl.BlockSpec((1, 1, D, EXP_H), lambda t, plan: (layer, plan[0, t], 0, 0)),
            pl.BlockSpec((1, 1, EXP_H, D), lambda t, plan: (layer, plan[0, t], 0, 0)),
        ],
        out_specs=pl.BlockSpec(_packed_shape(EXP_TILE), tile_idx),
        scratch_shapes=[pltpu.VMEM((D, 2 * EXP_H), BF16), pltpu.VMEM((EXP_H, D), BF16)],
    )
    return pl.pallas_call(
        _experts_body,
        grid_spec=grid_spec,
        out_shape=jax.ShapeDtypeStruct(_packed_shape(SORT_ROWS), jnp.uint32),
        input_output_aliases={1: 0},
        compiler_params=_cparams(("arbitrary",)),
        name="moe_experts",
    )(plan, xs, w_gate, w_up, w_down)


def _combine_body(*refs, tm, final):
    cur_slots, nxt_slots = refs[0:4], refs[4:8]
    meta_ref, x_ref, ada_ref = refs[8:11]
    rest = refs[11:]
    if final:
        fgain_ref, ys_ref, oc_ref, ol_ref, y_ref, sem = rest
    else:
        ys_ref, o_ref, y_ref, sem = rest
    step = pl.program_id(0)
    n_steps = pl.num_programs(0)

    def fetch(slot_refs, buf):
        slots = ((slot_refs[0], slot_refs[1]), (slot_refs[2], slot_refs[3]))

        def copy(blk, j, k, lane):
            i = blk * ROW_UNROLL + j
            return pltpu.make_async_copy(_packed_row(ys_ref, slots[k][0][i], slots[k][1][i]),
                                         _packed_row(y_ref.at[buf, k], blk, j), sem.at[buf, lane])

        _start_row_copies(tm, copy)

    @pl.when(step == 0)
    def _():
        fetch(cur_slots, 0)

    @pl.when(step + 1 < n_steps)
    def _():
        fetch(nxt_slots, (step + 1) % 2)

    buf = step % 2
    _drain_row_copies(ys_ref.at[pl.ds(0, tm // 8)], y_ref.at[buf, 0], sem.at[buf])
    y_ref = y_ref.at[buf]
    meta = meta_ref[...]
    mix = (meta[:, META_W1:META_W1 + 1] * _load_unpacked(y_ref.at[0])
           + meta[:, META_W2:META_W2 + 1] * _load_unpacked(y_ref.at[1]))
    x = x_ref[...] + ada_ref[0, 5:6, :] * mix
    if not final:
        o_ref[...] = x
        return
    ms = jnp.mean(x * x, axis=-1, keepdims=True)
    y = x * lax.rsqrt(ms + EPS) * fgain_ref[...]
    is_ctx = pl.program_id(0) < N_CTX // tm

    @pl.when(is_ctx)
    def _():
        oc_ref[...] = y

    @pl.when(jnp.logical_not(is_ctx))
    def _():
        ol_ref[...] = y


def _combine(slots, ys, meta, x, ada, final_gain=None, *, tm=1024):
    final = final_gain is not None
    extra_in = [pl.BlockSpec((1, D), lambda t: (0, 0))] if final else []
    extra_args = [final_gain.reshape(1, D)] if final else []
    if final:
        out_specs = _split_specs(tm, D)
        out_shape = [jax.ShapeDtypeStruct((N_CTX, D), F32), jax.ShapeDtypeStruct((N_LAT, D), F32)]
    else:
        out_specs = pl.BlockSpec((tm, D), lambda t: (t, 0))
        out_shape = jax.ShapeDtypeStruct((N_TOK, D), F32)
    return pl.pallas_call(
        functools.partial(_combine_body, tm=tm, final=final),
        grid=(N_TOK // tm,),
        in_specs=_slot_specs(tm) + _slot_specs(tm, ahead=1) + [
            pl.BlockSpec((tm, ROUTE_W), lambda t: (t, 0)),
            pl.BlockSpec((tm, D), lambda t: (t, 0)),
            pl.BlockSpec((1, 6, D), lambda t: (_cond_row(t, tm), 0, 0)),
        ] + extra_in + [pl.BlockSpec(memory_space=pl.ANY)],
        out_specs=out_specs,
        out_shape=out_shape,
        scratch_shapes=[pltpu.VMEM((2, 2) + _packed_shape(tm), jnp.uint32),
                        pltpu.SemaphoreType.DMA((2, N_DMA_LANES))],
        compiler_params=_cparams(("arbitrary",)),
        name="moe_combine",
    )(*slots, *slots, meta, x, ada, *extra_args, ys)


def _moe(routed, ada, w_gate, w_up, w_down, *, layer, sort_init, final_gain=None):
    x, hq, meta, cnt = routed
    pos, plan = _slots(meta, cnt)
    slots = [pos[:, r, :].reshape(N_TOK) for r in range(4)]
    xs = _dispatch(slots, hq, sort_init)
    ys = _experts(plan, xs, w_gate, w_up, w_down, layer=layer)
    return _combine(slots, ys, meta, x, ada, final_gain), ys


def kernel(x_prompt, x_sample, cache_k, cache_v, c, c_ctx, w_ada, b_ada, norm_mix, norm_ffn, attn_w_q, attn_w_kv, attn_q_norm, attn_k_norm, attn_w_o, hy_w_in, hy_b_in, hy_short_w, hy_short_b, hy_filt_w1, hy_filt_b1, hy_filt_w2, hy_filt_b2, hy_filt_w3, hy_filt_b3, hy_filt_freq, hy_log_decay, hy_skip, hy_w_out, hy_b_out, router_group_w, router_group_b, router_expert_w, router_expert_b, moe_w_gate, moe_w_up, moe_w_down, final_norm):
    depth = w_ada.shape[0]
    x_parts = (x_prompt.reshape(N_CTX, D), x_sample.reshape(N_LAT, D))
    cond8 = jnp.zeros((8, D), F32).at[0].set(c_ctx).at[1:1 + N_BATCH_LAT].set(c)
    ada_all = _ada(cond8, w_ada, b_ada).reshape(depth, 8, 6, D)
    new_k = []
    new_v = []
    sort_buf = jnp.zeros(_packed_shape(SORT_ROWS), jnp.uint32)
    for l in range(depth):
        ada = ada_all[l]
        if l % 2 == 0:
            a = l // 2
            w_qkv = jnp.concatenate([attn_w_q[a], attn_w_kv[a]], axis=1).astype(BF16)
            q, k_ctx, v_ctx, kc, vc, kl, vl = _qkv(x_parts, ada, norm_mix[l], w_qkv, attn_q_norm[a], attn_k_norm[a])
            new_k.append(k_ctx)
            new_v.append(v_ctx)
            o_ctx = _attention(q, [kc], [vc], row0=0, n_batch=N_BATCH_CTX, seq=L_CTX, tq=L_CTX,
                               n_kv=N_KV, stack=Q_PER_KV)
            past_keys = _lane_tile_heads(cache_k[:, a]).transpose(0, 1, 3, 2)
            o_lat = _attention(q, [kl, past_keys], [vl, _lane_tile_heads(cache_v[:, a])],
                               row0=N_CTX, n_batch=N_BATCH_LAT, seq=L_LAT, tq=1024, n_kv=1, stack=1)
            mixer = (o_ctx, o_lat, attn_w_o[a].astype(BF16), jnp.zeros((D,), F32))
        else:
            j = l // 2
            (x,) = x_parts
            up = _norm_proj(x, ada, norm_mix[l], hy_w_in[j].astype(BF16), hy_b_in[j], part=0)
            outs = []
            for row0, n_batch, L, tc in ((0, N_BATCH_CTX, L_CTX, D), (N_CTX, N_BATCH_LAT, L_LAT, 256)):
                cs = jnp.asarray(_dft_table(L)).astype(BF16)
                fr, fi, fn = _filter_spectra(L, cs, hy_filt_w1[j], hy_filt_b1[j], hy_filt_w2[j], hy_filt_b2[j],
                                             hy_filt_w3[j], hy_filt_b3[j], hy_filt_freq[j], hy_log_decay[j])
                outs.append(_hyena_conv(up, hy_short_w[j], hy_short_b[j], hy_skip[j], fr, fi, fn, cs,
                                        row0=row0, n_batch=n_batch, L=L, tc=tc))
            mixer = (outs[0], outs[1], hy_w_out[j].astype(BF16), hy_b_out[j])
        routed = _mix_route(*mixer, x_parts, ada, norm_ffn[l], router_group_w[l], router_group_b[l],
                            router_expert_w[l], router_expert_b[l])
        out, sort_buf = _moe(routed, ada, moe_w_gate, moe_w_up, moe_w_down, layer=l, sort_init=sort_buf,
                             final_gain=final_norm if l == depth - 1 else None)
        x_parts = (out,)
    y_ctx, y_lat = x_parts[0]
    return (y_ctx.reshape(N_BATCH_CTX, L_CTX, D), y_lat.reshape(N_BATCH_LAT, L_LAT, D),
            jnp.stack(new_k, axis=1), jnp.stack(new_v, axis=1))
```

```python
import functools
import math

import numpy as np
import jax
import jax.numpy as jnp
from jax import lax
from jax.experimental import pallas as pl
from jax.experimental.pallas import tpu as pltpu

F32 = jnp.float32
BF16 = jnp.bfloat16

D = 1024
N_BATCH_CTX = 32
L_CTX = 256
N_BATCH_LAT = 2
L_LAT = 2048
PAST = 256
N_CTX = N_BATCH_CTX * L_CTX
N_LAT = N_BATCH_LAT * L_LAT
N_TOK = N_CTX + N_LAT
GRID_W = 64
N_HEADS = 16
N_KV = 4
DH = 64
Q_PER_KV = N_HEADS // N_KV
KV_W = N_KV * DH
ROPE_THETA = 10000.0
FILTER_FEAT = 17
FEAT_PAD = 32
FILTER_HIDDEN = 64
N_GROUPS = 4
E_PER_G = 8
N_EXP = N_GROUPS * E_PER_G
EXP_H = D // 4
EPS = 1e-6
ROUTE_W = 128
VMEM_LIMIT = 56 * 1024 * 1024


def _cparams(sem):
    return pltpu.CompilerParams(dimension_semantics=sem, vmem_limit_bytes=VMEM_LIMIT)


def _cond_row(i, tm):
    n_ctx_tiles = N_CTX // tm
    return jnp.where(i < n_ctx_tiles, 0, 1 + (i - n_ctx_tiles) // (L_LAT // tm))


def _split_specs(tm, width):
    n_ctx_tiles = N_CTX // tm
    return [pl.BlockSpec((tm, width), lambda i: (jnp.minimum(i, n_ctx_tiles - 1), 0)),
            pl.BlockSpec((tm, width), lambda i: (jnp.maximum(i - n_ctx_tiles, 0), 0))]


def _pick_split(i, tm, ctx_ref, lat_ref):
    return jnp.where(i < N_CTX // tm, ctx_ref[...], lat_ref[...])


def _sigmoid(x):
    return 1.0 / (1.0 + jnp.exp(-x))


@functools.lru_cache(maxsize=None)
def _dft_table(L):
    k = np.arange(L, dtype=np.int64)
    ang = (np.outer(k, k) % (2 * L)).astype(np.float64) * (math.pi / L)
    return np.concatenate([np.cos(ang), np.sin(ang)], axis=0).astype(np.float32)


@functools.lru_cache(maxsize=None)
def _filter_feats(L):
    t = np.linspace(0.0, 1.0, L, dtype=np.float64)[:, None]
    bands = np.linspace(1e-4, 7.0, 8, dtype=np.float64)[None, :]
    w = (2.0 * math.pi) * np.arange(L, dtype=np.float64)[:, None] / L
    feats = np.concatenate([t, np.cos(bands * w), -np.sin(bands * w)], axis=-1)
    out = np.zeros((L, FEAT_PAD), np.float32)
    out[:, :FILTER_FEAT] = feats
    return out


@functools.lru_cache(maxsize=None)
def _rope_tables(tm):
    pos = np.arange(L_LAT)
    row = (pos // GRID_W).astype(np.float64)
    col = (pos % GRID_W).astype(np.float64)
    axis_dim = DH // 2
    inv_freq = ROPE_THETA ** (-np.arange(0, axis_dim, 2, dtype=np.float64) / axis_dim)
    lane = np.arange(KV_W)
    d = lane % DH
    is_col = (d // axis_dim) == 1
    fi = d % (axis_dim // 2)
    first_half = (d % axis_dim) < (axis_dim // 2)
    p = np.where(is_col[None, :], col[:, None], row[:, None])
    ang = p * inv_freq[fi][None, :]
    cos = np.cos(ang)
    sin = np.sin(ang) * np.where(first_half, -1.0, 1.0)[None, :]
    cos = np.concatenate([np.ones((tm, KV_W)), cos], axis=0).astype(np.float32)
    sin = np.concatenate([np.zeros((tm, KV_W)), sin], axis=0).astype(np.float32)
    return cos, sin


@functools.lru_cache(maxsize=None)
def _head_sum_matrix():
    lane = np.arange(KV_W)
    return (lane[:, None] // DH == lane[None, :] // DH).astype(np.float32)


def _ada_body(c_ref, w_ref, b_ref, o_ref):
    c = c_ref[...]
    s = c * _sigmoid(c)
    o_ref[0] = jnp.dot(s.astype(BF16), w_ref[0].astype(BF16), preferred_element_type=F32) + b_ref[0]


def _ada(cond8, w_ada, b_ada):
    depth = w_ada.shape[0]
    tn = 1536
    return pl.pallas_call(
        _ada_body,
        grid=(depth, 6 * D // tn),
        in_specs=[
            pl.BlockSpec((8, D), lambda l, j: (0, 0)),
            pl.BlockSpec((1, D, tn), lambda l, j: (l, 0, j)),
            pl.BlockSpec((1, 1, tn), lambda l, j: (l, 0, j)),
        ],
        out_specs=pl.BlockSpec((1, 8, tn), lambda l, j: (l, 0, j)),
        out_shape=jax.ShapeDtypeStruct((depth, 8, 6 * D), F32),
        compiler_params=_cparams(("arbitrary", "arbitrary")),
        name="ada",
    )(cond8, w_ada, b_ada.reshape(depth, 1, 6 * D))


def _norm_mod(x, gain_ref, ada_ref, part):
    ms = jnp.mean(x * x, axis=-1, keepdims=True)
    y = x * lax.rsqrt(ms + EPS) * gain_ref[...]
    return y * (1.0 + ada_ref[0, part + 1:part + 2, :]) + ada_ref[0, part:part + 1, :]


def _proj_body(x_ref, ada_ref, gain_ref, w_ref, b_ref, o_ref, *, part):
    h = _norm_mod(x_ref[...], gain_ref, ada_ref, part)
    acc = jnp.dot(h.astype(BF16), w_ref[...], preferred_element_type=F32) + b_ref[...]
    o_ref[...] = acc.astype(o_ref.dtype)


def _norm_proj(x, ada, gain, w, b, *, part, tm=1024):
    n_out = w.shape[1]
    return pl.pallas_call(
        functools.partial(_proj_body, part=part),
        grid=(N_TOK // tm,),
        in_specs=[
            pl.BlockSpec((tm, D), lambda i: (i, 0)),
            pl.BlockSpec((1, 6, D), lambda i: (_cond_row(i, tm), 0, 0)),
            pl.BlockSpec((1, D), lambda i: (0, 0)),
            pl.BlockSpec((D, n_out), lambda i: (0, 0)),
            pl.BlockSpec((1, n_out), lambda i: (0, 0)),
        ],
        out_specs=pl.BlockSpec((tm, n_out), lambda i: (i, 0)),
        out_shape=jax.ShapeDtypeStruct((N_TOK, n_out), BF16),
        compiler_params=_cparams(("arbitrary",)),
        name="norm_proj",
    )(x, ada, gain.reshape(1, D), w, b.reshape(1, n_out))


def _head_rms_rope(x, gain, hs, cos, sin, lane):
    ss = jnp.dot((x * x).astype(BF16), hs, preferred_element_type=F32)
    xn = x * lax.rsqrt(ss * (1.0 / DH) + EPS) * gain
    fwd = pltpu.roll(xn, KV_W - DH // 4, 1)
    bwd = pltpu.roll(xn, DH // 4, 1)
    partner = jnp.where((lane & (DH // 4)) == 0, fwd, bwd)
    return xn * cos + partner * sin


@functools.lru_cache(maxsize=None)
def _lane_tile_matrix():
    col = np.arange(N_KV * KV_W)
    src = (col // KV_W) * DH + col % DH
    return (np.arange(KV_W)[:, None] == src[None, :]).astype(np.float32)


def _qkv_body(*refs, tm, split_x):
    n_x = 2 if split_x else 1
    x_refs = refs[:n_x]
    (ada_ref, gain_ref, w_ref, qg_ref, kg_ref, hs_ref, cos_ref, sin_ref, tile_ref, tile_t_ref,
     q_ref, newk_ref, newv_ref, kc_ref, vc_ref, kl_ref, vl_ref) = refs[n_x:]
    step = pl.program_id(0)
    x = _pick_split(step, tm, *x_refs) if split_x else x_refs[0][...]
    h = _norm_mod(x, gain_ref, ada_ref, 0)
    acc = jnp.dot(h.astype(BF16), w_ref[...], preferred_element_type=F32)
    hs = hs_ref[...]
    cos = cos_ref[...]
    sin = sin_ref[...]
    lane = lax.broadcasted_iota(jnp.int32, (1, KV_W), 1)
    for c in range(N_KV):
        qc = _head_rms_rope(acc[:, c * KV_W:(c + 1) * KV_W], qg_ref[...], hs, cos, sin, lane)
        q_ref[:, c * KV_W:(c + 1) * KV_W] = (qc * (DH ** -0.5)).astype(BF16)
    k = _head_rms_rope(acc[:, D:D + KV_W], kg_ref[...], hs, cos, sin, lane)
    v = acc[:, D + KV_W:D + 2 * KV_W]
    k4t = lax.dot_general(tile_t_ref[...], k.astype(BF16), (((1,), (1,)), ((), ())),
                          preferred_element_type=F32).astype(BF16)
    v4 = jnp.dot(v.astype(BF16), tile_ref[...], preferred_element_type=F32).astype(BF16)
    is_ctx = step < N_CTX // tm

    @pl.when(is_ctx)
    def _():
        for bb in range(tm // L_CTX):
            rows = slice(bb * L_CTX, (bb + 1) * L_CTX)
            for hd in range(N_KV):
                newk_ref[bb, hd] = k[rows, hd * DH:(hd + 1) * DH]
                newv_ref[bb, hd] = v[rows, hd * DH:(hd + 1) * DH]
                kc_ref[bb, hd] = k4t[hd * KV_W:(hd + 1) * KV_W, rows]
                vc_ref[bb, hd] = v4[rows, hd * KV_W:(hd + 1) * KV_W]

    @pl.when(jnp.logical_not(is_ctx))
    def _():
        for hd in range(N_KV):
            kl_ref[0, hd] = k4t[hd * KV_W:(hd + 1) * KV_W, :]
            vl_ref[0, hd] = v4[:, hd * KV_W:(hd + 1) * KV_W]


def _qkv(x_parts, ada, gain, w_qkv, q_gain, k_gain):
    tm = 2 * L_CTX
    per_tile = tm // L_CTX
    cos, sin = _rope_tables(tm)
    split_x = len(x_parts) == 2
    x_specs = _split_specs(tm, D) if split_x else [pl.BlockSpec((tm, D), lambda i: (i, 0))]
    n_ctx_tiles = N_CTX // tm
    lat_tiles = L_LAT // tm

    def rope_idx(i):
        return (jnp.where(i < n_ctx_tiles, 0, 1 + (i - n_ctx_tiles) % lat_tiles), 0)

    def ctx_idx(i):
        return (jnp.minimum(i, n_ctx_tiles - 1), 0, 0, 0)

    def lat_idx(i):
        j = jnp.maximum(i - n_ctx_tiles, 0)
        return (j // lat_tiles, 0, j % lat_tiles, 0)

    def lat_idx_t(i):
        j = jnp.maximum(i - n_ctx_tiles, 0)
        return (j // lat_tiles, 0, 0, j % lat_tiles)

    const = lambda i: (0, 0)
    kv_ctx = jax.ShapeDtypeStruct((N_BATCH_CTX, N_KV, L_CTX, DH), F32)
    tiled_ctx = jax.ShapeDtypeStruct((N_BATCH_CTX, N_KV, L_CTX, KV_W), BF16)
    tiled_lat = jax.ShapeDtypeStruct((N_BATCH_LAT, N_KV, L_LAT, KV_W), BF16)
    keys_ctx = jax.ShapeDtypeStruct((N_BATCH_CTX, N_KV, KV_W, L_CTX), BF16)
    keys_lat = jax.ShapeDtypeStruct((N_BATCH_LAT, N_KV, KV_W, L_LAT), BF16)
    return pl.pallas_call(
        functools.partial(_qkv_body, tm=tm, split_x=split_x),
        grid=(N_TOK // tm,),
        in_specs=x_specs + [
            pl.BlockSpec((1, 6, D), lambda i: (_cond_row(i, tm), 0, 0)),
            pl.BlockSpec((1, D), const),
            pl.BlockSpec((D, D + 2 * KV_W), const),
            pl.BlockSpec((1, KV_W), const),
            pl.BlockSpec((1, KV_W), const),
            pl.BlockSpec((KV_W, KV_W), const),
            pl.BlockSpec((tm, KV_W), rope_idx),
            pl.BlockSpec((tm, KV_W), rope_idx),
            pl.BlockSpec((KV_W, N_KV * KV_W), const),
            pl.BlockSpec((N_KV * KV_W, KV_W), const),
        ],
        out_specs=[
            pl.BlockSpec((tm, D), lambda i: (i, 0)),
            pl.BlockSpec((per_tile, N_KV, L_CTX, DH), ctx_idx),
            pl.BlockSpec((per_tile, N_KV, L_CTX, DH), ctx_idx),
            pl.BlockSpec((per_tile, N_KV, KV_W, L_CTX), ctx_idx),
            pl.BlockSpec((per_tile, N_KV, L_CTX, KV_W), ctx_idx),
            pl.BlockSpec((1, N_KV, KV_W, tm), lat_idx_t),
            pl.BlockSpec((1, N_KV, tm, KV_W), lat_idx),
        ],
        out_shape=[jax.ShapeDtypeStruct((N_TOK, D), BF16), kv_ctx, kv_ctx, keys_ctx, tiled_ctx, keys_lat, tiled_lat],
        compiler_params=_cparams(("arbitrary",)),
        name="qkv",
    )(*x_parts, ada, gain.reshape(1, D), w_qkv,
      jnp.tile(q_gain, Q_PER_KV).reshape(1, KV_W), jnp.tile(k_gain, N_KV).reshape(1, KV_W),
      jnp.asarray(_head_sum_matrix()).astype(BF16), jnp.asarray(cos), jnp.asarray(sin),
      jnp.asarray(_lane_tile_matrix()).astype(BF16), jnp.asarray(_lane_tile_matrix().T.copy()).astype(BF16))


def _attn_body(*refs, n_kv, stack, n_seg):
    q_ref, k_refs, v_refs, o_ref = refs[0], refs[1:1 + n_seg], refs[1 + n_seg:1 + 2 * n_seg], refs[-1]
    lane = lax.broadcasted_iota(jnp.int32, (1, KV_W), 1)
    masks = [(lane >> 6) == g for g in range(Q_PER_KV)]
    tq = q_ref.shape[0]
    for kv in range(n_kv):
        q = q_ref[:, kv * KV_W:(kv + 1) * KV_W]
        out = jnp.zeros((tq, KV_W), F32)
        for c in range(0, Q_PER_KV, stack):
            pair = masks[c:c + stack]
            ones_blk = (c + stack) % Q_PER_KV if stack < Q_PER_KV else None
            stacked = jnp.concatenate([jnp.where(m, q, jnp.zeros_like(q)) for m in pair], axis=0)
            scores = [jnp.dot(stacked, k_ref[0, kv], preferred_element_type=F32) for k_ref in k_refs]
            top = functools.reduce(jnp.maximum, [jnp.max(s, axis=-1, keepdims=True) for s in scores])
            probs = [jnp.exp((s - top).astype(BF16)) for s in scores]
            if ones_blk is None:
                vals = [v_ref[0, kv] for v_ref in v_refs]
            else:
                vals = [jnp.where(masks[ones_blk], jnp.ones((), BF16), v_ref[0, kv]) for v_ref in v_refs]
            og = sum(jnp.dot(p, v, preferred_element_type=F32) for p, v in zip(probs, vals))
            if ones_blk is None:
                denom = sum(jnp.sum(p.astype(F32), axis=-1, keepdims=True) for p in probs)
            else:
                denom = og[:, ones_blk * DH:ones_blk * DH + 1]
            og = og * (1.0 / denom)
            for g, m in enumerate(pair):
                out = out + jnp.where(m, og[g * tq:(g + 1) * tq], 0.0)
        o_ref[:, kv * KV_W:(kv + 1) * KV_W] = out.astype(BF16)


def _attention(q, keys, values, *, row0, n_batch, seq, tq, n_kv, stack):
    per_b = seq // tq
    base = row0 // tq
    kv_specs = [pl.BlockSpec((1, n_kv) + a.shape[2:], lambda b, h, i: (b, h, 0, 0)) for a in keys + values]
    return pl.pallas_call(
        functools.partial(_attn_body, n_kv=n_kv, stack=stack, n_seg=len(keys)),
        grid=(n_batch, N_KV // n_kv, per_b),
        in_specs=[pl.BlockSpec((tq, n_kv * KV_W), lambda b, h, i: (base + b * per_b + i, h))] + kv_specs,
        out_specs=pl.BlockSpec((tq, n_kv * KV_W), lambda b, h, i: (b * per_b + i, h)),
        out_shape=jax.ShapeDtypeStruct((n_batch * seq, D), BF16),
        compiler_params=_cparams(("arbitrary", "arbitrary", "arbitrary")),
        name="attention",
    )(q, *keys, *values)


def _lane_tile_heads(x):
    return jnp.tile(x.astype(BF16), (1, 1, 1, KV_W // DH))


def _alt_sign(L):
    row = lax.broadcasted_iota(jnp.int32, (L, 1), 0)
    return row, jnp.where((row & 1) == 0, 1.0, -1.0)


def _split_dot(a, b):
    a_hi, b_hi = a.astype(BF16), b.astype(BF16)
    a_lo = (a - a_hi.astype(F32)).astype(BF16)
    b_lo = (b - b_hi.astype(F32)).astype(BF16)
    return (jnp.dot(a_hi, b_hi, preferred_element_type=F32) + jnp.dot(a_lo, b_hi, preferred_element_type=F32)
            + jnp.dot(a_hi, b_lo, preferred_element_type=F32))


def _filter_body(feat_ref, w1_ref, b1_ref, w2_ref, b2_ref, fq_ref, w3_ref, b3_ref, ld_ref, cs_ref,
                 fr_ref, fi_ref, fn_ref, hid_ref, *, L):
    feats = feat_ref[...]

    @pl.when(pl.program_id(0) == 0)
    def _():
        fq = fq_ref[...]
        h1 = jnp.sin(fq * (_split_dot(feats, w1_ref[...]) + b1_ref[...]))
        hid_ref[...] = jnp.sin(fq * (_split_dot(h1, w2_ref[...]) + b2_ref[...])).astype(BF16)

    t = feats[:, 0:1]
    row, alt = _alt_sign(L)
    filt = []
    for j in range(4):
        raw = jnp.dot(hid_ref[...], w3_ref[j].astype(BF16), preferred_element_type=F32) + b3_ref[j:j + 1, :]
        filt.append(raw * jnp.exp(-t * jnp.exp(ld_ref[j:j + 1, :])))
    for o in range(2):
        hf, hb = filt[2 * o], filt[2 * o + 1]
        l1 = jnp.sum(jnp.abs(hf), axis=0, keepdims=True) + jnp.sum(jnp.abs(hb), axis=0, keepdims=True)
        inv = 1.0 / (l1 + EPS)
        sym = (hf + hb) * inv
        asym = (hb - hf) * inv
        fr = jnp.dot(cs_ref[0:L, :], sym.astype(BF16), preferred_element_type=F32) * (1.0 / L)
        fr_ref[o] = jnp.where(row == 0, 0.5 * fr, fr)
        fi_ref[o] = jnp.dot(cs_ref[L:2 * L, :], asym.astype(BF16), preferred_element_type=F32) * (1.0 / L)
        fn_ref[o] = jnp.sum(sym * alt, axis=0, keepdims=True) * (0.5 / L)


def _filter_spectra(L, cs, w1, b1, w2, b2, w3, b3, freq, log_decay, *, tc=256):
    w1p = jnp.zeros((FEAT_PAD, FILTER_HIDDEN), F32).at[:FILTER_FEAT].set(w1)
    w3r = w3.reshape(FILTER_HIDDEN, 4, D).transpose(1, 0, 2)
    const = lambda j: (0, 0)
    return pl.pallas_call(
        functools.partial(_filter_body, L=L),
        grid=(D // tc,),
        in_specs=[
            pl.BlockSpec((L, FEAT_PAD), const),
            pl.BlockSpec((FEAT_PAD, FILTER_HIDDEN), const),
            pl.BlockSpec((1, FILTER_HIDDEN), const),
            pl.BlockSpec((FILTER_HIDDEN, FILTER_HIDDEN), const),
            pl.BlockSpec((1, FILTER_HIDDEN), const),
            pl.BlockSpec((1, FILTER_HIDDEN), const),
            pl.BlockSpec((4, FILTER_HIDDEN, tc), lambda j: (0, 0, j)),
            pl.BlockSpec((4, tc), lambda j: (0, j)),
            pl.BlockSpec((4, tc), lambda j: (0, j)),
            pl.BlockSpec((2 * L, L), const, pipeline_mode=pl.Buffered(1)),
        ],
        out_specs=[
            pl.BlockSpec((2, L, tc), lambda j: (0, 0, j)),
            pl.BlockSpec((2, L, tc), lambda j: (0, 0, j)),
            pl.BlockSpec((2, 1, tc), lambda j: (0, 0, j)),
        ],
        out_shape=[
            jax.ShapeDtypeStruct((2, L, D), F32),
            jax.ShapeDtypeStruct((2, L, D), F32),
            jax.ShapeDtypeStruct((2, 1, D), F32),
        ],
        scratch_shapes=[pltpu.VMEM((L, FILTER_HIDDEN), BF16)],
        compiler_params=_cparams(("arbitrary",)),
        name="hyena_filter",
    )(jnp.asarray(_filter_feats(L)), w1p, b1.reshape(1, -1), w2, b2.reshape(1, -1), freq.reshape(1, -1),
      w3r, b3.reshape(4, D), log_decay.reshape(4, D), cs)


FREQ_CHUNK = 1024


def _hyconv_body(x1_ref, x2_ref, v_ref, sw_ref, sb_ref, skip_ref, fr_ref, fi_ref, fn_ref, cs_ref, o_ref,
                 z_ref, g_ref, zb_ref, yr_ref, yi_ref, *, L):
    row, alt = _alt_sign(L)
    kc = min(FREQ_CHUNK, L)
    chunk_row = lax.broadcasted_iota(jnp.int32, (kc, 1), 0)

    def short_conv(u_ref, p):
        u = u_ref[...].astype(F32)
        prev = jnp.where(row == 0, 0.0, pltpu.roll(u, 1, 0))
        nxt = jnp.where(row == L - 1, 0.0, pltpu.roll(u, L - 1, 0))
        return (prev * sw_ref[0, p:p + 1, :] + u * sw_ref[1, p:p + 1, :] + nxt * sw_ref[2, p:p + 1, :]
                + sb_ref[p:p + 1, :])

    z_ref[...] = short_conv(v_ref, 2)
    g_ref[0] = short_conv(x1_ref, 0)
    g_ref[1] = short_conv(x2_ref, 1)
    for o in range(2):
        z = z_ref[...]
        zb_ref[...] = z.astype(BF16)
        nyq = jnp.sum(z * alt, axis=0, keepdims=True) * fn_ref[o]

        def to_freq(c, carry):
            lo = pl.multiple_of(c * kc, kc)
            zr = jnp.dot(cs_ref[pl.ds(lo, kc), :], zb_ref[...], preferred_element_type=F32)
            zs = jnp.dot(cs_ref[pl.ds(L + lo, kc), :], zb_ref[...], preferred_element_type=F32)
            fr = fr_ref[o, pl.ds(lo, kc), :]
            fi = fi_ref[o, pl.ds(lo, kc), :]
            yr_ref[pl.ds(lo, kc), :] = (zr * fr + zs * fi).astype(BF16)
            yi_ref[pl.ds(lo, kc), :] = (zr * fi - zs * fr).astype(BF16)
            return carry

        lax.fori_loop(0, L // kc, to_freq, 0, unroll=True)

        def to_time(c, carry):
            lo = pl.multiple_of(c * kc, kc)
            y = (jnp.dot(cs_ref[pl.ds(lo, kc), :], yr_ref[...], preferred_element_type=F32)
                 - jnp.dot(cs_ref[pl.ds(L + lo, kc), :], yi_ref[...], preferred_element_type=F32))
            y = y + jnp.where(((chunk_row + lo) & 1) == 0, nyq, -nyq)
            z_ref[pl.ds(lo, kc), :] = g_ref[o, pl.ds(lo, kc), :] * (y + skip_ref[o:o + 1, :] * z_ref[pl.ds(lo, kc), :])
            return carry

        lax.fori_loop(0, L // kc, to_time, 0, unroll=True)
    o_ref[...] = z_ref[...].astype(BF16)


def _hyena_conv(up, sw, sb, skip, fr, fi, fn, cs, *, row0, n_batch, L, tc):
    n_ct = D // tc
    base = row0 // L
    u_spec = lambda p: pl.BlockSpec((L, tc), lambda j, b: (base + b, p * n_ct + j))
    once = pl.Buffered(1)
    return pl.pallas_call(
        functools.partial(_hyconv_body, L=L),
        grid=(n_ct, n_batch),
        in_specs=[
            u_spec(0), u_spec(1), u_spec(2),
            pl.BlockSpec((3, 3, tc), lambda j, b: (0, 0, j)),
            pl.BlockSpec((3, tc), lambda j, b: (0, j)),
            pl.BlockSpec((2, tc), lambda j, b: (0, j)),
            pl.BlockSpec((2, L, tc), lambda j, b: (0, 0, j), pipeline_mode=once),
            pl.BlockSpec((2, L, tc), lambda j, b: (0, 0, j), pipeline_mode=once),
            pl.BlockSpec((2, 1, tc), lambda j, b: (0, 0, j)),
            pl.BlockSpec((2 * L, L), lambda j, b: (0, 0), pipeline_mode=once),
        ],
        out_specs=pl.BlockSpec((L, tc), lambda j, b: (b, j)),
        out_shape=jax.ShapeDtypeStruct((n_batch * L, D), BF16),
        scratch_shapes=[
            pltpu.VMEM((L, tc), F32),
            pltpu.VMEM((2, L, tc), F32),
            pltpu.VMEM((L, tc), BF16),
            pltpu.VMEM((L, tc), BF16),
            pltpu.VMEM((L, tc), BF16),
        ],
        compiler_params=_cparams(("arbitrary", "arbitrary")),
        name="hyena_conv",
    )(up, up, up, sw.reshape(3, 3, D), sb.reshape(3, D), skip, fr, fi, fn, cs)


EXP_TILE = 512
SORT_TILES = (2 * N_TOK) // EXP_TILE + N_EXP
SORT_ROWS = SORT_TILES * EXP_TILE
N_QUARTERS = 4
QUARTER_W = D // (2 * N_QUARTERS)
META_I1, META_I2, META_R1, META_R2, META_W1, META_W2 = range(6)
HI_HALF = 0xFFFF0000


def _pack_pairs(x):
    bits = pltpu.bitcast(x.astype(BF16).astype(F32), jnp.uint32)
    return (bits[:, :QUARTER_W] >> 16) | bits[:, QUARTER_W:]


def _unpack_pairs(w):
    return pltpu.bitcast(w << 16, F32), pltpu.bitcast(w & jnp.uint32(HI_HALF), F32)


def _packed_shape(rows):
    return (rows // 8, N_QUARTERS, 8, QUARTER_W)


def _store_packed(ref, x):
    for q in range(N_QUARTERS):
        ref[:, q] = _pack_pairs(x[:, q * 2 * QUARTER_W:(q + 1) * 2 * QUARTER_W]).reshape(-1, 8, QUARTER_W)


def _load_unpacked(ref):
    halves = []
    for q in range(N_QUARTERS):
        halves.extend(_unpack_pairs(ref[:, q].reshape(-1, QUARTER_W)))
    return jnp.concatenate(halves, axis=1)


def _packed_row(ref, tile, sublane):
    return ref.at[tile, :, sublane, :]


def _lane_put(lane, values):
    out = jnp.where(lane == 0, values[0], 0.0)
    for k in range(1, len(values)):
        out = out + jnp.where(lane == k, values[k], 0.0)
    return out


def _route_body(*refs, tm, split_x):
    n_x = 2 if split_x else 1
    ac_ref, al_ref, w_ref, b_ref = refs[:4]
    x_refs = refs[4:4 + n_x]
    (ada_ref, gain_ref, wr_hi_ref, wr_lo_ref, br_ref, tri_ref,
     xo_ref, hq_ref, meta_ref, cnt_ref, carry_ref) = refs[4 + n_x:]
    step = pl.program_id(0)

    @pl.when(step == 0)
    def _():
        carry_ref[...] = jnp.zeros_like(carry_ref)

    mixed = jnp.dot(_pick_split(step, tm, ac_ref, al_ref), w_ref[...], preferred_element_type=F32) + b_ref[...]
    x = _pick_split(step, tm, *x_refs) if split_x else x_refs[0][...]
    x = x + ada_ref[0, 2:3, :] * mixed
    xo_ref[...] = x
    h = _norm_mod(x, gain_ref, ada_ref, 3)
    _store_packed(hq_ref, h)
    h_hi = h.astype(BF16)
    h_lo = (h - h_hi.astype(F32)).astype(BF16)
    logits = (jnp.dot(h_hi, wr_hi_ref[...], preferred_element_type=F32)
              + jnp.dot(h_lo, wr_hi_ref[...], preferred_element_type=F32)
              + jnp.dot(h_hi, wr_lo_ref[...], preferred_element_type=F32) + br_ref[...])
    lane = lax.broadcasted_iota(jnp.int32, (1, ROUTE_W), 1)
    lane_f = lane.astype(F32)
    group_of_lane = (lane >> 3).astype(F32)
    neg = -jnp.inf
    big = float(ROUTE_W)
    is_g = (lane >= N_EXP) & (lane < N_EXP + N_GROUPS)
    gl = jnp.where(is_g, logits, neg)
    gmax = jnp.max(gl, axis=-1, keepdims=True)
    gidx = jnp.min(jnp.where(gl == gmax, lane_f - N_EXP, big), axis=-1, keepdims=True)
    g_top = 1.0 / jnp.sum(jnp.where(is_g, jnp.exp(gl - gmax), 0.0), axis=-1, keepdims=True)
    in_group = (lane < N_EXP) & (group_of_lane == gidx)
    el = jnp.where(in_group, logits, neg)
    v1 = jnp.max(el, axis=-1, keepdims=True)
    i1 = jnp.min(jnp.where(el == v1, lane_f, big), axis=-1, keepdims=True)
    el2 = jnp.where(lane_f == i1, neg, el)
    v2 = jnp.max(el2, axis=-1, keepdims=True)
    i2 = jnp.min(jnp.where(el2 == v2, lane_f, big), axis=-1, keepdims=True)
    r = jnp.exp(v2 - v1)
    w1 = g_top / (1.0 + r)
    w2 = g_top * r / (1.0 + r)
    sel = jnp.where((lane_f == i1) | (lane_f == i2), 1.0, 0.0)
    rank = jnp.dot(tri_ref[...], sel.astype(BF16), preferred_element_type=F32) + carry_ref[...]
    r1 = jnp.sum(jnp.where(lane_f == i1, rank, 0.0), axis=-1, keepdims=True)
    r2 = jnp.sum(jnp.where(lane_f == i2, rank, 0.0), axis=-1, keepdims=True)
    carry_ref[...] += jnp.sum(sel, axis=0, keepdims=True)
    cnt_ref[...] = carry_ref[...]
    meta_ref[...] = _lane_put(lane, (i1, i2, r1, r2, w1, w2))


@functools.lru_cache(maxsize=None)
def _strict_lower(n):
    r = np.arange(n)
    return (r[None, :] < r[:, None]).astype(np.float32)


def _mix_route(a_ctx, a_lat, w_out, b_out, x_parts, ada, gain, wg, bg, we, be, *, tm=1024):
    k = a_ctx.shape[1]
    split_x = len(x_parts) == 2
    x_specs = _split_specs(tm, D) if split_x else [pl.BlockSpec((tm, D), lambda i: (i, 0))]
    wr = jnp.zeros((D, ROUTE_W), F32).at[:, :N_EXP].set(we.reshape(D, N_EXP)).at[:, N_EXP:N_EXP + N_GROUPS].set(wg)
    br = jnp.zeros((1, ROUTE_W), F32).at[0, :N_EXP].set(be.reshape(N_EXP)).at[0, N_EXP:N_EXP + N_GROUPS].set(bg)
    wr_hi = wr.astype(BF16)
    wr_lo = (wr - wr_hi.astype(F32)).astype(BF16)
    return pl.pallas_call(
        functools.partial(_route_body, tm=tm, split_x=split_x),
        grid=(N_TOK // tm,),
        in_specs=_split_specs(tm, k) + [
            pl.BlockSpec((k, D), lambda i: (0, 0)),
            pl.BlockSpec((1, D), lambda i: (0, 0)),
        ] + x_specs + [
            pl.BlockSpec((1, 6, D), lambda i: (_cond_row(i, tm), 0, 0)),
            pl.BlockSpec((1, D), lambda i: (0, 0)),
            pl.BlockSpec((D, ROUTE_W), lambda i: (0, 0)),
            pl.BlockSpec((D, ROUTE_W), lambda i: (0, 0)),
            pl.BlockSpec((1, ROUTE_W), lambda i: (0, 0)),
            pl.BlockSpec((tm, tm), lambda i: (0, 0)),
        ],
        out_specs=[
            pl.BlockSpec((tm, D), lambda i: (i, 0)),
            pl.BlockSpec(_packed_shape(tm), lambda i: (i, 0, 0, 0)),
            pl.BlockSpec((tm, ROUTE_W), lambda i: (i, 0)),
            pl.BlockSpec((1, ROUTE_W), lambda i: (0, 0)),
        ],
        out_shape=[
            jax.ShapeDtypeStruct((N_TOK, D), F32),
            jax.ShapeDtypeStruct(_packed_shape(N_TOK), jnp.uint32),
            jax.ShapeDtypeStruct((N_TOK, ROUTE_W), F32),
            jax.ShapeDtypeStruct((1, ROUTE_W), F32),
        ],
        scratch_shapes=[pltpu.VMEM((1, ROUTE_W), F32)],
        compiler_params=_cparams(("arbitrary",)),
        name="mix_route",
    )(a_ctx, a_lat, w_out, b_out.reshape(1, D), *x_parts, ada, gain.reshape(1, D), wr_hi, wr_lo, br,
      jnp.asarray(_strict_lower(tm)).astype(BF16))


def _slots_body(meta_ref, cnt_ref, upper_ref, pos_ref, plan_ref):
    lane = lax.broadcasted_iota(jnp.int32, (1, ROUTE_W), 1)
    lane_f = lane.astype(F32)
    tiles = jnp.floor((cnt_ref[...] + (EXP_TILE - 1.0)) * (1.0 / EXP_TILE))
    end_tile = jnp.dot(jnp.broadcast_to(tiles, (8, ROUTE_W)).astype(BF16), upper_ref[...],
                       preferred_element_type=F32)[0:1]
    start_row = (end_tile - tiles) * EXP_TILE
    meta = meta_ref[...]
    i1 = meta[:, META_I1:META_I1 + 1]
    i2 = meta[:, META_I2:META_I2 + 1]
    p1 = jnp.sum(jnp.where(lane_f == i1, start_row, 0.0), axis=-1, keepdims=True) + meta[:, META_R1:META_R1 + 1]
    p2 = jnp.sum(jnp.where(lane_f == i2, start_row, 0.0), axis=-1, keepdims=True) + meta[:, META_R2:META_R2 + 1]
    hi1, hi2 = jnp.floor(p1 * 0.125), jnp.floor(p2 * 0.125)
    pos_ref[0] = _lane_put(lane, (hi1, p1 - 8.0 * hi1, hi2, p2 - 8.0 * hi2)).T[:8, :].astype(jnp.int32)

    @pl.when(pl.program_id(0) == 0)
    def _():
        end_col = jnp.broadcast_to(end_tile, (ROUTE_W, ROUTE_W)).T
        expert = lax.broadcasted_iota(jnp.int32, (ROUTE_W, ROUTE_W), 0)
        tile = lax.broadcasted_iota(jnp.int32, (ROUTE_W, ROUTE_W), 1).astype(F32)
        passed = jnp.where((end_col <= tile) & (expert < N_EXP), 1.0, 0.0)
        tile_expert = jnp.minimum(jnp.sum(passed, axis=0, keepdims=True), N_EXP - 1.0)
        used = jnp.max(end_tile, axis=-1, keepdims=True)
        row = lax.broadcasted_iota(jnp.int32, (8, ROUTE_W), 0)
        plan_ref[...] = jnp.where(row == 0, tile_expert, jnp.where(row == 1, used, 0.0)).astype(jnp.int32)


@functools.lru_cache(maxsize=None)
def _upper_incl(n):
    r = np.arange(n)
    return (r[:, None] <= r[None, :]).astype(np.float32)


def _slots(meta, cnt, *, tm=2048):
    n_tiles = N_TOK // tm
    return pl.pallas_call(
        _slots_body,
        grid=(n_tiles,),
        in_specs=[
            pl.BlockSpec((tm, ROUTE_W), lambda i: (i, 0)),
            pl.BlockSpec((1, ROUTE_W), lambda i: (0, 0)),
            pl.BlockSpec((ROUTE_W, ROUTE_W), lambda i: (0, 0)),
        ],
        out_specs=[
            pl.BlockSpec((1, 8, tm), lambda i: (i, 0, 0)),
            pl.BlockSpec((8, ROUTE_W), lambda i: (0, 0)),
        ],
        out_shape=[
            jax.ShapeDtypeStruct((n_tiles, 8, tm), jnp.int32),
            jax.ShapeDtypeStruct((8, ROUTE_W), jnp.int32),
        ],
        compiler_params=_cparams(("arbitrary",)),
        name="moe_slots",
    )(meta, cnt, jnp.asarray(_upper_incl(ROUTE_W)).astype(BF16))


ROW_UNROLL = 8
N_DMA_LANES = 2


def _start_row_copies(tm, make_copy):
    def block(blk, carry):
        for j in range(ROW_UNROLL):
            lane = j % N_DMA_LANES
            for k in range(2):
                make_copy(blk, j, k, lane).start(priority=lane)
        return carry

    lax.fori_loop(0, tm // ROW_UNROLL, block, 0)


def _drain_row_copies(like_src, like_dst, sem):
    for lane in range(N_DMA_LANES):
        pltpu.make_async_copy(like_src, like_dst, sem.at[lane]).wait()


def _dispatch_body(t1_ref, s1_ref, t2_ref, s2_ref, hq_ref, xs_in_ref, xs_ref, sem, *, tm):
    del xs_in_ref
    slots = ((t1_ref, s1_ref), (t2_ref, s2_ref))

    def copy(blk, j, k, lane):
        i = blk * ROW_UNROLL + j
        return pltpu.make_async_copy(_packed_row(hq_ref, blk, j), _packed_row(xs_ref, slots[k][0][i], slots[k][1][i]),
                                     sem.at[lane])

    _start_row_copies(tm, copy)
    _drain_row_copies(hq_ref, xs_ref.at[pl.ds(0, tm // 8)], sem)


def _slot_specs(tm, ahead=0):
    last = N_TOK // tm - 1
    return [pl.BlockSpec((tm,), lambda t: (jnp.minimum(t + ahead, last),), memory_space=pltpu.SMEM) for _ in range(4)]


def _dispatch(slots, hq, init, *, tm=1024):
    return pl.pallas_call(
        functools.partial(_dispatch_body, tm=tm),
        grid=(N_TOK // tm,),
        in_specs=_slot_specs(tm) + [
            pl.BlockSpec(_packed_shape(tm), lambda t: (t, 0, 0, 0)),
            pl.BlockSpec(memory_space=pl.ANY),
        ],
        out_specs=pl.BlockSpec(memory_space=pl.ANY),
        out_shape=jax.ShapeDtypeStruct(_packed_shape(SORT_ROWS), jnp.uint32),
        input_output_aliases={5: 0},
        scratch_shapes=[pltpu.SemaphoreType.DMA((N_DMA_LANES,))],
        compiler_params=_cparams(("arbitrary",)),
        name="moe_dispatch",
    )(*slots, hq, init)


def _experts_body(plan_ref, xs_ref, wg_ref, wu_ref, wd_ref, ys_ref, wgu_ref, wdn_ref):
    t = pl.program_id(0)
    in_use = t < plan_ref[1, 0]
    new_expert = (t == 0) | (plan_ref[0, t] != plan_ref[0, jnp.maximum(t - 1, 0)])

    @pl.when(in_use & new_expert)
    def _():
        wgu_ref[:, :EXP_H] = wg_ref[0, 0].astype(BF16)
        wgu_ref[:, EXP_H:] = wu_ref[0, 0].astype(BF16)
        wdn_ref[...] = wd_ref[0, 0].astype(BF16)

    @pl.when(in_use)
    def _():
        x = _load_unpacked(xs_ref).astype(BF16)
        ab = jnp.dot(x, wgu_ref[...], preferred_element_type=F32)
        a, b = ab[:, :EXP_H], ab[:, EXP_H:]
        act = a * _sigmoid(a) * b
        _store_packed(ys_ref, jnp.dot(act.astype(BF16), wdn_ref[...], preferred_element_type=F32))


def _experts(plan, xs, w_gate, w_up, w_down, *, layer):
    def tile_idx(t, plan):
        return (jnp.minimum(t, jnp.maximum(plan[1, 0] - 1, 0)), 0, 0, 0)

    grid_spec = pltpu.PrefetchScalarGridSpec(
        num_scalar_prefetch=1,
        grid=(SORT_TILES,),
        in_specs=[
            pl.BlockSpec(_packed_shape(EXP_TILE), tile_idx),
            pl.BlockSpec((1, 1, D, EXP_H), lambda t, plan: (layer, plan[0, t], 0, 0)),
            pl.BlockSpec((1, 1, D, EXP_H), lambda t, plan: (layer, plan[0, t], 0, 0)),
            pl.BlockSpec((1, 1, EXP_H, D), lambda t, plan: (layer, plan[0, t], 0, 0)),
        ],
        out_specs=pl.BlockSpec(_packed_shape(EXP_TILE), tile_idx),
        scratch_shapes=[pltpu.VMEM((D, 2 * EXP_H), BF16), pltpu.VMEM((EXP_H, D), BF16)],
    )
    return pl.pallas_call(
        _experts_body,
        grid_spec=grid_spec,
        out_shape=jax.ShapeDtypeStruct(_packed_shape(SORT_ROWS), jnp.uint32),
        input_output_aliases={1: 0},
        compiler_params=_cparams(("arbitrary",)),
        name="moe_experts",
    )(plan, xs, w_gate, w_up, w_down)


def _combine_body(*refs, tm, final):
    cur_slots, nxt_slots = refs[0:4], refs[4:8]
    meta_ref, x_ref, ada_ref = refs[8:11]
    rest = refs[11:]
    if final:
        fgain_ref, ys_ref, oc_ref, ol_ref, y_ref, sem = rest
    else:
        ys_ref, o_ref, y_ref, sem = rest
    step = pl.program_id(0)
    n_steps = pl.num_programs(0)

    def fetch(slot_refs, buf):
        slots = ((slot_refs[0], slot_refs[1]), (slot_refs[2], slot_refs[3]))

        def copy(blk, j, k, lane):
            i = blk * ROW_UNROLL + j
            return pltpu.make_async_copy(_packed_row(ys_ref, slots[k][0][i], slots[k][1][i]),
                                         _packed_row(y_ref.at[buf, k], blk, j), sem.at[buf, lane])

        _start_row_copies(tm, copy)

    @pl.when(step == 0)
    def _():
        fetch(cur_slots, 0)

    @pl.when(step + 1 < n_steps)
    def _():
        fetch(nxt_slots, (step + 1) % 2)

    buf = step % 2
    _drain_row_copies(ys_ref.at[pl.ds(0, tm // 8)], y_ref.at[buf, 0], sem.at[buf])
    y_ref = y_ref.at[buf]
    meta = meta_ref[...]
    mix = (meta[:, META_W1:META_W1 + 1] * _load_unpacked(y_ref.at[0])
           + meta[:, META_W2:META_W2 + 1] * _load_unpacked(y_ref.at[1]))
    x = x_ref[...] + ada_ref[0, 5:6, :] * mix
    if not final:
        o_ref[...] = x
        return
    ms = jnp.mean(x * x, axis=-1, keepdims=True)
    y = x * lax.rsqrt(ms + EPS) * fgain_ref[...]
    is_ctx = pl.program_id(0) < N_CTX // tm

    @pl.when(is_ctx)
    def _():
        oc_ref[...] = y

    @pl.when(jnp.logical_not(is_ctx))
    def _():
        ol_ref[...] = y


def _combine(slots, ys, meta, x, ada, final_gain=None, *, tm=512):
    final = final_gain is not None
    extra_in = [pl.BlockSpec((1, D), lambda t: (0, 0))] if final else []
    extra_args = [final_gain.reshape(1, D)] if final else []
    if final:
        out_specs = _split_specs(tm, D)
        out_shape = [jax.ShapeDtypeStruct((N_CTX, D), F32), jax.ShapeDtypeStruct((N_LAT, D), F32)]
    else:
        out_specs = pl.BlockSpec((tm, D), lambda t: (t, 0))
        out_shape = jax.ShapeDtypeStruct((N_TOK, D), F32)
    return pl.pallas_call(
        functools.partial(_combine_body, tm=tm, final=final),
        grid=(N_TOK // tm,),
        in_specs=_slot_specs(tm) + _slot_specs(tm, ahead=1) + [
            pl.BlockSpec((tm, ROUTE_W), lambda t: (t, 0)),
            pl.BlockSpec((tm, D), lambda t: (t, 0)),
            pl.BlockSpec((1, 6, D), lambda t: (_cond_row(t, tm), 0, 0)),
        ] + extra_in + [pl.BlockSpec(memory_space=pl.ANY)],
        out_specs=out_specs,
        out_shape=out_shape,
        scratch_shapes=[pltpu.VMEM((2, 2) + _packed_shape(tm), jnp.uint32),
                        pltpu.SemaphoreType.DMA((2, N_DMA_LANES))],
        compiler_params=_cparams(("arbitrary",)),
        name="moe_combine",
    )(*slots, *slots, meta, x, ada, *extra_args, ys)


def _moe(routed, ada, w_gate, w_up, w_down, *, layer, sort_init, final_gain=None):
    x, hq, meta, cnt = routed
    pos, plan = _slots(meta, cnt)
    slots = [pos[:, r, :].reshape(N_TOK) for r in range(4)]
    xs = _dispatch(slots, hq, sort_init)
    ys = _experts(plan, xs, w_gate, w_up, w_down, layer=layer)
    return _combine(slots, ys, meta, x, ada, final_gain), ys


def kernel(x_prompt, x_sample, cache_k, cache_v, c, c_ctx, w_ada, b_ada, norm_mix, norm_ffn, attn_w_q, attn_w_kv, attn_q_norm, attn_k_norm, attn_w_o, hy_w_in, hy_b_in, hy_short_w, hy_short_b, hy_filt_w1, hy_filt_b1, hy_filt_w2, hy_filt_b2, hy_filt_w3, hy_filt_b3, hy_filt_freq, hy_log_decay, hy_skip, hy_w_out, hy_b_out, router_group_w, router_group_b, router_expert_w, router_expert_b, moe_w_gate, moe_w_up, moe_w_down, final_norm):
    depth = w_ada.shape[0]
    x_parts = (x_prompt.reshape(N_CTX, D), x_sample.reshape(N_LAT, D))
    cond8 = jnp.zeros((8, D), F32).at[0].set(c_ctx).at[1:1 + N_BATCH_LAT].set(c)
    ada_all = _ada(cond8, w_ada, b_ada).reshape(depth, 8, 6, D)
    new_k = []
    new_v = []
    sort_buf = jnp.zeros(_packed_shape(SORT_ROWS), jnp.uint32)
    for l in range(depth):
        ada = ada_all[l]
        if l % 2 == 0:
            a = l // 2
            w_qkv = jnp.concatenate([attn_w_q[a], attn_w_kv[a]], axis=1).astype(BF16)
            q, k_ctx, v_ctx, kc, vc, kl, vl = _qkv(x_parts, ada, norm_mix[l], w_qkv, attn_q_norm[a], attn_k_norm[a])
            new_k.append(k_ctx)
            new_v.append(v_ctx)
            o_ctx = _attention(q, [kc], [vc], row0=0, n_batch=N_BATCH_CTX, seq=L_CTX, tq=L_CTX,
                               n_kv=N_KV, stack=Q_PER_KV)
            past_keys = _lane_tile_heads(cache_k[:, a]).transpose(0, 1, 3, 2)
            o_lat = _attention(q, [kl, past_keys], [vl, _lane_tile_heads(cache_v[:, a])],
                               row0=N_CTX, n_batch=N_BATCH_LAT, seq=L_LAT, tq=1024, n_kv=1, stack=1)
            mixer = (o_ctx, o_lat, attn_w_o[a].astype(BF16), jnp.zeros((D,), F32))
        else:
            j = l // 2
            (x,) = x_parts
            up = _norm_proj(x, ada, norm_mix[l], hy_w_in[j].astype(BF16), hy_b_in[j], part=0)
            outs = []
            for row0, n_batch, L, tc in ((0, N_BATCH_CTX, L_CTX, D), (N_CTX, N_BATCH_LAT, L_LAT, 256)):
                cs = jnp.asarray(_dft_table(L)).astype(BF16)
                fr, fi, fn = _filter_spectra(L, cs, hy_filt_w1[j], hy_filt_b1[j], hy_filt_w2[j], hy_filt_b2[j],
                                             hy_filt_w3[j], hy_filt_b3[j], hy_filt_freq[j], hy_log_decay[j])
                outs.append(_hyena_conv(up, hy_short_w[j], hy_short_b[j], hy_skip[j], fr, fi, fn, cs,
                                        row0=row0, n_batch=n_batch, L=L, tc=tc))
            mixer = (outs[0], outs[1], hy_w_out[j].astype(BF16), hy_b_out[j])
        routed = _mix_route(*mixer, x_parts, ada, norm_ffn[l], router_group_w[l], router_group_b[l],
                            router_expert_w[l], router_expert_b[l])
        out, sort_buf = _moe(routed, ada, moe_w_gate, moe_w_up, moe_w_down, layer=l, sort_init=sort_buf,
                             final_gain=final_norm if l == depth - 1 else None)
        x_parts = (out,)
    y_ctx, y_lat = x_parts[0]
    return (y_ctx.reshape(N_BATCH_CTX, L_CTX, D), y_lat.reshape(N_BATCH_LAT, L_LAT, D),
            jnp.stack(new_k, axis=1), jnp.stack(new_v, axis=1))
```

```python
import functools
import math

import numpy as np
import jax
import jax.numpy as jnp
from jax import lax
from jax.experimental import pallas as pl
from jax.experimental.pallas import tpu as pltpu

F32 = jnp.float32
BF16 = jnp.bfloat16

D = 1024
N_BATCH_CTX = 32
L_CTX = 256
N_BATCH_LAT = 2
L_LAT = 2048
PAST = 256
N_CTX = N_BATCH_CTX * L_CTX
N_LAT = N_BATCH_LAT * L_LAT
N_TOK = N_CTX + N_LAT
GRID_W = 64
N_HEADS = 16
N_KV = 4
DH = 64
Q_PER_KV = N_HEADS // N_KV
KV_W = N_KV * DH
ROPE_THETA = 10000.0
FILTER_FEAT = 17
FEAT_PAD = 32
FILTER_HIDDEN = 64
N_GROUPS = 4
E_PER_G = 8
N_EXP = N_GROUPS * E_PER_G
EXP_H = D // 4
EPS = 1e-6
ROUTE_W = 128
VMEM_LIMIT = 56 * 1024 * 1024


def _cparams(sem):
    return pltpu.CompilerParams(dimension_semantics=sem, vmem_limit_bytes=VMEM_LIMIT)


def _cond_row(i, tm):
    n_ctx_tiles = N_CTX // tm
    return jnp.where(i < n_ctx_tiles, 0, 1 + (i - n_ctx_tiles) // (L_LAT // tm))


def _split_specs(tm, width):
    n_ctx_tiles = N_CTX // tm
    return [pl.BlockSpec((tm, width), lambda i: (jnp.minimum(i, n_ctx_tiles - 1), 0)),
            pl.BlockSpec((tm, width), lambda i: (jnp.maximum(i - n_ctx_tiles, 0), 0))]


def _pick_split(i, tm, ctx_ref, lat_ref):
    return jnp.where(i < N_CTX // tm, ctx_ref[...], lat_ref[...])


def _sigmoid(x):
    return 1.0 / (1.0 + jnp.exp(-x))


@functools.lru_cache(maxsize=None)
def _dft_table(L):
    k = np.arange(L, dtype=np.int64)
    ang = (np.outer(k, k) % (2 * L)).astype(np.float64) * (math.pi / L)
    return np.concatenate([np.cos(ang), np.sin(ang)], axis=0).astype(np.float32)


@functools.lru_cache(maxsize=None)
def _filter_feats(L):
    t = np.linspace(0.0, 1.0, L, dtype=np.float64)[:, None]
    bands = np.linspace(1e-4, 7.0, 8, dtype=np.float64)[None, :]
    w = (2.0 * math.pi) * np.arange(L, dtype=np.float64)[:, None] / L
    feats = np.concatenate([t, np.cos(bands * w), -np.sin(bands * w)], axis=-1)
    out = np.zeros((L, FEAT_PAD), np.float32)
    out[:, :FILTER_FEAT] = feats
    return out


@functools.lru_cache(maxsize=None)
def _rope_tables(tm):
    pos = np.arange(L_LAT)
    row = (pos // GRID_W).astype(np.float64)
    col = (pos % GRID_W).astype(np.float64)
    axis_dim = DH // 2
    inv_freq = ROPE_THETA ** (-np.arange(0, axis_dim, 2, dtype=np.float64) / axis_dim)
    lane = np.arange(KV_W)
    d = lane % DH
    is_col = (d // axis_dim) == 1
    fi = d % (axis_dim // 2)
    first_half = (d % axis_dim) < (axis_dim // 2)
    p = np.where(is_col[None, :], col[:, None], row[:, None])
    ang = p * inv_freq[fi][None, :]
    cos = np.cos(ang)
    sin = np.sin(ang) * np.where(first_half, -1.0, 1.0)[None, :]
    cos = np.concatenate([np.ones((tm, KV_W)), cos], axis=0).astype(np.float32)
    sin = np.concatenate([np.zeros((tm, KV_W)), sin], axis=0).astype(np.float32)
    return cos, sin


@functools.lru_cache(maxsize=None)
def _head_sum_matrix():
    lane = np.arange(KV_W)
    return (lane[:, None] // DH == lane[None, :] // DH).astype(np.float32)


def _ada_body(c_ref, w_ref, b_ref, o_ref):
    c = c_ref[...]
    s = c * _sigmoid(c)
    o_ref[0] = jnp.dot(s.astype(BF16), w_ref[0].astype(BF16), preferred_element_type=F32) + b_ref[0]


def _ada(cond8, w_ada, b_ada):
    depth = w_ada.shape[0]
    tn = 1536
    return pl.pallas_call(
        _ada_body,
        grid=(depth, 6 * D // tn),
        in_specs=[
            pl.BlockSpec((8, D), lambda l, j: (0, 0)),
            pl.BlockSpec((1, D, tn), lambda l, j: (l, 0, j)),
            pl.BlockSpec((1, 1, tn), lambda l, j: (l, 0, j)),
        ],
        out_specs=pl.BlockSpec((1, 8, tn), lambda l, j: (l, 0, j)),
        out_shape=jax.ShapeDtypeStruct((depth, 8, 6 * D), F32),
        compiler_params=_cparams(("arbitrary", "arbitrary")),
        name="ada",
    )(cond8, w_ada, b_ada.reshape(depth, 1, 6 * D))


def _norm_mod(x, gain_ref, ada_ref, part):
    ms = jnp.mean(x * x, axis=-1, keepdims=True)
    y = x * lax.rsqrt(ms + EPS) * gain_ref[...]
    return y * (1.0 + ada_ref[0, part + 1:part + 2, :]) + ada_ref[0, part:part + 1, :]


def _proj_body(x_ref, ada_ref, gain_ref, w_ref, b_ref, o_ref, *, part):
    h = _norm_mod(x_ref[...], gain_ref, ada_ref, part)
    acc = jnp.dot(h.astype(BF16), w_ref[...], preferred_element_type=F32) + b_ref[...]
    o_ref[...] = acc.astype(o_ref.dtype)


def _norm_proj(x, ada, gain, w, b, *, part, tm=1024):
    n_out = w.shape[1]
    return pl.pallas_call(
        functools.partial(_proj_body, part=part),
        grid=(N_TOK // tm,),
        in_specs=[
            pl.BlockSpec((tm, D), lambda i: (i, 0)),
            pl.BlockSpec((1, 6, D), lambda i: (_cond_row(i, tm), 0, 0)),
            pl.BlockSpec((1, D), lambda i: (0, 0)),
            pl.BlockSpec((D, n_out), lambda i: (0, 0)),
            pl.BlockSpec((1, n_out), lambda i: (0, 0)),
        ],
        out_specs=pl.BlockSpec((tm, n_out), lambda i: (i, 0)),
        out_shape=jax.ShapeDtypeStruct((N_TOK, n_out), BF16),
        compiler_params=_cparams(("arbitrary",)),
        name="norm_proj",
    )(x, ada, gain.reshape(1, D), w, b.reshape(1, n_out))


def _head_rms_rope(x, gain, hs, cos, sin, lane):
    ss = jnp.dot((x * x).astype(BF16), hs, preferred_element_type=F32)
    xn = x * lax.rsqrt(ss * (1.0 / DH) + EPS) * gain
    fwd = pltpu.roll(xn, KV_W - DH // 4, 1)
    bwd = pltpu.roll(xn, DH // 4, 1)
    partner = jnp.where((lane & (DH // 4)) == 0, fwd, bwd)
    return xn * cos + partner * sin


@functools.lru_cache(maxsize=None)
def _lane_tile_matrix():
    col = np.arange(N_KV * KV_W)
    src = (col // KV_W) * DH + col % DH
    return (np.arange(KV_W)[:, None] == src[None, :]).astype(np.float32)


def _qkv_body(*refs, tm, split_x):
    n_x = 2 if split_x else 1
    x_refs = refs[:n_x]
    (ada_ref, gain_ref, w_ref, qg_ref, kg_ref, hs_ref, cos_ref, sin_ref, tile_ref, tile_t_ref,
     q_ref, newk_ref, newv_ref, kc_ref, vc_ref, kl_ref, vl_ref) = refs[n_x:]
    step = pl.program_id(0)
    x = _pick_split(step, tm, *x_refs) if split_x else x_refs[0][...]
    h = _norm_mod(x, gain_ref, ada_ref, 0)
    acc = jnp.dot(h.astype(BF16), w_ref[...], preferred_element_type=F32)
    hs = hs_ref[...]
    cos = cos_ref[...]
    sin = sin_ref[...]
    lane = lax.broadcasted_iota(jnp.int32, (1, KV_W), 1)
    for c in range(N_KV):
        qc = _head_rms_rope(acc[:, c * KV_W:(c + 1) * KV_W], qg_ref[...], hs, cos, sin, lane)
        q_ref[:, c * KV_W:(c + 1) * KV_W] = (qc * (DH ** -0.5)).astype(BF16)
    k = _head_rms_rope(acc[:, D:D + KV_W], kg_ref[...], hs, cos, sin, lane)
    v = acc[:, D + KV_W:D + 2 * KV_W]
    k4t = lax.dot_general(tile_t_ref[...], k.astype(BF16), (((1,), (1,)), ((), ())),
                          preferred_element_type=F32).astype(BF16)
    v4 = jnp.dot(v.astype(BF16), tile_ref[...], preferred_element_type=F32).astype(BF16)
    is_ctx = step < N_CTX // tm

    @pl.when(is_ctx)
    def _():
        for bb in range(tm // L_CTX):
            rows = slice(bb * L_CTX, (bb + 1) * L_CTX)
            for hd in range(N_KV):
                newk_ref[bb, hd] = k[rows, hd * DH:(hd + 1) * DH]
                newv_ref[bb, hd] = v[rows, hd * DH:(hd + 1) * DH]
                kc_ref[bb, hd] = k4t[hd * KV_W:(hd + 1) * KV_W, rows]
                vc_ref[bb, hd] = v4[rows, hd * KV_W:(hd + 1) * KV_W]

    @pl.when(jnp.logical_not(is_ctx))
    def _():
        for hd in range(N_KV):
            kl_ref[0, hd] = k4t[hd * KV_W:(hd + 1) * KV_W, :]
            vl_ref[0, hd] = v4[:, hd * KV_W:(hd + 1) * KV_W]


def _qkv(x_parts, ada, gain, w_qkv, q_gain, k_gain):
    tm = 2 * L_CTX
    per_tile = tm // L_CTX
    cos, sin = _rope_tables(tm)
    split_x = len(x_parts) == 2
    x_specs = _split_specs(tm, D) if split_x else [pl.BlockSpec((tm, D), lambda i: (i, 0))]
    n_ctx_tiles = N_CTX // tm
    lat_tiles = L_LAT // tm

    def rope_idx(i):
        return (jnp.where(i < n_ctx_tiles, 0, 1 + (i - n_ctx_tiles) % lat_tiles), 0)

    def ctx_idx(i):
        return (jnp.minimum(i, n_ctx_tiles - 1), 0, 0, 0)

    def lat_idx(i):
        j = jnp.maximum(i - n_ctx_tiles, 0)
        return (j // lat_tiles, 0, j % lat_tiles, 0)

    def lat_idx_t(i):
        j = jnp.maximum(i - n_ctx_tiles, 0)
        return (j // lat_tiles, 0, 0, j % lat_tiles)

    const = lambda i: (0, 0)
    kv_ctx = jax.ShapeDtypeStruct((N_BATCH_CTX, N_KV, L_CTX, DH), F32)
    tiled_ctx = jax.ShapeDtypeStruct((N_BATCH_CTX, N_KV, L_CTX, KV_W), BF16)
    tiled_lat = jax.ShapeDtypeStruct((N_BATCH_LAT, N_KV, L_LAT, KV_W), BF16)
    keys_ctx = jax.ShapeDtypeStruct((N_BATCH_CTX, N_KV, KV_W, L_CTX), BF16)
    keys_lat = jax.ShapeDtypeStruct((N_BATCH_LAT, N_KV, KV_W, L_LAT), BF16)
    return pl.pallas_call(
        functools.partial(_qkv_body, tm=tm, split_x=split_x),
        grid=(N_TOK // tm,),
        in_specs=x_specs + [
            pl.BlockSpec((1, 6, D), lambda i: (_cond_row(i, tm), 0, 0)),
            pl.BlockSpec((1, D), const),
            pl.BlockSpec((D, D + 2 * KV_W), const),
            pl.BlockSpec((1, KV_W), const),
            pl.BlockSpec((1, KV_W), const),
            pl.BlockSpec((KV_W, KV_W), const),
            pl.BlockSpec((tm, KV_W), rope_idx),
            pl.BlockSpec((tm, KV_W), rope_idx),
            pl.BlockSpec((KV_W, N_KV * KV_W), const),
            pl.BlockSpec((N_KV * KV_W, KV_W), const),
        ],
        out_specs=[
            pl.BlockSpec((tm, D), lambda i: (i, 0)),
            pl.BlockSpec((per_tile, N_KV, L_CTX, DH), ctx_idx),
            pl.BlockSpec((per_tile, N_KV, L_CTX, DH), ctx_idx),
            pl.BlockSpec((per_tile, N_KV, KV_W, L_CTX), ctx_idx),
            pl.BlockSpec((per_tile, N_KV, L_CTX, KV_W), ctx_idx),
            pl.BlockSpec((1, N_KV, KV_W, tm), lat_idx_t),
            pl.BlockSpec((1, N_KV, tm, KV_W), lat_idx),
        ],
        out_shape=[jax.ShapeDtypeStruct((N_TOK, D), BF16), kv_ctx, kv_ctx, keys_ctx, tiled_ctx, keys_lat, tiled_lat],
        compiler_params=_cparams(("arbitrary",)),
        name="qkv",
    )(*x_parts, ada, gain.reshape(1, D), w_qkv,
      jnp.tile(q_gain, Q_PER_KV).reshape(1, KV_W), jnp.tile(k_gain, N_KV).reshape(1, KV_W),
      jnp.asarray(_head_sum_matrix()).astype(BF16), jnp.asarray(cos), jnp.asarray(sin),
      jnp.asarray(_lane_tile_matrix()).astype(BF16), jnp.asarray(_lane_tile_matrix().T.copy()).astype(BF16))


def _attn_body(*refs, n_kv, stack, n_seg):
    q_ref, k_refs, v_refs, o_ref = refs[0], refs[1:1 + n_seg], refs[1 + n_seg:1 + 2 * n_seg], refs[-1]
    lane = lax.broadcasted_iota(jnp.int32, (1, KV_W), 1)
    masks = [(lane >> 6) == g for g in range(Q_PER_KV)]
    tq = q_ref.shape[0]
    for kv in range(n_kv):
        q = q_ref[:, kv * KV_W:(kv + 1) * KV_W]
        out = jnp.zeros((tq, KV_W), F32)
        for c in range(0, Q_PER_KV, stack):
            pair = masks[c:c + stack]
            ones_blk = (c + stack) % Q_PER_KV if stack < Q_PER_KV else None
            stacked = jnp.concatenate([jnp.where(m, q, jnp.zeros_like(q)) for m in pair], axis=0)
            scores = [jnp.dot(stacked, k_ref[0, kv], preferred_element_type=F32) for k_ref in k_refs]
            top = functools.reduce(jnp.maximum, [jnp.max(s, axis=-1, keepdims=True) for s in scores])
            probs = [jnp.exp((s - top).astype(BF16)) for s in scores]
            if ones_blk is None:
                vals = [v_ref[0, kv] for v_ref in v_refs]
            else:
                vals = [jnp.where(masks[ones_blk], jnp.ones((), BF16), v_ref[0, kv]) for v_ref in v_refs]
            og = sum(jnp.dot(p, v, preferred_element_type=F32) for p, v in zip(probs, vals))
            if ones_blk is None:
                denom = sum(jnp.sum(p.astype(F32), axis=-1, keepdims=True) for p in probs)
            else:
                denom = og[:, ones_blk * DH:ones_blk * DH + 1]
            og = og * (1.0 / denom)
            for g, m in enumerate(pair):
                out = out + jnp.where(m, og[g * tq:(g + 1) * tq], 0.0)
        o_ref[:, kv * KV_W:(kv + 1) * KV_W] = out.astype(BF16)


def _attention(q, keys, values, *, row0, n_batch, seq, tq, n_kv, stack):
    per_b = seq // tq
    base = row0 // tq
    kv_specs = [pl.BlockSpec((1, n_kv) + a.shape[2:], lambda b, h, i: (b, h, 0, 0)) for a in keys + values]
    return pl.pallas_call(
        functools.partial(_attn_body, n_kv=n_kv, stack=stack, n_seg=len(keys)),
        grid=(n_batch, N_KV // n_kv, per_b),
        in_specs=[pl.BlockSpec((tq, n_kv * KV_W), lambda b, h, i: (base + b * per_b + i, h))] + kv_specs,
        out_specs=pl.BlockSpec((tq, n_kv * KV_W), lambda b, h, i: (b * per_b + i, h)),
        out_shape=jax.ShapeDtypeStruct((n_batch * seq, D), BF16),
        compiler_params=_cparams(("arbitrary", "arbitrary", "arbitrary")),
        name="attention",
    )(q, *keys, *values)


def _lane_tile_heads(x):
    return jnp.tile(x.astype(BF16), (1, 1, 1, KV_W // DH))


def _alt_sign(L):
    row = lax.broadcasted_iota(jnp.int32, (L, 1), 0)
    return row, jnp.where((row & 1) == 0, 1.0, -1.0)


def _split_dot(a, b):
    a_hi, b_hi = a.astype(BF16), b.astype(BF16)
    a_lo = (a - a_hi.astype(F32)).astype(BF16)
    b_lo = (b - b_hi.astype(F32)).astype(BF16)
    return (jnp.dot(a_hi, b_hi, preferred_element_type=F32) + jnp.dot(a_lo, b_hi, preferred_element_type=F32)
            + jnp.dot(a_hi, b_lo, preferred_element_type=F32))


def _filter_body(feat_ref, w1_ref, b1_ref, w2_ref, b2_ref, fq_ref, w3_ref, b3_ref, ld_ref, cs_ref,
                 fr_ref, fi_ref, fn_ref, hid_ref, *, L):
    feats = feat_ref[...]

    @pl.when(pl.program_id(0) == 0)
    def _():
        fq = fq_ref[...]
        h1 = jnp.sin(fq * (_split_dot(feats, w1_ref[...]) + b1_ref[...]))
        hid_ref[...] = jnp.sin(fq * (_split_dot(h1, w2_ref[...]) + b2_ref[...])).astype(BF16)

    t = feats[:, 0:1]
    row, alt = _alt_sign(L)
    filt = []
    for j in range(4):
        raw = jnp.dot(hid_ref[...], w3_ref[j].astype(BF16), preferred_element_type=F32) + b3_ref[j:j + 1, :]
        filt.append(raw * jnp.exp(-t * jnp.exp(ld_ref[j:j + 1, :])))
    for o in range(2):
        hf, hb = filt[2 * o], filt[2 * o + 1]
        l1 = jnp.sum(jnp.abs(hf), axis=0, keepdims=True) + jnp.sum(jnp.abs(hb), axis=0, keepdims=True)
        inv = 1.0 / (l1 + EPS)
        sym = (hf + hb) * inv
        asym = (hb - hf) * inv
        fr = jnp.dot(cs_ref[0:L, :], sym.astype(BF16), preferred_element_type=F32) * (1.0 / L)
        fr_ref[o] = jnp.where(row == 0, 0.5 * fr, fr)
        fi_ref[o] = jnp.dot(cs_ref[L:2 * L, :], asym.astype(BF16), preferred_element_type=F32) * (1.0 / L)
        fn_ref[o] = jnp.sum(sym * alt, axis=0, keepdims=True) * (0.5 / L)


def _filter_spectra(L, cs, w1, b1, w2, b2, w3, b3, freq, log_decay, *, tc=256):
    w1p = jnp.zeros((FEAT_PAD, FILTER_HIDDEN), F32).at[:FILTER_FEAT].set(w1)
    w3r = w3.reshape(FILTER_HIDDEN, 4, D).transpose(1, 0, 2)
    const = lambda j: (0, 0)
    return pl.pallas_call(
        functools.partial(_filter_body, L=L),
        grid=(D // tc,),
        in_specs=[
            pl.BlockSpec((L, FEAT_PAD), const),
            pl.BlockSpec((FEAT_PAD, FILTER_HIDDEN), const),
            pl.BlockSpec((1, FILTER_HIDDEN), const),
            pl.BlockSpec((FILTER_HIDDEN, FILTER_HIDDEN), const),
            pl.BlockSpec((1, FILTER_HIDDEN), const),
            pl.BlockSpec((1, FILTER_HIDDEN), const),
            pl.BlockSpec((4, FILTER_HIDDEN, tc), lambda j: (0, 0, j)),
            pl.BlockSpec((4, tc), lambda j: (0, j)),
            pl.BlockSpec((4, tc), lambda j: (0, j)),
            pl.BlockSpec((2 * L, L), const, pipeline_mode=pl.Buffered(1)),
        ],
        out_specs=[
            pl.BlockSpec((2, L, tc), lambda j: (0, 0, j)),
            pl.BlockSpec((2, L, tc), lambda j: (0, 0, j)),
            pl.BlockSpec((2, 1, tc), lambda j: (0, 0, j)),
        ],
        out_shape=[
            jax.ShapeDtypeStruct((2, L, D), F32),
            jax.ShapeDtypeStruct((2, L, D), F32),
            jax.ShapeDtypeStruct((2, 1, D), F32),
        ],
        scratch_shapes=[pltpu.VMEM((L, FILTER_HIDDEN), BF16)],
        compiler_params=_cparams(("arbitrary",)),
        name="hyena_filter",
    )(jnp.asarray(_filter_feats(L)), w1p, b1.reshape(1, -1), w2, b2.reshape(1, -1), freq.reshape(1, -1),
      w3r, b3.reshape(4, D), log_decay.reshape(4, D), cs)


FREQ_CHUNK = 1024


def _hyconv_body(x1_ref, x2_ref, v_ref, sw_ref, sb_ref, skip_ref, fr_ref, fi_ref, fn_ref, cs_ref, o_ref,
                 z_ref, g_ref, zb_ref, yr_ref, yi_ref, *, L):
    row, alt = _alt_sign(L)
    kc = min(FREQ_CHUNK, L)
    chunk_row = lax.broadcasted_iota(jnp.int32, (kc, 1), 0)

    def short_conv(u_ref, p):
        u = u_ref[...].astype(F32)
        prev = jnp.where(row == 0, 0.0, pltpu.roll(u, 1, 0))
        nxt = jnp.where(row == L - 1, 0.0, pltpu.roll(u, L - 1, 0))
        return (prev * sw_ref[0, p:p + 1, :] + u * sw_ref[1, p:p + 1, :] + nxt * sw_ref[2, p:p + 1, :]
                + sb_ref[p:p + 1, :])

    z_ref[...] = short_conv(v_ref, 2)
    g_ref[0] = short_conv(x1_ref, 0)
    g_ref[1] = short_conv(x2_ref, 1)
    for o in range(2):
        z = z_ref[...]
        zb_ref[...] = z.astype(BF16)
        nyq = jnp.sum(z * alt, axis=0, keepdims=True) * fn_ref[o]

        def to_freq(c, carry):
            lo = pl.multiple_of(c * kc, kc)
            zr = jnp.dot(cs_ref[pl.ds(lo, kc), :], zb_ref[...], preferred_element_type=F32)
            zs = jnp.dot(cs_ref[pl.ds(L + lo, kc), :], zb_ref[...], preferred_element_type=F32)
            fr = fr_ref[o, pl.ds(lo, kc), :]
            fi = fi_ref[o, pl.ds(lo, kc), :]
            yr_ref[pl.ds(lo, kc), :] = (zr * fr + zs * fi).astype(BF16)
            yi_ref[pl.ds(lo, kc), :] = (zr * fi - zs * fr).astype(BF16)
            return carry

        lax.fori_loop(0, L // kc, to_freq, 0, unroll=True)

        def to_time(c, carry):
            lo = pl.multiple_of(c * kc, kc)
            y = (jnp.dot(cs_ref[pl.ds(lo, kc), :], yr_ref[...], preferred_element_type=F32)
                 - jnp.dot(cs_ref[pl.ds(L + lo, kc), :], yi_ref[...], preferred_element_type=F32))
            y = y + jnp.where(((chunk_row + lo) & 1) == 0, nyq, -nyq)
            z_ref[pl.ds(lo, kc), :] = g_ref[o, pl.ds(lo, kc), :] * (y + skip_ref[o:o + 1, :] * z_ref[pl.ds(lo, kc), :])
            return carry

        lax.fori_loop(0, L // kc, to_time, 0, unroll=True)
    o_ref[...] = z_ref[...].astype(BF16)


def _hyena_conv(up, sw, sb, skip, fr, fi, fn, cs, *, row0, n_batch, L, tc):
    n_ct = D // tc
    base = row0 // L
    u_spec = lambda p: pl.BlockSpec((L, tc), lambda j, b: (base + b, p * n_ct + j))
    once = pl.Buffered(1)
    return pl.pallas_call(
        functools.partial(_hyconv_body, L=L),
        grid=(n_ct, n_batch),
        in_specs=[
            u_spec(0), u_spec(1), u_spec(2),
            pl.BlockSpec((3, 3, tc), lambda j, b: (0, 0, j)),
            pl.BlockSpec((3, tc), lambda j, b: (0, j)),
            pl.BlockSpec((2, tc), lambda j, b: (0, j)),
            pl.BlockSpec((2, L, tc), lambda j, b: (0, 0, j), pipeline_mode=once),
            pl.BlockSpec((2, L, tc), lambda j, b: (0, 0, j), pipeline_mode=once),
            pl.BlockSpec((2, 1, tc), lambda j, b: (0, 0, j)),
            pl.BlockSpec((2 * L, L), lambda j, b: (0, 0), pipeline_mode=once),
        ],
        out_specs=pl.BlockSpec((L, tc), lambda j, b: (b, j)),
        out_shape=jax.ShapeDtypeStruct((n_batch * L, D), BF16),
        scratch_shapes=[
            pltpu.VMEM((L, tc), F32),
            pltpu.VMEM((2, L, tc), F32),
            pltpu.VMEM((L, tc), BF16),
            pltpu.VMEM((L, tc), BF16),
            pltpu.VMEM((L, tc), BF16),
        ],
        compiler_params=_cparams(("arbitrary", "arbitrary")),
        name="hyena_conv",
    )(up, up, up, sw.reshape(3, 3, D), sb.reshape(3, D), skip, fr, fi, fn, cs)


EXP_TILE = 512
SORT_TILES = (2 * N_TOK) // EXP_TILE + N_EXP
SORT_ROWS = SORT_TILES * EXP_TILE
N_QUARTERS = 4
QUARTER_W = D // (2 * N_QUARTERS)
META_I1, META_I2, META_R1, META_R2, META_W1, META_W2 = range(6)
HI_HALF = 0xFFFF0000


def _pack_pairs(x):
    bits = pltpu.bitcast(x.astype(BF16).astype(F32), jnp.uint32)
    return (bits[:, :QUARTER_W] >> 16) | bits[:, QUARTER_W:]


def _unpack_pairs(w):
    return pltpu.bitcast(w << 16, F32), pltpu.bitcast(w & jnp.uint32(HI_HALF), F32)


def _packed_shape(rows):
    return (rows // 8, N_QUARTERS, 8, QUARTER_W)


def _store_packed(ref, x):
    for q in range(N_QUARTERS):
        ref[:, q] = _pack_pairs(x[:, q * 2 * QUARTER_W:(q + 1) * 2 * QUARTER_W]).reshape(-1, 8, QUARTER_W)


def _load_unpacked(ref):
    halves = []
    for q in range(N_QUARTERS):
        halves.extend(_unpack_pairs(ref[:, q].reshape(-1, QUARTER_W)))
    return jnp.concatenate(halves, axis=1)


def _packed_row(ref, tile, sublane):
    return ref.at[tile, :, sublane, :]


def _lane_put(lane, values):
    out = jnp.where(lane == 0, values[0], 0.0)
    for k in range(1, len(values)):
        out = out + jnp.where(lane == k, values[k], 0.0)
    return out


def _route_body(*refs, tm, split_x):
    n_x = 2 if split_x else 1
    ac_ref, al_ref, w_ref, b_ref = refs[:4]
    x_refs = refs[4:4 + n_x]
    (ada_ref, gain_ref, wr_hi_ref, wr_lo_ref, br_ref, tri_ref,
     xo_ref, hq_ref, meta_ref, cnt_ref, carry_ref) = refs[4 + n_x:]
    step = pl.program_id(0)

    @pl.when(step == 0)
    def _():
        carry_ref[...] = jnp.zeros_like(carry_ref)

    mixed = jnp.dot(_pick_split(step, tm, ac_ref, al_ref), w_ref[...], preferred_element_type=F32) + b_ref[...]
    x = _pick_split(step, tm, *x_refs) if split_x else x_refs[0][...]
    x = x + ada_ref[0, 2:3, :] * mixed
    xo_ref[...] = x
    h = _norm_mod(x, gain_ref, ada_ref, 3)
    _store_packed(hq_ref, h)
    h_hi = h.astype(BF16)
    h_lo = (h - h_hi.astype(F32)).astype(BF16)
    logits = (jnp.dot(h_hi, wr_hi_ref[...], preferred_element_type=F32)
              + jnp.dot(h_lo, wr_hi_ref[...], preferred_element_type=F32)
              + jnp.dot(h_hi, wr_lo_ref[...], preferred_element_type=F32) + br_ref[...])
    lane = lax.broadcasted_iota(jnp.int32, (1, ROUTE_W), 1)
    lane_f = lane.astype(F32)
    group_of_lane = (lane >> 3).astype(F32)
    neg = -jnp.inf
    big = float(ROUTE_W)
    is_g = (lane >= N_EXP) & (lane < N_EXP + N_GROUPS)
    gl = jnp.where(is_g, logits, neg)
    gmax = jnp.max(gl, axis=-1, keepdims=True)
    gidx = jnp.min(jnp.where(gl == gmax, lane_f - N_EXP, big), axis=-1, keepdims=True)
    g_top = 1.0 / jnp.sum(jnp.where(is_g, jnp.exp(gl - gmax), 0.0), axis=-1, keepdims=True)
    in_group = (lane < N_EXP) & (group_of_lane == gidx)
    el = jnp.where(in_group, logits, neg)
    v1 = jnp.max(el, axis=-1, keepdims=True)
    i1 = jnp.min(jnp.where(el == v1, lane_f, big), axis=-1, keepdims=True)
    el2 = jnp.where(lane_f == i1, neg, el)
    v2 = jnp.max(el2, axis=-1, keepdims=True)
    i2 = jnp.min(jnp.where(el2 == v2, lane_f, big), axis=-1, keepdims=True)
    r = jnp.exp(v2 - v1)
    w1 = g_top / (1.0 + r)
    w2 = g_top * r / (1.0 + r)
    sel = jnp.where((lane_f == i1) | (lane_f == i2), 1.0, 0.0)
    rank = jnp.dot(tri_ref[...], sel.astype(BF16), preferred_element_type=F32) + carry_ref[...]
    r1 = jnp.sum(jnp.where(lane_f == i1, rank, 0.0), axis=-1, keepdims=True)
    r2 = jnp.sum(jnp.where(lane_f == i2, rank, 0.0), axis=-1, keepdims=True)
    carry_ref[...] += jnp.sum(sel, axis=0, keepdims=True)
    cnt_ref[...] = carry_ref[...]
    meta_ref[...] = _lane_put(lane, (i1, i2, r1, r2, w1, w2))


@functools.lru_cache(maxsize=None)
def _strict_lower(n):
    r = np.arange(n)
    return (r[None, :] < r[:, None]).astype(np.float32)


def _mix_route(a_ctx, a_lat, w_out, b_out, x_parts, ada, gain, wg, bg, we, be, *, tm=1024):
    k = a_ctx.shape[1]
    split_x = len(x_parts) == 2
    x_specs = _split_specs(tm, D) if split_x else [pl.BlockSpec((tm, D), lambda i: (i, 0))]
    wr = jnp.zeros((D, ROUTE_W), F32).at[:, :N_EXP].set(we.reshape(D, N_EXP)).at[:, N_EXP:N_EXP + N_GROUPS].set(wg)
    br = jnp.zeros((1, ROUTE_W), F32).at[0, :N_EXP].set(be.reshape(N_EXP)).at[0, N_EXP:N_EXP + N_GROUPS].set(bg)
    wr_hi = wr.astype(BF16)
    wr_lo = (wr - wr_hi.astype(F32)).astype(BF16)
    return pl.pallas_call(
        functools.partial(_route_body, tm=tm, split_x=split_x),
        grid=(N_TOK // tm,),
        in_specs=_split_specs(tm, k) + [
            pl.BlockSpec((k, D), lambda i: (0, 0)),
            pl.BlockSpec((1, D), lambda i: (0, 0)),
        ] + x_specs + [
            pl.BlockSpec((1, 6, D), lambda i: (_cond_row(i, tm), 0, 0)),
            pl.BlockSpec((1, D), lambda i: (0, 0)),
            pl.BlockSpec((D, ROUTE_W), lambda i: (0, 0)),
            pl.BlockSpec((D, ROUTE_W), lambda i: (0, 0)),
            pl.BlockSpec((1, ROUTE_W), lambda i: (0, 0)),
            pl.BlockSpec((tm, tm), lambda i: (0, 0)),
        ],
        out_specs=[
            pl.BlockSpec((tm, D), lambda i: (i, 0)),
            pl.BlockSpec(_packed_shape(tm), lambda i: (i, 0, 0, 0)),
            pl.BlockSpec((tm, ROUTE_W), lambda i: (i, 0)),
            pl.BlockSpec((1, ROUTE_W), lambda i: (0, 0)),
        ],
        out_shape=[
            jax.ShapeDtypeStruct((N_TOK, D), F32),
            jax.ShapeDtypeStruct(_packed_shape(N_TOK), jnp.uint32),
            jax.ShapeDtypeStruct((N_TOK, ROUTE_W), F32),
            jax.ShapeDtypeStruct((1, ROUTE_W), F32),
        ],
        scratch_shapes=[pltpu.VMEM((1, ROUTE_W), F32)],
        compiler_params=_cparams(("arbitrary",)),
        name="mix_route",
    )(a_ctx, a_lat, w_out, b_out.reshape(1, D), *x_parts, ada, gain.reshape(1, D), wr_hi, wr_lo, br,
      jnp.asarray(_strict_lower(tm)).astype(BF16))


def _slots_body(meta_ref, cnt_ref, upper_ref, pos_ref, plan_ref):
    lane = lax.broadcasted_iota(jnp.int32, (1, ROUTE_W), 1)
    lane_f = lane.astype(F32)
    tiles = jnp.floor((cnt_ref[...] + (EXP_TILE - 1.0)) * (1.0 / EXP_TILE))
    end_tile = jnp.dot(jnp.broadcast_to(tiles, (8, ROUTE_W)).astype(BF16), upper_ref[...],
                       preferred_element_type=F32)[0:1]
    start_row = (end_tile - tiles) * EXP_TILE
    meta = meta_ref[...]
    i1 = meta[:, META_I1:META_I1 + 1]
    i2 = meta[:, META_I2:META_I2 + 1]
    p1 = jnp.sum(jnp.where(lane_f == i1, start_row, 0.0), axis=-1, keepdims=True) + meta[:, META_R1:META_R1 + 1]
    p2 = jnp.sum(jnp.where(lane_f == i2, start_row, 0.0), axis=-1, keepdims=True) + meta[:, META_R2:META_R2 + 1]
    hi1, hi2 = jnp.floor(p1 * 0.125), jnp.floor(p2 * 0.125)
    pos_ref[0] = _lane_put(lane, (hi1, p1 - 8.0 * hi1, hi2, p2 - 8.0 * hi2)).T[:8, :].astype(jnp.int32)

    @pl.when(pl.program_id(0) == 0)
    def _():
        end_col = jnp.broadcast_to(end_tile, (ROUTE_W, ROUTE_W)).T
        expert = lax.broadcasted_iota(jnp.int32, (ROUTE_W, ROUTE_W), 0)
        tile = lax.broadcasted_iota(jnp.int32, (ROUTE_W, ROUTE_W), 1).astype(F32)
        passed = jnp.where((end_col <= tile) & (expert < N_EXP), 1.0, 0.0)
        tile_expert = jnp.minimum(jnp.sum(passed, axis=0, keepdims=True), N_EXP - 1.0)
        used = jnp.max(end_tile, axis=-1, keepdims=True)
        row = lax.broadcasted_iota(jnp.int32, (8, ROUTE_W), 0)
        plan_ref[...] = jnp.where(row == 0, tile_expert, jnp.where(row == 1, used, 0.0)).astype(jnp.int32)


@functools.lru_cache(maxsize=None)
def _upper_incl(n):
    r = np.arange(n)
    return (r[:, None] <= r[None, :]).astype(np.float32)


def _slots(meta, cnt, *, tm=2048):
    n_tiles = N_TOK // tm
    return pl.pallas_call(
        _slots_body,
        grid=(n_tiles,),
        in_specs=[
            pl.BlockSpec((tm, ROUTE_W), lambda i: (i, 0)),
            pl.BlockSpec((1, ROUTE_W), lambda i: (0, 0)),
            pl.BlockSpec((ROUTE_W, ROUTE_W), lambda i: (0, 0)),
        ],
        out_specs=[
            pl.BlockSpec((1, 8, tm), lambda i: (i, 0, 0)),
            pl.BlockSpec((8, ROUTE_W), lambda i: (0, 0)),
        ],
        out_shape=[
            jax.ShapeDtypeStruct((n_tiles, 8, tm), jnp.int32),
            jax.ShapeDtypeStruct((8, ROUTE_W), jnp.int32),
        ],
        compiler_params=_cparams(("arbitrary",)),
        name="moe_slots",
    )(meta, cnt, jnp.asarray(_upper_incl(ROUTE_W)).astype(BF16))


ROW_UNROLL = 8
N_DMA_LANES = 2


def _start_row_copies(tm, make_copy):
    def block(blk, carry):
        for j in range(ROW_UNROLL):
            lane = j % N_DMA_LANES
            for k in range(2):
                make_copy(blk, j, k, lane).start(priority=lane)
        return carry

    lax.fori_loop(0, tm // ROW_UNROLL, block, 0)


def _drain_row_copies(like_src, like_dst, sem):
    for lane in range(N_DMA_LANES):
        pltpu.make_async_copy(like_src, like_dst, sem.at[lane]).wait()


def _dispatch_body(t1_ref, s1_ref, t2_ref, s2_ref, hq_ref, xs_in_ref, xs_ref, sem, *, tm):
    del xs_in_ref
    slots = ((t1_ref, s1_ref), (t2_ref, s2_ref))

    def copy(blk, j, k, lane):
        i = blk * ROW_UNROLL + j
        return pltpu.make_async_copy(_packed_row(hq_ref, blk, j), _packed_row(xs_ref, slots[k][0][i], slots[k][1][i]),
                                     sem.at[lane])

    _start_row_copies(tm, copy)
    _drain_row_copies(hq_ref, xs_ref.at[pl.ds(0, tm // 8)], sem)


def _slot_specs(tm, ahead=0):
    last = N_TOK // tm - 1
    return [pl.BlockSpec((tm,), lambda t: (jnp.minimum(t + ahead, last),), memory_space=pltpu.SMEM) for _ in range(4)]


def _dispatch(slots, hq, init, *, tm=2048):
    return pl.pallas_call(
        functools.partial(_dispatch_body, tm=tm),
        grid=(N_TOK // tm,),
        in_specs=_slot_specs(tm) + [
            pl.BlockSpec(_packed_shape(tm), lambda t: (t, 0, 0, 0)),
            pl.BlockSpec(memory_space=pl.ANY),
        ],
        out_specs=pl.BlockSpec(memory_space=pl.ANY),
        out_shape=jax.ShapeDtypeStruct(_packed_shape(SORT_ROWS), jnp.uint32),
        input_output_aliases={5: 0},
        scratch_shapes=[pltpu.SemaphoreType.DMA((N_DMA_LANES,))],
        compiler_params=_cparams(("arbitrary",)),
        name="moe_dispatch",
    )(*slots, hq, init)


def _experts_body(plan_ref, xs_ref, wg_ref, wu_ref, wd_ref, ys_ref, wgu_ref, wdn_ref):
    t = pl.program_id(0)
    in_use = t < plan_ref[1, 0]
    new_expert = (t == 0) | (plan_ref[0, t] != plan_ref[0, jnp.maximum(t - 1, 0)])

    @pl.when(in_use & new_expert)
    def _():
        wgu_ref[:, :EXP_H] = wg_ref[0, 0].astype(BF16)
        wgu_ref[:, EXP_H:] = wu_ref[0, 0].astype(BF16)
        wdn_ref[...] = wd_ref[0, 0].astype(BF16)

    @pl.when(in_use)
    def _():
        x = _load_unpacked(xs_ref).astype(BF16)
        ab = jnp.dot(x, wgu_ref[...], preferred_element_type=F32)
        a, b = ab[:, :EXP_H], ab[:, EXP_H:]
        act = a * _sigmoid(a) * b
        _store_packed(ys_ref, jnp.dot(act.astype(BF16), wdn_ref[...], preferred_element_type=F32))


def _experts(plan, xs, w_gate, w_up, w_down, *, layer):
    def tile_idx(t, plan):
        return (jnp.minimum(t, jnp.maximum(plan[1, 0] - 1, 0)), 0, 0, 0)

    grid_spec = pltpu.PrefetchScalarGridSpec(
        num_scalar_prefetch=1,
        grid=(SORT_TILES,),
        in_specs=[
            pl.BlockSpec(_packed_shape(EXP_TILE), tile_idx),
            pl.BlockSpec((1, 1, D, EXP_H), lambda t, plan: (layer, plan[0, t], 0, 0)),
            pl.BlockSpec((1, 1, D, EXP_H), lambda t, plan: (layer, plan[0, t], 0, 0)),
            pl.BlockSpec((1, 1, EXP_H, D), lambda t, plan: (layer, plan[0, t], 0, 0)),
        ],
        out_specs=pl.BlockSpec(_packed_shape(EXP_TILE), tile_idx),
        scratch_shapes=[pltpu.VMEM((D, 2 * EXP_H), BF16), pltpu.VMEM((EXP_H, D), BF16)],
    )
    return pl.pallas_call(
        _experts_body,
        grid_spec=grid_spec,
        out_shape=jax.ShapeDtypeStruct(_packed_shape(SORT_ROWS), jnp.uint32),
        input_output_aliases={1: 0},
        compiler_params=_cparams(("arbitrary",)),
        name="moe_experts",
    )(plan, xs, w_gate, w_up, w_down)


def _combine_body(*refs, tm, final):
    cur_slots, nxt_slots = refs[0:4], refs[4:8]
    meta_ref, x_ref, ada_ref = refs[8:11]
    rest = refs[11:]
    if final:
        fgain_ref, ys_ref, oc_ref, ol_ref, y_ref, sem = rest
    else:
        ys_ref, o_ref, y_ref, sem = rest
    step = pl.program_id(0)
    n_steps = pl.num_programs(0)

    def fetch(slot_refs, buf):
        slots = ((slot_refs[0], slot_refs[1]), (slot_refs[2], slot_refs[3]))

        def copy(blk, j, k, lane):
            i = blk * ROW_UNROLL + j
            return pltpu.make_async_copy(_packed_row(ys_ref, slots[k][0][i], slots[k][1][i]),
                                         _packed_row(y_ref.at[buf, k], blk, j), sem.at[buf, lane])

        _start_row_copies(tm, copy)

    @pl.when(step == 0)
    def _():
        fetch(cur_slots, 0)

    @pl.when(step + 1 < n_steps)
    def _():
        fetch(nxt_slots, (step + 1) % 2)

    buf = step % 2
    _drain_row_copies(ys_ref.at[pl.ds(0, tm // 8)], y_ref.at[buf, 0], sem.at[buf])
    y_ref = y_ref.at[buf]
    meta = meta_ref[...]
    mix = (meta[:, META_W1:META_W1 + 1] * _load_unpacked(y_ref.at[0])
           + meta[:, META_W2:META_W2 + 1] * _load_unpacked(y_ref.at[1]))
    x = x_ref[...] + ada_ref[0, 5:6, :] * mix
    if not final:
        o_ref[...] = x
        return
    ms = jnp.mean(x * x, axis=-1, keepdims=True)
    y = x * lax.rsqrt(ms + EPS) * fgain_ref[...]
    is_ctx = pl.program_id(0) < N_CTX // tm

    @pl.when(is_ctx)
    def _():
        oc_ref[...] = y

    @pl.when(jnp.logical_not(is_ctx))
    def _():
        ol_ref[...] = y


def _combine(slots, ys, meta, x, ada, final_gain=None, *, tm=256):
    final = final_gain is not None
    extra_in = [pl.BlockSpec((1, D), lambda t: (0, 0))] if final else []
    extra_args = [final_gain.reshape(1, D)] if final else []
    if final:
        out_specs = _split_specs(tm, D)
        out_shape = [jax.ShapeDtypeStruct((N_CTX, D), F32), jax.ShapeDtypeStruct((N_LAT, D), F32)]
    else:
        out_specs = pl.BlockSpec((tm, D), lambda t: (t, 0))
        out_shape = jax.ShapeDtypeStruct((N_TOK, D), F32)
    return pl.pallas_call(
        functools.partial(_combine_body, tm=tm, final=final),
        grid=(N_TOK // tm,),
        in_specs=_slot_specs(tm) + _slot_specs(tm, ahead=1) + [
            pl.BlockSpec((tm, ROUTE_W), lambda t: (t, 0)),
            pl.BlockSpec((tm, D), lambda t: (t, 0)),
            pl.BlockSpec((1, 6, D), lambda t: (_cond_row(t, tm), 0, 0)),
        ] + extra_in + [pl.BlockSpec(memory_space=pl.ANY)],
        out_specs=out_specs,
        out_shape=out_shape,
        scratch_shapes=[pltpu.VMEM((2, 2) + _packed_shape(tm), jnp.uint32),
                        pltpu.SemaphoreType.DMA((2, N_DMA_LANES))],
        compiler_params=_cparams(("arbitrary",)),
        name="moe_combine",
    )(*slots, *slots, meta, x, ada, *extra_args, ys)


def _moe(routed, ada, w_gate, w_up, w_down, *, layer, sort_init, final_gain=None):
    x, hq, meta, cnt = routed
    pos, plan = _slots(meta, cnt)
    slots = [pos[:, r, :].reshape(N_TOK) for r in range(4)]
    xs = _dispatch(slots, hq, sort_init)
    ys = _experts(plan, xs, w_gate, w_up, w_down, layer=layer)
    return _combine(slots, ys, meta, x, ada, final_gain), ys


def kernel(x_prompt, x_sample, cache_k, cache_v, c, c_ctx, w_ada, b_ada, norm_mix, norm_ffn, attn_w_q, attn_w_kv, attn_q_norm, attn_k_norm, attn_w_o, hy_w_in, hy_b_in, hy_short_w, hy_short_b, hy_filt_w1, hy_filt_b1, hy_filt_w2, hy_filt_b2, hy_filt_w3, hy_filt_b3, hy_filt_freq, hy_log_decay, hy_skip, hy_w_out, hy_b_out, router_group_w, router_group_b, router_expert_w, router_expert_b, moe_w_gate, moe_w_up, moe_w_down, final_norm):
    depth = w_ada.shape[0]
    x_parts = (x_prompt.reshape(N_CTX, D), x_sample.reshape(N_LAT, D))
    cond8 = jnp.zeros((8, D), F32).at[0].set(c_ctx).at[1:1 + N_BATCH_LAT].set(c)
    ada_all = _ada(cond8, w_ada, b_ada).reshape(depth, 8, 6, D)
    new_k = []
    new_v = []
    sort_buf = jnp.zeros(_packed_shape(SORT_ROWS), jnp.uint32)
    for l in range(depth):
        ada = ada_all[l]
        if l % 2 == 0:
            a = l // 2
            w_qkv = jnp.concatenate([attn_w_q[a], attn_w_kv[a]], axis=1).astype(BF16)
            q, k_ctx, v_ctx, kc, vc, kl, vl = _qkv(x_parts, ada, norm_mix[l], w_qkv, attn_q_norm[a], attn_k_norm[a])
            new_k.append(k_ctx)
            new_v.append(v_ctx)
            o_ctx = _attention(q, [kc], [vc], row0=0, n_batch=N_BATCH_CTX, seq=L_CTX, tq=L_CTX,
                               n_kv=N_KV, stack=Q_PER_KV)
            past_keys = _lane_tile_heads(cache_k[:, a]).transpose(0, 1, 3, 2)
            o_lat = _attention(q, [kl, past_keys], [vl, _lane_tile_heads(cache_v[:, a])],
                               row0=N_CTX, n_batch=N_BATCH_LAT, seq=L_LAT, tq=1024, n_kv=1, stack=1)
            mixer = (o_ctx, o_lat, attn_w_o[a].astype(BF16), jnp.zeros((D,), F32))
        else:
            j = l // 2
            (x,) = x_parts
            up = _norm_proj(x, ada, norm_mix[l], hy_w_in[j].astype(BF16), hy_b_in[j], part=0)
            outs = []
            for row0, n_batch, L, tc in ((0, N_BATCH_CTX, L_CTX, D), (N_CTX, N_BATCH_LAT, L_LAT, 256)):
                cs = jnp.asarray(_dft_table(L)).astype(BF16)
                fr, fi, fn = _filter_spectra(L, cs, hy_filt_w1[j], hy_filt_b1[j], hy_filt_w2[j], hy_filt_b2[j],
                                             hy_filt_w3[j], hy_filt_b3[j], hy_filt_freq[j], hy_log_decay[j])
                outs.append(_hyena_conv(up, hy_short_w[j], hy_short_b[j], hy_skip[j], fr, fi, fn, cs,
                                        row0=row0, n_batch=n_batch, L=L, tc=tc))
            mixer = (outs[0], outs[1], hy_w_out[j].astype(BF16), hy_b_out[j])
        routed = _mix_route(*mixer, x_parts, ada, norm_ffn[l], router_group_w[l], router_group_b[l],
                            router_expert_w[l], router_expert_b[l])
        out, sort_buf = _moe(routed, ada, moe_w_gate, moe_w_up, moe_w_down, layer=l, sort_init=sort_buf,
                             final_gain=final_norm if l == depth - 1 else None)
        x_parts = (out,)
    y_ctx, y_lat = x_parts[0]
    return (y_ctx.reshape(N_BATCH_CTX, L_CTX, D), y_lat.reshape(N_BATCH_LAT, L_LAT, D),
            jnp.stack(new_k, axis=1), jnp.stack(new_v, axis=1))
```

```python
import functools
import math

import numpy as np
import jax
import jax.numpy as jnp
from jax import lax
from jax.experimental import pallas as pl
from jax.experimental.pallas import tpu as pltpu

F32 = jnp.float32
BF16 = jnp.bfloat16

D = 1024
N_BATCH_CTX = 32
L_CTX = 256
N_BATCH_LAT = 2
L_LAT = 2048
PAST = 256
N_CTX = N_BATCH_CTX * L_CTX
N_LAT = N_BATCH_LAT * L_LAT
N_TOK = N_CTX + N_LAT
GRID_W = 64
N_HEADS = 16
N_KV = 4
DH = 64
Q_PER_KV = N_HEADS // N_KV
KV_W = N_KV * DH
ROPE_THETA = 10000.0
FILTER_FEAT = 17
FEAT_PAD = 32
FILTER_HIDDEN = 64
N_GROUPS = 4
E_PER_G = 8
N_EXP = N_GROUPS * E_PER_G
EXP_H = D // 4
EPS = 1e-6
ROUTE_W = 128
VMEM_LIMIT = 56 * 1024 * 1024


def _cparams(sem):
    return pltpu.CompilerParams(dimension_semantics=sem, vmem_limit_bytes=VMEM_LIMIT)


def _cond_row(i, tm):
    n_ctx_tiles = N_CTX // tm
    return jnp.where(i < n_ctx_tiles, 0, 1 + (i - n_ctx_tiles) // (L_LAT // tm))


def _split_specs(tm, width):
    n_ctx_tiles = N_CTX // tm
    return [pl.BlockSpec((tm, width), lambda i: (jnp.minimum(i, n_ctx_tiles - 1), 0)),
            pl.BlockSpec((tm, width), lambda i: (jnp.maximum(i - n_ctx_tiles, 0), 0))]


def _pick_split(i, tm, ctx_ref, lat_ref):
    return jnp.where(i < N_CTX // tm, ctx_ref[...], lat_ref[...])


def _sigmoid(x):
    return 1.0 / (1.0 + jnp.exp(-x))


@functools.lru_cache(maxsize=None)
def _dft_table(L):
    k = np.arange(L, dtype=np.int64)
    ang = (np.outer(k, k) % (2 * L)).astype(np.float64) * (math.pi / L)
    return np.concatenate([np.cos(ang), np.sin(ang)], axis=0).astype(np.float32)


@functools.lru_cache(maxsize=None)
def _filter_feats(L):
    t = np.linspace(0.0, 1.0, L, dtype=np.float64)[:, None]
    bands = np.linspace(1e-4, 7.0, 8, dtype=np.float64)[None, :]
    w = (2.0 * math.pi) * np.arange(L, dtype=np.float64)[:, None] / L
    feats = np.concatenate([t, np.cos(bands * w), -np.sin(bands * w)], axis=-1)
    out = np.zeros((L, FEAT_PAD), np.float32)
    out[:, :FILTER_FEAT] = feats
    return out


@functools.lru_cache(maxsize=None)
def _rope_tables(tm):
    pos = np.arange(L_LAT)
    row = (pos // GRID_W).astype(np.float64)
    col = (pos % GRID_W).astype(np.float64)
    axis_dim = DH // 2
    inv_freq = ROPE_THETA ** (-np.arange(0, axis_dim, 2, dtype=np.float64) / axis_dim)
    lane = np.arange(KV_W)
    d = lane % DH
    is_col = (d // axis_dim) == 1
    fi = d % (axis_dim // 2)
    first_half = (d % axis_dim) < (axis_dim // 2)
    p = np.where(is_col[None, :], col[:, None], row[:, None])
    ang = p * inv_freq[fi][None, :]
    cos = np.cos(ang)
    sin = np.sin(ang) * np.where(first_half, -1.0, 1.0)[None, :]
    cos = np.concatenate([np.ones((tm, KV_W)), cos], axis=0).astype(np.float32)
    sin = np.concatenate([np.zeros((tm, KV_W)), sin], axis=0).astype(np.float32)
    return cos, sin


@functools.lru_cache(maxsize=None)
def _head_sum_matrix():
    lane = np.arange(KV_W)
    return (lane[:, None] // DH == lane[None, :] // DH).astype(np.float32)


def _ada_body(c_ref, w_ref, b_ref, o_ref):
    c = c_ref[...]
    s = c * _sigmoid(c)
    o_ref[0] = jnp.dot(s.astype(BF16), w_ref[0].astype(BF16), preferred_element_type=F32) + b_ref[0]


def _ada(cond8, w_ada, b_ada):
    depth = w_ada.shape[0]
    tn = 1536
    return pl.pallas_call(
        _ada_body,
        grid=(depth, 6 * D // tn),
        in_specs=[
            pl.BlockSpec((8, D), lambda l, j: (0, 0)),
            pl.BlockSpec((1, D, tn), lambda l, j: (l, 0, j)),
            pl.BlockSpec((1, 1, tn), lambda l, j: (l, 0, j)),
        ],
        out_specs=pl.BlockSpec((1, 8, tn), lambda l, j: (l, 0, j)),
        out_shape=jax.ShapeDtypeStruct((depth, 8, 6 * D), F32),
        compiler_params=_cparams(("arbitrary", "arbitrary")),
        name="ada",
    )(cond8, w_ada, b_ada.reshape(depth, 1, 6 * D))


def _norm_mod(x, gain_ref, ada_ref, part):
    ms = jnp.mean(x * x, axis=-1, keepdims=True)
    y = x * lax.rsqrt(ms + EPS) * gain_ref[...]
    return y * (1.0 + ada_ref[0, part + 1:part + 2, :]) + ada_ref[0, part:part + 1, :]


def _proj_body(x_ref, ada_ref, gain_ref, w_ref, b_ref, o_ref, *, part):
    h = _norm_mod(x_ref[...], gain_ref, ada_ref, part)
    acc = jnp.dot(h.astype(BF16), w_ref[...], preferred_element_type=F32) + b_ref[...]
    o_ref[...] = acc.astype(o_ref.dtype)


def _norm_proj(x, ada, gain, w, b, *, part, tm=1024):
    n_out = w.shape[1]
    return pl.pallas_call(
        functools.partial(_proj_body, part=part),
        grid=(N_TOK // tm,),
        in_specs=[
            pl.BlockSpec((tm, D), lambda i: (i, 0)),
            pl.BlockSpec((1, 6, D), lambda i: (_cond_row(i, tm), 0, 0)),
            pl.BlockSpec((1, D), lambda i: (0, 0)),
            pl.BlockSpec((D, n_out), lambda i: (0, 0)),
            pl.BlockSpec((1, n_out), lambda i: (0, 0)),
        ],
        out_specs=pl.BlockSpec((tm, n_out), lambda i: (i, 0)),
        out_shape=jax.ShapeDtypeStruct((N_TOK, n_out), BF16),
        compiler_params=_cparams(("arbitrary",)),
        name="norm_proj",
    )(x, ada, gain.reshape(1, D), w, b.reshape(1, n_out))


def _head_rms_rope(x, gain, hs, cos, sin, lane):
    ss = jnp.dot((x * x).astype(BF16), hs, preferred_element_type=F32)
    xn = x * lax.rsqrt(ss * (1.0 / DH) + EPS) * gain
    fwd = pltpu.roll(xn, KV_W - DH // 4, 1)
    bwd = pltpu.roll(xn, DH // 4, 1)
    partner = jnp.where((lane & (DH // 4)) == 0, fwd, bwd)
    return xn * cos + partner * sin


@functools.lru_cache(maxsize=None)
def _lane_tile_matrix():
    col = np.arange(N_KV * KV_W)
    src = (col // KV_W) * DH + col % DH
    return (np.arange(KV_W)[:, None] == src[None, :]).astype(np.float32)


def _qkv_body(*refs, tm, split_x):
    n_x = 2 if split_x else 1
    x_refs = refs[:n_x]
    (ada_ref, gain_ref, w_ref, qg_ref, kg_ref, hs_ref, cos_ref, sin_ref, tile_ref, tile_t_ref,
     q_ref, newk_ref, newv_ref, kc_ref, vc_ref, kl_ref, vl_ref) = refs[n_x:]
    step = pl.program_id(0)
    x = _pick_split(step, tm, *x_refs) if split_x else x_refs[0][...]
    h = _norm_mod(x, gain_ref, ada_ref, 0)
    acc = jnp.dot(h.astype(BF16), w_ref[...], preferred_element_type=F32)
    hs = hs_ref[...]
    cos = cos_ref[...]
    sin = sin_ref[...]
    lane = lax.broadcasted_iota(jnp.int32, (1, KV_W), 1)
    for c in range(N_KV):
        qc = _head_rms_rope(acc[:, c * KV_W:(c + 1) * KV_W], qg_ref[...], hs, cos, sin, lane)
        q_ref[:, c * KV_W:(c + 1) * KV_W] = (qc * (DH ** -0.5)).astype(BF16)
    k = _head_rms_rope(acc[:, D:D + KV_W], kg_ref[...], hs, cos, sin, lane)
    v = acc[:, D + KV_W:D + 2 * KV_W]
    k4t = lax.dot_general(tile_t_ref[...], k.astype(BF16), (((1,), (1,)), ((), ())),
                          preferred_element_type=F32).astype(BF16)
    v4 = jnp.dot(v.astype(BF16), tile_ref[...], preferred_element_type=F32).astype(BF16)
    is_ctx = step < N_CTX // tm

    @pl.when(is_ctx)
    def _():
        for bb in range(tm // L_CTX):
            rows = slice(bb * L_CTX, (bb + 1) * L_CTX)
            for hd in range(N_KV):
                newk_ref[bb, hd] = k[rows, hd * DH:(hd + 1) * DH]
                newv_ref[bb, hd] = v[rows, hd * DH:(hd + 1) * DH]
                kc_ref[bb, hd] = k4t[hd * KV_W:(hd + 1) * KV_W, rows]
                vc_ref[bb, hd] = v4[rows, hd * KV_W:(hd + 1) * KV_W]

    @pl.when(jnp.logical_not(is_ctx))
    def _():
        for hd in range(N_KV):
            kl_ref[0, hd] = k4t[hd * KV_W:(hd + 1) * KV_W, :]
            vl_ref[0, hd] = v4[:, hd * KV_W:(hd + 1) * KV_W]


def _qkv(x_parts, ada, gain, w_qkv, q_gain, k_gain):
    tm = 2 * L_CTX
    per_tile = tm // L_CTX
    cos, sin = _rope_tables(tm)
    split_x = len(x_parts) == 2
    x_specs = _split_specs(tm, D) if split_x else [pl.BlockSpec((tm, D), lambda i: (i, 0))]
    n_ctx_tiles = N_CTX // tm
    lat_tiles = L_LAT // tm

    def rope_idx(i):
        return (jnp.where(i < n_ctx_tiles, 0, 1 + (i - n_ctx_tiles) % lat_tiles), 0)

    def ctx_idx(i):
        return (jnp.minimum(i, n_ctx_tiles - 1), 0, 0, 0)

    def lat_idx(i):
        j = jnp.maximum(i - n_ctx_tiles, 0)
        return (j // lat_tiles, 0, j % lat_tiles, 0)

    def lat_idx_t(i):
        j = jnp.maximum(i - n_ctx_tiles, 0)
        return (j // lat_tiles, 0, 0, j % lat_tiles)

    const = lambda i: (0, 0)
    kv_ctx = jax.ShapeDtypeStruct((N_BATCH_CTX, N_KV, L_CTX, DH), F32)
    tiled_ctx = jax.ShapeDtypeStruct((N_BATCH_CTX, N_KV, L_CTX, KV_W), BF16)
    tiled_lat = jax.ShapeDtypeStruct((N_BATCH_LAT, N_KV, L_LAT, KV_W), BF16)
    keys_ctx = jax.ShapeDtypeStruct((N_BATCH_CTX, N_KV, KV_W, L_CTX), BF16)
    keys_lat = jax.ShapeDtypeStruct((N_BATCH_LAT, N_KV, KV_W, L_LAT), BF16)
    return pl.pallas_call(
        functools.partial(_qkv_body, tm=tm, split_x=split_x),
        grid=(N_TOK // tm,),
        in_specs=x_specs + [
            pl.BlockSpec((1, 6, D), lambda i: (_cond_row(i, tm), 0, 0)),
            pl.BlockSpec((1, D), const),
            pl.BlockSpec((D, D + 2 * KV_W), const),
            pl.BlockSpec((1, KV_W), const),
            pl.BlockSpec((1, KV_W), const),
            pl.BlockSpec((KV_W, KV_W), const),
            pl.BlockSpec((tm, KV_W), rope_idx),
            pl.BlockSpec((tm, KV_W), rope_idx),
            pl.BlockSpec((KV_W, N_KV * KV_W), const),
            pl.BlockSpec((N_KV * KV_W, KV_W), const),
        ],
        out_specs=[
            pl.BlockSpec((tm, D), lambda i: (i, 0)),
            pl.BlockSpec((per_tile, N_KV, L_CTX, DH), ctx_idx),
            pl.BlockSpec((per_tile, N_KV, L_CTX, DH), ctx_idx),
            pl.BlockSpec((per_tile, N_KV, KV_W, L_CTX), ctx_idx),
            pl.BlockSpec((per_tile, N_KV, L_CTX, KV_W), ctx_idx),
            pl.BlockSpec((1, N_KV, KV_W, tm), lat_idx_t),
            pl.BlockSpec((1, N_KV, tm, KV_W), lat_idx),
        ],
        out_shape=[jax.ShapeDtypeStruct((N_TOK, D), BF16), kv_ctx, kv_ctx, keys_ctx, tiled_ctx, keys_lat, tiled_lat],
        compiler_params=_cparams(("arbitrary",)),
        name="qkv",
    )(*x_parts, ada, gain.reshape(1, D), w_qkv,
      jnp.tile(q_gain, Q_PER_KV).reshape(1, KV_W), jnp.tile(k_gain, N_KV).reshape(1, KV_W),
      jnp.asarray(_head_sum_matrix()).astype(BF16), jnp.asarray(cos), jnp.asarray(sin),
      jnp.asarray(_lane_tile_matrix()).astype(BF16), jnp.asarray(_lane_tile_matrix().T.copy()).astype(BF16))


def _attn_body(*refs, n_kv, stack, n_seg):
    q_ref, k_refs, v_refs, o_ref = refs[0], refs[1:1 + n_seg], refs[1 + n_seg:1 + 2 * n_seg], refs[-1]
    lane = lax.broadcasted_iota(jnp.int32, (1, KV_W), 1)
    masks = [(lane >> 6) == g for g in range(Q_PER_KV)]
    tq = q_ref.shape[0]
    for kv in range(n_kv):
        q = q_ref[:, kv * KV_W:(kv + 1) * KV_W]
        out = jnp.zeros((tq, KV_W), F32)
        for c in range(0, Q_PER_KV, stack):
            pair = masks[c:c + stack]
            ones_blk = (c + stack) % Q_PER_KV if stack < Q_PER_KV else None
            stacked = jnp.concatenate([jnp.where(m, q, jnp.zeros_like(q)) for m in pair], axis=0)
            scores = [jnp.dot(stacked, k_ref[0, kv], preferred_element_type=F32) for k_ref in k_refs]
            top = functools.reduce(jnp.maximum, [jnp.max(s, axis=-1, keepdims=True) for s in scores])
            probs = [jnp.exp((s - top).astype(BF16)) for s in scores]
            if ones_blk is None:
                vals = [v_ref[0, kv] for v_ref in v_refs]
            else:
                vals = [jnp.where(masks[ones_blk], jnp.ones((), BF16), v_ref[0, kv]) for v_ref in v_refs]
            og = sum(jnp.dot(p, v, preferred_element_type=F32) for p, v in zip(probs, vals))
            if ones_blk is None:
                denom = sum(jnp.sum(p.astype(F32), axis=-1, keepdims=True) for p in probs)
            else:
                denom = og[:, ones_blk * DH:ones_blk * DH + 1]
            og = og * (1.0 / denom)
            for g, m in enumerate(pair):
                out = out + jnp.where(m, og[g * tq:(g + 1) * tq], 0.0)
        o_ref[:, kv * KV_W:(kv + 1) * KV_W] = out.astype(BF16)


def _attention(q, keys, values, *, row0, n_batch, seq, tq, n_kv, stack):
    per_b = seq // tq
    base = row0 // tq
    kv_specs = [pl.BlockSpec((1, n_kv) + a.shape[2:], lambda b, h, i: (b, h, 0, 0)) for a in keys + values]
    return pl.pallas_call(
        functools.partial(_attn_body, n_kv=n_kv, stack=stack, n_seg=len(keys)),
        grid=(n_batch, N_KV // n_kv, per_b),
        in_specs=[pl.BlockSpec((tq, n_kv * KV_W), lambda b, h, i: (base + b * per_b + i, h))] + kv_specs,
        out_specs=pl.BlockSpec((tq, n_kv * KV_W), lambda b, h, i: (b * per_b + i, h)),
        out_shape=jax.ShapeDtypeStruct((n_batch * seq, D), BF16),
        compiler_params=_cparams(("arbitrary", "arbitrary", "arbitrary")),
        name="attention",
    )(q, *keys, *values)


def _lane_tile_heads(x):
    return jnp.tile(x.astype(BF16), (1, 1, 1, KV_W // DH))


def _alt_sign(L):
    row = lax.broadcasted_iota(jnp.int32, (L, 1), 0)
    return row, jnp.where((row & 1) == 0, 1.0, -1.0)


def _split_dot(a, b):
    a_hi, b_hi = a.astype(BF16), b.astype(BF16)
    a_lo = (a - a_hi.astype(F32)).astype(BF16)
    b_lo = (b - b_hi.astype(F32)).astype(BF16)
    return (jnp.dot(a_hi, b_hi, preferred_element_type=F32) + jnp.dot(a_lo, b_hi, preferred_element_type=F32)
            + jnp.dot(a_hi, b_lo, preferred_element_type=F32))


def _filter_body(feat_ref, w1_ref, b1_ref, w2_ref, b2_ref, fq_ref, w3_ref, b3_ref, ld_ref, cs_ref,
                 fr_ref, fi_ref, fn_ref, hid_ref, *, L):
    feats = feat_ref[...]

    @pl.when(pl.program_id(0) == 0)
    def _():
        fq = fq_ref[...]
        h1 = jnp.sin(fq * (_split_dot(feats, w1_ref[...]) + b1_ref[...]))
        hid_ref[...] = jnp.sin(fq * (_split_dot(h1, w2_ref[...]) + b2_ref[...])).astype(BF16)

    t = feats[:, 0:1]
    row, alt = _alt_sign(L)
    filt = []
    for j in range(4):
        raw = jnp.dot(hid_ref[...], w3_ref[j].astype(BF16), preferred_element_type=F32) + b3_ref[j:j + 1, :]
        filt.append(raw * jnp.exp(-t * jnp.exp(ld_ref[j:j + 1, :])))
    for o in range(2):
        hf, hb = filt[2 * o], filt[2 * o + 1]
        l1 = jnp.sum(jnp.abs(hf), axis=0, keepdims=True) + jnp.sum(jnp.abs(hb), axis=0, keepdims=True)
        inv = 1.0 / (l1 + EPS)
        sym = (hf + hb) * inv
        asym = (hb - hf) * inv
        fr = jnp.dot(cs_ref[0:L, :], sym.astype(BF16), preferred_element_type=F32) * (1.0 / L)
        fr_ref[o] = jnp.where(row == 0, 0.5 * fr, fr)
        fi_ref[o] = jnp.dot(cs_ref[L:2 * L, :], asym.astype(BF16), preferred_element_type=F32) * (1.0 / L)
        fn_ref[o] = jnp.sum(sym * alt, axis=0, keepdims=True) * (0.5 / L)


def _filter_spectra(L, cs, w1, b1, w2, b2, w3, b3, freq, log_decay, *, tc=256):
    w1p = jnp.zeros((FEAT_PAD, FILTER_HIDDEN), F32).at[:FILTER_FEAT].set(w1)
    w3r = w3.reshape(FILTER_HIDDEN, 4, D).transpose(1, 0, 2)
    const = lambda j: (0, 0)
    return pl.pallas_call(
        functools.partial(_filter_body, L=L),
        grid=(D // tc,),
        in_specs=[
            pl.BlockSpec((L, FEAT_PAD), const),
            pl.BlockSpec((FEAT_PAD, FILTER_HIDDEN), const),
            pl.BlockSpec((1, FILTER_HIDDEN), const),
            pl.BlockSpec((FILTER_HIDDEN, FILTER_HIDDEN), const),
            pl.BlockSpec((1, FILTER_HIDDEN), const),
            pl.BlockSpec((1, FILTER_HIDDEN), const),
            pl.BlockSpec((4, FILTER_HIDDEN, tc), lambda j: (0, 0, j)),
            pl.BlockSpec((4, tc), lambda j: (0, j)),
            pl.BlockSpec((4, tc), lambda j: (0, j)),
            pl.BlockSpec((2 * L, L), const, pipeline_mode=pl.Buffered(1)),
        ],
        out_specs=[
            pl.BlockSpec((2, L, tc), lambda j: (0, 0, j)),
            pl.BlockSpec((2, L, tc), lambda j: (0, 0, j)),
            pl.BlockSpec((2, 1, tc), lambda j: (0, 0, j)),
        ],
        out_shape=[
            jax.ShapeDtypeStruct((2, L, D), F32),
            jax.ShapeDtypeStruct((2, L, D), F32),
            jax.ShapeDtypeStruct((2, 1, D), F32),
        ],
        scratch_shapes=[pltpu.VMEM((L, FILTER_HIDDEN), BF16)],
        compiler_params=_cparams(("arbitrary",)),
        name="hyena_filter",
    )(jnp.asarray(_filter_feats(L)), w1p, b1.reshape(1, -1), w2, b2.reshape(1, -1), freq.reshape(1, -1),
      w3r, b3.reshape(4, D), log_decay.reshape(4, D), cs)


FREQ_CHUNK = 1024


def _hyconv_body(x1_ref, x2_ref, v_ref, sw_ref, sb_ref, skip_ref, fr_ref, fi_ref, fn_ref, cs_ref, o_ref,
                 z_ref, g_ref, zb_ref, yr_ref, yi_ref, *, L):
    row, alt = _alt_sign(L)
    kc = min(FREQ_CHUNK, L)
    chunk_row = lax.broadcasted_iota(jnp.int32, (kc, 1), 0)

    def short_conv(u_ref, p):
        u = u_ref[...].astype(F32)
        prev = jnp.where(row == 0, 0.0, pltpu.roll(u, 1, 0))
        nxt = jnp.where(row == L - 1, 0.0, pltpu.roll(u, L - 1, 0))
        return (prev * sw_ref[0, p:p + 1, :] + u * sw_ref[1, p:p + 1, :] + nxt * sw_ref[2, p:p + 1, :]
                + sb_ref[p:p + 1, :])

    z_ref[...] = short_conv(v_ref, 2)
    g_ref[0] = short_conv(x1_ref, 0)
    g_ref[1] = short_conv(x2_ref, 1)
    for o in range(2):
        z = z_ref[...]
        zb_ref[...] = z.astype(BF16)
        nyq = jnp.sum(z * alt, axis=0, keepdims=True) * fn_ref[o]

        def to_freq(c, carry):
            lo = pl.multiple_of(c * kc, kc)
            zr = jnp.dot(cs_ref[pl.ds(lo, kc), :], zb_ref[...], preferred_element_type=F32)
            zs = jnp.dot(cs_ref[pl.ds(L + lo, kc), :], zb_ref[...], preferred_element_type=F32)
            fr = fr_ref[o, pl.ds(lo, kc), :]
            fi = fi_ref[o, pl.ds(lo, kc), :]
            yr_ref[pl.ds(lo, kc), :] = (zr * fr + zs * fi).astype(BF16)
            yi_ref[pl.ds(lo, kc), :] = (zr * fi - zs * fr).astype(BF16)
            return carry

        lax.fori_loop(0, L // kc, to_freq, 0, unroll=True)

        def to_time(c, carry):
            lo = pl.multiple_of(c * kc, kc)
            y = (jnp.dot(cs_ref[pl.ds(lo, kc), :], yr_ref[...], preferred_element_type=F32)
                 - jnp.dot(cs_ref[pl.ds(L + lo, kc), :], yi_ref[...], preferred_element_type=F32))
            y = y + jnp.where(((chunk_row + lo) & 1) == 0, nyq, -nyq)
            z_ref[pl.ds(lo, kc), :] = g_ref[o, pl.ds(lo, kc), :] * (y + skip_ref[o:o + 1, :] * z_ref[pl.ds(lo, kc), :])
            return carry

        lax.fori_loop(0, L // kc, to_time, 0, unroll=True)
    o_ref[...] = z_ref[...].astype(BF16)


def _hyena_conv(up, sw, sb, skip, fr, fi, fn, cs, *, row0, n_batch, L, tc):
    n_ct = D // tc
    base = row0 // L
    u_spec = lambda p: pl.BlockSpec((L, tc), lambda j, b: (base + b, p * n_ct + j))
    once = pl.Buffered(1)
    return pl.pallas_call(
        functools.partial(_hyconv_body, L=L),
        grid=(n_ct, n_batch),
        in_specs=[
            u_spec(0), u_spec(1), u_spec(2),
            pl.BlockSpec((3, 3, tc), lambda j, b: (0, 0, j)),
            pl.BlockSpec((3, tc), lambda j, b: (0, j)),
            pl.BlockSpec((2, tc), lambda j, b: (0, j)),
            pl.BlockSpec((2, L, tc), lambda j, b: (0, 0, j), pipeline_mode=once),
            pl.BlockSpec((2, L, tc), lambda j, b: (0, 0, j), pipeline_mode=once),
            pl.BlockSpec((2, 1, tc), lambda j, b: (0, 0, j)),
            pl.BlockSpec((2 * L, L), lambda j, b: (0, 0), pipeline_mode=once),
        ],
        out_specs=pl.BlockSpec((L, tc), lambda j, b: (b, j)),
        out_shape=jax.ShapeDtypeStruct((n_batch * L, D), BF16),
        scratch_shapes=[
            pltpu.VMEM((L, tc), F32),
            pltpu.VMEM((2, L, tc), F32),
            pltpu.VMEM((L, tc), BF16),
            pltpu.VMEM((L, tc), BF16),
            pltpu.VMEM((L, tc), BF16),
        ],
        compiler_params=_cparams(("arbitrary", "arbitrary")),
        name="hyena_conv",
    )(up, up, up, sw.reshape(3, 3, D), sb.reshape(3, D), skip, fr, fi, fn, cs)


EXP_TILE = 512
SORT_TILES = (2 * N_TOK) // EXP_TILE + N_EXP
SORT_ROWS = SORT_TILES * EXP_TILE
N_QUARTERS = 4
QUARTER_W = D // (2 * N_QUARTERS)
META_I1, META_I2, META_R1, META_R2, META_W1, META_W2 = range(6)
HI_HALF = 0xFFFF0000


def _pack_pairs(x):
    bits = pltpu.bitcast(x.astype(BF16).astype(F32), jnp.uint32)
    return (bits[:, :QUARTER_W] >> 16) | bits[:, QUARTER_W:]


def _unpack_pairs(w):
    return pltpu.bitcast(w << 16, F32), pltpu.bitcast(w & jnp.uint32(HI_HALF), F32)


def _packed_shape(rows):
    return (rows // 8, N_QUARTERS, 8, QUARTER_W)


def _store_packed(ref, x):
    for q in range(N_QUARTERS):
        ref[:, q] = _pack_pairs(x[:, q * 2 * QUARTER_W:(q + 1) * 2 * QUARTER_W]).reshape(-1, 8, QUARTER_W)


def _load_unpacked(ref):
    halves = []
    for q in range(N_QUARTERS):
        halves.extend(_unpack_pairs(ref[:, q].reshape(-1, QUARTER_W)))
    return jnp.concatenate(halves, axis=1)


def _packed_row(ref, tile, sublane):
    return ref.at[tile, :, sublane, :]


def _lane_put(lane, values):
    out = jnp.where(lane == 0, values[0], 0.0)
    for k in range(1, len(values)):
        out = out + jnp.where(lane == k, values[k], 0.0)
    return out


def _route_body(*refs, tm, split_x):
    n_x = 2 if split_x else 1
    ac_ref, al_ref, w_ref, b_ref = refs[:4]
    x_refs = refs[4:4 + n_x]
    (ada_ref, gain_ref, wr_hi_ref, wr_lo_ref, br_ref, tri_ref,
     xo_ref, hq_ref, meta_ref, cnt_ref, carry_ref) = refs[4 + n_x:]
    step = pl.program_id(0)

    @pl.when(step == 0)
    def _():
        carry_ref[...] = jnp.zeros_like(carry_ref)

    mixed = jnp.dot(_pick_split(step, tm, ac_ref, al_ref), w_ref[...], preferred_element_type=F32) + b_ref[...]
    x = _pick_split(step, tm, *x_refs) if split_x else x_refs[0][...]
    x = x + ada_ref[0, 2:3, :] * mixed
    xo_ref[...] = x
    h = _norm_mod(x, gain_ref, ada_ref, 3)
    _store_packed(hq_ref, h)
    h_hi = h.astype(BF16)
    h_lo = (h - h_hi.astype(F32)).astype(BF16)
    both = jnp.dot(h_hi, jnp.concatenate([wr_hi_ref[...], wr_lo_ref[...]], axis=1), preferred_element_type=F32)
    logits = (both[:, :ROUTE_W] + both[:, ROUTE_W:]
              + jnp.dot(h_lo, wr_hi_ref[...], preferred_element_type=F32) + br_ref[...])
    lane = lax.broadcasted_iota(jnp.int32, (1, ROUTE_W), 1)
    lane_f = lane.astype(F32)
    group_of_lane = (lane >> 3).astype(F32)
    neg = -jnp.inf
    big = float(ROUTE_W)
    is_g = (lane >= N_EXP) & (lane < N_EXP + N_GROUPS)
    gl = jnp.where(is_g, logits, neg)
    gmax = jnp.max(gl, axis=-1, keepdims=True)
    gidx = jnp.min(jnp.where(gl == gmax, lane_f - N_EXP, big), axis=-1, keepdims=True)
    g_top = 1.0 / jnp.sum(jnp.where(is_g, jnp.exp(gl - gmax), 0.0), axis=-1, keepdims=True)
    in_group = (lane < N_EXP) & (group_of_lane == gidx)
    el = jnp.where(in_group, logits, neg)
    v1 = jnp.max(el, axis=-1, keepdims=True)
    i1 = jnp.min(jnp.where(el == v1, lane_f, big), axis=-1, keepdims=True)
    el2 = jnp.where(lane_f == i1, neg, el)
    v2 = jnp.max(el2, axis=-1, keepdims=True)
    i2 = jnp.min(jnp.where(el2 == v2, lane_f, big), axis=-1, keepdims=True)
    r = jnp.exp(v2 - v1)
    w1 = g_top / (1.0 + r)
    w2 = g_top * r / (1.0 + r)
    sel = jnp.where((lane_f == i1) | (lane_f == i2), 1.0, 0.0)
    rank = jnp.dot(tri_ref[...], sel.astype(BF16), preferred_element_type=F32) + carry_ref[...]
    r1 = jnp.sum(jnp.where(lane_f == i1, rank, 0.0), axis=-1, keepdims=True)
    r2 = jnp.sum(jnp.where(lane_f == i2, rank, 0.0), axis=-1, keepdims=True)
    carry_ref[...] += jnp.sum(sel, axis=0, keepdims=True)
    cnt_ref[...] = carry_ref[...]
    meta_ref[...] = _lane_put(lane, (i1, i2, r1, r2, w1, w2))


@functools.lru_cache(maxsize=None)
def _strict_lower(n):
    r = np.arange(n)
    return (r[None, :] < r[:, None]).astype(np.float32)


def _mix_route(a_ctx, a_lat, w_out, b_out, x_parts, ada, gain, wg, bg, we, be, *, tm=1024):
    k = a_ctx.shape[1]
    split_x = len(x_parts) == 2
    x_specs = _split_specs(tm, D) if split_x else [pl.BlockSpec((tm, D), lambda i: (i, 0))]
    wr = jnp.zeros((D, ROUTE_W), F32).at[:, :N_EXP].set(we.reshape(D, N_EXP)).at[:, N_EXP:N_EXP + N_GROUPS].set(wg)
    br = jnp.zeros((1, ROUTE_W), F32).at[0, :N_EXP].set(be.reshape(N_EXP)).at[0, N_EXP:N_EXP + N_GROUPS].set(bg)
    wr_hi = wr.astype(BF16)
    wr_lo = (wr - wr_hi.astype(F32)).astype(BF16)
    return pl.pallas_call(
        functools.partial(_route_body, tm=tm, split_x=split_x),
        grid=(N_TOK // tm,),
        in_specs=_split_specs(tm, k) + [
            pl.BlockSpec((k, D), lambda i: (0, 0)),
            pl.BlockSpec((1, D), lambda i: (0, 0)),
        ] + x_specs + [
            pl.BlockSpec((1, 6, D), lambda i: (_cond_row(i, tm), 0, 0)),
            pl.BlockSpec((1, D), lambda i: (0, 0)),
            pl.BlockSpec((D, ROUTE_W), lambda i: (0, 0)),
            pl.BlockSpec((D, ROUTE_W), lambda i: (0, 0)),
            pl.BlockSpec((1, ROUTE_W), lambda i: (0, 0)),
            pl.BlockSpec((tm, tm), lambda i: (0, 0)),
        ],
        out_specs=[
            pl.BlockSpec((tm, D), lambda i: (i, 0)),
            pl.BlockSpec(_packed_shape(tm), lambda i: (i, 0, 0, 0)),
            pl.BlockSpec((tm, ROUTE_W), lambda i: (i, 0)),
            pl.BlockSpec((1, ROUTE_W), lambda i: (0, 0)),
        ],
        out_shape=[
            jax.ShapeDtypeStruct((N_TOK, D), F32),
            jax.ShapeDtypeStruct(_packed_shape(N_TOK), jnp.uint32),
            jax.ShapeDtypeStruct((N_TOK, ROUTE_W), F32),
            jax.ShapeDtypeStruct((1, ROUTE_W), F32),
        ],
        scratch_shapes=[pltpu.VMEM((1, ROUTE_W), F32)],
        compiler_params=_cparams(("arbitrary",)),
        name="mix_route",
    )(a_ctx, a_lat, w_out, b_out.reshape(1, D), *x_parts, ada, gain.reshape(1, D), wr_hi, wr_lo, br,
      jnp.asarray(_strict_lower(tm)).astype(BF16))


def _slots_body(meta_ref, cnt_ref, upper_ref, pos_ref, plan_ref):
    lane = lax.broadcasted_iota(jnp.int32, (1, ROUTE_W), 1)
    lane_f = lane.astype(F32)
    tiles = jnp.floor((cnt_ref[...] + (EXP_TILE - 1.0)) * (1.0 / EXP_TILE))
    end_tile = jnp.dot(jnp.broadcast_to(tiles, (8, ROUTE_W)).astype(BF16), upper_ref[...],
                       preferred_element_type=F32)[0:1]
    start_row = (end_tile - tiles) * EXP_TILE
    meta = meta_ref[...]
    i1 = meta[:, META_I1:META_I1 + 1]
    i2 = meta[:, META_I2:META_I2 + 1]
    p1 = jnp.sum(jnp.where(lane_f == i1, start_row, 0.0), axis=-1, keepdims=True) + meta[:, META_R1:META_R1 + 1]
    p2 = jnp.sum(jnp.where(lane_f == i2, start_row, 0.0), axis=-1, keepdims=True) + meta[:, META_R2:META_R2 + 1]
    hi1, hi2 = jnp.floor(p1 * 0.125), jnp.floor(p2 * 0.125)
    pos_ref[0] = _lane_put(lane, (hi1, p1 - 8.0 * hi1, hi2, p2 - 8.0 * hi2)).T[:8, :].astype(jnp.int32)

    @pl.when(pl.program_id(0) == 0)
    def _():
        end_col = jnp.broadcast_to(end_tile, (ROUTE_W, ROUTE_W)).T
        expert = lax.broadcasted_iota(jnp.int32, (ROUTE_W, ROUTE_W), 0)
        tile = lax.broadcasted_iota(jnp.int32, (ROUTE_W, ROUTE_W), 1).astype(F32)
        passed = jnp.where((end_col <= tile) & (expert < N_EXP), 1.0, 0.0)
        tile_expert = jnp.minimum(jnp.sum(passed, axis=0, keepdims=True), N_EXP - 1.0)
        used = jnp.max(end_tile, axis=-1, keepdims=True)
        row = lax.broadcasted_iota(jnp.int32, (8, ROUTE_W), 0)
        plan_ref[...] = jnp.where(row == 0, tile_expert, jnp.where(row == 1, used, 0.0)).astype(jnp.int32)


@functools.lru_cache(maxsize=None)
def _upper_incl(n):
    r = np.arange(n)
    return (r[:, None] <= r[None, :]).astype(np.float32)


def _slots(meta, cnt, *, tm=2048):
    n_tiles = N_TOK // tm
    return pl.pallas_call(
        _slots_body,
        grid=(n_tiles,),
        in_specs=[
            pl.BlockSpec((tm, ROUTE_W), lambda i: (i, 0)),
            pl.BlockSpec((1, ROUTE_W), lambda i: (0, 0)),
            pl.BlockSpec((ROUTE_W, ROUTE_W), lambda i: (0, 0)),
        ],
        out_specs=[
            pl.BlockSpec((1, 8, tm), lambda i: (i, 0, 0)),
            pl.BlockSpec((8, ROUTE_W), lambda i: (0, 0)),
        ],
        out_shape=[
            jax.ShapeDtypeStruct((n_tiles, 8, tm), jnp.int32),
            jax.ShapeDtypeStruct((8, ROUTE_W), jnp.int32),
        ],
        compiler_params=_cparams(("arbitrary",)),
        name="moe_slots",
    )(meta, cnt, jnp.asarray(_upper_incl(ROUTE_W)).astype(BF16))


ROW_UNROLL = 8
N_DMA_LANES = 2


def _start_row_copies(tm, make_copy):
    def block(blk, carry):
        for j in range(ROW_UNROLL):
            lane = j % N_DMA_LANES
            for k in range(2):
                make_copy(blk, j, k, lane).start(priority=lane)
        return carry

    lax.fori_loop(0, tm // ROW_UNROLL, block, 0)


def _drain_row_copies(like_src, like_dst, sem):
    for lane in range(N_DMA_LANES):
        pltpu.make_async_copy(like_src, like_dst, sem.at[lane]).wait()


def _dispatch_body(t1_ref, s1_ref, t2_ref, s2_ref, hq_ref, xs_in_ref, xs_ref, sem, *, tm):
    del xs_in_ref
    slots = ((t1_ref, s1_ref), (t2_ref, s2_ref))

    def copy(blk, j, k, lane):
        i = blk * ROW_UNROLL + j
        return pltpu.make_async_copy(_packed_row(hq_ref, blk, j), _packed_row(xs_ref, slots[k][0][i], slots[k][1][i]),
                                     sem.at[lane])

    _start_row_copies(tm, copy)
    _drain_row_copies(hq_ref, xs_ref.at[pl.ds(0, tm // 8)], sem)


def _slot_specs(tm, ahead=0):
    last = N_TOK // tm - 1
    return [pl.BlockSpec((tm,), lambda t: (jnp.minimum(t + ahead, last),), memory_space=pltpu.SMEM) for _ in range(4)]


def _dispatch(slots, hq, init, *, tm=1024):
    return pl.pallas_call(
        functools.partial(_dispatch_body, tm=tm),
        grid=(N_TOK // tm,),
        in_specs=_slot_specs(tm) + [
            pl.BlockSpec(_packed_shape(tm), lambda t: (t, 0, 0, 0)),
            pl.BlockSpec(memory_space=pl.ANY),
        ],
        out_specs=pl.BlockSpec(memory_space=pl.ANY),
        out_shape=jax.ShapeDtypeStruct(_packed_shape(SORT_ROWS), jnp.uint32),
        input_output_aliases={5: 0},
        scratch_shapes=[pltpu.SemaphoreType.DMA((N_DMA_LANES,))],
        compiler_params=_cparams(("arbitrary",)),
        name="moe_dispatch",
    )(*slots, hq, init)


def _experts_body(plan_ref, xs_ref, wg_ref, wu_ref, wd_ref, ys_ref, wgu_ref, wdn_ref):
    t = pl.program_id(0)
    in_use = t < plan_ref[1, 0]
    new_expert = (t == 0) | (plan_ref[0, t] != plan_ref[0, jnp.maximum(t - 1, 0)])

    @pl.when(in_use & new_expert)
    def _():
        wgu_ref[:, :EXP_H] = wg_ref[0, 0].astype(BF16)
        wgu_ref[:, EXP_H:] = wu_ref[0, 0].astype(BF16)
        wdn_ref[...] = wd_ref[0, 0].astype(BF16)

    @pl.when(in_use)
    def _():
        x = _load_unpacked(xs_ref).astype(BF16)
        ab = jnp.dot(x, wgu_ref[...], preferred_element_type=F32)
        a, b = ab[:, :EXP_H], ab[:, EXP_H:]
        act = a * _sigmoid(a) * b
        _store_packed(ys_ref, jnp.dot(act.astype(BF16), wdn_ref[...], preferred_element_type=F32))


def _experts(plan, xs, w_gate, w_up, w_down, *, layer):
    def tile_idx(t, plan):
        return (jnp.minimum(t, jnp.maximum(plan[1, 0] - 1, 0)), 0, 0, 0)

    grid_spec = pltpu.PrefetchScalarGridSpec(
        num_scalar_prefetch=1,
        grid=(SORT_TILES,),
        in_specs=[
            pl.BlockSpec(_packed_shape(EXP_TILE), tile_idx),
            pl.BlockSpec((1, 1, D, EXP_H), lambda t, plan: (layer, plan[0, t], 0, 0)),
            pl.BlockSpec((1, 1, D, EXP_H), lambda t, plan: (layer, plan[0, t], 0, 0)),
            pl.BlockSpec((1, 1, EXP_H, D), lambda t, plan: (layer, plan[0, t], 0, 0)),
        ],
        out_specs=pl.BlockSpec(_packed_shape(EXP_TILE), tile_idx),
        scratch_shapes=[pltpu.VMEM((D, 2 * EXP_H), BF16), pltpu.VMEM((EXP_H, D), BF16)],
    )
    return pl.pallas_call(
        _experts_body,
        grid_spec=grid_spec,
        out_shape=jax.ShapeDtypeStruct(_packed_shape(SORT_ROWS), jnp.uint32),
        input_output_aliases={1: 0},
        compiler_params=_cparams(("arbitrary",)),
        name="moe_experts",
    )(plan, xs, w_gate, w_up, w_down)


def _combine_body(*refs, tm, final):
    cur_slots, nxt_slots = refs[0:4], refs[4:8]
    meta_ref, x_ref, ada_ref = refs[8:11]
    rest = refs[11:]
    if final:
        fgain_ref, ys_ref, oc_ref, ol_ref, y_ref, sem = rest
    else:
        ys_ref, o_ref, y_ref, sem = rest
    step = pl.program_id(0)
    n_steps = pl.num_programs(0)

    def fetch(slot_refs, buf):
        slots = ((slot_refs[0], slot_refs[1]), (slot_refs[2], slot_refs[3]))

        def copy(blk, j, k, lane):
            i = blk * ROW_UNROLL + j
            return pltpu.make_async_copy(_packed_row(ys_ref, slots[k][0][i], slots[k][1][i]),
                                         _packed_row(y_ref.at[buf, k], blk, j), sem.at[buf, lane])

        _start_row_copies(tm, copy)

    @pl.when(step == 0)
    def _():
        fetch(cur_slots, 0)

    @pl.when(step + 1 < n_steps)
    def _():
        fetch(nxt_slots, (step + 1) % 2)

    buf = step % 2
    _drain_row_copies(ys_ref.at[pl.ds(0, tm // 8)], y_ref.at[buf, 0], sem.at[buf])
    y_ref = y_ref.at[buf]
    meta = meta_ref[...]
    mix = (meta[:, META_W1:META_W1 + 1] * _load_unpacked(y_ref.at[0])
           + meta[:, META_W2:META_W2 + 1] * _load_unpacked(y_ref.at[1]))
    x = x_ref[...] + ada_ref[0, 5:6, :] * mix
    if not final:
        o_ref[...] = x
        return
    ms = jnp.mean(x * x, axis=-1, keepdims=True)
    y = x * lax.rsqrt(ms + EPS) * fgain_ref[...]
    is_ctx = pl.program_id(0) < N_CTX // tm

    @pl.when(is_ctx)
    def _():
        oc_ref[...] = y

    @pl.when(jnp.logical_not(is_ctx))
    def _():
        ol_ref[...] = y


def _combine(slots, ys, meta, x, ada, final_gain=None, *, tm=512):
    final = final_gain is not None
    extra_in = [pl.BlockSpec((1, D), lambda t: (0, 0))] if final else []
    extra_args = [final_gain.reshape(1, D)] if final else []
    if final:
        out_specs = _split_specs(tm, D)
        out_shape = [jax.ShapeDtypeStruct((N_CTX, D), F32), jax.ShapeDtypeStruct((N_LAT, D), F32)]
    else:
        out_specs = pl.BlockSpec((tm, D), lambda t: (t, 0))
        out_shape = jax.ShapeDtypeStruct((N_TOK, D), F32)
    return pl.pallas_call(
        functools.partial(_combine_body, tm=tm, final=final),
        grid=(N_TOK // tm,),
        in_specs=_slot_specs(tm) + _slot_specs(tm, ahead=1) + [
            pl.BlockSpec((tm, ROUTE_W), lambda t: (t, 0)),
            pl.BlockSpec((tm, D), lambda t: (t, 0)),
            pl.BlockSpec((1, 6, D), lambda t: (_cond_row(t, tm), 0, 0)),
        ] + extra_in + [pl.BlockSpec(memory_space=pl.ANY)],
        out_specs=out_specs,
        out_shape=out_shape,
        scratch_shapes=[pltpu.VMEM((2, 2) + _packed_shape(tm), jnp.uint32),
                        pltpu.SemaphoreType.DMA((2, N_DMA_LANES))],
        compiler_params=_cparams(("arbitrary",)),
        name="moe_combine",
    )(*slots, *slots, meta, x, ada, *extra_args, ys)


def _moe(routed, ada, w_gate, w_up, w_down, *, layer, sort_init, final_gain=None):
    x, hq, meta, cnt = routed
    pos, plan = _slots(meta, cnt)
    slots = [pos[:, r, :].reshape(N_TOK) for r in range(4)]
    xs = _dispatch(slots, hq, sort_init)
    ys = _experts(plan, xs, w_gate, w_up, w_down, layer=layer)
    return _combine(slots, ys, meta, x, ada, final_gain), ys


def kernel(x_prompt, x_sample, cache_k, cache_v, c, c_ctx, w_ada, b_ada, norm_mix, norm_ffn, attn_w_q, attn_w_kv, attn_q_norm, attn_k_norm, attn_w_o, hy_w_in, hy_b_in, hy_short_w, hy_short_b, hy_filt_w1, hy_filt_b1, hy_filt_w2, hy_filt_b2, hy_filt_w3, hy_filt_b3, hy_filt_freq, hy_log_decay, hy_skip, hy_w_out, hy_b_out, router_group_w, router_group_b, router_expert_w, router_expert_b, moe_w_gate, moe_w_up, moe_w_down, final_norm):
    depth = w_ada.shape[0]
    x_parts = (x_prompt.reshape(N_CTX, D), x_sample.reshape(N_LAT, D))
    cond8 = jnp.zeros((8, D), F32).at[0].set(c_ctx).at[1:1 + N_BATCH_LAT].set(c)
    ada_all = _ada(cond8, w_ada, b_ada).reshape(depth, 8, 6, D)
    new_k = []
    new_v = []
    sort_buf = jnp.zeros(_packed_shape(SORT_ROWS), jnp.uint32)
    for l in range(depth):
        ada = ada_all[l]
        if l % 2 == 0:
            a = l // 2
            w_qkv = jnp.concatenate([attn_w_q[a], attn_w_kv[a]], axis=1).astype(BF16)
            q, k_ctx, v_ctx, kc, vc, kl, vl = _qkv(x_parts, ada, norm_mix[l], w_qkv, attn_q_norm[a], attn_k_norm[a])
            new_k.append(k_ctx)
            new_v.append(v_ctx)
            o_ctx = _attention(q, [kc], [vc], row0=0, n_batch=N_BATCH_CTX, seq=L_CTX, tq=L_CTX,
                               n_kv=N_KV, stack=Q_PER_KV)
            past_keys = _lane_tile_heads(cache_k[:, a]).transpose(0, 1, 3, 2)
            o_lat = _attention(q, [kl, past_keys], [vl, _lane_tile_heads(cache_v[:, a])],
                               row0=N_CTX, n_batch=N_BATCH_LAT, seq=L_LAT, tq=1024, n_kv=1, stack=1)
            mixer = (o_ctx, o_lat, attn_w_o[a].astype(BF16), jnp.zeros((D,), F32))
        else:
            j = l // 2
            (x,) = x_parts
            up = _norm_proj(x, ada, norm_mix[l], hy_w_in[j].astype(BF16), hy_b_in[j], part=0)
            outs = []
            for row0, n_batch, L, tc in ((0, N_BATCH_CTX, L_CTX, D), (N_CTX, N_BATCH_LAT, L_LAT, 256)):
                cs = jnp.asarray(_dft_table(L)).astype(BF16)
                fr, fi, fn = _filter_spectra(L, cs, hy_filt_w1[j], hy_filt_b1[j], hy_filt_w2[j], hy_filt_b2[j],
                                             hy_filt_w3[j], hy_filt_b3[j], hy_filt_freq[j], hy_log_decay[j])
                outs.append(_hyena_conv(up, hy_short_w[j], hy_short_b[j], hy_skip[j], fr, fi, fn, cs,
                                        row0=row0, n_batch=n_batch, L=L, tc=tc))
            mixer = (outs[0], outs[1], hy_w_out[j].astype(BF16), hy_b_out[j])
        routed = _mix_route(*mixer, x_parts, ada, norm_ffn[l], router_group_w[l], router_group_b[l],
                            router_expert_w[l], router_expert_b[l])
        out, sort_buf = _moe(routed, ada, moe_w_gate, moe_w_up, moe_w_down, layer=l, sort_init=sort_buf,
                             final_gain=final_norm if l == depth - 1 else None)
        x_parts = (out,)
    y_ctx, y_lat = x_parts[0]
    return (y_ctx.reshape(N_BATCH_CTX, L_CTX, D), y_lat.reshape(N_BATCH_LAT, L_LAT, D),
            jnp.stack(new_k, axis=1), jnp.stack(new_v, axis=1))
```

```python
import functools
import math

import numpy as np
import jax
import jax.numpy as jnp
from jax import lax
from jax.experimental import pallas as pl
from jax.experimental.pallas import tpu as pltpu

F32 = jnp.float32
BF16 = jnp.bfloat16

D = 1024
N_BATCH_CTX = 32
L_CTX = 256
N_BATCH_LAT = 2
L_LAT = 2048
PAST = 256
N_CTX = N_BATCH_CTX * L_CTX
N_LAT = N_BATCH_LAT * L_LAT
N_TOK = N_CTX + N_LAT
GRID_W = 64
N_HEADS = 16
N_KV = 4
DH = 64
Q_PER_KV = N_HEADS // N_KV
KV_W = N_KV * DH
ROPE_THETA = 10000.0
FILTER_FEAT = 17
FEAT_PAD = 32
FILTER_HIDDEN = 64
N_GROUPS = 4
E_PER_G = 8
N_EXP = N_GROUPS * E_PER_G
EXP_H = D // 4
EPS = 1e-6
ROUTE_W = 128
RANK_BLOCK = 256
VMEM_LIMIT = 56 * 1024 * 1024


def _cparams(sem):
    return pltpu.CompilerParams(dimension_semantics=sem, vmem_limit_bytes=VMEM_LIMIT)


def _cond_row(i, tm):
    n_ctx_tiles = N_CTX // tm
    return jnp.where(i < n_ctx_tiles, 0, 1 + (i - n_ctx_tiles) // (L_LAT // tm))


def _split_specs(tm, width):
    n_ctx_tiles = N_CTX // tm
    return [pl.BlockSpec((tm, width), lambda i: (jnp.minimum(i, n_ctx_tiles - 1), 0)),
            pl.BlockSpec((tm, width), lambda i: (jnp.maximum(i - n_ctx_tiles, 0), 0))]


def _pick_split(i, tm, ctx_ref, lat_ref):
    return jnp.where(i < N_CTX // tm, ctx_ref[...], lat_ref[...])


def _sigmoid(x):
    return 1.0 / (1.0 + jnp.exp(-x))


@functools.lru_cache(maxsize=None)
def _dft_table(L):
    k = np.arange(L, dtype=np.int64)
    ang = (np.outer(k, k) % (2 * L)).astype(np.float64) * (math.pi / L)
    return np.concatenate([np.cos(ang), np.sin(ang)], axis=0).astype(np.float32)


@functools.lru_cache(maxsize=None)
def _filter_feats(L):
    t = np.linspace(0.0, 1.0, L, dtype=np.float64)[:, None]
    bands = np.linspace(1e-4, 7.0, 8, dtype=np.float64)[None, :]
    w = (2.0 * math.pi) * np.arange(L, dtype=np.float64)[:, None] / L
    feats = np.concatenate([t, np.cos(bands * w), -np.sin(bands * w)], axis=-1)
    out = np.zeros((L, FEAT_PAD), np.float32)
    out[:, :FILTER_FEAT] = feats
    return out


@functools.lru_cache(maxsize=None)
def _rope_tables(tm):
    pos = np.arange(L_LAT)
    row = (pos // GRID_W).astype(np.float64)
    col = (pos % GRID_W).astype(np.float64)
    axis_dim = DH // 2
    inv_freq = ROPE_THETA ** (-np.arange(0, axis_dim, 2, dtype=np.float64) / axis_dim)
    lane = np.arange(KV_W)
    d = lane % DH
    is_col = (d // axis_dim) == 1
    fi = d % (axis_dim // 2)
    first_half = (d % axis_dim) < (axis_dim // 2)
    p = np.where(is_col[None, :], col[:, None], row[:, None])
    ang = p * inv_freq[fi][None, :]
    cos = np.cos(ang)
    sin = np.sin(ang) * np.where(first_half, -1.0, 1.0)[None, :]
    cos = np.concatenate([np.ones((tm, KV_W)), cos], axis=0).astype(np.float32)
    sin = np.concatenate([np.zeros((tm, KV_W)), sin], axis=0).astype(np.float32)
    return cos, sin


@functools.lru_cache(maxsize=None)
def _head_sum_matrix():
    lane = np.arange(KV_W)
    return (lane[:, None] // DH == lane[None, :] // DH).astype(np.float32)


def _ada_body(c_ref, w_ref, b_ref, o_ref):
    c = c_ref[...]
    s = c * _sigmoid(c)
    o_ref[0] = jnp.dot(s.astype(BF16), w_ref[0].astype(BF16), preferred_element_type=F32) + b_ref[0]


def _ada(cond8, w_ada, b_ada):
    depth = w_ada.shape[0]
    tn = 1536
    return pl.pallas_call(
        _ada_body,
        grid=(depth, 6 * D // tn),
        in_specs=[
            pl.BlockSpec((8, D), lambda l, j: (0, 0)),
            pl.BlockSpec((1, D, tn), lambda l, j: (l, 0, j)),
            pl.BlockSpec((1, 1, tn), lambda l, j: (l, 0, j)),
        ],
        out_specs=pl.BlockSpec((1, 8, tn), lambda l, j: (l, 0, j)),
        out_shape=jax.ShapeDtypeStruct((depth, 8, 6 * D), F32),
        compiler_params=_cparams(("arbitrary", "arbitrary")),
        name="ada",
    )(cond8, w_ada, b_ada.reshape(depth, 1, 6 * D))


def _norm_mod(x, gain_ref, ada_ref, part):
    ms = jnp.mean(x * x, axis=-1, keepdims=True)
    y = x * lax.rsqrt(ms + EPS) * gain_ref[...]
    return y * (1.0 + ada_ref[0, part + 1:part + 2, :]) + ada_ref[0, part:part + 1, :]


def _proj_body(x_ref, ada_ref, gain_ref, w_ref, b_ref, o_ref, *, part):
    h = _norm_mod(x_ref[...], gain_ref, ada_ref, part)
    acc = jnp.dot(h.astype(BF16), w_ref[...], preferred_element_type=F32) + b_ref[...]
    o_ref[...] = acc.astype(o_ref.dtype)


def _norm_proj(x, ada, gain, w, b, *, part, tm=1024):
    n_out = w.shape[1]
    return pl.pallas_call(
        functools.partial(_proj_body, part=part),
        grid=(N_TOK // tm,),
        in_specs=[
            pl.BlockSpec((tm, D), lambda i: (i, 0)),
            pl.BlockSpec((1, 6, D), lambda i: (_cond_row(i, tm), 0, 0)),
            pl.BlockSpec((1, D), lambda i: (0, 0)),
            pl.BlockSpec((D, n_out), lambda i: (0, 0)),
            pl.BlockSpec((1, n_out), lambda i: (0, 0)),
        ],
        out_specs=pl.BlockSpec((tm, n_out), lambda i: (i, 0)),
        out_shape=jax.ShapeDtypeStruct((N_TOK, n_out), BF16),
        compiler_params=_cparams(("arbitrary",)),
        name="norm_proj",
    )(x, ada, gain.reshape(1, D), w, b.reshape(1, n_out))


def _head_rms_rope(x, gain, hs, cos, sin, lane):
    ss = jnp.dot((x * x).astype(BF16), hs, preferred_element_type=F32)
    xn = x * lax.rsqrt(ss * (1.0 / DH) + EPS) * gain
    fwd = pltpu.roll(xn, KV_W - DH // 4, 1)
    bwd = pltpu.roll(xn, DH // 4, 1)
    partner = jnp.where((lane & (DH // 4)) == 0, fwd, bwd)
    return xn * cos + partner * sin


@functools.lru_cache(maxsize=None)
def _lane_tile_matrix():
    col = np.arange(N_KV * KV_W)
    src = (col // KV_W) * DH + col % DH
    return (np.arange(KV_W)[:, None] == src[None, :]).astype(np.float32)


def _qkv_body(*refs, tm, split_x):
    n_x = 2 if split_x else 1
    x_refs = refs[:n_x]
    (ada_ref, gain_ref, w_ref, qg_ref, kg_ref, hs_ref, cos_ref, sin_ref, tile_ref, tile_t_ref,
     q_ref, newk_ref, newv_ref, kc_ref, vc_ref, kl_ref, vl_ref) = refs[n_x:]
    step = pl.program_id(0)
    x = _pick_split(step, tm, *x_refs) if split_x else x_refs[0][...]
    h = _norm_mod(x, gain_ref, ada_ref, 0)
    acc = jnp.dot(h.astype(BF16), w_ref[...], preferred_element_type=F32)
    hs = hs_ref[...]
    cos = cos_ref[...]
    sin = sin_ref[...]
    lane = lax.broadcasted_iota(jnp.int32, (1, KV_W), 1)
    for c in range(N_KV):
        qc = _head_rms_rope(acc[:, c * KV_W:(c + 1) * KV_W], qg_ref[...], hs, cos, sin, lane)
        q_ref[:, c * KV_W:(c + 1) * KV_W] = (qc * (DH ** -0.5)).astype(BF16)
    k = _head_rms_rope(acc[:, D:D + KV_W], kg_ref[...], hs, cos, sin, lane)
    v = acc[:, D + KV_W:D + 2 * KV_W]
    k4t = lax.dot_general(tile_t_ref[...], k.astype(BF16), (((1,), (1,)), ((), ())),
                          preferred_element_type=F32).astype(BF16)
    v4 = jnp.dot(v.astype(BF16), tile_ref[...], preferred_element_type=F32).astype(BF16)
    is_ctx = step < N_CTX // tm

    @pl.when(is_ctx)
    def _():
        for bb in range(tm // L_CTX):
            rows = slice(bb * L_CTX, (bb + 1) * L_CTX)
            for hd in range(N_KV):
                newk_ref[bb, hd] = k[rows, hd * DH:(hd + 1) * DH]
                newv_ref[bb, hd] = v[rows, hd * DH:(hd + 1) * DH]
                kc_ref[bb, hd] = k4t[hd * KV_W:(hd + 1) * KV_W, rows]
                vc_ref[bb, hd] = v4[rows, hd * KV_W:(hd + 1) * KV_W]

    @pl.when(jnp.logical_not(is_ctx))
    def _():
        for hd in range(N_KV):
            kl_ref[0, hd] = k4t[hd * KV_W:(hd + 1) * KV_W, :]
            vl_ref[0, hd] = v4[:, hd * KV_W:(hd + 1) * KV_W]


def _qkv(x_parts, ada, gain, w_qkv, q_gain, k_gain):
    tm = 2 * L_CTX
    per_tile = tm // L_CTX
    cos, sin = _rope_tables(tm)
    split_x = len(x_parts) == 2
    x_specs = _split_specs(tm, D) if split_x else [pl.BlockSpec((tm, D), lambda i: (i, 0))]
    n_ctx_tiles = N_CTX // tm
    lat_tiles = L_LAT // tm

    def rope_idx(i):
        return (jnp.where(i < n_ctx_tiles, 0, 1 + (i - n_ctx_tiles) % lat_tiles), 0)

    def ctx_idx(i):
        return (jnp.minimum(i, n_ctx_tiles - 1), 0, 0, 0)

    def lat_idx(i):
        j = jnp.maximum(i - n_ctx_tiles, 0)
        return (j // lat_tiles, 0, j % lat_tiles, 0)

    def lat_idx_t(i):
        j = jnp.maximum(i - n_ctx_tiles, 0)
        return (j // lat_tiles, 0, 0, j % lat_tiles)

    const = lambda i: (0, 0)
    kv_ctx = jax.ShapeDtypeStruct((N_BATCH_CTX, N_KV, L_CTX, DH), F32)
    tiled_ctx = jax.ShapeDtypeStruct((N_BATCH_CTX, N_KV, L_CTX, KV_W), BF16)
    tiled_lat = jax.ShapeDtypeStruct((N_BATCH_LAT, N_KV, L_LAT, KV_W), BF16)
    keys_ctx = jax.ShapeDtypeStruct((N_BATCH_CTX, N_KV, KV_W, L_CTX), BF16)
    keys_lat = jax.ShapeDtypeStruct((N_BATCH_LAT, N_KV, KV_W, L_LAT), BF16)
    return pl.pallas_call(
        functools.partial(_qkv_body, tm=tm, split_x=split_x),
        grid=(N_TOK // tm,),
        in_specs=x_specs + [
            pl.BlockSpec((1, 6, D), lambda i: (_cond_row(i, tm), 0, 0)),
            pl.BlockSpec((1, D), const),
            pl.BlockSpec((D, D + 2 * KV_W), const),
            pl.BlockSpec((1, KV_W), const),
            pl.BlockSpec((1, KV_W), const),
            pl.BlockSpec((KV_W, KV_W), const),
            pl.BlockSpec((tm, KV_W), rope_idx),
            pl.BlockSpec((tm, KV_W), rope_idx),
            pl.BlockSpec((KV_W, N_KV * KV_W), const),
            pl.BlockSpec((N_KV * KV_W, KV_W), const),
        ],
        out_specs=[
            pl.BlockSpec((tm, D), lambda i: (i, 0)),
            pl.BlockSpec((per_tile, N_KV, L_CTX, DH), ctx_idx),
            pl.BlockSpec((per_tile, N_KV, L_CTX, DH), ctx_idx),
            pl.BlockSpec((per_tile, N_KV, KV_W, L_CTX), ctx_idx),
            pl.BlockSpec((per_tile, N_KV, L_CTX, KV_W), ctx_idx),
            pl.BlockSpec((1, N_KV, KV_W, tm), lat_idx_t),
            pl.BlockSpec((1, N_KV, tm, KV_W), lat_idx),
        ],
        out_shape=[jax.ShapeDtypeStruct((N_TOK, D), BF16), kv_ctx, kv_ctx, keys_ctx, tiled_ctx, keys_lat, tiled_lat],
        compiler_params=_cparams(("arbitrary",)),
        name="qkv",
    )(*x_parts, ada, gain.reshape(1, D), w_qkv,
      jnp.tile(q_gain, Q_PER_KV).reshape(1, KV_W), jnp.tile(k_gain, N_KV).reshape(1, KV_W),
      jnp.asarray(_head_sum_matrix()).astype(BF16), jnp.asarray(cos), jnp.asarray(sin),
      jnp.asarray(_lane_tile_matrix()).astype(BF16), jnp.asarray(_lane_tile_matrix().T.copy()).astype(BF16))


def _attn_body(*refs, n_kv, stack, n_seg):
    q_ref, k_refs, v_refs, o_ref = refs[0], refs[1:1 + n_seg], refs[1 + n_seg:1 + 2 * n_seg], refs[-1]
    lane = lax.broadcasted_iota(jnp.int32, (1, KV_W), 1)
    masks = [(lane >> 6) == g for g in range(Q_PER_KV)]
    tq = q_ref.shape[0]
    for kv in range(n_kv):
        q = q_ref[:, kv * KV_W:(kv + 1) * KV_W]
        out = jnp.zeros((tq, KV_W), F32)
        for c in range(0, Q_PER_KV, stack):
            pair = masks[c:c + stack]
            ones_blk = (c + stack) % Q_PER_KV if stack < Q_PER_KV else None
            stacked = jnp.concatenate([jnp.where(m, q, jnp.zeros_like(q)) for m in pair], axis=0)
            scores = [jnp.dot(stacked, k_ref[0, kv], preferred_element_type=F32) for k_ref in k_refs]
            top = functools.reduce(jnp.maximum, [jnp.max(s, axis=-1, keepdims=True) for s in scores])
            probs = [jnp.exp((s - top).astype(BF16)) for s in scores]
            if ones_blk is None:
                vals = [v_ref[0, kv] for v_ref in v_refs]
            else:
                vals = [jnp.where(masks[ones_blk], jnp.ones((), BF16), v_ref[0, kv]) for v_ref in v_refs]
            og = sum(jnp.dot(p, v, preferred_element_type=F32) for p, v in zip(probs, vals))
            if ones_blk is None:
                denom = sum(jnp.sum(p.astype(F32), axis=-1, keepdims=True) for p in probs)
            else:
                denom = og[:, ones_blk * DH:ones_blk * DH + 1]
            og = og * (1.0 / denom)
            for g, m in enumerate(pair):
                out = out + jnp.where(m, og[g * tq:(g + 1) * tq], 0.0)
        o_ref[:, kv * KV_W:(kv + 1) * KV_W] = out.astype(BF16)


def _attention(q, keys, values, *, row0, n_batch, seq, tq, n_kv, stack):
    per_b = seq // tq
    base = row0 // tq
    kv_specs = [pl.BlockSpec((1, n_kv) + a.shape[2:], lambda b, h, i: (b, h, 0, 0)) for a in keys + values]
    return pl.pallas_call(
        functools.partial(_attn_body, n_kv=n_kv, stack=stack, n_seg=len(keys)),
        grid=(n_batch, N_KV // n_kv, per_b),
        in_specs=[pl.BlockSpec((tq, n_kv * KV_W), lambda b, h, i: (base + b * per_b + i, h))] + kv_specs,
        out_specs=pl.BlockSpec((tq, n_kv * KV_W), lambda b, h, i: (b * per_b + i, h)),
        out_shape=jax.ShapeDtypeStruct((n_batch * seq, D), BF16),
        compiler_params=_cparams(("arbitrary", "arbitrary", "arbitrary")),
        name="attention",
    )(q, *keys, *values)


def _lane_tile_heads(x):
    return jnp.tile(x.astype(BF16), (1, 1, 1, KV_W // DH))


def _alt_sign(L):
    row = lax.broadcasted_iota(jnp.int32, (L, 1), 0)
    return row, jnp.where((row & 1) == 0, 1.0, -1.0)


def _split_dot(a, b):
    a_hi, b_hi = a.astype(BF16), b.astype(BF16)
    a_lo = (a - a_hi.astype(F32)).astype(BF16)
    b_lo = (b - b_hi.astype(F32)).astype(BF16)
    return (jnp.dot(a_hi, b_hi, preferred_element_type=F32) + jnp.dot(a_lo, b_hi, preferred_element_type=F32)
            + jnp.dot(a_hi, b_lo, preferred_element_type=F32))


def _filter_body(feat_ref, w1_ref, b1_ref, w2_ref, b2_ref, fq_ref, w3_ref, b3_ref, ld_ref, cs_ref,
                 fr_ref, fi_ref, fn_ref, hid_ref, *, L):
    feats = feat_ref[...]

    @pl.when(pl.program_id(0) == 0)
    def _():
        fq = fq_ref[...]
        h1 = jnp.sin(fq * (_split_dot(feats, w1_ref[...]) + b1_ref[...]))
        hid_ref[...] = jnp.sin(fq * (_split_dot(h1, w2_ref[...]) + b2_ref[...])).astype(BF16)

    t = feats[:, 0:1]
    row, alt = _alt_sign(L)
    filt = []
    for j in range(4):
        raw = jnp.dot(hid_ref[...], w3_ref[j].astype(BF16), preferred_element_type=F32) + b3_ref[j:j + 1, :]
        filt.append(raw * jnp.exp(-t * jnp.exp(ld_ref[j:j + 1, :])))
    for o in range(2):
        hf, hb = filt[2 * o], filt[2 * o + 1]
        l1 = jnp.sum(jnp.abs(hf), axis=0, keepdims=True) + jnp.sum(jnp.abs(hb), axis=0, keepdims=True)
        inv = 1.0 / (l1 + EPS)
        sym = (hf + hb) * inv
        asym = (hb - hf) * inv
        fr = jnp.dot(cs_ref[0:L, :], sym.astype(BF16), preferred_element_type=F32) * (1.0 / L)
        fr_ref[o] = jnp.where(row == 0, 0.5 * fr, fr)
        fi_ref[o] = jnp.dot(cs_ref[L:2 * L, :], asym.astype(BF16), preferred_element_type=F32) * (1.0 / L)
        fn_ref[o] = jnp.sum(sym * alt, axis=0, keepdims=True) * (0.5 / L)


def _filter_spectra(L, cs, w1, b1, w2, b2, w3, b3, freq, log_decay, *, tc=256):
    w1p = jnp.zeros((FEAT_PAD, FILTER_HIDDEN), F32).at[:FILTER_FEAT].set(w1)
    w3r = w3.reshape(FILTER_HIDDEN, 4, D).transpose(1, 0, 2)
    const = lambda j: (0, 0)
    return pl.pallas_call(
        functools.partial(_filter_body, L=L),
        grid=(D // tc,),
        in_specs=[
            pl.BlockSpec((L, FEAT_PAD), const),
            pl.BlockSpec((FEAT_PAD, FILTER_HIDDEN), const),
            pl.BlockSpec((1, FILTER_HIDDEN), const),
            pl.BlockSpec((FILTER_HIDDEN, FILTER_HIDDEN), const),
            pl.BlockSpec((1, FILTER_HIDDEN), const),
            pl.BlockSpec((1, FILTER_HIDDEN), const),
            pl.BlockSpec((4, FILTER_HIDDEN, tc), lambda j: (0, 0, j)),
            pl.BlockSpec((4, tc), lambda j: (0, j)),
            pl.BlockSpec((4, tc), lambda j: (0, j)),
            pl.BlockSpec((2 * L, L), const, pipeline_mode=pl.Buffered(1)),
        ],
        out_specs=[
            pl.BlockSpec((2, L, tc), lambda j: (0, 0, j)),
            pl.BlockSpec((2, L, tc), lambda j: (0, 0, j)),
            pl.BlockSpec((2, 1, tc), lambda j: (0, 0, j)),
        ],
        out_shape=[
            jax.ShapeDtypeStruct((2, L, D), F32),
            jax.ShapeDtypeStruct((2, L, D), F32),
            jax.ShapeDtypeStruct((2, 1, D), F32),
        ],
        scratch_shapes=[pltpu.VMEM((L, FILTER_HIDDEN), BF16)],
        compiler_params=_cparams(("arbitrary",)),
        name="hyena_filter",
    )(jnp.asarray(_filter_feats(L)), w1p, b1.reshape(1, -1), w2, b2.reshape(1, -1), freq.reshape(1, -1),
      w3r, b3.reshape(4, D), log_decay.reshape(4, D), cs)


FREQ_CHUNK = 1024


def _hyconv_body(x1_ref, x2_ref, v_ref, sw_ref, sb_ref, skip_ref, fr_ref, fi_ref, fn_ref, cs_ref, o_ref,
                 z_ref, g_ref, zb_ref, yr_ref, yi_ref, *, L):
    row, alt = _alt_sign(L)
    kc = min(FREQ_CHUNK, L)
    chunk_row = lax.broadcasted_iota(jnp.int32, (kc, 1), 0)

    def short_conv(u_ref, p):
        u = u_ref[...].astype(F32)
        prev = jnp.where(row == 0, 0.0, pltpu.roll(u, 1, 0))
        nxt = jnp.where(row == L - 1, 0.0, pltpu.roll(u, L - 1, 0))
        return (prev * sw_ref[0, p:p + 1, :] + u * sw_ref[1, p:p + 1, :] + nxt * sw_ref[2, p:p + 1, :]
                + sb_ref[p:p + 1, :])

    z_ref[...] = short_conv(v_ref, 2)
    g_ref[0] = short_conv(x1_ref, 0)
    g_ref[1] = short_conv(x2_ref, 1)
    for o in range(2):
        z = z_ref[...]
        zb_ref[...] = z.astype(BF16)
        nyq = jnp.sum(z * alt, axis=0, keepdims=True) * fn_ref[o]

        def to_freq(c, carry):
            lo = pl.multiple_of(c * kc, kc)
            zr = jnp.dot(cs_ref[pl.ds(lo, kc), :], zb_ref[...], preferred_element_type=F32)
            zs = jnp.dot(cs_ref[pl.ds(L + lo, kc), :], zb_ref[...], preferred_element_type=F32)
            fr = fr_ref[o, pl.ds(lo, kc), :]
            fi = fi_ref[o, pl.ds(lo, kc), :]
            yr_ref[pl.ds(lo, kc), :] = (zr * fr + zs * fi).astype(BF16)
            yi_ref[pl.ds(lo, kc), :] = (zr * fi - zs * fr).astype(BF16)
            return carry

        lax.fori_loop(0, L // kc, to_freq, 0, unroll=True)

        def to_time(c, carry):
            lo = pl.multiple_of(c * kc, kc)
            y = (jnp.dot(cs_ref[pl.ds(lo, kc), :], yr_ref[...], preferred_element_type=F32)
                 - jnp.dot(cs_ref[pl.ds(L + lo, kc), :], yi_ref[...], preferred_element_type=F32))
            y = y + jnp.where(((chunk_row + lo) & 1) == 0, nyq, -nyq)
            z_ref[pl.ds(lo, kc), :] = g_ref[o, pl.ds(lo, kc), :] * (y + skip_ref[o:o + 1, :] * z_ref[pl.ds(lo, kc), :])
            return carry

        lax.fori_loop(0, L // kc, to_time, 0, unroll=True)
    o_ref[...] = z_ref[...].astype(BF16)


def _hyena_conv(up, sw, sb, skip, fr, fi, fn, cs, *, row0, n_batch, L, tc):
    n_ct = D // tc
    base = row0 // L
    u_spec = lambda p: pl.BlockSpec((L, tc), lambda j, b: (base + b, p * n_ct + j))
    once = pl.Buffered(1)
    return pl.pallas_call(
        functools.partial(_hyconv_body, L=L),
        grid=(n_ct, n_batch),
        in_specs=[
            u_spec(0), u_spec(1), u_spec(2),
            pl.BlockSpec((3, 3, tc), lambda j, b: (0, 0, j)),
            pl.BlockSpec((3, tc), lambda j, b: (0, j)),
            pl.BlockSpec((2, tc), lambda j, b: (0, j)),
            pl.BlockSpec((2, L, tc), lambda j, b: (0, 0, j), pipeline_mode=once),
            pl.BlockSpec((2, L, tc), lambda j, b: (0, 0, j), pipeline_mode=once),
            pl.BlockSpec((2, 1, tc), lambda j, b: (0, 0, j)),
            pl.BlockSpec((2 * L, L), lambda j, b: (0, 0), pipeline_mode=once),
        ],
        out_specs=pl.BlockSpec((L, tc), lambda j, b: (b, j)),
        out_shape=jax.ShapeDtypeStruct((n_batch * L, D), BF16),
        scratch_shapes=[
            pltpu.VMEM((L, tc), F32),
            pltpu.VMEM((2, L, tc), F32),
            pltpu.VMEM((L, tc), BF16),
            pltpu.VMEM((L, tc), BF16),
            pltpu.VMEM((L, tc), BF16),
        ],
        compiler_params=_cparams(("arbitrary", "arbitrary")),
        name="hyena_conv",
    )(up, up, up, sw.reshape(3, 3, D), sb.reshape(3, D), skip, fr, fi, fn, cs)


EXP_TILE = 512
SORT_TILES = (2 * N_TOK) // EXP_TILE + N_EXP
SORT_ROWS = SORT_TILES * EXP_TILE
N_QUARTERS = 4
QUARTER_W = D // (2 * N_QUARTERS)
META_I1, META_I2, META_R1, META_R2, META_W1, META_W2 = range(6)
HI_HALF = 0xFFFF0000


def _pack_pairs(x):
    bits = pltpu.bitcast(x.astype(BF16).astype(F32), jnp.uint32)
    return (bits[:, :QUARTER_W] >> 16) | bits[:, QUARTER_W:]


def _unpack_pairs(w):
    return pltpu.bitcast(w << 16, F32), pltpu.bitcast(w & jnp.uint32(HI_HALF), F32)


def _packed_shape(rows):
    return (rows // 8, N_QUARTERS, 8, QUARTER_W)


def _store_packed(ref, x):
    for q in range(N_QUARTERS):
        ref[:, q] = _pack_pairs(x[:, q * 2 * QUARTER_W:(q + 1) * 2 * QUARTER_W]).reshape(-1, 8, QUARTER_W)


def _load_unpacked(ref):
    halves = []
    for q in range(N_QUARTERS):
        halves.extend(_unpack_pairs(ref[:, q].reshape(-1, QUARTER_W)))
    return jnp.concatenate(halves, axis=1)


def _packed_row(ref, tile, sublane):
    return ref.at[tile, :, sublane, :]


def _lane_put(lane, values):
    out = jnp.where(lane == 0, values[0], 0.0)
    for k in range(1, len(values)):
        out = out + jnp.where(lane == k, values[k], 0.0)
    return out


def _route_body(*refs, tm, split_x):
    n_x = 2 if split_x else 1
    ac_ref, al_ref, w_ref, b_ref = refs[:4]
    x_refs = refs[4:4 + n_x]
    (ada_ref, gain_ref, wr_hi_ref, wr_lo_ref, br_ref, tri_ref,
     xo_ref, hq_ref, meta_ref, cnt_ref, carry_ref) = refs[4 + n_x:]
    step = pl.program_id(0)

    @pl.when(step == 0)
    def _():
        carry_ref[...] = jnp.zeros_like(carry_ref)

    mixed = jnp.dot(_pick_split(step, tm, ac_ref, al_ref), w_ref[...], preferred_element_type=F32) + b_ref[...]
    x = _pick_split(step, tm, *x_refs) if split_x else x_refs[0][...]
    x = x + ada_ref[0, 2:3, :] * mixed
    xo_ref[...] = x
    h = _norm_mod(x, gain_ref, ada_ref, 3)
    _store_packed(hq_ref, h)
    h_hi = h.astype(BF16)
    h_lo = (h - h_hi.astype(F32)).astype(BF16)
    both = jnp.dot(h_hi, jnp.concatenate([wr_hi_ref[...], wr_lo_ref[...]], axis=1), preferred_element_type=F32)
    logits = (both[:, :ROUTE_W] + both[:, ROUTE_W:]
              + jnp.dot(h_lo, wr_hi_ref[...], preferred_element_type=F32) + br_ref[...])
    lane = lax.broadcasted_iota(jnp.int32, (1, ROUTE_W), 1)
    lane_f = lane.astype(F32)
    group_of_lane = (lane >> 3).astype(F32)
    neg = -jnp.inf
    big = float(ROUTE_W)
    is_g = (lane >= N_EXP) & (lane < N_EXP + N_GROUPS)
    gl = jnp.where(is_g, logits, neg)
    gmax = jnp.max(gl, axis=-1, keepdims=True)
    gidx = jnp.min(jnp.where(gl == gmax, lane_f - N_EXP, big), axis=-1, keepdims=True)
    g_top = 1.0 / jnp.sum(jnp.where(is_g, jnp.exp(gl - gmax), 0.0), axis=-1, keepdims=True)
    in_group = (lane < N_EXP) & (group_of_lane == gidx)
    el = jnp.where(in_group, logits, neg)
    v1 = jnp.max(el, axis=-1, keepdims=True)
    i1 = jnp.min(jnp.where(el == v1, lane_f, big), axis=-1, keepdims=True)
    el2 = jnp.where(lane_f == i1, neg, el)
    v2 = jnp.max(el2, axis=-1, keepdims=True)
    i2 = jnp.min(jnp.where(el2 == v2, lane_f, big), axis=-1, keepdims=True)
    r = jnp.exp(v2 - v1)
    w1 = g_top / (1.0 + r)
    w2 = g_top * r / (1.0 + r)
    sel = jnp.where((lane_f == i1) | (lane_f == i2), 1.0, 0.0)
    sel_b = sel.astype(BF16)
    before = carry_ref[...]
    ranks = []
    for b in range(tm // RANK_BLOCK):
        rows = slice(b * RANK_BLOCK, (b + 1) * RANK_BLOCK)
        ranks.append(jnp.dot(tri_ref[...], sel_b[rows], preferred_element_type=F32) + before)
        before = before + jnp.sum(sel[rows], axis=0, keepdims=True)
    rank = jnp.concatenate(ranks, axis=0)
    r1 = jnp.sum(jnp.where(lane_f == i1, rank, 0.0), axis=-1, keepdims=True)
    r2 = jnp.sum(jnp.where(lane_f == i2, rank, 0.0), axis=-1, keepdims=True)
    carry_ref[...] = before
    cnt_ref[...] = carry_ref[...]
    meta_ref[...] = _lane_put(lane, (i1, i2, r1, r2, w1, w2))


@functools.lru_cache(maxsize=None)
def _strict_lower(n):
    r = np.arange(n)
    return (r[None, :] < r[:, None]).astype(np.float32)


def _mix_route(a_ctx, a_lat, w_out, b_out, x_parts, ada, gain, wg, bg, we, be, *, tm=1024):
    k = a_ctx.shape[1]
    assert tm % RANK_BLOCK == 0
    split_x = len(x_parts) == 2
    x_specs = _split_specs(tm, D) if split_x else [pl.BlockSpec((tm, D), lambda i: (i, 0))]
    wr = jnp.zeros((D, ROUTE_W), F32).at[:, :N_EXP].set(we.reshape(D, N_EXP)).at[:, N_EXP:N_EXP + N_GROUPS].set(wg)
    br = jnp.zeros((1, ROUTE_W), F32).at[0, :N_EXP].set(be.reshape(N_EXP)).at[0, N_EXP:N_EXP + N_GROUPS].set(bg)
    wr_hi = wr.astype(BF16)
    wr_lo = (wr - wr_hi.astype(F32)).astype(BF16)
    return pl.pallas_call(
        functools.partial(_route_body, tm=tm, split_x=split_x),
        grid=(N_TOK // tm,),
        in_specs=_split_specs(tm, k) + [
            pl.BlockSpec((k, D), lambda i: (0, 0)),
            pl.BlockSpec((1, D), lambda i: (0, 0)),
        ] + x_specs + [
            pl.BlockSpec((1, 6, D), lambda i: (_cond_row(i, tm), 0, 0)),
            pl.BlockSpec((1, D), lambda i: (0, 0)),
            pl.BlockSpec((D, ROUTE_W), lambda i: (0, 0)),
            pl.BlockSpec((D, ROUTE_W), lambda i: (0, 0)),
            pl.BlockSpec((1, ROUTE_W), lambda i: (0, 0)),
            pl.BlockSpec((RANK_BLOCK, RANK_BLOCK), lambda i: (0, 0)),
        ],
        out_specs=[
            pl.BlockSpec((tm, D), lambda i: (i, 0)),
            pl.BlockSpec(_packed_shape(tm), lambda i: (i, 0, 0, 0)),
            pl.BlockSpec((tm, ROUTE_W), lambda i: (i, 0)),
            pl.BlockSpec((1, ROUTE_W), lambda i: (0, 0)),
        ],
        out_shape=[
            jax.ShapeDtypeStruct((N_TOK, D), F32),
            jax.ShapeDtypeStruct(_packed_shape(N_TOK), jnp.uint32),
            jax.ShapeDtypeStruct((N_TOK, ROUTE_W), F32),
            jax.ShapeDtypeStruct((1, ROUTE_W), F32),
        ],
        scratch_shapes=[pltpu.VMEM((1, ROUTE_W), F32)],
        compiler_params=_cparams(("arbitrary",)),
        name="mix_route",
    )(a_ctx, a_lat, w_out, b_out.reshape(1, D), *x_parts, ada, gain.reshape(1, D), wr_hi, wr_lo, br,
      jnp.asarray(_strict_lower(RANK_BLOCK)).astype(BF16))


def _slots_body(meta_ref, cnt_ref, upper_ref, pos_ref, plan_ref):
    lane = lax.broadcasted_iota(jnp.int32, (1, ROUTE_W), 1)
    lane_f = lane.astype(F32)
    tiles = jnp.floor((cnt_ref[...] + (EXP_TILE - 1.0)) * (1.0 / EXP_TILE))
    end_tile = jnp.dot(jnp.broadcast_to(tiles, (8, ROUTE_W)).astype(BF16), upper_ref[...],
                       preferred_element_type=F32)[0:1]
    start_row = (end_tile - tiles) * EXP_TILE
    meta = meta_ref[...]
    i1 = meta[:, META_I1:META_I1 + 1]
    i2 = meta[:, META_I2:META_I2 + 1]
    p1 = jnp.sum(jnp.where(lane_f == i1, start_row, 0.0), axis=-1, keepdims=True) + meta[:, META_R1:META_R1 + 1]
    p2 = jnp.sum(jnp.where(lane_f == i2, start_row, 0.0), axis=-1, keepdims=True) + meta[:, META_R2:META_R2 + 1]
    hi1, hi2 = jnp.floor(p1 * 0.125), jnp.floor(p2 * 0.125)
    pos_ref[0] = _lane_put(lane, (hi1, p1 - 8.0 * hi1, hi2, p2 - 8.0 * hi2)).T[:8, :].astype(jnp.int32)

    @pl.when(pl.program_id(0) == 0)
    def _():
        end_col = jnp.broadcast_to(end_tile, (ROUTE_W, ROUTE_W)).T
        expert = lax.broadcasted_iota(jnp.int32, (ROUTE_W, ROUTE_W), 0)
        tile = lax.broadcasted_iota(jnp.int32, (ROUTE_W, ROUTE_W), 1).astype(F32)
        passed = jnp.where((end_col <= tile) & (expert < N_EXP), 1.0, 0.0)
        tile_expert = jnp.minimum(jnp.sum(passed, axis=0, keepdims=True), N_EXP - 1.0)
        used = jnp.max(end_tile, axis=-1, keepdims=True)
        row = lax.broadcasted_iota(jnp.int32, (8, ROUTE_W), 0)
        plan_ref[...] = jnp.where(row == 0, tile_expert, jnp.where(row == 1, used, 0.0)).astype(jnp.int32)


@functools.lru_cache(maxsize=None)
def _upper_incl(n):
    r = np.arange(n)
    return (r[:, None] <= r[None, :]).astype(np.float32)


def _slots(meta, cnt, *, tm=2048):
    n_tiles = N_TOK // tm
    return pl.pallas_call(
        _slots_body,
        grid=(n_tiles,),
        in_specs=[
            pl.BlockSpec((tm, ROUTE_W), lambda i: (i, 0)),
            pl.BlockSpec((1, ROUTE_W), lambda i: (0, 0)),
            pl.BlockSpec((ROUTE_W, ROUTE_W), lambda i: (0, 0)),
        ],
        out_specs=[
            pl.BlockSpec((1, 8, tm), lambda i: (i, 0, 0)),
            pl.BlockSpec((8, ROUTE_W), lambda i: (0, 0)),
        ],
        out_shape=[
            jax.ShapeDtypeStruct((n_tiles, 8, tm), jnp.int32),
            jax.ShapeDtypeStruct((8, ROUTE_W), jnp.int32),
        ],
        compiler_params=_cparams(("arbitrary",)),
        name="moe_slots",
    )(meta, cnt, jnp.asarray(_upper_incl(ROUTE_W)).astype(BF16))


ROW_UNROLL = 8
N_DMA_LANES = 2


def _start_row_copies(tm, make_copy):
    def block(blk, carry):
        for j in range(ROW_UNROLL):
            lane = j % N_DMA_LANES
            for k in range(2):
                make_copy(blk, j, k, lane).start(priority=lane)
        return carry

    lax.fori_loop(0, tm // ROW_UNROLL, block, 0)


def _drain_row_copies(like_src, like_dst, sem):
    for lane in range(N_DMA_LANES):
        pltpu.make_async_copy(like_src, like_dst, sem.at[lane]).wait()


def _dispatch_body(t1_ref, s1_ref, t2_ref, s2_ref, hq_ref, xs_in_ref, xs_ref, sem, *, tm):
    del xs_in_ref
    slots = ((t1_ref, s1_ref), (t2_ref, s2_ref))

    def copy(blk, j, k, lane):
        i = blk * ROW_UNROLL + j
        return pltpu.make_async_copy(_packed_row(hq_ref, blk, j), _packed_row(xs_ref, slots[k][0][i], slots[k][1][i]),
                                     sem.at[lane])

    _start_row_copies(tm, copy)
    _drain_row_copies(hq_ref, xs_ref.at[pl.ds(0, tm // 8)], sem)


def _slot_specs(tm, ahead=0):
    last = N_TOK // tm - 1
    return [pl.BlockSpec((tm,), lambda t: (jnp.minimum(t + ahead, last),), memory_space=pltpu.SMEM) for _ in range(4)]


def _dispatch(slots, hq, init, *, tm=1024):
    return pl.pallas_call(
        functools.partial(_dispatch_body, tm=tm),
        grid=(N_TOK // tm,),
        in_specs=_slot_specs(tm) + [
            pl.BlockSpec(_packed_shape(tm), lambda t: (t, 0, 0, 0)),
            pl.BlockSpec(memory_space=pl.ANY),
        ],
        out_specs=pl.BlockSpec(memory_space=pl.ANY),
        out_shape=jax.ShapeDtypeStruct(_packed_shape(SORT_ROWS), jnp.uint32),
        input_output_aliases={5: 0},
        scratch_shapes=[pltpu.SemaphoreType.DMA((N_DMA_LANES,))],
        compiler_params=_cparams(("arbitrary",)),
        name="moe_dispatch",
    )(*slots, hq, init)


def _experts_body(plan_ref, xs_ref, wg_ref, wu_ref, wd_ref, ys_ref, wgu_ref, wdn_ref):
    t = pl.program_id(0)
    in_use = t < plan_ref[1, 0]
    new_expert = (t == 0) | (plan_ref[0, t] != plan_ref[0, jnp.maximum(t - 1, 0)])

    @pl.when(in_use & new_expert)
    def _():
        wgu_ref[:, :EXP_H] = wg_ref[0, 0].astype(BF16)
        wgu_ref[:, EXP_H:] = wu_ref[0, 0].astype(BF16)
        wdn_ref[...] = wd_ref[0, 0].astype(BF16)

    @pl.when(in_use)
    def _():
        x = _load_unpacked(xs_ref).astype(BF16)
        ab = jnp.dot(x, wgu_ref[...], preferred_element_type=F32)
        a, b = ab[:, :EXP_H], ab[:, EXP_H:]
        act = a * _sigmoid(a) * b
        _store_packed(ys_ref, jnp.dot(act.astype(BF16), wdn_ref[...], preferred_element_type=F32))


def _experts(plan, xs, w_gate, w_up, w_down, *, layer):
    def tile_idx(t, plan):
        return (jnp.minimum(t, jnp.maximum(plan[1, 0] - 1, 0)), 0, 0, 0)

    grid_spec = pltpu.PrefetchScalarGridSpec(
        num_scalar_prefetch=1,
        grid=(SORT_TILES,),
        in_specs=[
            pl.BlockSpec(_packed_shape(EXP_TILE), tile_idx),
            pl.BlockSpec((1, 1, D, EXP_H), lambda t, plan: (layer, plan[0, t], 0, 0)),
            pl.BlockSpec((1, 1, D, EXP_H), lambda t, plan: (layer, plan[0, t], 0, 0)),
            pl.BlockSpec((1, 1, EXP_H, D), lambda t, plan: (layer, plan[0, t], 0, 0)),
        ],
        out_specs=pl.BlockSpec(_packed_shape(EXP_TILE), tile_idx),
        scratch_shapes=[pltpu.VMEM((D, 2 * EXP_H), BF16), pltpu.VMEM((EXP_H, D), BF16)],
    )
    return pl.pallas_call(
        _experts_body,
        grid_spec=grid_spec,
        out_shape=jax.ShapeDtypeStruct(_packed_shape(SORT_ROWS), jnp.uint32),
        input_output_aliases={1: 0},
        compiler_params=_cparams(("arbitrary",)),
        name="moe_experts",
    )(plan, xs, w_gate, w_up, w_down)


def _combine_body(*refs, tm, final):
    cur_slots, nxt_slots = refs[0:4], refs[4:8]
    meta_ref, x_ref, ada_ref = refs[8:11]
    rest = refs[11:]
    if final:
        fgain_ref, ys_ref, oc_ref, ol_ref, y_ref, sem = rest
    else:
        ys_ref, o_ref, y_ref, sem = rest
    step = pl.program_id(0)
    n_steps = pl.num_programs(0)

    def fetch(slot_refs, buf):
        slots = ((slot_refs[0], slot_refs[1]), (slot_refs[2], slot_refs[3]))

        def copy(blk, j, k, lane):
            i = blk * ROW_UNROLL + j
            return pltpu.make_async_copy(_packed_row(ys_ref, slots[k][0][i], slots[k][1][i]),
                                         _packed_row(y_ref.at[buf, k], blk, j), sem.at[buf, lane])

        _start_row_copies(tm, copy)

    @pl.when(step == 0)
    def _():
        fetch(cur_slots, 0)

    @pl.when(step + 1 < n_steps)
    def _():
        fetch(nxt_slots, (step + 1) % 2)

    buf = step % 2
    _drain_row_copies(ys_ref.at[pl.ds(0, tm // 8)], y_ref.at[buf, 0], sem.at[buf])
    y_ref = y_ref.at[buf]
    meta = meta_ref[...]
    mix = (meta[:, META_W1:META_W1 + 1] * _load_unpacked(y_ref.at[0])
           + meta[:, META_W2:META_W2 + 1] * _load_unpacked(y_ref.at[1]))
    x = x_ref[...] + ada_ref[0, 5:6, :] * mix
    if not final:
        o_ref[...] = x
        return
    ms = jnp.mean(x * x, axis=-1, keepdims=True)
    y = x * lax.rsqrt(ms + EPS) * fgain_ref[...]
    is_ctx = pl.program_id(0) < N_CTX // tm

    @pl.when(is_ctx)
    def _():
        oc_ref[...] = y

    @pl.when(jnp.logical_not(is_ctx))
    def _():
        ol_ref[...] = y


def _combine(slots, ys, meta, x, ada, final_gain=None, *, tm=512):
    final = final_gain is not None
    extra_in = [pl.BlockSpec((1, D), lambda t: (0, 0))] if final else []
    extra_args = [final_gain.reshape(1, D)] if final else []
    if final:
        out_specs = _split_specs(tm, D)
        out_shape = [jax.ShapeDtypeStruct((N_CTX, D), F32), jax.ShapeDtypeStruct((N_LAT, D), F32)]
    else:
        out_specs = pl.BlockSpec((tm, D), lambda t: (t, 0))
        out_shape = jax.ShapeDtypeStruct((N_TOK, D), F32)
    return pl.pallas_call(
        functools.partial(_combine_body, tm=tm, final=final),
        grid=(N_TOK // tm,),
        in_specs=_slot_specs(tm) + _slot_specs(tm, ahead=1) + [
            pl.BlockSpec((tm, ROUTE_W), lambda t: (t, 0)),
            pl.BlockSpec((tm, D), lambda t: (t, 0)),
            pl.BlockSpec((1, 6, D), lambda t: (_cond_row(t, tm), 0, 0)),
        ] + extra_in + [pl.BlockSpec(memory_space=pl.ANY)],
        out_specs=out_specs,
        out_shape=out_shape,
        scratch_shapes=[pltpu.VMEM((2, 2) + _packed_shape(tm), jnp.uint32),
                        pltpu.SemaphoreType.DMA((2, N_DMA_LANES))],
        compiler_params=_cparams(("arbitrary",)),
        name="moe_combine",
    )(*slots, *slots, meta, x, ada, *extra_args, ys)


def _moe(routed, ada, w_gate, w_up, w_down, *, layer, sort_init, final_gain=None):
    x, hq, meta, cnt = routed
    pos, plan = _slots(meta, cnt)
    slots = [pos[:, r, :].reshape(N_TOK) for r in range(4)]
    xs = _dispatch(slots, hq, sort_init)
    ys = _experts(plan, xs, w_gate, w_up, w_down, layer=layer)
    return _combine(slots, ys, meta, x, ada, final_gain), ys


def kernel(x_prompt, x_sample, cache_k, cache_v, c, c_ctx, w_ada, b_ada, norm_mix, norm_ffn, attn_w_q, attn_w_kv, attn_q_norm, attn_k_norm, attn_w_o, hy_w_in, hy_b_in, hy_short_w, hy_short_b, hy_filt_w1, hy_filt_b1, hy_filt_w2, hy_filt_b2, hy_filt_w3, hy_filt_b3, hy_filt_freq, hy_log_decay, hy_skip, hy_w_out, hy_b_out, router_group_w, router_group_b, router_expert_w, router_expert_b, moe_w_gate, moe_w_up, moe_w_down, final_norm):
    depth = w_ada.shape[0]
    x_parts = (x_prompt.reshape(N_CTX, D), x_sample.reshape(N_LAT, D))
    cond8 = jnp.zeros((8, D), F32).at[0].set(c_ctx).at[1:1 + N_BATCH_LAT].set(c)
    ada_all = _ada(cond8, w_ada, b_ada).reshape(depth, 8, 6, D)
    new_k = []
    new_v = []
    sort_buf = jnp.zeros(_packed_shape(SORT_ROWS), jnp.uint32)
    for l in range(depth):
        ada = ada_all[l]
        if l % 2 == 0:
            a = l // 2
            w_qkv = jnp.concatenate([attn_w_q[a], attn_w_kv[a]], axis=1).astype(BF16)
            q, k_ctx, v_ctx, kc, vc, kl, vl = _qkv(x_parts, ada, norm_mix[l], w_qkv, attn_q_norm[a], attn_k_norm[a])
            new_k.append(k_ctx)
            new_v.append(v_ctx)
            o_ctx = _attention(q, [kc], [vc], row0=0, n_batch=N_BATCH_CTX, seq=L_CTX, tq=L_CTX,
                               n_kv=N_KV, stack=Q_PER_KV)
            past_keys = _lane_tile_heads(cache_k[:, a]).transpose(0, 1, 3, 2)
            o_lat = _attention(q, [kl, past_keys], [vl, _lane_tile_heads(cache_v[:, a])],
                               row0=N_CTX, n_batch=N_BATCH_LAT, seq=L_LAT, tq=1024, n_kv=1, stack=1)
            mixer = (o_ctx, o_lat, attn_w_o[a].astype(BF16), jnp.zeros((D,), F32))
        else:
            j = l // 2
            (x,) = x_parts
            up = _norm_proj(x, ada, norm_mix[l], hy_w_in[j].astype(BF16), hy_b_in[j], part=0)
            outs = []
            for row0, n_batch, L, tc in ((0, N_BATCH_CTX, L_CTX, D), (N_CTX, N_BATCH_LAT, L_LAT, 256)):
                cs = jnp.asarray(_dft_table(L)).astype(BF16)
                fr, fi, fn = _filter_spectra(L, cs, hy_filt_w1[j], hy_filt_b1[j], hy_filt_w2[j], hy_filt_b2[j],
                                             hy_filt_w3[j], hy_filt_b3[j], hy_filt_freq[j], hy_log_decay[j])
                outs.append(_hyena_conv(up, hy_short_w[j], hy_short_b[j], hy_skip[j], fr, fi, fn, cs,
                                        row0=row0, n_batch=n_batch, L=L, tc=tc))
            mixer = (outs[0], outs[1], hy_w_out[j].astype(BF16), hy_b_out[j])
        routed = _mix_route(*mixer, x_parts, ada, norm_ffn[l], router_group_w[l], router_group_b[l],
                            router_expert_w[l], router_expert_b[l])
        out, sort_buf = _moe(routed, ada, moe_w_gate, moe_w_up, moe_w_down, layer=l, sort_init=sort_buf,
                             final_gain=final_norm if l == depth - 1 else None)
        x_parts = (out,)
    y_ctx, y_lat = x_parts[0]
    return (y_ctx.reshape(N_BATCH_CTX, L_CTX, D), y_lat.reshape(N_BATCH_LAT, L_LAT, D),
            jnp.stack(new_k, axis=1), jnp.stack(new_v, axis=1))
```
